```python
import jax, jax.numpy as jnp
from jax import lax
import numpy as np

D_MODEL = 1024
BATCH = 4
SEQ = 4096
DEPTH = 1

CHUNK = 64
Q_BLOCK = 128
HEAD_DIM = 64
N_HEADS_A = 8
N_HEADS_B = 8
IDX_HEADS = 8
IDX_DIM = 64
TOPK_MAX = 256
ROPE_DIM = HEAD_DIM // 4
ROPE_THETA = 500000.0
N_GROUPS = 4
EXPERTS_PER_GROUP = 8
N_EXPERTS = N_GROUPS * EXPERTS_PER_GROUP
TOPK_IN_GROUP = 2
D_EXPERT = 512
EXPERT_BLOCK = 128
RMS_EPS = 1e-6
NEG_INF = -1e30
ATTN_SCALE = HEAD_DIM ** -0.5
IDX_SCALE = IDX_DIM ** -0.5

W_IN_SPLITS = (N_HEADS_A * HEAD_DIM, HEAD_DIM, HEAD_DIM, IDX_HEADS * IDX_DIM, IDX_DIM, IDX_HEADS,
               N_HEADS_B * HEAD_DIM, N_HEADS_B * HEAD_DIM, N_HEADS_B * HEAD_DIM, N_HEADS_B,
               D_MODEL, D_MODEL)
W_IN_COLS = sum(W_IN_SPLITS)

kernel_name = 'hybrid_dsa_fox_hiermoe_adaln_block'


def _split_cols(proj):
    offs = []
    acc = 0
    for w in W_IN_SPLITS[:-1]:
        acc += w
        offs.append(acc)
    return jnp.split(proj, offs, axis=-1)


def rmsnorm(t, g):
    tf = t.astype(jnp.float32)
    tf = tf * lax.rsqrt(jnp.mean(tf * tf, axis=-1, keepdims=True) + RMS_EPS)
    return (tf * g.astype(jnp.float32)).astype(t.dtype)


def modulate(h, shift, scale):
    return h * (1.0 + scale[:, None, :]) + shift[:, None, :]


def rope_tables(positions):
    inv_freq = ROPE_THETA ** (-jnp.arange(0, ROPE_DIM, 2, dtype=jnp.float32) / ROPE_DIM)
    ang = positions.astype(jnp.float32)[..., None] * inv_freq
    return jnp.cos(ang), jnp.sin(ang)


def partial_rope(t, cos, sin):
    extra = t.ndim - 3
    cos = cos.reshape(cos.shape[:2] + (1,) * extra + cos.shape[2:])
    sin = sin.reshape(sin.shape[:2] + (1,) * extra + sin.shape[2:])
    half = ROPE_DIM // 2
    r = t[..., :ROPE_DIM].astype(jnp.float32)
    r1, r2 = r[..., :half], r[..., half:]
    rot = jnp.concatenate([r1 * cos - r2 * sin, r2 * cos + r1 * sin], axis=-1)
    return jnp.concatenate([rot.astype(t.dtype), t[..., ROPE_DIM:]], axis=-1)


def token_mixers(h, cos, sin, w_in, b_forget, w_out_a, w_out_b, w_out):
    B, S, _ = h.shape
    nb = S // Q_BLOCK
    topk = min(TOPK_MAX, S // 4)
    qa, ka, va, qi, ki, wi, qb, kb, vb, fb, ga, gb = _split_cols(h @ w_in)

    qa = partial_rope(qa.reshape(B, S, N_HEADS_A, HEAD_DIM), cos, sin)
    ka = partial_rope(ka, cos, sin)
    qi = partial_rope(qi.reshape(B, S, IDX_HEADS, IDX_DIM), cos, sin)
    ki = partial_rope(ki, cos, sin)
    wi = wi * IDX_HEADS ** -0.5
    chunk_id = jnp.arange(S) // CHUNK

    def blocks(t):
        return jnp.swapaxes(t.reshape((B, nb, Q_BLOCK) + t.shape[2:]), 0, 1)

    def unblock(t):
        return jnp.swapaxes(t, 0, 1).reshape((B, S) + t.shape[3:])

    def dsa_block(args):
        bi, q_blk, qi_blk, wi_blk = args
        q_chunk = (bi * Q_BLOCK + jnp.arange(Q_BLOCK)) // CHUNK
        rel = jnp.einsum('bqhd,bsd->bqhs', qi_blk, ki,
                         preferred_element_type=jnp.float32) * IDX_SCALE
        score = jnp.einsum('bqh,bqhs->bqs', wi_blk.astype(jnp.float32), jax.nn.relu(rel))
        admissible = chunk_id[None, :] <= q_chunk[:, None]
        score = jnp.where(admissible[None], score, NEG_INF)
        _, sel = lax.top_k(score, topk)
        valid = chunk_id[sel] <= q_chunk[None, :, None]
        k_sel = jax.vmap(lambda t, i: t[i])(ka, sel)
        v_sel = jax.vmap(lambda t, i: t[i])(va, sel)
        logit = jnp.einsum('bqhd,bqkd->bqhk', q_blk, k_sel,
                           preferred_element_type=jnp.float32) * ATTN_SCALE
        logit = jnp.where(valid[:, :, None, :], logit, NEG_INF)
        p = jax.nn.softmax(logit, axis=-1).astype(v_sel.dtype)
        return jnp.einsum('bqhk,bqkd->bqhd', p, v_sel)

    oa = unblock(lax.map(dsa_block, (jnp.arange(nb), blocks(qa), blocks(qi), blocks(wi))))

    qb = qb.reshape(B, S, N_HEADS_B, HEAD_DIM)
    kb = kb.reshape(B, S, N_HEADS_B, HEAD_DIM)
    vb = vb.reshape(B, S, N_HEADS_B, HEAD_DIM)
    logf = jax.nn.log_sigmoid(fb.astype(jnp.float32) + b_forget.astype(jnp.float32))
    cum = jnp.cumsum(logf, axis=1)
    cum_t = jnp.swapaxes(cum, 1, 2)
    key_pos = jnp.arange(S)

    def fox_block(args):
        bi, q_blk, cq_blk = args
        q_pos = bi * Q_BLOCK + jnp.arange(Q_BLOCK)
        logit = jnp.einsum('bqhd,bshd->bhqs', q_blk, kb,
                           preferred_element_type=jnp.float32) * ATTN_SCALE
        logit = logit + jnp.swapaxes(cq_blk, 1, 2)[..., None] - cum_t[:, :, None, :]
        causal = key_pos[None, :] <= q_pos[:, None]
        logit = jnp.where(causal[None, None], logit, NEG_INF)
        p = jax.nn.softmax(logit, axis=-1).astype(vb.dtype)
        return jnp.einsum('bhqs,bshd->bqhd', p, vb)

    ob = unblock(lax.map(fox_block, (jnp.arange(nb), blocks(qb), blocks(cum))))

    ya = oa.reshape(B, S, N_HEADS_A * HEAD_DIM) @ w_out_a
    yb = ob.reshape(B, S, N_HEADS_B * HEAD_DIM) @ w_out_b
    merged = jax.nn.sigmoid(ga) * ya + jax.nn.sigmoid(gb) * yb
    return merged @ w_out


def hier_moe(h, w_group, b_group, w_router, b_router, w_e_gate, w_e_up, w_e_down):
    B, S, D = h.shape
    xt = h.reshape(B * S, D)
    N = xt.shape[0]
    g_logits = (xt @ w_group).astype(jnp.float32) + b_group.astype(jnp.float32)
    g_prob = jax.nn.softmax(g_logits, axis=-1)
    grp = jnp.argmax(g_logits, axis=-1)
    p_grp = jnp.take_along_axis(g_prob, grp[:, None], axis=1)[:, 0]
    e_logits = ((xt @ w_router).astype(jnp.float32) + b_router.astype(jnp.float32))
    e_logits = e_logits.reshape(N, N_GROUPS, EXPERTS_PER_GROUP)
    e_in = jnp.take_along_axis(e_logits, grp[:, None, None], axis=1)[:, 0]
    top_v, top_i = lax.top_k(e_in, TOPK_IN_GROUP)
    gate_w = jax.nn.softmax(top_v, axis=-1) * p_grp[:, None]
    eid = grp[:, None] * EXPERTS_PER_GROUP + top_i

    A = N * TOPK_IN_GROUP
    flat_e = eid.reshape(-1)
    flat_tok = jnp.repeat(jnp.arange(N, dtype=jnp.int32), TOPK_IN_GROUP)
    flat_w = gate_w.reshape(-1)
    order = jnp.argsort(flat_e)
    se, stok, sw = flat_e[order], flat_tok[order], flat_w[order]
    counts = jnp.bincount(flat_e, length=N_EXPERTS)
    start = jnp.cumsum(counts) - counts
    padded = (counts + EXPERT_BLOCK - 1) // EXPERT_BLOCK * EXPERT_BLOCK
    pend = jnp.cumsum(padded)
    pstart = pend - padded
    dest = pstart[se] + (jnp.arange(A) - start[se])
    P = (A + EXPERT_BLOCK - 1) // EXPERT_BLOCK * EXPERT_BLOCK + N_EXPERTS * EXPERT_BLOCK
    n_blocks = P // EXPERT_BLOCK
    tok_buf = jnp.zeros((P,), jnp.int32).at[dest].set(stok)
    w_buf = jnp.zeros((P,), jnp.float32).at[dest].set(sw)
    block_e = jnp.minimum(jnp.searchsorted(pend, jnp.arange(n_blocks) * EXPERT_BLOCK, side='right'),
                          N_EXPERTS - 1)

    def expert_block(args):
        e, toks = args
        xb = xt[toks]
        hid = jax.nn.silu(xb @ w_e_gate[e]) * (xb @ w_e_up[e])
        return hid @ w_e_down[e]

    y_buf = lax.map(expert_block, (block_e, tok_buf.reshape(n_blocks, EXPERT_BLOCK)))
    contrib = (y_buf.reshape(P, D) * w_buf[:, None]).astype(xt.dtype)
    y = jnp.zeros_like(xt).at[tok_buf].add(contrib)
    return y.reshape(B, S, D)


def setup_inputs(seed: int = 0) -> dict:
    key = jax.random.key(seed)
    ks = jax.random.split(key, 24)
    L, D = DEPTH, D_MODEL

    def nrm(k, shape, scale):
        return jax.random.normal(k, shape, jnp.float32) * scale

    x = nrm(ks[0], (BATCH, SEQ, D), 1.0)
    c = nrm(ks[1], (BATCH, D), 1.0)
    start = jax.random.randint(ks[2], (BATCH, 1), 0, 64, dtype=jnp.int32) * CHUNK
    positions = (start + jnp.arange(SEQ, dtype=jnp.int32)[None, :]).astype(jnp.int32)
    return {
        'x': x,
        'c': c,
        'positions': positions,
        'w_mod': nrm(ks[3], (L, D, 6 * D), D ** -0.5),
        'b_mod': nrm(ks[4], (L, 6 * D), 0.02),
        'g_mix': 1.0 + nrm(ks[5], (L, D), 0.02),
        'w_in': nrm(ks[6], (L, D, W_IN_COLS), D ** -0.5),
        'b_forget': 2.0 + nrm(ks[7], (L, N_HEADS_B), 0.5),
        'w_out_a': nrm(ks[8], (L, N_HEADS_A * HEAD_DIM, D), (N_HEADS_A * HEAD_DIM) ** -0.5),
        'w_out_b': nrm(ks[9], (L, N_HEADS_B * HEAD_DIM, D), (N_HEADS_B * HEAD_DIM) ** -0.5),
        'w_out': nrm(ks[10], (L, D, D), D ** -0.5),
        'g_ffn': 1.0 + nrm(ks[11], (L, D), 0.02),
        'w_group': nrm(ks[12], (L, D, N_GROUPS), D ** -0.5),
        'b_group': nrm(ks[13], (L, N_GROUPS), 0.01),
        'w_router': nrm(ks[14], (L, D, N_EXPERTS), D ** -0.5),
        'b_router': nrm(ks[15], (L, N_EXPERTS), 0.01),
        'w_e_gate': nrm(ks[16], (L, N_EXPERTS, D, D_EXPERT), D ** -0.5),
        'w_e_up': nrm(ks[17], (L, N_EXPERTS, D, D_EXPERT), D ** -0.5),
        'w_e_down': nrm(ks[18], (L, N_EXPERTS, D_EXPERT, D), D_EXPERT ** -0.5),
        'g_final': 1.0 + nrm(ks[19], (D,), 0.02),
    }


def reference(x, c, positions, w_mod, b_mod, g_mix, w_in, b_forget, w_out_a, w_out_b, w_out,
              g_ffn, w_group, b_group, w_router, b_router, w_e_gate, w_e_up, w_e_down, g_final):
    cos, sin = rope_tables(positions)
    c_act = jax.nn.silu(c)
    for l in range(DEPTH):
        mod = c_act @ w_mod[l] + b_mod[l]
        sh1, sc1, gt1, sh2, sc2, gt2 = jnp.split(mod, 6, axis=-1)
        h = modulate(rmsnorm(x, g_mix[l]), sh1, sc1)
        x = x + gt1[:, None, :] * token_mixers(h, cos, sin, w_in[l], b_forget[l],
                                               w_out_a[l], w_out_b[l], w_out[l])
        h = modulate(rmsnorm(x, g_ffn[l]), sh2, sc2)
        x = x + gt2[:, None, :] * hier_moe(h, w_group[l], b_group[l], w_router[l], b_router[l],
                                           w_e_gate[l], w_e_up[l], w_e_down[l])
    return rmsnorm(x, g_final)
```

```python
import functools

import numpy as np
import jax
import jax.numpy as jnp
from jax import lax
from jax.experimental import pallas as pl
from jax.experimental.pallas import tpu as pltpu

F32 = jnp.float32
BF16 = jnp.bfloat16
I32 = jnp.int32

D_MODEL = 1024
HEAD_DIM = 64
N_HEADS = 8
CHUNK_SHIFT = 6
TOPK = 256
ROPE_DIM = 16
ROPE_THETA = 500000.0
N_GROUPS = 4
EXPERTS_PER_GROUP = 8
N_EXPERTS = 32
D_EXPERT = 512
RMS_EPS = 1e-6
NEG_INF = -1e30
ATTN_SCALE = HEAD_DIM ** -0.5
IDX_SCALE = HEAD_DIM ** -0.5

LANES = 128
SUBLANES = 8
VMEM_LIMIT = 56 * 1024 * 1024

TM_IN = 512
TQ_FOX = 256
TK_FOX = 256
TQ_DSA = 128
TK_DSA = 256
TM_OUT = 512
TM_ROUTE = 1024
TM_DISPATCH = 512
BLK_E = 256
TM_FINAL = 256

_SEG_QA, _SEG_QI, _SEG_QB, _SEG_KB, _SEG_VB = 0, 512, 1024, 1536, 2048
_SEG_GA, _SEG_GB = 2560, 3584
_SEG_KA, _SEG_VA, _SEG_KI, _SEG_AUX = 4608, 4736, 4864, 4992
_NC_IN = 5120

_SENT_KEY = int(np.array(NEG_INF, np.float32).view(np.int32) ^ 0x7FFFFFFF)
_INT_MIN = -(2 ** 31)


def _cparams(sem):
    return pltpu.CompilerParams(dimension_semantics=sem, vmem_limit_bytes=VMEM_LIMIT)


def _nt_dot(a, b):
    return lax.dot_general(a, b, (((1,), (1,)), ((), ())), preferred_element_type=F32)


def _dot(a, b):
    return jnp.dot(a, b, preferred_element_type=F32)


def _sigmoid(x):
    return 1.0 / (1.0 + jnp.exp(-x))


def _mod_kernel(c_ref, w_ref, b_ref, o_ref):
    c = c_ref[...]
    ca = c * _sigmoid(c)
    o_ref[...] = _dot(ca.astype(BF16), w_ref[...].astype(BF16)) + b_ref[...]


def _mod_call(c8, w_mod, b_mod):
    n_out = w_mod.shape[1]
    tn = 1024
    return pl.pallas_call(
        _mod_kernel,
        grid=(n_out // tn,),
        in_specs=[pl.BlockSpec((8, D_MODEL), lambda j: (0, 0)),
                  pl.BlockSpec((D_MODEL, tn), lambda j: (0, j)),
                  pl.BlockSpec((1, tn), lambda j: (0, j))],
        out_specs=pl.BlockSpec((8, tn), lambda j: (0, j)),
        out_shape=jax.ShapeDtypeStruct((8, n_out), F32),
        compiler_params=_cparams(("arbitrary",)),
        name="mod",
    )(c8, w_mod, b_mod)


def _inproj_kernel(x_ref, pos_ref, mod_ref, g_ref, w_ref, invf_ref, fbias_ref,
                   qa_ref, qi_ref, qb_ref, kb_ref, vb_ref, sga_ref, sgb_ref,
                   ka_ref, va_ref, ki_ref, aux_ref, h_scr, carry_scr):
    tm = TM_IN
    x = x_ref[...]
    var = jnp.mean(x * x, axis=-1, keepdims=True)
    tf = x * lax.rsqrt(var + RMS_EPS) * g_ref[...]
    sh = mod_ref[0, 0:1, :]
    sc = mod_ref[0, 1:2, :]
    h_scr[...] = (tf * (1.0 + sc) + sh).astype(BF16)
    hb = h_scr[...]

    lane = lax.broadcasted_iota(I32, (tm, LANES), 1)
    j = lane & (HEAD_DIM - 1)
    ang = pos_ref[...].astype(F32) * invf_ref[...]
    cs = jnp.cos(ang)
    sn = jnp.sin(ang)
    coef_next = jnp.where(j < ROPE_DIM // 2, -sn, 0.0)
    coef_prev = jnp.where((j >= ROPE_DIM // 2) & (j < ROPE_DIM), sn, 0.0)

    def rope(tc):
        return (tc * cs + pltpu.roll(tc, LANES - ROPE_DIM // 2, 1) * coef_next
                + pltpu.roll(tc, ROPE_DIM // 2, 1) * coef_prev)

    def seg(off, width):
        return _dot(hb, w_ref[:, off:off + width])

    t = seg(_SEG_QA, 512)
    for c in range(4):
        qa_ref[:, c * LANES:(c + 1) * LANES] = (rope(t[:, c * LANES:(c + 1) * LANES]) * ATTN_SCALE).astype(BF16)
    t = seg(_SEG_QI, 512)
    for c in range(4):
        qi_ref[:, c * LANES:(c + 1) * LANES] = (rope(t[:, c * LANES:(c + 1) * LANES]) * IDX_SCALE).astype(BF16)
    qb_ref[...] = (seg(_SEG_QB, 512) * ATTN_SCALE).astype(BF16)
    kb_ref[...] = seg(_SEG_KB, 512).astype(BF16)
    vb_ref[...] = seg(_SEG_VB, 512).astype(BF16)
    sga_ref[...] = _sigmoid(seg(_SEG_GA, 1024)).astype(BF16)
    sgb_ref[...] = _sigmoid(seg(_SEG_GB, 1024)).astype(BF16)
    ka_ref[...] = rope(seg(_SEG_KA, LANES)).astype(BF16)
    va_ref[...] = seg(_SEG_VA, LANES).astype(BF16)
    ki_ref[...] = rope(seg(_SEG_KI, LANES)).astype(BF16)

    z = seg(_SEG_AUX, LANES)
    zf = z + fbias_ref[...]
    logf = jnp.minimum(zf, 0.0) - jnp.log(1.0 + jnp.exp(-jnp.abs(zf)))
    is_f = (lane >= N_HEADS) & (lane < 2 * N_HEADS)
    logf = jnp.where(is_f, logf, 0.0)
    rr = lax.broadcasted_iota(I32, (tm, tm), 0)
    cc = lax.broadcasted_iota(I32, (tm, tm), 1)
    tri = jnp.where(cc <= rr, 1.0, 0.0).astype(BF16)
    p_hi = logf.astype(BF16)
    r1 = logf - p_hi.astype(F32)
    p_mid = r1.astype(BF16)
    p_lo = (r1 - p_mid.astype(F32)).astype(BF16)

    @pl.when(pl.program_id(1) == 0)
    def _():
        carry_scr[...] = jnp.zeros_like(carry_scr)

    cum = _dot(tri, p_hi) + _dot(tri, p_mid) + _dot(tri, p_lo) + carry_scr[...]
    carry_scr[...] = cum[tm - 1:tm, :]
    aux_ref[...] = jnp.where(lane < N_HEADS, z * (N_HEADS ** -0.5), jnp.where(is_f, cum, 0.0))


def _inproj_call(x2, pos2, mod3, g_mix, w_in_p, invf, fbias, batch, seq):
    n = x2.shape[0]
    tpb = seq // TM_IN
    row = lambda b, s: (b * tpb + s, 0)
    bf = lambda w: jax.ShapeDtypeStruct((n, w), BF16)
    out_shapes = [bf(512), bf(512), bf(512), bf(512), bf(512), bf(1024), bf(1024),
                  bf(LANES), bf(LANES), bf(LANES), jax.ShapeDtypeStruct((n, LANES), F32)]
    out_specs = [pl.BlockSpec((TM_IN, s.shape[1]), row) for s in out_shapes]
    return pl.pallas_call(
        _inproj_kernel,
        grid=(batch, tpb),
        in_specs=[pl.BlockSpec((TM_IN, D_MODEL), row),
                  pl.BlockSpec((TM_IN, 1), row),
                  pl.BlockSpec((1, 6, D_MODEL), lambda b, s: (b, 0, 0)),
                  pl.BlockSpec((1, D_MODEL), lambda b, s: (0, 0)),
                  pl.BlockSpec((D_MODEL, _NC_IN), lambda b, s: (0, 0)),
                  pl.BlockSpec((1, LANES), lambda b, s: (0, 0)),
                  pl.BlockSpec((1, LANES), lambda b, s: (0, 0))],
        out_specs=out_specs,
        out_shape=out_shapes,
        scratch_shapes=[pltpu.VMEM((TM_IN, D_MODEL), BF16), pltpu.VMEM((1, LANES), F32)],
        compiler_params=_cparams(("arbitrary", "arbitrary")),
        name="inproj",
    )(x2, pos2, mod3, g_mix, w_in_p, invf, fbias)


def _fox_kernel(q_ref, k_ref, v_ref, aux_ref, cumt_ref, o_ref):
    tq, tk = TQ_FOX, TK_FOX
    pair = pl.program_id(1)
    qi = pl.program_id(2)
    q0 = qi * tq
    q2 = q_ref[0]
    lane = lax.broadcasted_iota(I32, (tq, LANES), 1)
    auxq = aux_ref[0]
    rows = q0 + lax.broadcasted_iota(I32, (tq, tk), 0)
    cols_local = lax.broadcasted_iota(I32, (tq, tk), 1)
    outs = []
    for hh in range(2):
        h = 2 * pair + hh
        in_half = (lane < HEAD_DIM) if hh == 0 else (lane >= HEAD_DIM)
        qh = jnp.where(in_half, q2, jnp.zeros_like(q2))
        cq = jnp.sum(jnp.where(lane == N_HEADS + h, auxq, 0.0), axis=-1, keepdims=True)

        def tile(kt, carry, masked, qh=qh, cq=cq, h=h):
            m, l, acc = carry
            k0 = pl.multiple_of(kt * tk, tk)
            kk = k_ref[0, pl.ds(k0, tk), :]
            vv = v_ref[0, pl.ds(k0, tk), :]
            ck = cumt_ref[0, pl.ds(h, 1), pl.ds(kt, 1), :].reshape(1, tk)
            s = _nt_dot(qh, kk) + cq - ck
            if masked:
                s = jnp.where(k0 + cols_local <= rows, s, NEG_INF)
            m_new = jnp.maximum(m, jnp.max(s, axis=-1, keepdims=True))
            p = jnp.exp(s - m_new)
            alpha = jnp.exp(m - m_new)
            l = alpha * l + jnp.sum(p, axis=-1, keepdims=True)
            acc = alpha * acc + _dot(p.astype(BF16), vv)
            return m_new, l, acc

        carry = (jnp.full((tq, 1), NEG_INF, F32), jnp.zeros((tq, 1), F32), jnp.zeros((tq, LANES), F32))
        n_full = qi * (tq // tk)
        carry = lax.fori_loop(0, n_full, functools.partial(tile, masked=False), carry)
        for d in range(tq // tk):
            carry = tile(n_full + d, carry, True)
        _, l, acc = carry
        outs.append(acc * (1.0 / l))
    o_ref[0] = jnp.where(lane < HEAD_DIM, outs[0], outs[1]).astype(BF16)


def _fox_call(qb3, kb3, vb3, aux3, cumt4):
    batch, seq, _ = qb3.shape
    return pl.pallas_call(
        _fox_kernel,
        grid=(batch, N_HEADS // 2, seq // TQ_FOX),
        in_specs=[pl.BlockSpec((1, TQ_FOX, LANES), lambda b, p, i: (b, i, p)),
                  pl.BlockSpec((1, seq, LANES), lambda b, p, i: (b, 0, p)),
                  pl.BlockSpec((1, seq, LANES), lambda b, p, i: (b, 0, p)),
                  pl.BlockSpec((1, TQ_FOX, LANES), lambda b, p, i: (b, i, 0)),
                  pl.BlockSpec((1, N_HEADS, seq // TK_FOX, TK_FOX), lambda b, p, i: (b, 0, 0, 0))],
        out_specs=pl.BlockSpec((1, TQ_FOX, LANES), lambda b, p, i: (b, i, p)),
        out_shape=jax.ShapeDtypeStruct((batch, seq, 512), BF16),
        compiler_params=_cparams(("arbitrary", "arbitrary", "arbitrary")),
        name="fox",
    )(qb3, kb3, vb3, aux3, cumt4)


def _dsa_kernel(qa_ref, qi_ref, aux_ref, ki_ref, ka_ref, va_ref, o_ref, aqi_scr, aqa_scr, key_scr, *, seq):
    tq, tk = TQ_DSA, TK_DSA
    i = pl.program_id(1)
    q0 = i * tq
    nk = lax.shift_right_logical(q0 + tq + tk - 1, int(np.log2(tk)))
    n_rest = (seq - nk * tk).astype(F32)
    lane = lax.broadcasted_iota(I32, (tq, LANES), 1)

    for h in range(N_HEADS):
        in_half = (lane < HEAD_DIM) if h % 2 == 0 else (lane >= HEAD_DIM)
        blk = slice((h // 2) * LANES, (h // 2 + 1) * LANES)
        qi_blk = qi_ref[0, :, blk]
        qa_blk = qa_ref[0, :, blk]
        aqi_scr[h * tq:(h + 1) * tq, :] = jnp.where(in_half, qi_blk, jnp.zeros_like(qi_blk))
        aqa_scr[h * tq:(h + 1) * tq, :] = jnp.where(in_half, qa_blk, jnp.zeros_like(qa_blk))

    auxq = aux_ref[0]
    wcols = [jnp.sum(jnp.where(lane == h, auxq, 0.0), axis=-1, keepdims=True) for h in range(N_HEADS)]
    row_chunk = lax.shift_right_logical(q0 + lax.broadcasted_iota(I32, (tq, tk), 0), CHUNK_SHIFT)
    col_local = lax.broadcasted_iota(I32, (tq, tk), 1)

    def admissible(k0):
        return lax.shift_right_logical(k0 + col_local, CHUNK_SHIFT) <= row_chunk

    def p1(kt, _):
        k0 = pl.multiple_of(kt * tk, tk)
        rel = _nt_dot(aqi_scr[...], ki_ref[0, pl.ds(k0, tk), :])
        sc = wcols[0] * jnp.maximum(rel[0:tq], 0.0)
        for h in range(1, N_HEADS):
            sc = sc + wcols[h] * jnp.maximum(rel[h * tq:(h + 1) * tq], 0.0)
        sc = jnp.where(admissible(k0), sc, NEG_INF)
        sc = jnp.where(sc == 0.0, 0.0, sc)
        b = pltpu.bitcast(sc, I32)
        key_scr[kt] = b ^ (lax.shift_right_arithmetic(b, 31) & 0x7FFFFFFF)
        return 0

    lax.fori_loop(0, nk, p1, 0)

    def count(pred_fn, cand):
        def body(kt, acc):
            ind = jnp.where(pred_fn(key_scr[kt], cand), 1.0, 0.0)
            for c in range(tk // LANES):
                acc = acc + ind[:, c * LANES:(c + 1) * LANES]
            return acc
        acc = lax.fori_loop(0, nk, body, jnp.zeros((tq, LANES), F32))
        return jnp.sum(acc, axis=-1, keepdims=True)

    ge = lambda a, b: a >= b
    gt = lambda a, b: a > b
    kf = float(TOPK)

    def count_ge(cand):
        return count(ge, cand) + jnp.where(cand <= _SENT_KEY, n_rest, 0.0)

    zero = jnp.zeros((tq, 1), I32)
    thr = jnp.where(count_ge(zero) >= kf, zero, jnp.full((tq, 1), _INT_MIN, I32))

    def bit_step(jb, thr):
        cand = thr + lax.shift_left(jnp.int32(1), 30 - jb)
        return jnp.where(count_ge(cand) >= kf, cand, thr)

    thr = lax.fori_loop(0, 31, bit_step, thr)
    n_gt = count(gt, thr) + jnp.where(thr < _SENT_KEY, n_rest, 0.0)
    need = kf - n_gt

    tr = lax.broadcasted_iota(I32, (tk, tk), 0)
    tc = lax.broadcasted_iota(I32, (tk, tk), 1)
    tri = jnp.where(tr <= tc, 1.0, 0.0).astype(BF16)

    def p3(kt, carry):
        ms, ls, accs, tie = carry
        k0 = pl.multiple_of(kt * tk, tk)
        kk = key_scr[kt]
        eq = kk == thr
        pref = _dot(jnp.where(eq, 1.0, 0.0).astype(BF16), tri) + tie
        sel = (kk > thr) | (eq & (pref <= need))
        bias = jnp.where(sel & admissible(k0), 0.0, NEG_INF)
        tie = pref[:, tk - 1:tk]
        logit = _nt_dot(aqa_scr[...], ka_ref[0, pl.ds(k0, tk), :])
        new_ms, new_ls, alphas, ps = [], [], [], []
        for h in range(N_HEADS):
            s = logit[h * tq:(h + 1) * tq] + bias
            m_new = jnp.maximum(ms[h], jnp.max(s, axis=-1, keepdims=True))
            p = jnp.exp(s - m_new)
            alpha = jnp.exp(ms[h] - m_new)
            new_ms.append(m_new)
            new_ls.append(alpha * ls[h] + jnp.sum(p, axis=-1, keepdims=True))
            alphas.append(alpha)
            ps.append(p.astype(BF16))
        pv = _dot(jnp.concatenate(ps, axis=0), va_ref[0, pl.ds(k0, tk), :])
        new_accs = [alphas[h] * accs[h] + pv[h * tq:(h + 1) * tq] for h in range(N_HEADS)]
        return new_ms, new_ls, new_accs, tie

    carry = ([jnp.full((tq, 1), NEG_INF, F32) for _ in range(N_HEADS)],
             [jnp.zeros((tq, 1), F32) for _ in range(N_HEADS)],
             [jnp.zeros((tq, LANES), F32) for _ in range(N_HEADS)],
             jnp.zeros((tq, 1), F32))
    _, ls, accs, _ = lax.fori_loop(0, nk, p3, carry)
    for p in range(N_HEADS // 2):
        o0 = accs[2 * p] * (1.0 / ls[2 * p])
        o1 = accs[2 * p + 1] * (1.0 / ls[2 * p + 1])
        o_ref[0, :, p * LANES:(p + 1) * LANES] = jnp.where(lane < HEAD_DIM, o0, o1).astype(BF16)


def _dsa_call(qa3, qi3, aux3, ki3, ka3, va3):
    batch, seq, _ = qa3.shape
    qspec = lambda w: pl.BlockSpec((1, TQ_DSA, w), lambda b, i: (b, i, 0))
    kspec = pl.BlockSpec((1, seq, LANES), lambda b, i: (b, 0, 0))
    return pl.pallas_call(
        functools.partial(_dsa_kernel, seq=seq),
        grid=(batch, seq // TQ_DSA),
        in_specs=[qspec(512), qspec(512), qspec(LANES), kspec, kspec, kspec],
        out_specs=qspec(512),
        out_shape=jax.ShapeDtypeStruct((batch, seq, 512), BF16),
        scratch_shapes=[pltpu.VMEM((N_HEADS * TQ_DSA, LANES), BF16),
                        pltpu.VMEM((N_HEADS * TQ_DSA, LANES), BF16),
                        pltpu.VMEM((seq // TK_DSA, TQ_DSA, TK_DSA), I32)],
        compiler_params=_cparams(("arbitrary", "arbitrary")),
        name="dsa",
    )(qa3, qi3, aux3, ki3, ka3, va3)


def _outproj_kernel(oa_ref, ob_ref, sga_ref, sgb_ref, x_ref, mod_ref, gffn_ref,
                    woa_ref, wob_ref, wo_ref, wr_hi_ref, wr_lo_ref, br_ref,
                    x1_ref, h2_ref, lg_ref):
    tm = TM_OUT
    ya = _dot(oa_ref[...], woa_ref[...])
    yb = _dot(ob_ref[...], wob_ref[...])
    merged = (sga_ref[...].astype(F32) * ya + sgb_ref[...].astype(F32) * yb).astype(BF16)
    mix = _dot(merged, wo_ref[...])
    gt1 = mod_ref[0, 2:3, :]
    sh2 = mod_ref[0, 3:4, :]
    sc2 = mod_ref[0, 4:5, :]
    x1 = x_ref[...] + gt1 * mix
    x1_ref[...] = x1
    var = jnp.mean(x1 * x1, axis=-1, keepdims=True)
    h2 = x1 * lax.rsqrt(var + RMS_EPS) * gffn_ref[...] * (1.0 + sc2) + sh2
    for c in range(D_MODEL // LANES):
        h2_ref[pl.ds(c, tm, stride=SUBLANES), :] = h2[:, c * LANES:(c + 1) * LANES]
    hi = h2.astype(BF16)
    lo = (h2 - hi.astype(F32)).astype(BF16)
    lg_ref[...] = (_dot(hi, wr_hi_ref[...]) + _dot(hi, wr_lo_ref[...]) + _dot(lo, wr_hi_ref[...])
                   + br_ref[...])


def _outproj_call(oa2, ob2, sga, sgb, x2, mod3, g_ffn, woa, wob, wo, wr_hi, wr_lo, br, seq):
    n = x2.shape[0]
    tpb = seq // TM_OUT
    row = lambda w: pl.BlockSpec((TM_OUT, w), lambda i: (i, 0))
    full = lambda a: pl.BlockSpec(a.shape, lambda i: (0,) * a.ndim)
    return pl.pallas_call(
        _outproj_kernel,
        grid=(n // TM_OUT,),
        in_specs=[row(512), row(512), row(1024), row(1024), row(D_MODEL),
                  pl.BlockSpec((1, 6, D_MODEL), lambda i: (i // tpb, 0, 0)),
                  full(g_ffn), full(woa), full(wob), full(wo), full(wr_hi), full(wr_lo), full(br)],
        out_specs=[row(D_MODEL),
                   pl.BlockSpec((TM_OUT * SUBLANES, LANES), lambda i: (i, 0)),
                   row(LANES)],
        out_shape=[jax.ShapeDtypeStruct((n, D_MODEL), F32),
                   jax.ShapeDtypeStruct((n * SUBLANES, LANES), F32),
                   jax.ShapeDtypeStruct((n, LANES), F32)],
        compiler_params=_cparams(("arbitrary",)),
        name="outproj",
    )(oa2, ob2, sga, sgb, x2, mod3, g_ffn, woa, wob, wo, wr_hi, wr_lo, br)


def _route_kernel(lg_ref, rt_ref):
    tm = TM_ROUTE
    lg = lg_ref[...]
    lane = lax.broadcasted_iota(I32, (tm, LANES), 1)
    big = jnp.int32(LANES)
    ninf = -jnp.inf
    gmask = (lane >= N_EXPERTS) & (lane < N_EXPERTS + N_GROUPS)
    g = jnp.where(gmask, lg, ninf)
    gmax = jnp.max(g, axis=-1, keepdims=True)
    grp = jnp.min(jnp.where(g == gmax, lane - N_EXPERTS, big), axis=-1, keepdims=True)
    p_grp = 1.0 / jnp.sum(jnp.where(gmask, jnp.exp(lg - gmax), 0.0), axis=-1, keepdims=True)
    lo = grp * EXPERTS_PER_GROUP
    emask = (lane >= lo) & (lane < lo + EXPERTS_PER_GROUP)
    ev = jnp.where(emask, lg, ninf)
    v0 = jnp.max(ev, axis=-1, keepdims=True)
    i0 = jnp.min(jnp.where(emask & (ev == v0), lane, big), axis=-1, keepdims=True)
    rest = emask & (lane != i0)
    ev1 = jnp.where(rest, lg, ninf)
    v1 = jnp.max(ev1, axis=-1, keepdims=True)
    i1 = jnp.min(jnp.where(rest & (ev1 == v1), lane, big), axis=-1, keepdims=True)
    e1 = jnp.exp(v1 - v0)
    w0 = p_grp / (1.0 + e1)
    w1 = p_grp * e1 / (1.0 + e1)
    rt_ref[...] = jnp.where(lane == 0, i0.astype(F32),
                            jnp.where(lane == 1, i1.astype(F32),
                                      jnp.where(lane == 2, w0, jnp.where(lane == 3, w1, 0.0))))


def _route_call(lg):
    n = lg.shape[0]
    spec = pl.BlockSpec((TM_ROUTE, LANES), lambda i: (i, 0))
    return pl.pallas_call(
        _route_kernel, grid=(n // TM_ROUTE,), in_specs=[spec], out_specs=spec,
        out_shape=jax.ShapeDtypeStruct((n, LANES), F32),
        compiler_params=_cparams(("arbitrary",)), name="route",
    )(lg)


def _slab_copy(src, src_row, dst, dst_row, sem):
    return pltpu.make_async_copy(src.at[pl.ds(pl.multiple_of(src_row * SUBLANES, SUBLANES), SUBLANES)],
                                 dst.at[pl.ds(pl.multiple_of(dst_row * SUBLANES, SUBLANES), SUBLANES)], sem)


def _dispatch_kernel(dest_ref, h2_hbm, xs_in_hbm, xs_hbm, sem):
    del xs_in_hbm
    tm = TM_DISPATCH
    base = pl.program_id(0) * tm

    def issue(r, _):
        tok = base + r
        for k in range(2):
            _slab_copy(h2_hbm, tok, xs_hbm, dest_ref[2 * tok + k], sem).start()
        return 0

    lax.fori_loop(0, tm, issue, 0)
    rows = 2 * tm * SUBLANES
    pltpu.make_async_copy(h2_hbm.at[pl.ds(0, rows)], xs_hbm.at[pl.ds(0, rows)], sem).wait()


def _dispatch_call(dest, h2s, xs_zero):
    n = h2s.shape[0] // SUBLANES
    anyspec = pl.BlockSpec(memory_space=pl.ANY)
    return pl.pallas_call(
        _dispatch_kernel,
        grid_spec=pltpu.PrefetchScalarGridSpec(
            num_scalar_prefetch=1, grid=(n // TM_DISPATCH,),
            in_specs=[anyspec, anyspec], out_specs=anyspec,
            scratch_shapes=[pltpu.SemaphoreType.DMA(())]),
        out_shape=jax.ShapeDtypeStruct(xs_zero.shape, F32),
        input_output_aliases={2: 0},
        compiler_params=_cparams(("arbitrary",)),
        name="dispatch",
    )(dest, h2s, xs_zero)


def _experts_kernel(be_ref, nused_ref, xs_ref, wg_ref, wu_ref, wd_ref, yb_ref, x_scr, wg_scr, wu_scr, wd_scr):
    i = pl.program_id(0)

    @pl.when(i >= nused_ref[0])
    def _():
        yb_ref[...] = jnp.zeros_like(yb_ref)

    @pl.when(i < nused_ref[0])
    def _():
        prev = be_ref[jnp.maximum(i - 1, 0)]

        @pl.when((i == 0) | (be_ref[i] != prev))
        def _():
            wg_scr[...] = wg_ref[0].astype(BF16)
            wu_scr[...] = wu_ref[0].astype(BF16)
            wd_scr[...] = wd_ref[0].astype(BF16)

        for c in range(D_MODEL // LANES):
            x_scr[:, c * LANES:(c + 1) * LANES] = xs_ref[pl.ds(c, BLK_E, stride=SUBLANES), :].astype(BF16)
        xb = x_scr[...]
        g = _dot(xb, wg_scr[...])
        u = _dot(xb, wu_scr[...])
        hid = (g * _sigmoid(g) * u).astype(BF16)
        y = _dot(hid, wd_scr[...])
        for c in range(D_MODEL // LANES):
            yb_ref[pl.ds(c, BLK_E, stride=SUBLANES), :] = y[:, c * LANES:(c + 1) * LANES]


def _experts_call(block_e, n_used, xs, w_gate, w_up, w_down):
    n_blocks = block_e.shape[0]
    slab = pl.BlockSpec((BLK_E * SUBLANES, LANES), lambda i, be, nu: (i, 0))
    return pl.pallas_call(
        _experts_kernel,
        grid_spec=pltpu.PrefetchScalarGridSpec(
            num_scalar_prefetch=2, grid=(n_blocks,),
            in_specs=[slab,
                      pl.BlockSpec((1, D_MODEL, D_EXPERT), lambda i, be, nu: (be[i], 0, 0)),
                      pl.BlockSpec((1, D_MODEL, D_EXPERT), lambda i, be, nu: (be[i], 0, 0)),
                      pl.BlockSpec((1, D_EXPERT, D_MODEL), lambda i, be, nu: (be[i], 0, 0))],
            out_specs=slab,
            scratch_shapes=[pltpu.VMEM((BLK_E, D_MODEL), BF16),
                            pltpu.VMEM((D_MODEL, D_EXPERT), BF16),
                            pltpu.VMEM((D_MODEL, D_EXPERT), BF16),
                            pltpu.VMEM((D_EXPERT, D_MODEL), BF16)]),
        out_shape=jax.ShapeDtypeStruct(xs.shape, F32),
        compiler_params=_cparams(("arbitrary",)),
        name="experts",
    )(block_e, n_used, xs, w_gate, w_up, w_down)


def _final_kernel(dest_ref, x1_ref, rt_ref, mod_ref, gfin_ref, yb_hbm, o_ref, g_scr, sem):
    tm = TM_FINAL
    base = pl.program_id(0) * tm

    def issue(r, _):
        tok = base + r
        for k in range(2):
            _slab_copy(yb_hbm, dest_ref[2 * tok + k], g_scr, k * tm + r, sem).start()
        return 0

    lax.fori_loop(0, tm, issue, 0)
    pltpu.make_async_copy(yb_hbm.at[pl.ds(0, 2 * tm * SUBLANES)], g_scr, sem).wait()

    rt = rt_ref[...]
    lane = lax.broadcasted_iota(I32, (tm, LANES), 1)
    gw0 = jnp.sum(jnp.where(lane == 2, rt, 0.0), axis=-1, keepdims=True)
    gw1 = jnp.sum(jnp.where(lane == 3, rt, 0.0), axis=-1, keepdims=True)
    gt2 = mod_ref[0, 5:6, :]
    x1 = x1_ref[...]
    cols = []
    for c in range(D_MODEL // LANES):
        y0 = g_scr[pl.ds(c, tm, stride=SUBLANES), :]
        y1 = g_scr[pl.ds(tm * SUBLANES + c, tm, stride=SUBLANES), :]
        y = gw0 * y0 + gw1 * y1
        cols.append(x1[:, c * LANES:(c + 1) * LANES] + gt2[:, c * LANES:(c + 1) * LANES] * y)
    x2 = jnp.concatenate(cols, axis=1)
    var = jnp.mean(x2 * x2, axis=-1, keepdims=True)
    o_ref[...] = x2 * lax.rsqrt(var + RMS_EPS) * gfin_ref[...]


def _final_call(dest, x1, rt, mod3, g_final, yb, seq):
    n = x1.shape[0]
    tpb = seq // TM_FINAL
    return pl.pallas_call(
        _final_kernel,
        grid_spec=pltpu.PrefetchScalarGridSpec(
            num_scalar_prefetch=1, grid=(n // TM_FINAL,),
            in_specs=[pl.BlockSpec((TM_FINAL, D_MODEL), lambda i, d: (i, 0)),
                      pl.BlockSpec((TM_FINAL, LANES), lambda i, d: (i, 0)),
                      pl.BlockSpec((1, 6, D_MODEL), lambda i, d: (i // tpb, 0, 0)),
                      pl.BlockSpec((1, D_MODEL), lambda i, d: (0, 0)),
                      pl.BlockSpec(memory_space=pl.ANY)],
            out_specs=pl.BlockSpec((TM_FINAL, D_MODEL), lambda i, d: (i, 0)),
            scratch_shapes=[pltpu.VMEM((2 * TM_FINAL * SUBLANES, LANES), F32),
                            pltpu.SemaphoreType.DMA(())]),
        out_shape=jax.ShapeDtypeStruct((n, D_MODEL), F32),
        compiler_params=_cparams(("arbitrary",)),
        name="final",
    )(dest, x1, rt, mod3, g_final, yb)


def _permute_w_in(w):
    o = np.cumsum([0, 512, 64, 64, 512, 64, 8, 512, 512, 512, 8, 1024, 1024])
    qa, ka, va, qi, ki, wi, qb, kb, vb, fb, ga, gb = [w[:, o[k]:o[k + 1]] for k in range(12)]
    aux = jnp.concatenate([wi, fb, jnp.zeros((w.shape[0], LANES - 2 * N_HEADS), w.dtype)], axis=1)
    return jnp.concatenate([qa, qi, qb, kb, vb, ga, gb, ka, ka, va, va, ki, ki, aux], axis=1).astype(BF16)


def _layer(x2, pos2, mod3, batch, seq, g_mix, w_in, b_forget, w_out_a, w_out_b, w_out, g_ffn,
           w_group, b_group, w_router, b_router, w_e_gate, w_e_up, w_e_down, g_final):
    n = x2.shape[0]
    lane = np.arange(LANES)
    jj = lane % HEAD_DIM
    invf_np = np.where(jj < ROPE_DIM, ROPE_THETA ** (-(2.0 * (jj % (ROPE_DIM // 2))) / ROPE_DIM), 0.0)
    invf = jnp.asarray(invf_np[None, :], F32)
    fbias = jnp.zeros((1, LANES), F32).at[0, N_HEADS:2 * N_HEADS].set(b_forget.astype(F32))

    (qa, qi, qb, kb, vb, sga, sgb, ka2, va2, ki2, aux) = _inproj_call(
        x2, pos2, mod3, g_mix.reshape(1, D_MODEL), _permute_w_in(w_in), invf, fbias, batch, seq)

    r3 = lambda a: a.reshape(batch, seq, a.shape[-1])
    aux3 = r3(aux)
    cumt4 = jnp.swapaxes(aux3[:, :, N_HEADS:2 * N_HEADS], 1, 2).reshape(batch, N_HEADS, seq // TK_FOX, TK_FOX)
    ob = _fox_call(r3(qb), r3(kb), r3(vb), aux3, cumt4)
    oa = _dsa_call(r3(qa), r3(qi), aux3, r3(ki2), r3(ka2), r3(va2))

    w_rt = jnp.concatenate([w_router, w_group, jnp.zeros((D_MODEL, LANES - N_EXPERTS - N_GROUPS), F32)], axis=1)
    wr_hi = w_rt.astype(BF16)
    wr_lo = (w_rt - wr_hi.astype(F32)).astype(BF16)
    br = jnp.concatenate([b_router, b_group, jnp.zeros((LANES - N_EXPERTS - N_GROUPS,), F32)])[None, :]
    x1, h2s, lg = _outproj_call(oa.reshape(n, 512), ob.reshape(n, 512), sga, sgb, x2, mod3,
                                g_ffn.reshape(1, D_MODEL), w_out_a.astype(BF16), w_out_b.astype(BF16),
                                w_out.astype(BF16), wr_hi, wr_lo, br, seq)
    rt = _route_call(lg)

    flat_e = rt[:, 0:2].astype(I32).reshape(-1)
    onehot = (flat_e[:, None] == jnp.arange(N_EXPERTS, dtype=I32)[None, :]).astype(I32)
    csum = jnp.cumsum(onehot, axis=0)
    counts = csum[-1]
    rank = jnp.take_along_axis(csum, flat_e[:, None], axis=1)[:, 0] - 1
    padded = (counts + BLK_E - 1) // BLK_E * BLK_E
    pend = jnp.cumsum(padded)
    pstart = pend - padded
    dest = (pstart[flat_e] + rank).astype(I32)
    n_slots = 2 * n + N_EXPERTS * BLK_E
    n_blocks = n_slots // BLK_E
    block_e = jnp.minimum(jnp.searchsorted(pend, jnp.arange(n_blocks, dtype=I32) * BLK_E, side='right'),
                          N_EXPERTS - 1).astype(I32)
    n_used = (pend[-1:] // BLK_E).astype(I32)

    xs = _dispatch_call(dest, h2s, jnp.zeros((n_slots * SUBLANES, LANES), F32))
    yb = _experts_call(block_e, n_used, xs, w_e_gate, w_e_up, w_e_down)
    return _final_call(dest, x1, rt, mod3, g_final.reshape(1, D_MODEL), yb, seq)


def kernel(x, c, positions, w_mod, b_mod, g_mix, w_in, b_forget, w_out_a, w_out_b, w_out, g_ffn, w_group,
           b_group, w_router, b_router, w_e_gate, w_e_up, w_e_down, g_final):
    batch, seq, d = x.shape
    depth = w_mod.shape[0]
    assert depth == 1 and d == D_MODEL, "kernel fuses the final norm into the single layer"
    n = batch * seq
    c8 = jnp.zeros((8, d), F32).at[:batch].set(c)
    mod = _mod_call(c8, w_mod[0], b_mod[0][None, :])
    mod3 = mod[:batch].reshape(batch, 6, d)
    out = _layer(x.reshape(n, d), positions.reshape(n, 1), mod3, batch, seq, g_mix[0], w_in[0], b_forget[0],
                 w_out_a[0], w_out_b[0], w_out[0], g_ffn[0], w_group[0], b_group[0], w_router[0], b_router[0],
                 w_e_gate[0], w_e_up[0], w_e_down[0], g_final)
    return out.reshape(batch, seq, d)
```

```python
import functools

import numpy as np
import jax
import jax.numpy as jnp
from jax import lax
from jax.experimental import pallas as pl
from jax.experimental.pallas import tpu as pltpu

F32 = jnp.float32
BF16 = jnp.bfloat16
I32 = jnp.int32

D_MODEL = 1024
HEAD_DIM = 64
N_HEADS = 8
CHUNK_SHIFT = 6
TOPK = 256
ROPE_DIM = 16
ROPE_THETA = 500000.0
N_GROUPS = 4
EXPERTS_PER_GROUP = 8
N_EXPERTS = 32
D_EXPERT = 512
RMS_EPS = 1e-6
NEG_INF = -1e30
ATTN_SCALE = HEAD_DIM ** -0.5
IDX_SCALE = HEAD_DIM ** -0.5

LANES = 128
SUBLANES = 8
VMEM_LIMIT = 56 * 1024 * 1024

TM_IN = 512
TQ_FOX = 256
TK_FOX = 256
TQ_DSA = 128
TK_DSA = 256
TM_OUT = 512
TM_ROUTE = 1024
TM_DISPATCH = 512
BLK_E = 256
TM_FINAL = 256

_SEG_QA, _SEG_QI, _SEG_QB, _SEG_KB, _SEG_VB = 0, 512, 1024, 1536, 2048
_SEG_GA, _SEG_GB = 2560, 3584
_SEG_KA, _SEG_VA, _SEG_KI, _SEG_AUX = 4608, 4736, 4864, 4992
_NC_IN = 5120

_SENT_KEY = int(np.array(NEG_INF, np.float32).view(np.int32) ^ 0x7FFFFFFF)
_INT_MIN = -(2 ** 31)


def _cparams(sem):
    return pltpu.CompilerParams(dimension_semantics=sem, vmem_limit_bytes=VMEM_LIMIT)


def _nt_dot(a, b):
    return lax.dot_general(a, b, (((1,), (1,)), ((), ())), preferred_element_type=F32)


def _dot(a, b):
    return jnp.dot(a, b, preferred_element_type=F32)


def _sigmoid(x):
    return 1.0 / (1.0 + jnp.exp(-x))


def _mod_kernel(c_ref, w_ref, b_ref, o_ref):
    c = c_ref[...]
    ca = c * _sigmoid(c)
    o_ref[...] = _dot(ca.astype(BF16), w_ref[...].astype(BF16)) + b_ref[...]


def _mod_call(c8, w_mod, b_mod):
    n_out = w_mod.shape[1]
    tn = 1024
    return pl.pallas_call(
        _mod_kernel,
        grid=(n_out // tn,),
        in_specs=[pl.BlockSpec((8, D_MODEL), lambda j: (0, 0)),
                  pl.BlockSpec((D_MODEL, tn), lambda j: (0, j)),
                  pl.BlockSpec((1, tn), lambda j: (0, j))],
        out_specs=pl.BlockSpec((8, tn), lambda j: (0, j)),
        out_shape=jax.ShapeDtypeStruct((8, n_out), F32),
        compiler_params=_cparams(("arbitrary",)),
        name="mod",
    )(c8, w_mod, b_mod)


def _inproj_kernel(x_ref, pos_ref, mod_ref, g_ref, w_ref, invf_ref, fbias_ref,
                   qa_ref, qi_ref, qb_ref, kb_ref, vb_ref, sga_ref, sgb_ref,
                   ka_ref, va_ref, ki_ref, aux_ref, h_scr, carry_scr):
    tm = TM_IN
    x = x_ref[...]
    var = jnp.mean(x * x, axis=-1, keepdims=True)
    tf = x * lax.rsqrt(var + RMS_EPS) * g_ref[...]
    sh = mod_ref[0, 0:1, :]
    sc = mod_ref[0, 1:2, :]
    h_scr[...] = (tf * (1.0 + sc) + sh).astype(BF16)
    hb = h_scr[...]

    lane = lax.broadcasted_iota(I32, (tm, LANES), 1)
    j = lane & (HEAD_DIM - 1)
    ang = pos_ref[...].astype(F32) * invf_ref[...]
    cs = jnp.cos(ang)
    sn = jnp.sin(ang)
    coef_next = jnp.where(j < ROPE_DIM // 2, -sn, 0.0)
    coef_prev = jnp.where((j >= ROPE_DIM // 2) & (j < ROPE_DIM), sn, 0.0)

    def rope(tc):
        return (tc * cs + pltpu.roll(tc, LANES - ROPE_DIM // 2, 1) * coef_next
                + pltpu.roll(tc, ROPE_DIM // 2, 1) * coef_prev)

    def seg(off, width):
        return _dot(hb, w_ref[:, off:off + width])

    t = seg(_SEG_QA, 512)
    for c in range(4):
        qa_ref[:, c * LANES:(c + 1) * LANES] = (rope(t[:, c * LANES:(c + 1) * LANES]) * ATTN_SCALE).astype(BF16)
    t = seg(_SEG_QI, 512)
    for c in range(4):
        qi_ref[:, c * LANES:(c + 1) * LANES] = (rope(t[:, c * LANES:(c + 1) * LANES]) * IDX_SCALE).astype(BF16)
    qb_ref[...] = (seg(_SEG_QB, 512) * ATTN_SCALE).astype(BF16)
    kb_ref[...] = seg(_SEG_KB, 512).astype(BF16)
    vb_ref[...] = seg(_SEG_VB, 512).astype(BF16)
    sga_ref[...] = _sigmoid(seg(_SEG_GA, 1024)).astype(BF16)
    sgb_ref[...] = _sigmoid(seg(_SEG_GB, 1024)).astype(BF16)
    ka_ref[...] = rope(seg(_SEG_KA, LANES)).astype(BF16)
    va_ref[...] = seg(_SEG_VA, LANES).astype(BF16)
    ki_ref[...] = rope(seg(_SEG_KI, LANES)).astype(BF16)

    z = seg(_SEG_AUX, LANES)
    zf = z + fbias_ref[...]
    logf = jnp.minimum(zf, 0.0) - jnp.log(1.0 + jnp.exp(-jnp.abs(zf)))
    is_f = (lane >= N_HEADS) & (lane < 2 * N_HEADS)
    logf = jnp.where(is_f, logf, 0.0)
    rr = lax.broadcasted_iota(I32, (tm, tm), 0)
    cc = lax.broadcasted_iota(I32, (tm, tm), 1)
    tri = jnp.where(cc <= rr, 1.0, 0.0).astype(BF16)
    p_hi = logf.astype(BF16)
    r1 = logf - p_hi.astype(F32)
    p_mid = r1.astype(BF16)
    p_lo = (r1 - p_mid.astype(F32)).astype(BF16)

    @pl.when(pl.program_id(1) == 0)
    def _():
        carry_scr[...] = jnp.zeros_like(carry_scr)

    cum = _dot(tri, p_hi) + _dot(tri, p_mid) + _dot(tri, p_lo) + carry_scr[...]
    carry_scr[...] = cum[tm - 1:tm, :]
    aux_ref[...] = jnp.where(lane < N_HEADS, z * (N_HEADS ** -0.5), jnp.where(is_f, cum, 0.0))


def _inproj_call(x2, pos2, mod3, g_mix, w_in_p, invf, fbias, batch, seq):
    n = x2.shape[0]
    tpb = seq // TM_IN
    row = lambda b, s: (b * tpb + s, 0)
    bf = lambda w: jax.ShapeDtypeStruct((n, w), BF16)
    out_shapes = [bf(512), bf(512), bf(512), bf(512), bf(512), bf(1024), bf(1024),
                  bf(LANES), bf(LANES), bf(LANES), jax.ShapeDtypeStruct((n, LANES), F32)]
    out_specs = [pl.BlockSpec((TM_IN, s.shape[1]), row) for s in out_shapes]
    return pl.pallas_call(
        _inproj_kernel,
        grid=(batch, tpb),
        in_specs=[pl.BlockSpec((TM_IN, D_MODEL), row),
                  pl.BlockSpec((TM_IN, 1), row),
                  pl.BlockSpec((1, 6, D_MODEL), lambda b, s: (b, 0, 0)),
                  pl.BlockSpec((1, D_MODEL), lambda b, s: (0, 0)),
                  pl.BlockSpec((D_MODEL, _NC_IN), lambda b, s: (0, 0)),
                  pl.BlockSpec((1, LANES), lambda b, s: (0, 0)),
                  pl.BlockSpec((1, LANES), lambda b, s: (0, 0))],
        out_specs=out_specs,
        out_shape=out_shapes,
        scratch_shapes=[pltpu.VMEM((TM_IN, D_MODEL), BF16), pltpu.VMEM((1, LANES), F32)],
        compiler_params=_cparams(("arbitrary", "arbitrary")),
        name="inproj",
    )(x2, pos2, mod3, g_mix, w_in_p, invf, fbias)


def _fox_kernel(q_ref, k_ref, v_ref, aux_ref, cumt_ref, o_ref):
    tq, tk = TQ_FOX, TK_FOX
    pair = pl.program_id(1)
    qi = pl.program_id(2)
    q0 = qi * tq
    q2 = q_ref[0]
    lane = lax.broadcasted_iota(I32, (tq, LANES), 1)
    auxq = aux_ref[0]
    rows = q0 + lax.broadcasted_iota(I32, (tq, tk), 0)
    cols_local = lax.broadcasted_iota(I32, (tq, tk), 1)
    outs = []
    for hh in range(2):
        h = 2 * pair + hh
        in_half = (lane < HEAD_DIM) if hh == 0 else (lane >= HEAD_DIM)
        qh = jnp.where(in_half, q2, jnp.zeros_like(q2))
        cq = jnp.sum(jnp.where(lane == N_HEADS + h, auxq, 0.0), axis=-1, keepdims=True)

        def tile(kt, carry, masked, qh=qh, cq=cq, h=h):
            m, l, acc = carry
            k0 = pl.multiple_of(kt * tk, tk)
            kk = k_ref[0, pl.ds(k0, tk), :]
            vv = v_ref[0, pl.ds(k0, tk), :]
            ck = cumt_ref[0, pl.ds(h, 1), pl.ds(kt, 1), :].reshape(1, tk)
            s = _nt_dot(qh, kk) + cq - ck
            if masked:
                s = jnp.where(k0 + cols_local <= rows, s, NEG_INF)
            m_new = jnp.maximum(m, jnp.max(s, axis=-1, keepdims=True))
            p = jnp.exp(s - m_new)
            alpha = jnp.exp(m - m_new)
            l = alpha * l + jnp.sum(p, axis=-1, keepdims=True)
            acc = alpha * acc + _dot(p.astype(BF16), vv)
            return m_new, l, acc

        carry = (jnp.full((tq, 1), NEG_INF, F32), jnp.zeros((tq, 1), F32), jnp.zeros((tq, LANES), F32))
        n_full = qi * (tq // tk)
        carry = lax.fori_loop(0, n_full, functools.partial(tile, masked=False), carry)
        for d in range(tq // tk):
            carry = tile(n_full + d, carry, True)
        _, l, acc = carry
        outs.append(acc * (1.0 / l))
    o_ref[0] = jnp.where(lane < HEAD_DIM, outs[0], outs[1]).astype(BF16)


def _fox_call(qb3, kb3, vb3, aux3, cumt4):
    batch, seq, _ = qb3.shape
    return pl.pallas_call(
        _fox_kernel,
        grid=(batch, N_HEADS // 2, seq // TQ_FOX),
        in_specs=[pl.BlockSpec((1, TQ_FOX, LANES), lambda b, p, i: (b, i, p)),
                  pl.BlockSpec((1, seq, LANES), lambda b, p, i: (b, 0, p)),
                  pl.BlockSpec((1, seq, LANES), lambda b, p, i: (b, 0, p)),
                  pl.BlockSpec((1, TQ_FOX, LANES), lambda b, p, i: (b, i, 0)),
                  pl.BlockSpec((1, N_HEADS, seq // TK_FOX, TK_FOX), lambda b, p, i: (b, 0, 0, 0))],
        out_specs=pl.BlockSpec((1, TQ_FOX, LANES), lambda b, p, i: (b, i, p)),
        out_shape=jax.ShapeDtypeStruct((batch, seq, 512), BF16),
        compiler_params=_cparams(("arbitrary", "arbitrary", "arbitrary")),
        name="fox",
    )(qb3, kb3, vb3, aux3, cumt4)


def _dsa_kernel(qa_ref, qi_ref, aux_ref, ki_ref, ka_ref, va_ref, o_ref, aqi_scr, aqa_scr, key_scr, *, seq):
    tq, tk = TQ_DSA, TK_DSA
    i = pl.program_id(1)
    q0 = i * tq
    nk = lax.shift_right_logical(q0 + tq + tk - 1, int(np.log2(tk)))
    n_rest = (seq - nk * tk).astype(F32)
    lane = lax.broadcasted_iota(I32, (tq, LANES), 1)

    for h in range(N_HEADS):
        in_half = (lane < HEAD_DIM) if h % 2 == 0 else (lane >= HEAD_DIM)
        blk = slice((h // 2) * LANES, (h // 2 + 1) * LANES)
        qi_blk = qi_ref[0, :, blk]
        qa_blk = qa_ref[0, :, blk]
        aqi_scr[h * tq:(h + 1) * tq, :] = jnp.where(in_half, qi_blk, jnp.zeros_like(qi_blk))
        aqa_scr[h * tq:(h + 1) * tq, :] = jnp.where(in_half, qa_blk, jnp.zeros_like(qa_blk))

    auxq = aux_ref[0]
    wcols = [jnp.sum(jnp.where(lane == h, auxq, 0.0), axis=-1, keepdims=True) for h in range(N_HEADS)]
    row_chunk = lax.shift_right_logical(q0 + lax.broadcasted_iota(I32, (tq, tk), 0), CHUNK_SHIFT)
    col_local = lax.broadcasted_iota(I32, (tq, tk), 1)

    def admissible(k0):
        return lax.shift_right_logical(k0 + col_local, CHUNK_SHIFT) <= row_chunk

    def p1(kt, _):
        k0 = pl.multiple_of(kt * tk, tk)
        rel = _nt_dot(aqi_scr[...], ki_ref[0, pl.ds(k0, tk), :])
        sc = wcols[0] * jnp.maximum(rel[0:tq], 0.0)
        for h in range(1, N_HEADS):
            sc = sc + wcols[h] * jnp.maximum(rel[h * tq:(h + 1) * tq], 0.0)
        sc = jnp.where(admissible(k0), sc, NEG_INF)
        sc = jnp.where(sc == 0.0, 0.0, sc)
        b = pltpu.bitcast(sc, I32)
        key_scr[kt] = b ^ (lax.shift_right_arithmetic(b, 31) & 0x7FFFFFFF)
        return 0

    lax.fori_loop(0, nk, p1, 0)

    def count(pred_fn, cand):
        def body(kt, acc):
            ind = jnp.where(pred_fn(key_scr[kt], cand), 1.0, 0.0)
            for c in range(tk // LANES):
                acc = acc + ind[:, c * LANES:(c + 1) * LANES]
            return acc
        acc = lax.fori_loop(0, nk, body, jnp.zeros((tq, LANES), F32))
        return jnp.sum(acc, axis=-1, keepdims=True)

    ge = lambda a, b: a >= b
    gt = lambda a, b: a > b
    kf = float(TOPK)

    def count_ge(cand):
        return count(ge, cand) + jnp.where(cand <= _SENT_KEY, n_rest, 0.0)

    zero = jnp.zeros((tq, 1), I32)
    thr = jnp.where(count_ge(zero) >= kf, zero, jnp.full((tq, 1), _INT_MIN, I32))

    def bit_step(jb, thr):
        cand = thr + lax.shift_left(jnp.int32(1), 30 - jb)
        return jnp.where(count_ge(cand) >= kf, cand, thr)

    thr = lax.fori_loop(0, 31, bit_step, thr)
    n_gt = count(gt, thr) + jnp.where(thr < _SENT_KEY, n_rest, 0.0)
    need = kf - n_gt

    tr = lax.broadcasted_iota(I32, (tk, tk), 0)
    tc = lax.broadcasted_iota(I32, (tk, tk), 1)
    tri = jnp.where(tr <= tc, 1.0, 0.0).astype(BF16)

    def p3(kt, carry):
        ms, ls, accs, tie = carry
        k0 = pl.multiple_of(kt * tk, tk)
        kk = key_scr[kt]
        eq = kk == thr
        pref = _dot(jnp.where(eq, 1.0, 0.0).astype(BF16), tri) + tie
        sel = (kk > thr) | (eq & (pref <= need))
        bias = jnp.where(sel & admissible(k0), 0.0, NEG_INF)
        tie = pref[:, tk - 1:tk]
        logit = _nt_dot(aqa_scr[...], ka_ref[0, pl.ds(k0, tk), :])
        new_ms, new_ls, alphas, ps = [], [], [], []
        for h in range(N_HEADS):
            s = logit[h * tq:(h + 1) * tq] + bias
            m_new = jnp.maximum(ms[h], jnp.max(s, axis=-1, keepdims=True))
            p = jnp.exp(s - m_new)
            alpha = jnp.exp(ms[h] - m_new)
            new_ms.append(m_new)
            new_ls.append(alpha * ls[h] + jnp.sum(p, axis=-1, keepdims=True))
            alphas.append(alpha)
            ps.append(p.astype(BF16))
        pv = _dot(jnp.concatenate(ps, axis=0), va_ref[0, pl.ds(k0, tk), :])
        new_accs = [alphas[h] * accs[h] + pv[h * tq:(h + 1) * tq] for h in range(N_HEADS)]
        return new_ms, new_ls, new_accs, tie

    carry = ([jnp.full((tq, 1), NEG_INF, F32) for _ in range(N_HEADS)],
             [jnp.zeros((tq, 1), F32) for _ in range(N_HEADS)],
             [jnp.zeros((tq, LANES), F32) for _ in range(N_HEADS)],
             jnp.zeros((tq, 1), F32))
    _, ls, accs, _ = lax.fori_loop(0, nk, p3, carry)
    for p in range(N_HEADS // 2):
        o0 = accs[2 * p] * (1.0 / ls[2 * p])
        o1 = accs[2 * p + 1] * (1.0 / ls[2 * p + 1])
        o_ref[0, :, p * LANES:(p + 1) * LANES] = jnp.where(lane < HEAD_DIM, o0, o1).astype(BF16)


def _dsa_call(qa3, qi3, aux3, ki3, ka3, va3):
    batch, seq, _ = qa3.shape
    qspec = lambda w: pl.BlockSpec((1, TQ_DSA, w), lambda b, i: (b, i, 0))
    kspec = pl.BlockSpec((1, seq, LANES), lambda b, i: (b, 0, 0))
    return pl.pallas_call(
        functools.partial(_dsa_kernel, seq=seq),
        grid=(batch, seq // TQ_DSA),
        in_specs=[qspec(512), qspec(512), qspec(LANES), kspec, kspec, kspec],
        out_specs=qspec(512),
        out_shape=jax.ShapeDtypeStruct((batch, seq, 512), BF16),
        scratch_shapes=[pltpu.VMEM((N_HEADS * TQ_DSA, LANES), BF16),
                        pltpu.VMEM((N_HEADS * TQ_DSA, LANES), BF16),
                        pltpu.VMEM((seq // TK_DSA, TQ_DSA, TK_DSA), I32)],
        compiler_params=_cparams(("arbitrary", "arbitrary")),
        name="dsa",
    )(qa3, qi3, aux3, ki3, ka3, va3)


def _outproj_kernel(oa_ref, ob_ref, sga_ref, sgb_ref, x_ref, mod_ref, gffn_ref,
                    woa_ref, wob_ref, wo_ref, wr_hi_ref, wr_lo_ref, br_ref,
                    x1_ref, h2_ref, lg_ref):
    tm = TM_OUT
    ya = _dot(oa_ref[...], woa_ref[...])
    yb = _dot(ob_ref[...], wob_ref[...])
    merged = (sga_ref[...].astype(F32) * ya + sgb_ref[...].astype(F32) * yb).astype(BF16)
    mix = _dot(merged, wo_ref[...])
    gt1 = mod_ref[0, 2:3, :]
    sh2 = mod_ref[0, 3:4, :]
    sc2 = mod_ref[0, 4:5, :]
    x1 = x_ref[...] + gt1 * mix
    x1_ref[...] = x1
    var = jnp.mean(x1 * x1, axis=-1, keepdims=True)
    h2 = x1 * lax.rsqrt(var + RMS_EPS) * gffn_ref[...] * (1.0 + sc2) + sh2
    for c in range(D_MODEL // LANES):
        h2_ref[pl.ds(c, tm, stride=SUBLANES), :] = h2[:, c * LANES:(c + 1) * LANES]
    hi = h2.astype(BF16)
    lo = (h2 - hi.astype(F32)).astype(BF16)
    lg_ref[...] = (_dot(hi, wr_hi_ref[...]) + _dot(hi, wr_lo_ref[...]) + _dot(lo, wr_hi_ref[...])
                   + br_ref[...])


def _outproj_call(oa2, ob2, sga, sgb, x2, mod3, g_ffn, woa, wob, wo, wr_hi, wr_lo, br, seq):
    n = x2.shape[0]
    tpb = seq // TM_OUT
    row = lambda w: pl.BlockSpec((TM_OUT, w), lambda i: (i, 0))
    full = lambda a: pl.BlockSpec(a.shape, lambda i: (0,) * a.ndim)
    return pl.pallas_call(
        _outproj_kernel,
        grid=(n // TM_OUT,),
        in_specs=[row(512), row(512), row(1024), row(1024), row(D_MODEL),
                  pl.BlockSpec((1, 6, D_MODEL), lambda i: (i // tpb, 0, 0)),
                  full(g_ffn), full(woa), full(wob), full(wo), full(wr_hi), full(wr_lo), full(br)],
        out_specs=[row(D_MODEL),
                   pl.BlockSpec((TM_OUT * SUBLANES, LANES), lambda i: (i, 0)),
                   row(LANES)],
        out_shape=[jax.ShapeDtypeStruct((n, D_MODEL), F32),
                   jax.ShapeDtypeStruct((n * SUBLANES, LANES), F32),
                   jax.ShapeDtypeStruct((n, LANES), F32)],
        compiler_params=_cparams(("arbitrary",)),
        name="outproj",
    )(oa2, ob2, sga, sgb, x2, mod3, g_ffn, woa, wob, wo, wr_hi, wr_lo, br)


def _route_kernel(lg_ref, rt_ref):
    tm = TM_ROUTE
    lg = lg_ref[...]
    lane = lax.broadcasted_iota(I32, (tm, LANES), 1)
    big = jnp.int32(LANES)
    ninf = -jnp.inf
    gmask = (lane >= N_EXPERTS) & (lane < N_EXPERTS + N_GROUPS)
    g = jnp.where(gmask, lg, ninf)
    gmax = jnp.max(g, axis=-1, keepdims=True)
    grp = jnp.min(jnp.where(g == gmax, lane - N_EXPERTS, big), axis=-1, keepdims=True)
    p_grp = 1.0 / jnp.sum(jnp.where(gmask, jnp.exp(lg - gmax), 0.0), axis=-1, keepdims=True)
    lo = grp * EXPERTS_PER_GROUP
    emask = (lane >= lo) & (lane < lo + EXPERTS_PER_GROUP)
    ev = jnp.where(emask, lg, ninf)
    v0 = jnp.max(ev, axis=-1, keepdims=True)
    i0 = jnp.min(jnp.where(emask & (ev == v0), lane, big), axis=-1, keepdims=True)
    rest = emask & (lane != i0)
    ev1 = jnp.where(rest, lg, ninf)
    v1 = jnp.max(ev1, axis=-1, keepdims=True)
    i1 = jnp.min(jnp.where(rest & (ev1 == v1), lane, big), axis=-1, keepdims=True)
    e1 = jnp.exp(v1 - v0)
    w0 = p_grp / (1.0 + e1)
    w1 = p_grp * e1 / (1.0 + e1)
    rt_ref[...] = jnp.where(lane == 0, i0.astype(F32),
                            jnp.where(lane == 1, i1.astype(F32),
                                      jnp.where(lane == 2, w0, jnp.where(lane == 3, w1, 0.0))))


def _route_call(lg):
    n = lg.shape[0]
    spec = pl.BlockSpec((TM_ROUTE, LANES), lambda i: (i, 0))
    return pl.pallas_call(
        _route_kernel, grid=(n // TM_ROUTE,), in_specs=[spec], out_specs=spec,
        out_shape=jax.ShapeDtypeStruct((n, LANES), F32),
        compiler_params=_cparams(("arbitrary",)), name="route",
    )(lg)


def _slab_copy(src, src_row, dst, dst_row, sem):
    return pltpu.make_async_copy(src.at[pl.ds(pl.multiple_of(src_row * SUBLANES, SUBLANES), SUBLANES)],
                                 dst.at[pl.ds(pl.multiple_of(dst_row * SUBLANES, SUBLANES), SUBLANES)], sem)


def _dispatch_kernel(dest_ref, h2_ref, xs_in_hbm, xs_hbm, sem):
    del xs_in_hbm
    tm = TM_DISPATCH
    base = pl.program_id(0) * tm

    def issue(r, _):
        for k in range(2):
            _slab_copy(h2_ref, r, xs_hbm, dest_ref[2 * (base + r) + k], sem).start()
        return 0

    lax.fori_loop(0, tm, issue, 0)
    for _ in range(2):
        pltpu.make_async_copy(h2_ref, xs_hbm.at[pl.ds(0, tm * SUBLANES)], sem).wait()


def _dispatch_call(dest, h2s, xs_zero):
    n = h2s.shape[0] // SUBLANES
    anyspec = pl.BlockSpec(memory_space=pl.ANY)
    return pl.pallas_call(
        _dispatch_kernel,
        grid_spec=pltpu.PrefetchScalarGridSpec(
            num_scalar_prefetch=1, grid=(n // TM_DISPATCH,),
            in_specs=[pl.BlockSpec((TM_DISPATCH * SUBLANES, LANES), lambda i, d: (i, 0)), anyspec],
            out_specs=anyspec,
            scratch_shapes=[pltpu.SemaphoreType.DMA(())]),
        out_shape=jax.ShapeDtypeStruct(xs_zero.shape, F32),
        input_output_aliases={2: 0},
        compiler_params=_cparams(("arbitrary",)),
        name="dispatch",
    )(dest, h2s, xs_zero)


def _experts_kernel(be_ref, nused_ref, xs_ref, wg_ref, wu_ref, wd_ref, yb_ref, x_scr, wg_scr, wu_scr, wd_scr):
    i = pl.program_id(0)

    @pl.when(i >= nused_ref[0])
    def _():
        yb_ref[...] = jnp.zeros_like(yb_ref)

    @pl.when(i < nused_ref[0])
    def _():
        prev = be_ref[jnp.maximum(i - 1, 0)]

        @pl.when((i == 0) | (be_ref[i] != prev))
        def _():
            wg_scr[...] = wg_ref[0].astype(BF16)
            wu_scr[...] = wu_ref[0].astype(BF16)
            wd_scr[...] = wd_ref[0].astype(BF16)

        for c in range(D_MODEL // LANES):
            x_scr[:, c * LANES:(c + 1) * LANES] = xs_ref[pl.ds(c, BLK_E, stride=SUBLANES), :].astype(BF16)
        xb = x_scr[...]
        g = _dot(xb, wg_scr[...])
        u = _dot(xb, wu_scr[...])
        hid = (g * _sigmoid(g) * u).astype(BF16)
        y = _dot(hid, wd_scr[...])
        for c in range(D_MODEL // LANES):
            yb_ref[pl.ds(c, BLK_E, stride=SUBLANES), :] = y[:, c * LANES:(c + 1) * LANES]


def _experts_call(block_e, n_used, xs, w_gate, w_up, w_down):
    n_blocks = block_e.shape[0]
    slab = pl.BlockSpec((BLK_E * SUBLANES, LANES), lambda i, be, nu: (i, 0))
    return pl.pallas_call(
        _experts_kernel,
        grid_spec=pltpu.PrefetchScalarGridSpec(
            num_scalar_prefetch=2, grid=(n_blocks,),
            in_specs=[slab,
                      pl.BlockSpec((1, D_MODEL, D_EXPERT), lambda i, be, nu: (be[i], 0, 0)),
                      pl.BlockSpec((1, D_MODEL, D_EXPERT), lambda i, be, nu: (be[i], 0, 0)),
                      pl.BlockSpec((1, D_EXPERT, D_MODEL), lambda i, be, nu: (be[i], 0, 0))],
            out_specs=slab,
            scratch_shapes=[pltpu.VMEM((BLK_E, D_MODEL), BF16),
                            pltpu.VMEM((D_MODEL, D_EXPERT), BF16),
                            pltpu.VMEM((D_MODEL, D_EXPERT), BF16),
                            pltpu.VMEM((D_EXPERT, D_MODEL), BF16)]),
        out_shape=jax.ShapeDtypeStruct(xs.shape, F32),
        compiler_params=_cparams(("arbitrary",)),
        name="experts",
    )(block_e, n_used, xs, w_gate, w_up, w_down)


def _final_kernel(dest_ref, x1_ref, rt_ref, mod_ref, gfin_ref, yb_hbm, o_ref, g_scr, sem):
    tm = TM_FINAL
    base = pl.program_id(0) * tm

    def issue(r, _):
        tok = base + r
        for k in range(2):
            _slab_copy(yb_hbm, dest_ref[2 * tok + k], g_scr, k * tm + r, sem).start()
        return 0

    lax.fori_loop(0, tm, issue, 0)
    pltpu.make_async_copy(yb_hbm.at[pl.ds(0, 2 * tm * SUBLANES)], g_scr, sem).wait()

    rt = rt_ref[...]
    lane = lax.broadcasted_iota(I32, (tm, LANES), 1)
    gw0 = jnp.sum(jnp.where(lane == 2, rt, 0.0), axis=-1, keepdims=True)
    gw1 = jnp.sum(jnp.where(lane == 3, rt, 0.0), axis=-1, keepdims=True)
    gt2 = mod_ref[0, 5:6, :]
    x1 = x1_ref[...]
    cols = []
    for c in range(D_MODEL // LANES):
        y0 = g_scr[pl.ds(c, tm, stride=SUBLANES), :]
        y1 = g_scr[pl.ds(tm * SUBLANES + c, tm, stride=SUBLANES), :]
        y = gw0 * y0 + gw1 * y1
        cols.append(x1[:, c * LANES:(c + 1) * LANES] + gt2[:, c * LANES:(c + 1) * LANES] * y)
    x2 = jnp.concatenate(cols, axis=1)
    var = jnp.mean(x2 * x2, axis=-1, keepdims=True)
    o_ref[...] = x2 * lax.rsqrt(var + RMS_EPS) * gfin_ref[...]


def _final_call(dest, x1, rt, mod3, g_final, yb, seq):
    n = x1.shape[0]
    tpb = seq // TM_FINAL
    return pl.pallas_call(
        _final_kernel,
        grid_spec=pltpu.PrefetchScalarGridSpec(
            num_scalar_prefetch=1, grid=(n // TM_FINAL,),
            in_specs=[pl.BlockSpec((TM_FINAL, D_MODEL), lambda i, d: (i, 0)),
                      pl.BlockSpec((TM_FINAL, LANES), lambda i, d: (i, 0)),
                      pl.BlockSpec((1, 6, D_MODEL), lambda i, d: (i // tpb, 0, 0)),
                      pl.BlockSpec((1, D_MODEL), lambda i, d: (0, 0)),
                      pl.BlockSpec(memory_space=pl.ANY)],
            out_specs=pl.BlockSpec((TM_FINAL, D_MODEL), lambda i, d: (i, 0)),
            scratch_shapes=[pltpu.VMEM((2 * TM_FINAL * SUBLANES, LANES), F32),
                            pltpu.SemaphoreType.DMA(())]),
        out_shape=jax.ShapeDtypeStruct((n, D_MODEL), F32),
        compiler_params=_cparams(("arbitrary",)),
        name="final",
    )(dest, x1, rt, mod3, g_final, yb)


def _permute_w_in(w):
    o = np.cumsum([0, 512, 64, 64, 512, 64, 8, 512, 512, 512, 8, 1024, 1024])
    qa, ka, va, qi, ki, wi, qb, kb, vb, fb, ga, gb = [w[:, o[k]:o[k + 1]] for k in range(12)]
    aux = jnp.concatenate([wi, fb, jnp.zeros((w.shape[0], LANES - 2 * N_HEADS), w.dtype)], axis=1)
    return jnp.concatenate([qa, qi, qb, kb, vb, ga, gb, ka, ka, va, va, ki, ki, aux], axis=1).astype(BF16)


def _layer(x2, pos2, mod3, batch, seq, g_mix, w_in, b_forget, w_out_a, w_out_b, w_out, g_ffn,
           w_group, b_group, w_router, b_router, w_e_gate, w_e_up, w_e_down, g_final):
    n = x2.shape[0]
    lane = np.arange(LANES)
    jj = lane % HEAD_DIM
    invf_np = np.where(jj < ROPE_DIM, ROPE_THETA ** (-(2.0 * (jj % (ROPE_DIM // 2))) / ROPE_DIM), 0.0)
    invf = jnp.asarray(invf_np[None, :], F32)
    fbias = jnp.zeros((1, LANES), F32).at[0, N_HEADS:2 * N_HEADS].set(b_forget.astype(F32))

    (qa, qi, qb, kb, vb, sga, sgb, ka2, va2, ki2, aux) = _inproj_call(
        x2, pos2, mod3, g_mix.reshape(1, D_MODEL), _permute_w_in(w_in), invf, fbias, batch, seq)

    r3 = lambda a: a.reshape(batch, seq, a.shape[-1])
    aux3 = r3(aux)
    cumt4 = jnp.swapaxes(aux3[:, :, N_HEADS:2 * N_HEADS], 1, 2).reshape(batch, N_HEADS, seq // TK_FOX, TK_FOX)
    ob = _fox_call(r3(qb), r3(kb), r3(vb), aux3, cumt4)
    oa = _dsa_call(r3(qa), r3(qi), aux3, r3(ki2), r3(ka2), r3(va2))

    w_rt = jnp.concatenate([w_router, w_group, jnp.zeros((D_MODEL, LANES - N_EXPERTS - N_GROUPS), F32)], axis=1)
    wr_hi = w_rt.astype(BF16)
    wr_lo = (w_rt - wr_hi.astype(F32)).astype(BF16)
    br = jnp.concatenate([b_router, b_group, jnp.zeros((LANES - N_EXPERTS - N_GROUPS,), F32)])[None, :]
    x1, h2s, lg = _outproj_call(oa.reshape(n, 512), ob.reshape(n, 512), sga, sgb, x2, mod3,
                                g_ffn.reshape(1, D_MODEL), w_out_a.astype(BF16), w_out_b.astype(BF16),
                                w_out.astype(BF16), wr_hi, wr_lo, br, seq)
    rt = _route_call(lg)

    flat_e = rt[:, 0:2].astype(I32).reshape(-1)
    onehot = (flat_e[:, None] == jnp.arange(N_EXPERTS, dtype=I32)[None, :]).astype(I32)
    csum = jnp.cumsum(onehot, axis=0)
    counts = csum[-1]
    rank = jnp.take_along_axis(csum, flat_e[:, None], axis=1)[:, 0] - 1
    padded = (counts + BLK_E - 1) // BLK_E * BLK_E
    pend = jnp.cumsum(padded)
    pstart = pend - padded
    dest = (pstart[flat_e] + rank).astype(I32)
    n_slots = 2 * n + N_EXPERTS * BLK_E
    n_blocks = n_slots // BLK_E
    block_e = jnp.minimum(jnp.searchsorted(pend, jnp.arange(n_blocks, dtype=I32) * BLK_E, side='right'),
                          N_EXPERTS - 1).astype(I32)
    n_used = (pend[-1:] // BLK_E).astype(I32)

    xs = _dispatch_call(dest, h2s, jnp.zeros((n_slots * SUBLANES, LANES), F32))
    yb = _experts_call(block_e, n_used, xs, w_e_gate, w_e_up, w_e_down)
    return _final_call(dest, x1, rt, mod3, g_final.reshape(1, D_MODEL), yb, seq)


def kernel(x, c, positions, w_mod, b_mod, g_mix, w_in, b_forget, w_out_a, w_out_b, w_out, g_ffn, w_group,
           b_group, w_router, b_router, w_e_gate, w_e_up, w_e_down, g_final):
    batch, seq, d = x.shape
    depth = w_mod.shape[0]
    assert depth == 1 and d == D_MODEL, "kernel fuses the final norm into the single layer"
    n = batch * seq
    c8 = jnp.zeros((8, d), F32).at[:batch].set(c)
    mod = _mod_call(c8, w_mod[0], b_mod[0][None, :])
    mod3 = mod[:batch].reshape(batch, 6, d)
    out = _layer(x.reshape(n, d), positions.reshape(n, 1), mod3, batch, seq, g_mix[0], w_in[0], b_forget[0],
                 w_out_a[0], w_out_b[0], w_out[0], g_ffn[0], w_group[0], b_group[0], w_router[0], b_router[0],
                 w_e_gate[0], w_e_up[0], w_e_down[0], g_final)
    return out.reshape(batch, seq, d)
```

```python
import functools

import numpy as np
import jax
import jax.numpy as jnp
from jax import lax
from jax.experimental import pallas as pl
from jax.experimental.pallas import tpu as pltpu

F32 = jnp.float32
BF16 = jnp.bfloat16
I32 = jnp.int32

D_MODEL = 1024
HEAD_DIM = 64
N_HEADS = 8
CHUNK_SHIFT = 6
TOPK = 256
ROPE_DIM = 16
ROPE_THETA = 500000.0
N_GROUPS = 4
EXPERTS_PER_GROUP = 8
N_EXPERTS = 32
D_EXPERT = 512
RMS_EPS = 1e-6
NEG_INF = -1e30
ATTN_SCALE = HEAD_DIM ** -0.5
IDX_SCALE = HEAD_DIM ** -0.5
LOG2E = 1.4426950408889634

LANES = 128
SUBLANES = 8
VMEM_LIMIT = 56 * 1024 * 1024

TM_IN = 512
TQ_FOX = 512
TK_FOX = 512
TQ_DSA = 256
TK_DSA = 512
TM_OUT = 512
TM_ROUTE = 1024
TM_DISPATCH = 512
BLK_E = 256
TM_FINAL = 256

RADIX_BLOCK = 32 * SUBLANES
FOX_K = 256
FOX_AUG_ROWS = 16

_SEG_QA, _SEG_QI, _SEG_QB, _SEG_KB, _SEG_VB = 0, 512, 1024, 1536, 2048
_SEG_GA, _SEG_GB = 2560, 3584
_SEG_KA, _SEG_VA, _SEG_KI, _SEG_AUX = 4608, 4736, 4864, 4992
_NC_IN = 5120

_SENT_KEY = int(np.array(NEG_INF, np.float32).view(np.int32) ^ 0x7FFFFFFF)
_INT_MIN = -(2 ** 31)


def _cparams(sem):
    return pltpu.CompilerParams(dimension_semantics=sem, vmem_limit_bytes=VMEM_LIMIT)


def _dot(a, b):
    return jnp.dot(a, b, preferred_element_type=F32)


def _sigmoid(x):
    return 1.0 / (1.0 + jnp.exp(-x))


def _mod_kernel(c_ref, w_ref, b_ref, o_ref):
    c = c_ref[...]
    ca = c * _sigmoid(c)
    o_ref[...] = _dot(ca.astype(BF16), w_ref[...].astype(BF16)) + b_ref[...]


def _mod_call(c8, w_mod, b_mod):
    n_out = w_mod.shape[1]
    tn = 1024
    return pl.pallas_call(
        _mod_kernel,
        grid=(n_out // tn,),
        in_specs=[pl.BlockSpec((8, D_MODEL), lambda j: (0, 0)),
                  pl.BlockSpec((D_MODEL, tn), lambda j: (0, j)),
                  pl.BlockSpec((1, tn), lambda j: (0, j))],
        out_specs=pl.BlockSpec((8, tn), lambda j: (0, j)),
        out_shape=jax.ShapeDtypeStruct((8, n_out), F32),
        compiler_params=_cparams(("arbitrary",)),
        name="mod",
    )(c8, w_mod, b_mod)


def _inproj_kernel(x_ref, pos_ref, mod_ref, g_ref, w_ref, invf_ref, fbias_ref, sel_ref, ones_ref,
                   qa_ref, qi_ref, qb_ref, kb_ref, vb_ref, sga_ref, sgb_ref,
                   ka_ref, va_ref, ki_ref, aux_ref, h_scr, carry_scr):
    tm = TM_IN
    x = x_ref[...]
    var = jnp.mean(x * x, axis=-1, keepdims=True)
    tf = x * lax.rsqrt(var + RMS_EPS) * g_ref[...]
    sh = mod_ref[0, 0:1, :]
    sc = mod_ref[0, 1:2, :]
    h_scr[...] = (tf * (1.0 + sc) + sh).astype(BF16)
    hb = h_scr[...]

    lane = lax.broadcasted_iota(I32, (tm, LANES), 1)
    j = lane & (HEAD_DIM - 1)
    ang = pos_ref[...].astype(F32) * invf_ref[...]
    cs = jnp.cos(ang)
    sn = jnp.sin(ang)
    coef_next = jnp.where(j < ROPE_DIM // 2, -sn, 0.0)
    coef_prev = jnp.where((j >= ROPE_DIM // 2) & (j < ROPE_DIM), sn, 0.0)

    def rope(tc):
        return (tc * cs + pltpu.roll(tc, LANES - ROPE_DIM // 2, 1) * coef_next
                + pltpu.roll(tc, ROPE_DIM // 2, 1) * coef_prev)

    def seg(off, width):
        return _dot(hb, w_ref[:, off:off + width])

    t = seg(_SEG_QA, 512)
    for c in range(4):
        qa_ref[:, c * LANES:(c + 1) * LANES] = (
            rope(t[:, c * LANES:(c + 1) * LANES]) * (ATTN_SCALE * LOG2E)).astype(BF16)
    t = seg(_SEG_QI, 512)
    for c in range(4):
        qi_ref[:, c * LANES:(c + 1) * LANES] = (rope(t[:, c * LANES:(c + 1) * LANES]) * IDX_SCALE).astype(BF16)
    qb_ref[...] = (seg(_SEG_QB, 512) * (ATTN_SCALE * LOG2E)).astype(BF16)
    vb_ref[...] = seg(_SEG_VB, 512).astype(BF16)
    sga_ref[...] = _sigmoid(seg(_SEG_GA, 1024)).astype(BF16)
    sgb_ref[...] = _sigmoid(seg(_SEG_GB, 1024)).astype(BF16)
    ka_ref[...] = rope(seg(_SEG_KA, LANES)).astype(BF16)
    va_ref[...] = seg(_SEG_VA, LANES).astype(BF16)
    ki_ref[...] = rope(seg(_SEG_KI, LANES)).astype(BF16)

    z = seg(_SEG_AUX, LANES)
    zf = z + fbias_ref[...]
    logf = jnp.minimum(zf, 0.0) - jnp.log(1.0 + jnp.exp(-jnp.abs(zf)))
    is_f = (lane >= N_HEADS) & (lane < 2 * N_HEADS)
    logf = jnp.where(is_f, logf, 0.0)

    def split3(v):
        hi = v.astype(BF16)
        r1 = v - hi.astype(F32)
        mid = r1.astype(BF16)
        return hi, mid, (r1 - mid.astype(F32)).astype(BF16)

    rr = lax.broadcasted_iota(I32, (tm, tm), 0)
    cc = lax.broadcasted_iota(I32, (tm, tm), 1)
    tri = jnp.where(cc <= rr, 1.0, 0.0).astype(BF16)
    p_hi, p_mid, p_lo = split3(logf)

    @pl.when(pl.program_id(1) == 0)
    def _():
        carry_scr[...] = jnp.zeros_like(carry_scr)

    cum = _dot(tri, p_hi) + _dot(tri, p_mid) + _dot(tri, p_lo) + carry_scr[...]
    carry_scr[...] = cum[tm - 1:tm, :]
    cum2 = cum * LOG2E
    aux_ref[...] = jnp.where(lane < N_HEADS, z * (N_HEADS ** -0.5), jnp.where(is_f, cum2, 0.0))

    c_hi, c_mid, c_lo = split3(cum2)
    kaug = _dot(c_hi, sel_ref[0]) + _dot(c_mid, sel_ref[1]) + _dot(c_lo, sel_ref[2]) + ones_ref[...]
    kb = seg(_SEG_KB, 512)
    for p in range(N_HEADS // 2):
        kb_ref[:, p * FOX_K:p * FOX_K + LANES] = kb[:, p * LANES:(p + 1) * LANES].astype(BF16)
        kb_ref[:, p * FOX_K + LANES:(p + 1) * FOX_K] = kaug[:, p * LANES:(p + 1) * LANES].astype(BF16)


def _fox_routing_constants():
    sel = np.zeros((3, LANES, (N_HEADS // 2) * LANES), np.float32)
    ones = np.zeros((1, (N_HEADS // 2) * LANES), np.float32)
    for h in range(N_HEADS):
        for t in range(3):
            sel[t, N_HEADS + h, (h // 2) * LANES + 3 + 3 * (h % 2) + t] = -1.0
    for p in range(N_HEADS // 2):
        ones[0, p * LANES:p * LANES + 3] = 1.0
    return jnp.asarray(sel, BF16), jnp.asarray(ones, F32)


def _inproj_call(x2, pos2, mod3, g_mix, w_in_p, invf, fbias, batch, seq):
    n = x2.shape[0]
    tpb = seq // TM_IN
    row = lambda b, s: (b * tpb + s, 0)
    bf = lambda w: jax.ShapeDtypeStruct((n, w), BF16)
    sel, ones = _fox_routing_constants()
    out_shapes = [bf(512), bf(512), bf(512), bf((N_HEADS // 2) * FOX_K), bf(512), bf(1024), bf(1024),
                  bf(LANES), bf(LANES), bf(LANES), jax.ShapeDtypeStruct((n, LANES), F32)]
    out_specs = [pl.BlockSpec((TM_IN, s.shape[1]), row) for s in out_shapes]
    return pl.pallas_call(
        _inproj_kernel,
        grid=(batch, tpb),
        in_specs=[pl.BlockSpec((TM_IN, D_MODEL), row),
                  pl.BlockSpec((TM_IN, 1), row),
                  pl.BlockSpec((1, 6, D_MODEL), lambda b, s: (b, 0, 0)),
                  pl.BlockSpec((1, D_MODEL), lambda b, s: (0, 0)),
                  pl.BlockSpec((D_MODEL, _NC_IN), lambda b, s: (0, 0)),
                  pl.BlockSpec((1, LANES), lambda b, s: (0, 0)),
                  pl.BlockSpec((1, LANES), lambda b, s: (0, 0)),
                  pl.BlockSpec(sel.shape, lambda b, s: (0, 0, 0)),
                  pl.BlockSpec(ones.shape, lambda b, s: (0, 0))],
        out_specs=out_specs,
        out_shape=out_shapes,
        scratch_shapes=[pltpu.VMEM((TM_IN, D_MODEL), BF16), pltpu.VMEM((1, LANES), F32)],
        compiler_params=_cparams(("arbitrary", "arbitrary")),
        name="inproj",
    )(x2, pos2, mod3, g_mix, w_in_p, invf, fbias, sel, ones)


def _softmax_max(s, m_ref):
    m_ref[...] = jnp.maximum(m_ref[...], jnp.max(s, axis=0, keepdims=True))


def _softmax_accumulate(s, m_ref, l_ref, acc_ref, v_t):
    p = jnp.exp2(s - m_ref[...])
    l_ref[...] = l_ref[...] + jnp.sum(p, axis=0, keepdims=True)
    acc_ref[...] = acc_ref[...] + _dot(v_t, p.astype(BF16))


def _softmax_scratch(tq):
    return ([pltpu.VMEM((1, tq), F32) for _ in range(2 * N_HEADS)]
            + [pltpu.VMEM((HEAD_DIM, tq), F32) for _ in range(N_HEADS)])


def _softmax_split(refs):
    return refs[:N_HEADS], refs[N_HEADS:2 * N_HEADS], refs[2 * N_HEADS:3 * N_HEADS]


def _softmax_init(m_refs, l_refs, acc_refs):
    for h in range(N_HEADS):
        m_refs[h][...] = jnp.full(m_refs[h].shape, NEG_INF, F32)
        l_refs[h][...] = jnp.zeros(l_refs[h].shape, F32)
        acc_refs[h][...] = jnp.zeros(acc_refs[h].shape, F32)


def _softmax_finish(o_ref, l_refs, acc_refs):
    for h in range(N_HEADS):
        o_ref[0, h * HEAD_DIM:(h + 1) * HEAD_DIM, :] = (acc_refs[h][...] * (1.0 / l_refs[h][...])).astype(BF16)


def _fox_kernel(qt_ref, cumt_ref, kaug_ref, vt_ref, o_ref, rhs_scr, *softmax_refs):
    m_scr, l_scr, acc_scr = _softmax_split(softmax_refs)
    tq, tk = TQ_FOX, TK_FOX
    i = pl.program_id(1)
    q0 = i * tq

    cq = cumt_ref[0]
    c_hi = cq.astype(BF16).astype(F32)
    c_r = cq - c_hi
    c_mid = c_r.astype(BF16).astype(F32)
    c_lo = c_r - c_mid
    row_q = lax.broadcasted_iota(I32, (LANES, tq), 0)
    row_a = lax.broadcasted_iota(I32, (FOX_AUG_ROWS, 2 * tq), 0)
    second = lax.broadcasted_iota(I32, (FOX_AUG_ROWS, 2 * tq), 1) >= tq
    for p in range(N_HEADS // 2):
        qp = qt_ref[0, p * LANES:(p + 1) * LANES, :]
        zq = jnp.zeros_like(qp)
        rhs_scr[p, 0:LANES, 0:tq] = jnp.where(row_q < HEAD_DIM, qp, zq)
        rhs_scr[p, 0:LANES, tq:2 * tq] = jnp.where(row_q >= HEAD_DIM, qp, zq)
        pair_row = lambda a: jnp.concatenate([a[2 * p:2 * p + 1, :], a[2 * p + 1:2 * p + 2, :]], axis=1)
        aug = jnp.where(row_a == 0, pair_row(c_hi),
              jnp.where(row_a == 1, pair_row(c_mid),
              jnp.where(row_a == 2, pair_row(c_lo),
              jnp.where((row_a >= 3) & (row_a < 6), jnp.where(second, 0.0, 1.0),
              jnp.where((row_a >= 6) & (row_a < 9), jnp.where(second, 1.0, 0.0), 0.0)))))
        rhs_scr[p, LANES:LANES + FOX_AUG_ROWS, :] = aug.astype(BF16)
        rhs_scr[p, LANES + FOX_AUG_ROWS:FOX_K, :] = jnp.zeros((FOX_K - LANES - FOX_AUG_ROWS, 2 * tq), BF16)

    _softmax_init(m_scr, l_scr, acc_scr)
    key_j = lax.broadcasted_iota(I32, (tk, tq), 0)
    qry_i = q0 + lax.broadcasted_iota(I32, (tk, tq), 1)

    def tile(kt, masked, second_pass):
        k0 = pl.multiple_of(kt * tk, tk)
        for p in range(N_HEADS // 2):
            st = _dot(kaug_ref[0, pl.ds(k0, tk), p * FOX_K:(p + 1) * FOX_K], rhs_scr[p])
            for hh in range(2):
                h = 2 * p + hh
                s = st[:, hh * tq:(hh + 1) * tq]
                if masked:
                    s = jnp.where(k0 + key_j <= qry_i, s, NEG_INF)
                if second_pass:
                    _softmax_accumulate(s, m_scr[h], l_scr[h], acc_scr[h],
                                        vt_ref[0, kt, h * HEAD_DIM:(h + 1) * HEAD_DIM, :])
                else:
                    _softmax_max(s, m_scr[h])

    n_full = i * (tq // tk)
    for second_pass in (False, True):
        def full_tile(kt, carry, second_pass=second_pass):
            tile(kt, False, second_pass)
            return carry

        lax.fori_loop(0, n_full, full_tile, 0)
        for d in range(tq // tk):
            tile(n_full + d, True, second_pass)
    _softmax_finish(o_ref, l_scr, acc_scr)


def _fox_call(qbt, cumt, kaug3, vbt4):
    batch, _, seq = qbt.shape
    return pl.pallas_call(
        _fox_kernel,
        grid=(batch, seq // TQ_FOX),
        in_specs=[pl.BlockSpec((1, 512, TQ_FOX), lambda b, i: (b, 0, i)),
                  pl.BlockSpec((1, N_HEADS, TQ_FOX), lambda b, i: (b, 0, i)),
                  pl.BlockSpec((1, seq, (N_HEADS // 2) * FOX_K), lambda b, i: (b, 0, 0)),
                  pl.BlockSpec((1, seq // TK_FOX, 512, TK_FOX), lambda b, i: (b, 0, 0, 0))],
        out_specs=pl.BlockSpec((1, 512, TQ_FOX), lambda b, i: (b, 0, i)),
        out_shape=jax.ShapeDtypeStruct((batch, 512, seq), BF16),
        scratch_shapes=[pltpu.VMEM((N_HEADS // 2, FOX_K, 2 * TQ_FOX), BF16)] + _softmax_scratch(TQ_FOX),
        compiler_params=_cparams(("arbitrary", "arbitrary")),
        name="fox",
    )(qbt, cumt, kaug3, vbt4)


def _bit_transpose32(words):
    a = list(words)
    j, mask = 16, 0x0000FFFF
    while j:
        mask_i = int(np.array(mask, np.uint32).view(np.int32))
        k = 0
        while k < 32:
            t = (a[k] ^ lax.shift_right_logical(a[k + j], j)) & mask_i
            a[k] = a[k] ^ t
            a[k + j] = a[k + j] ^ lax.shift_left(t, j)
            k = (k + j + 1) & ~j
        j >>= 1
        mask = (mask ^ (mask << j)) & 0xFFFFFFFF
    return a


def _dsa_kernel(qat_ref, qit_ref, wit_ref, ki_ref, ka_ref, vat_ref, o_ref,
                aqi_scr, aqa_scr, key_scr, bias_scr, plane_scr, alive_scr, *softmax_refs, seq):
    m_scr, l_scr, acc_scr = _softmax_split(softmax_refs)
    tq, tk = TQ_DSA, TK_DSA
    blk_per_tile = tk // RADIX_BLOCK
    n_blocks = seq // RADIX_BLOCK
    i = pl.program_id(1)
    q0 = i * tq
    nk = lax.shift_right_logical(q0 + tq + tk - 1, int(np.log2(tk)))
    n_rest = (seq - nk * tk).astype(F32)

    row_q = lax.broadcasted_iota(I32, (LANES, tq), 0)
    for h in range(N_HEADS):
        keep = (row_q < HEAD_DIM) if h % 2 == 0 else (row_q >= HEAD_DIM)
        rows = slice((h // 2) * LANES, (h // 2 + 1) * LANES)
        qi_blk = qit_ref[0, rows, :]
        qa_blk = qat_ref[0, rows, :]
        aqi_scr[:, h * tq:(h + 1) * tq] = jnp.where(keep, qi_blk, jnp.zeros_like(qi_blk))
        aqa_scr[:, h * tq:(h + 1) * tq] = jnp.where(keep, qa_blk, jnp.zeros_like(qa_blk))

    w = wit_ref[0]
    key_j = lax.broadcasted_iota(I32, (tk, tq), 0)
    qry_chunk = lax.shift_right_logical(q0 + lax.broadcasted_iota(I32, (tk, tq), 1), CHUNK_SHIFT)

    def admissible(k0):
        return lax.shift_right_logical(k0 + key_j, CHUNK_SHIFT) <= qry_chunk

    def p1(kt, carry):
        k0 = pl.multiple_of(kt * tk, tk)
        rel = _dot(ki_ref[0, pl.ds(k0, tk), :], aqi_scr[...])
        sc = w[0:1, :] * jnp.maximum(rel[:, 0:tq], 0.0)
        for h in range(1, N_HEADS):
            sc = sc + w[h:h + 1, :] * jnp.maximum(rel[:, h * tq:(h + 1) * tq], 0.0)
        sc = jnp.where(admissible(k0), sc, NEG_INF)
        sc = jnp.where(sc == 0.0, 0.0, sc)
        b = pltpu.bitcast(sc, I32)
        key = b ^ (lax.shift_right_arithmetic(b, 31) & 0x7FFFFFFF)
        key_scr[kt] = key
        for half in range(blk_per_tile):
            base = half * RADIX_BLOCK
            planes = _bit_transpose32([key[base + SUBLANES * m:base + SUBLANES * (m + 1), :] ^ _INT_MIN
                                       for m in range(32)])
            bl = kt * blk_per_tile + half
            for p in range(32):
                plane_scr[bl, p] = planes[p]
            alive_scr[bl] = jnp.full((SUBLANES, tq), -1, I32)
        return carry

    lax.fori_loop(0, nk, p1, 0)

    def clear_block(bl, carry):
        plane_scr[bl] = jnp.zeros((32, SUBLANES, tq), I32)
        alive_scr[bl] = jnp.zeros((SUBLANES, tq), I32)
        return carry

    lax.fori_loop(nk * blk_per_tile, n_blocks, clear_block, 0)

    sent_u = jnp.int32(_SENT_KEY ^ _INT_MIN)

    def bit_step(p, carry):
        k_rem, thr_bits, rest_alive = carry
        shift = 31 - p
        sent_bit = lax.shift_right_logical(sent_u, shift) & 1
        ones = [alive_scr[bl] & plane_scr[bl, p] for bl in range(n_blocks)]
        tot = lax.population_count(ones[0])
        for bl in range(1, n_blocks):
            tot = tot + lax.population_count(ones[bl])
        cnt = jnp.sum(tot.astype(F32), axis=0, keepdims=True)
        cnt = cnt + jnp.where((rest_alive != 0) & (sent_bit != 0), n_rest, 0.0)
        take1 = cnt >= k_rem
        for bl in range(n_blocks):
            alive_scr[bl] = jnp.where(take1, ones[bl], alive_scr[bl] & ~plane_scr[bl, p])
        k_rem = jnp.where(take1, k_rem, k_rem - cnt)
        thr_bits = thr_bits | jnp.where(take1, lax.shift_left(jnp.int32(1), shift), 0)
        rest_alive = jnp.where(take1 == (sent_bit != 0), rest_alive, 0)
        return k_rem, thr_bits, rest_alive

    need, thr_bits, _ = lax.fori_loop(
        0, 32, bit_step,
        (jnp.full((1, tq), float(TOPK), F32), jnp.zeros((1, tq), I32), jnp.ones((1, tq), I32)))
    thr = thr_bits ^ _INT_MIN

    tr = lax.broadcasted_iota(I32, (tk, tk), 0)
    tc = lax.broadcasted_iota(I32, (tk, tk), 1)
    tri = jnp.where(tc <= tr, 1.0, 0.0).astype(BF16)
    _softmax_init(m_scr, l_scr, acc_scr)

    def p3_max(kt, tie):
        k0 = pl.multiple_of(kt * tk, tk)
        kk = key_scr[kt]
        eq = kk == thr
        pref = _dot(tri, jnp.where(eq, 1.0, 0.0).astype(BF16)) + tie
        sel = (kk > thr) | (eq & (pref <= need))
        bias = jnp.where(sel & admissible(k0), 0.0, NEG_INF)
        bias_scr[kt] = bias
        logit = _dot(ka_ref[0, pl.ds(k0, tk), :], aqa_scr[...])
        for h in range(N_HEADS):
            _softmax_max(logit[:, h * tq:(h + 1) * tq] + bias, m_scr[h])
        return pref[tk - 1:tk, :]

    def p3_accumulate(kt, carry):
        k0 = pl.multiple_of(kt * tk, tk)
        bias = bias_scr[kt]
        logit = _dot(ka_ref[0, pl.ds(k0, tk), :], aqa_scr[...])
        v_t = vat_ref[0, kt]
        for h in range(N_HEADS):
            _softmax_accumulate(logit[:, h * tq:(h + 1) * tq] + bias, m_scr[h], l_scr[h], acc_scr[h], v_t)
        return carry

    lax.fori_loop(0, nk, p3_max, jnp.zeros((1, tq), F32))
    lax.fori_loop(0, nk, p3_accumulate, 0)
    _softmax_finish(o_ref, l_scr, acc_scr)


def _dsa_call(qat, qit, wit, ki3, ka3, vat4):
    batch, _, seq = qat.shape
    qspec = pl.BlockSpec((1, 512, TQ_DSA), lambda b, i: (b, 0, i))
    kspec = pl.BlockSpec((1, seq, LANES), lambda b, i: (b, 0, 0))
    return pl.pallas_call(
        functools.partial(_dsa_kernel, seq=seq),
        grid=(batch, seq // TQ_DSA),
        in_specs=[qspec, qspec,
                  pl.BlockSpec((1, N_HEADS, TQ_DSA), lambda b, i: (b, 0, i)),
                  kspec, kspec,
                  pl.BlockSpec((1, seq // TK_DSA, HEAD_DIM, TK_DSA), lambda b, i: (b, 0, 0, 0))],
        out_specs=qspec,
        out_shape=jax.ShapeDtypeStruct((batch, 512, seq), BF16),
        scratch_shapes=[pltpu.VMEM((LANES, N_HEADS * TQ_DSA), BF16),
                        pltpu.VMEM((LANES, N_HEADS * TQ_DSA), BF16),
                        pltpu.VMEM((seq // TK_DSA, TK_DSA, TQ_DSA), I32),
                        pltpu.VMEM((seq // TK_DSA, TK_DSA, TQ_DSA), F32),
                        pltpu.VMEM((seq // RADIX_BLOCK, 32, SUBLANES, TQ_DSA), I32),
                        pltpu.VMEM((seq // RADIX_BLOCK, SUBLANES, TQ_DSA), I32)] + _softmax_scratch(TQ_DSA),
        compiler_params=_cparams(("arbitrary", "arbitrary")),
        name="dsa",
    )(qat, qit, wit, ki3, ka3, vat4)


def _outproj_kernel(oa_ref, ob_ref, sga_ref, sgb_ref, x_ref, mod_ref, gffn_ref,
                    woa_ref, wob_ref, wo_ref, wr_hi_ref, wr_lo_ref, br_ref,
                    x1_ref, h2_ref, lg_ref):
    tm = TM_OUT
    ya = _dot(oa_ref[...], woa_ref[...])
    yb = _dot(ob_ref[...], wob_ref[...])
    merged = (sga_ref[...].astype(F32) * ya + sgb_ref[...].astype(F32) * yb).astype(BF16)
    mix = _dot(merged, wo_ref[...])
    gt1 = mod_ref[0, 2:3, :]
    sh2 = mod_ref[0, 3:4, :]
    sc2 = mod_ref[0, 4:5, :]
    x1 = x_ref[...] + gt1 * mix
    x1_ref[...] = x1
    var = jnp.mean(x1 * x1, axis=-1, keepdims=True)
    h2 = x1 * lax.rsqrt(var + RMS_EPS) * gffn_ref[...] * (1.0 + sc2) + sh2
    for c in range(D_MODEL // LANES):
        h2_ref[pl.ds(c, tm, stride=SUBLANES), :] = h2[:, c * LANES:(c + 1) * LANES]
    hi = h2.astype(BF16)
    lo = (h2 - hi.astype(F32)).astype(BF16)
    lg_ref[...] = (_dot(hi, wr_hi_ref[...]) + _dot(hi, wr_lo_ref[...]) + _dot(lo, wr_hi_ref[...])
                   + br_ref[...])


def _outproj_call(oa2, ob2, sga, sgb, x2, mod3, g_ffn, woa, wob, wo, wr_hi, wr_lo, br, seq):
    n = x2.shape[0]
    tpb = seq // TM_OUT
    row = lambda w: pl.BlockSpec((TM_OUT, w), lambda i: (i, 0))
    full = lambda a: pl.BlockSpec(a.shape, lambda i: (0,) * a.ndim)
    return pl.pallas_call(
        _outproj_kernel,
        grid=(n // TM_OUT,),
        in_specs=[row(512), row(512), row(1024), row(1024), row(D_MODEL),
                  pl.BlockSpec((1, 6, D_MODEL), lambda i: (i // tpb, 0, 0)),
                  full(g_ffn), full(woa), full(wob), full(wo), full(wr_hi), full(wr_lo), full(br)],
        out_specs=[row(D_MODEL),
                   pl.BlockSpec((TM_OUT * SUBLANES, LANES), lambda i: (i, 0)),
                   row(LANES)],
        out_shape=[jax.ShapeDtypeStruct((n, D_MODEL), F32),
                   jax.ShapeDtypeStruct((n * SUBLANES, LANES), F32),
                   jax.ShapeDtypeStruct((n, LANES), F32)],
        compiler_params=_cparams(("arbitrary",)),
        name="outproj",
    )(oa2, ob2, sga, sgb, x2, mod3, g_ffn, woa, wob, wo, wr_hi, wr_lo, br)


def _route_kernel(lg_ref, rt_ref):
    tm = TM_ROUTE
    lg = lg_ref[...]
    lane = lax.broadcasted_iota(I32, (tm, LANES), 1)
    big = jnp.int32(LANES)
    ninf = -jnp.inf
    gmask = (lane >= N_EXPERTS) & (lane < N_EXPERTS + N_GROUPS)
    g = jnp.where(gmask, lg, ninf)
    gmax = jnp.max(g, axis=-1, keepdims=True)
    grp = jnp.min(jnp.where(g == gmax, lane - N_EXPERTS, big), axis=-1, keepdims=True)
    p_grp = 1.0 / jnp.sum(jnp.where(gmask, jnp.exp(lg - gmax), 0.0), axis=-1, keepdims=True)
    lo = grp * EXPERTS_PER_GROUP
    emask = (lane >= lo) & (lane < lo + EXPERTS_PER_GROUP)
    ev = jnp.where(emask, lg, ninf)
    v0 = jnp.max(ev, axis=-1, keepdims=True)
    i0 = jnp.min(jnp.where(emask & (ev == v0), lane, big), axis=-1, keepdims=True)
    rest = emask & (lane != i0)
    ev1 = jnp.where(rest, lg, ninf)
    v1 = jnp.max(ev1, axis=-1, keepdims=True)
    i1 = jnp.min(jnp.where(rest & (ev1 == v1), lane, big), axis=-1, keepdims=True)
    e1 = jnp.exp(v1 - v0)
    w0 = p_grp / (1.0 + e1)
    w1 = p_grp * e1 / (1.0 + e1)
    rt_ref[...] = jnp.where(lane == 0, i0.astype(F32),
                            jnp.where(lane == 1, i1.astype(F32),
                                      jnp.where(lane == 2, w0, jnp.where(lane == 3, w1, 0.0))))


def _route_call(lg):
    n = lg.shape[0]
    spec = pl.BlockSpec((TM_ROUTE, LANES), lambda i: (i, 0))
    return pl.pallas_call(
        _route_kernel, grid=(n // TM_ROUTE,), in_specs=[spec], out_specs=spec,
        out_shape=jax.ShapeDtypeStruct((n, LANES), F32),
        compiler_params=_cparams(("arbitrary",)), name="route",
    )(lg)


def _slab_copy(src, src_row, dst, dst_row, sem):
    return pltpu.make_async_copy(src.at[pl.ds(pl.multiple_of(src_row * SUBLANES, SUBLANES), SUBLANES)],
                                 dst.at[pl.ds(pl.multiple_of(dst_row * SUBLANES, SUBLANES), SUBLANES)], sem)


def _dispatch_kernel(dest_ref, h2_ref, xs_in_hbm, xs_hbm, sem):
    del xs_in_hbm
    tm = TM_DISPATCH
    base = pl.program_id(0) * tm

    def issue(r, _):
        for k in range(2):
            _slab_copy(h2_ref, r, xs_hbm, dest_ref[2 * (base + r) + k], sem).start()
        return 0

    lax.fori_loop(0, tm, issue, 0)
    for _ in range(2):
        pltpu.make_async_copy(h2_ref, xs_hbm.at[pl.ds(0, tm * SUBLANES)], sem).wait()


def _dispatch_call(dest, h2s, xs_zero):
    n = h2s.shape[0] // SUBLANES
    anyspec = pl.BlockSpec(memory_space=pl.ANY)
    return pl.pallas_call(
        _dispatch_kernel,
        grid_spec=pltpu.PrefetchScalarGridSpec(
            num_scalar_prefetch=1, grid=(n // TM_DISPATCH,),
            in_specs=[pl.BlockSpec((TM_DISPATCH * SUBLANES, LANES), lambda i, d: (i, 0)), anyspec],
            out_specs=anyspec,
            scratch_shapes=[pltpu.SemaphoreType.DMA(())]),
        out_shape=jax.ShapeDtypeStruct(xs_zero.shape, F32),
        input_output_aliases={2: 0},
        compiler_params=_cparams(("arbitrary",)),
        name="dispatch",
    )(dest, h2s, xs_zero)


def _experts_kernel(be_ref, nused_ref, xs_ref, wg_ref, wu_ref, wd_ref, yb_ref, x_scr, wg_scr, wu_scr, wd_scr):
    i = pl.program_id(0)

    @pl.when(i >= nused_ref[0])
    def _():
        yb_ref[...] = jnp.zeros_like(yb_ref)

    @pl.when(i < nused_ref[0])
    def _():
        prev = be_ref[jnp.maximum(i - 1, 0)]

        @pl.when((i == 0) | (be_ref[i] != prev))
        def _():
            wg_scr[...] = wg_ref[0].astype(BF16)
            wu_scr[...] = wu_ref[0].astype(BF16)
            wd_scr[...] = wd_ref[0].astype(BF16)

        for c in range(D_MODEL // LANES):
            x_scr[:, c * LANES:(c + 1) * LANES] = xs_ref[pl.ds(c, BLK_E, stride=SUBLANES), :].astype(BF16)
        xb = x_scr[...]
        g = _dot(xb, wg_scr[...])
        u = _dot(xb, wu_scr[...])
        hid = (g * _sigmoid(g) * u).astype(BF16)
        y = _dot(hid, wd_scr[...])
        for c in range(D_MODEL // LANES):
            yb_ref[pl.ds(c, BLK_E, stride=SUBLANES), :] = y[:, c * LANES:(c + 1) * LANES]


def _experts_call(block_e, n_used, xs, w_gate, w_up, w_down):
    n_blocks = block_e.shape[0]
    slab = pl.BlockSpec((BLK_E * SUBLANES, LANES), lambda i, be, nu: (i, 0))
    return pl.pallas_call(
        _experts_kernel,
        grid_spec=pltpu.PrefetchScalarGridSpec(
            num_scalar_prefetch=2, grid=(n_blocks,),
            in_specs=[slab,
                      pl.BlockSpec((1, D_MODEL, D_EXPERT), lambda i, be, nu: (be[i], 0, 0)),
                      pl.BlockSpec((1, D_MODEL, D_EXPERT), lambda i, be, nu: (be[i], 0, 0)),
                      pl.BlockSpec((1, D_EXPERT, D_MODEL), lambda i, be, nu: (be[i], 0, 0))],
            out_specs=slab,
            scratch_shapes=[pltpu.VMEM((BLK_E, D_MODEL), BF16),
                            pltpu.VMEM((D_MODEL, D_EXPERT), BF16),
                            pltpu.VMEM((D_MODEL, D_EXPERT), BF16),
                            pltpu.VMEM((D_EXPERT, D_MODEL), BF16)]),
        out_shape=jax.ShapeDtypeStruct(xs.shape, F32),
        compiler_params=_cparams(("arbitrary",)),
        name="experts",
    )(block_e, n_used, xs, w_gate, w_up, w_down)


def _final_kernel(dest_ref, x1_ref, rt_ref, mod_ref, gfin_ref, yb_hbm, o_ref, g_scr, sem):
    tm = TM_FINAL
    base = pl.program_id(0) * tm

    def issue(r, _):
        tok = base + r
        for k in range(2):
            _slab_copy(yb_hbm, dest_ref[2 * tok + k], g_scr, k * tm + r, sem).start()
        return 0

    lax.fori_loop(0, tm, issue, 0)
    pltpu.make_async_copy(yb_hbm.at[pl.ds(0, 2 * tm * SUBLANES)], g_scr, sem).wait()

    rt = rt_ref[...]
    lane = lax.broadcasted_iota(I32, (tm, LANES), 1)
    gw0 = jnp.sum(jnp.where(lane == 2, rt, 0.0), axis=-1, keepdims=True)
    gw1 = jnp.sum(jnp.where(lane == 3, rt, 0.0), axis=-1, keepdims=True)
    gt2 = mod_ref[0, 5:6, :]
    x1 = x1_ref[...]
    cols = []
    for c in range(D_MODEL // LANES):
        y0 = g_scr[pl.ds(c, tm, stride=SUBLANES), :]
        y1 = g_scr[pl.ds(tm * SUBLANES + c, tm, stride=SUBLANES), :]
        y = gw0 * y0 + gw1 * y1
        cols.append(x1[:, c * LANES:(c + 1) * LANES] + gt2[:, c * LANES:(c + 1) * LANES] * y)
    x2 = jnp.concatenate(cols, axis=1)
    var = jnp.mean(x2 * x2, axis=-1, keepdims=True)
    o_ref[...] = x2 * lax.rsqrt(var + RMS_EPS) * gfin_ref[...]


def _final_call(dest, x1, rt, mod3, g_final, yb, seq):
    n = x1.shape[0]
    tpb = seq // TM_FINAL
    return pl.pallas_call(
        _final_kernel,
        grid_spec=pltpu.PrefetchScalarGridSpec(
            num_scalar_prefetch=1, grid=(n // TM_FINAL,),
            in_specs=[pl.BlockSpec((TM_FINAL, D_MODEL), lambda i, d: (i, 0)),
                      pl.BlockSpec((TM_FINAL, LANES), lambda i, d: (i, 0)),
                      pl.BlockSpec((1, 6, D_MODEL), lambda i, d: (i // tpb, 0, 0)),
                      pl.BlockSpec((1, D_MODEL), lambda i, d: (0, 0)),
                      pl.BlockSpec(memory_space=pl.ANY)],
            out_specs=pl.BlockSpec((TM_FINAL, D_MODEL), lambda i, d: (i, 0)),
            scratch_shapes=[pltpu.VMEM((2 * TM_FINAL * SUBLANES, LANES), F32),
                            pltpu.SemaphoreType.DMA(())]),
        out_shape=jax.ShapeDtypeStruct((n, D_MODEL), F32),
        compiler_params=_cparams(("arbitrary",)),
        name="final",
    )(dest, x1, rt, mod3, g_final, yb)


def _permute_w_in(w):
    o = np.cumsum([0, 512, 64, 64, 512, 64, 8, 512, 512, 512, 8, 1024, 1024])
    qa, ka, va, qi, ki, wi, qb, kb, vb, fb, ga, gb = [w[:, o[k]:o[k + 1]] for k in range(12)]
    aux = jnp.concatenate([wi, fb, jnp.zeros((w.shape[0], LANES - 2 * N_HEADS), w.dtype)], axis=1)
    return jnp.concatenate([qa, qi, qb, kb, vb, ga, gb, ka, ka, va, va, ki, ki, aux], axis=1).astype(BF16)


def _layer(x2, pos2, mod3, batch, seq, g_mix, w_in, b_forget, w_out_a, w_out_b, w_out, g_ffn,
           w_group, b_group, w_router, b_router, w_e_gate, w_e_up, w_e_down, g_final):
    n = x2.shape[0]
    lane = np.arange(LANES)
    jj = lane % HEAD_DIM
    invf_np = np.where(jj < ROPE_DIM, ROPE_THETA ** (-(2.0 * (jj % (ROPE_DIM // 2))) / ROPE_DIM), 0.0)
    invf = jnp.asarray(invf_np[None, :], F32)
    fbias = jnp.zeros((1, LANES), F32).at[0, N_HEADS:2 * N_HEADS].set(b_forget.astype(F32))

    (qa, qi, qb, kaug, vb, sga, sgb, ka2, va2, ki2, aux) = _inproj_call(
        x2, pos2, mod3, g_mix.reshape(1, D_MODEL), _permute_w_in(w_in), invf, fbias, batch, seq)

    r3 = lambda a: a.reshape(batch, seq, a.shape[-1])
    tr = lambda a: jnp.swapaxes(r3(a), 1, 2)
    tr_tiles = lambda a, tk: jnp.swapaxes(a.reshape(batch, seq // tk, tk, a.shape[-1]), 2, 3)
    aux_t = tr(aux[:, :2 * N_HEADS])
    wit, cumt = aux_t[:, :N_HEADS], aux_t[:, N_HEADS:]
    obt = _fox_call(tr(qb), cumt, r3(kaug), tr_tiles(vb, TK_FOX))
    oat = _dsa_call(tr(qa), tr(qi), wit, r3(ki2), r3(ka2), tr_tiles(va2[:, :HEAD_DIM], TK_DSA))
    oa = jnp.swapaxes(oat, 1, 2).reshape(n, 512)
    ob = jnp.swapaxes(obt, 1, 2).reshape(n, 512)

    w_rt = jnp.concatenate([w_router, w_group, jnp.zeros((D_MODEL, LANES - N_EXPERTS - N_GROUPS), F32)], axis=1)
    wr_hi = w_rt.astype(BF16)
    wr_lo = (w_rt - wr_hi.astype(F32)).astype(BF16)
    br = jnp.concatenate([b_router, b_group, jnp.zeros((LANES - N_EXPERTS - N_GROUPS,), F32)])[None, :]
    x1, h2s, lg = _outproj_call(oa, ob, sga, sgb, x2, mod3,
                                g_ffn.reshape(1, D_MODEL), w_out_a.astype(BF16), w_out_b.astype(BF16),
                                w_out.astype(BF16), wr_hi, wr_lo, br, seq)
    rt = _route_call(lg)

    flat_e = rt[:, 0:2].astype(I32).reshape(-1)
    onehot = (flat_e[:, None] == jnp.arange(N_EXPERTS, dtype=I32)[None, :]).astype(I32)
    csum = jnp.cumsum(onehot, axis=0)
    counts = csum[-1]
    rank = jnp.take_along_axis(csum, flat_e[:, None], axis=1)[:, 0] - 1
    padded = (counts + BLK_E - 1) // BLK_E * BLK_E
    pend = jnp.cumsum(padded)
    pstart = pend - padded
    dest = (pstart[flat_e] + rank).astype(I32)
    n_slots = 2 * n + N_EXPERTS * BLK_E
    n_blocks = n_slots // BLK_E
    block_e = jnp.minimum(jnp.searchsorted(pend, jnp.arange(n_blocks, dtype=I32) * BLK_E, side='right'),
                          N_EXPERTS - 1).astype(I32)
    n_used = (pend[-1:] // BLK_E).astype(I32)

    xs = _dispatch_call(dest, h2s, jnp.zeros((n_slots * SUBLANES, LANES), F32))
    yb = _experts_call(block_e, n_used, xs, w_e_gate, w_e_up, w_e_down)
    return _final_call(dest, x1, rt, mod3, g_final.reshape(1, D_MODEL), yb, seq)


def kernel(x, c, positions, w_mod, b_mod, g_mix, w_in, b_forget, w_out_a, w_out_b, w_out, g_ffn, w_group,
           b_group, w_router, b_router, w_e_gate, w_e_up, w_e_down, g_final):
    batch, seq, d = x.shape
    depth = w_mod.shape[0]
    assert depth == 1 and d == D_MODEL, "kernel fuses the final norm into the single layer"
    n = batch * seq
    c8 = jnp.zeros((8, d), F32).at[:batch].set(c)
    mod = _mod_call(c8, w_mod[0], b_mod[0][None, :])
    mod3 = mod[:batch].reshape(batch, 6, d)
    out = _layer(x.reshape(n, d), positions.reshape(n, 1), mod3, batch, seq, g_mix[0], w_in[0], b_forget[0],
                 w_out_a[0], w_out_b[0], w_out[0], g_ffn[0], w_group[0], b_group[0], w_router[0], b_router[0],
                 w_e_gate[0], w_e_up[0], w_e_down[0], g_final)
    return out.reshape(batch, seq, d)
```

```python
import functools

import numpy as np
import jax
import jax.numpy as jnp
from jax import lax
from jax.experimental import pallas as pl
from jax.experimental.pallas import tpu as pltpu

F32 = jnp.float32
BF16 = jnp.bfloat16
I32 = jnp.int32

D_MODEL = 1024
HEAD_DIM = 64
N_HEADS = 8
CHUNK_SHIFT = 6
TOPK = 256
ROPE_DIM = 16
ROPE_THETA = 500000.0
N_GROUPS = 4
EXPERTS_PER_GROUP = 8
N_EXPERTS = 32
D_EXPERT = 512
RMS_EPS = 1e-6
NEG_INF = -1e30
ATTN_SCALE = HEAD_DIM ** -0.5
IDX_SCALE = HEAD_DIM ** -0.5
LOG2E = 1.4426950408889634

LANES = 128
SUBLANES = 8
VMEM_LIMIT = 56 * 1024 * 1024

TM_IN = 512
TQ_FOX = 512
TK_FOX = 512
TQ_DSA = 256
TK_DSA = 512
TM_OUT = 512
TM_ROUTE = 1024
TM_DISPATCH = 512
BLK_E = 256
TM_FINAL = 256
GATHER_UNROLL = 8

RADIX_BLOCK = 32 * SUBLANES
BOUND_MARGIN = 1.02
BOUND_SAFE = 40.0
FOX_BIAS_SLACK = 1.0
FOX_K = 256
FOX_AUG_ROWS = 16

_SEG_QA, _SEG_QI, _SEG_QB, _SEG_KB, _SEG_VB = 0, 512, 1024, 1536, 2048
_SEG_GA, _SEG_GB = 2560, 3584
_SEG_KA, _SEG_VA, _SEG_KI, _SEG_AUX = 4608, 4736, 4864, 4992
_NC_IN = 5120

_SENT_KEY = int(np.array(NEG_INF, np.float32).view(np.int32) ^ 0x7FFFFFFF)
_INT_MIN = -(2 ** 31)


def _cparams(sem):
    return pltpu.CompilerParams(dimension_semantics=sem, vmem_limit_bytes=VMEM_LIMIT)


def _dot(a, b):
    return jnp.dot(a, b, preferred_element_type=F32)


def _sigmoid(x):
    return 1.0 / (1.0 + jnp.exp(-x))


def _mod_kernel(c_ref, w_ref, b_ref, o_ref):
    c = c_ref[...]
    ca = c * _sigmoid(c)
    o_ref[...] = _dot(ca.astype(BF16), w_ref[...].astype(BF16)) + b_ref[...]


def _mod_call(c8, w_mod, b_mod):
    n_out = w_mod.shape[1]
    tn = 1024
    return pl.pallas_call(
        _mod_kernel,
        grid=(n_out // tn,),
        in_specs=[pl.BlockSpec((8, D_MODEL), lambda j: (0, 0)),
                  pl.BlockSpec((D_MODEL, tn), lambda j: (0, j)),
                  pl.BlockSpec((1, tn), lambda j: (0, j))],
        out_specs=pl.BlockSpec((8, tn), lambda j: (0, j)),
        out_shape=jax.ShapeDtypeStruct((8, n_out), F32),
        compiler_params=_cparams(("arbitrary",)),
        name="mod",
    )(c8, w_mod, b_mod)


def _inproj_kernel(x_ref, pos_ref, mod_ref, g_ref, w_ref, invf_ref, fbias_ref, sel_ref, ones_ref,
                   qa_ref, qi_ref, qb_ref, kb_ref, vb_ref, sga_ref, sgb_ref,
                   ka_ref, va_ref, ki_ref, aux_ref, h_scr, carry_scr):
    tm = TM_IN
    x = x_ref[...]
    var = jnp.mean(x * x, axis=-1, keepdims=True)
    tf = x * lax.rsqrt(var + RMS_EPS) * g_ref[...]
    sh = mod_ref[0, 0:1, :]
    sc = mod_ref[0, 1:2, :]
    h_scr[...] = (tf * (1.0 + sc) + sh).astype(BF16)
    hb = h_scr[...]

    lane = lax.broadcasted_iota(I32, (tm, LANES), 1)
    j = lane & (HEAD_DIM - 1)
    ang = pos_ref[...].astype(F32) * invf_ref[...]
    cs = jnp.cos(ang)
    sn = jnp.sin(ang)
    coef_next = jnp.where(j < ROPE_DIM // 2, -sn, 0.0)
    coef_prev = jnp.where((j >= ROPE_DIM // 2) & (j < ROPE_DIM), sn, 0.0)

    def rope(tc):
        return (tc * cs + pltpu.roll(tc, LANES - ROPE_DIM // 2, 1) * coef_next
                + pltpu.roll(tc, ROPE_DIM // 2, 1) * coef_prev)

    def seg(off, width):
        return _dot(hb, w_ref[:, off:off + width])

    t = seg(_SEG_QA, 512)
    for c in range(4):
        qa_ref[:, c * LANES:(c + 1) * LANES] = (
            rope(t[:, c * LANES:(c + 1) * LANES]) * (ATTN_SCALE * LOG2E)).astype(BF16)
    t = seg(_SEG_QI, 512)
    for c in range(4):
        qi_ref[:, c * LANES:(c + 1) * LANES] = (rope(t[:, c * LANES:(c + 1) * LANES]) * IDX_SCALE).astype(BF16)
    qb_ref[...] = (seg(_SEG_QB, 512) * (ATTN_SCALE * LOG2E)).astype(BF16)
    vb_ref[...] = seg(_SEG_VB, 512).astype(BF16)
    sga_ref[...] = _sigmoid(seg(_SEG_GA, 1024)).astype(BF16)
    sgb_ref[...] = _sigmoid(seg(_SEG_GB, 1024)).astype(BF16)
    ka = rope(seg(_SEG_KA, LANES))
    ka_ref[...] = ka.astype(BF16)
    va_ref[...] = seg(_SEG_VA, LANES).astype(BF16)
    ki_ref[...] = rope(seg(_SEG_KI, LANES)).astype(BF16)

    z = seg(_SEG_AUX, LANES)
    zf = z + fbias_ref[...]
    logf = jnp.minimum(zf, 0.0) - jnp.log(1.0 + jnp.exp(-jnp.abs(zf)))
    is_f = (lane >= N_HEADS) & (lane < 2 * N_HEADS)
    logf = jnp.where(is_f, logf, 0.0)

    def split3(v):
        hi = v.astype(BF16)
        r1 = v - hi.astype(F32)
        mid = r1.astype(BF16)
        return hi, mid, (r1 - mid.astype(F32)).astype(BF16)

    rr = lax.broadcasted_iota(I32, (tm, tm), 0)
    cc = lax.broadcasted_iota(I32, (tm, tm), 1)
    tri = jnp.where(cc <= rr, 1.0, 0.0).astype(BF16)
    p_hi, p_mid, p_lo = split3(logf)

    @pl.when(pl.program_id(1) == 0)
    def _():
        carry_scr[...] = jnp.zeros_like(carry_scr)

    cum = _dot(tri, p_hi) + _dot(tri, p_mid) + _dot(tri, p_lo) + carry_scr[...]
    carry_scr[...] = cum[tm - 1:tm, :]
    cum2 = cum * LOG2E
    aux = jnp.where(lane < N_HEADS, z * (N_HEADS ** -0.5), jnp.where(is_f, cum2, 0.0))

    kb = seg(_SEG_KB, 512)
    first = lane < HEAD_DIM

    def half_norms(blk):
        sq = blk * blk
        return (jnp.sum(jnp.where(first, sq, 0.0), axis=-1, keepdims=True),
                jnp.sum(jnp.where(first, 0.0, sq), axis=-1, keepdims=True))

    for p in range(N_HEADS // 2):
        n_even, n_odd = half_norms(kb[:, p * LANES:(p + 1) * LANES])
        aux = jnp.where(lane == 2 * N_HEADS + 2 * p, n_even, jnp.where(lane == 2 * N_HEADS + 2 * p + 1, n_odd, aux))
    aux_ref[...] = jnp.where(lane == 3 * N_HEADS, half_norms(ka)[0], aux)

    c_hi, c_mid, c_lo = split3(cum2)
    kaug = _dot(c_hi, sel_ref[0]) + _dot(c_mid, sel_ref[1]) + _dot(c_lo, sel_ref[2]) + ones_ref[...]
    for p in range(N_HEADS // 2):
        kb_ref[:, p * FOX_K:p * FOX_K + LANES] = kb[:, p * LANES:(p + 1) * LANES].astype(BF16)
        kb_ref[:, p * FOX_K + LANES:(p + 1) * FOX_K] = kaug[:, p * LANES:(p + 1) * LANES].astype(BF16)


def _fox_routing_constants():
    sel = np.zeros((3, LANES, (N_HEADS // 2) * LANES), np.float32)
    ones = np.zeros((1, (N_HEADS // 2) * LANES), np.float32)
    for h in range(N_HEADS):
        for t in range(3):
            sel[t, N_HEADS + h, (h // 2) * LANES + 3 + 3 * (h % 2) + t] = -1.0
    for p in range(N_HEADS // 2):
        ones[0, p * LANES:p * LANES + 3] = 1.0
    return jnp.asarray(sel, BF16), jnp.asarray(ones, F32)


def _inproj_call(x2, pos2, mod3, g_mix, w_in_p, invf, fbias, batch, seq):
    n = x2.shape[0]
    tpb = seq // TM_IN
    row = lambda b, s: (b * tpb + s, 0)
    bf = lambda w: jax.ShapeDtypeStruct((n, w), BF16)
    sel, ones = _fox_routing_constants()
    out_shapes = [bf(512), bf(512), bf(512), bf((N_HEADS // 2) * FOX_K), bf(512), bf(1024), bf(1024),
                  bf(LANES), bf(LANES), bf(LANES), jax.ShapeDtypeStruct((n, LANES), F32)]
    out_specs = [pl.BlockSpec((TM_IN, s.shape[1]), row) for s in out_shapes]
    return pl.pallas_call(
        _inproj_kernel,
        grid=(batch, tpb),
        in_specs=[pl.BlockSpec((TM_IN, D_MODEL), row),
                  pl.BlockSpec((TM_IN, 1), row),
                  pl.BlockSpec((1, 6, D_MODEL), lambda b, s: (b, 0, 0)),
                  pl.BlockSpec((1, D_MODEL), lambda b, s: (0, 0)),
                  pl.BlockSpec((D_MODEL, _NC_IN), lambda b, s: (0, 0)),
                  pl.BlockSpec((1, LANES), lambda b, s: (0, 0)),
                  pl.BlockSpec((1, LANES), lambda b, s: (0, 0)),
                  pl.BlockSpec(sel.shape, lambda b, s: (0, 0, 0)),
                  pl.BlockSpec(ones.shape, lambda b, s: (0, 0))],
        out_specs=out_specs,
        out_shape=out_shapes,
        scratch_shapes=[pltpu.VMEM((TM_IN, D_MODEL), BF16), pltpu.VMEM((1, LANES), F32)],
        compiler_params=_cparams(("arbitrary", "arbitrary")),
        name="inproj",
    )(x2, pos2, mod3, g_mix, w_in_p, invf, fbias, sel, ones)


def _softmax_max(s, m_ref):
    m_ref[...] = jnp.maximum(m_ref[...], jnp.max(s, axis=0, keepdims=True))


def _softmax_accumulate(s, m_ref, l_ref, acc_ref, v_t):
    p = jnp.exp2(s - m_ref[...])
    l_ref[...] = l_ref[...] + jnp.sum(p, axis=0, keepdims=True)
    acc_ref[...] = acc_ref[...] + _dot(v_t, p.astype(BF16))


def _softmax_scratch(tq):
    return ([pltpu.VMEM((1, tq), F32) for _ in range(2 * N_HEADS)]
            + [pltpu.VMEM((HEAD_DIM, tq), F32) for _ in range(N_HEADS)])


def _softmax_split(refs):
    return refs[:N_HEADS], refs[N_HEADS:2 * N_HEADS], refs[2 * N_HEADS:3 * N_HEADS]


def _softmax_init(m_refs, l_refs, acc_refs):
    for h in range(N_HEADS):
        m_refs[h][...] = jnp.full(m_refs[h].shape, NEG_INF, F32)
        l_refs[h][...] = jnp.zeros(l_refs[h].shape, F32)
        acc_refs[h][...] = jnp.zeros(acc_refs[h].shape, F32)


def _softmax_finish(o_ref, l_refs, acc_refs):
    for h in range(N_HEADS):
        o_ref[0, h * HEAD_DIM:(h + 1) * HEAD_DIM, :] = (acc_refs[h][...] * (1.0 / l_refs[h][...])).astype(BF16)


def _logit_bounds(qt_ref, kn2, slack):
    k_max = jnp.sqrt(jnp.max(kn2, axis=-1, keepdims=True))
    bounds = []
    worst = jnp.float32(0.0)
    for h in range(N_HEADS):
        qh = qt_ref[0, h * HEAD_DIM:(h + 1) * HEAD_DIM, :].astype(F32)
        q_norm = jnp.sqrt(jnp.sum(qh * qh, axis=0, keepdims=True))
        b = q_norm * k_max[h:h + 1, :] * BOUND_MARGIN + slack
        bounds.append(b)
        worst = jnp.maximum(worst, jnp.max(b))
    return bounds, worst <= BOUND_SAFE


def _fox_kernel(qt_ref, cumt_ref, kn_ref, kaug_ref, vt_ref, o_ref, rhs_scr, *softmax_refs):
    m_scr, l_scr, acc_scr = _softmax_split(softmax_refs)
    tq, tk = TQ_FOX, TK_FOX
    i = pl.program_id(1)
    q0 = i * tq

    cq = cumt_ref[0]
    c_hi = cq.astype(BF16).astype(F32)
    c_r = cq - c_hi
    c_mid = c_r.astype(BF16).astype(F32)
    c_lo = c_r - c_mid
    row_q = lax.broadcasted_iota(I32, (LANES, tq), 0)
    row_a = lax.broadcasted_iota(I32, (FOX_AUG_ROWS, 2 * tq), 0)
    second = lax.broadcasted_iota(I32, (FOX_AUG_ROWS, 2 * tq), 1) >= tq
    for p in range(N_HEADS // 2):
        qp = qt_ref[0, p * LANES:(p + 1) * LANES, :]
        zq = jnp.zeros_like(qp)
        rhs_scr[p, 0:LANES, 0:tq] = jnp.where(row_q < HEAD_DIM, qp, zq)
        rhs_scr[p, 0:LANES, tq:2 * tq] = jnp.where(row_q >= HEAD_DIM, qp, zq)
        pair_row = lambda a: jnp.concatenate([a[2 * p:2 * p + 1, :], a[2 * p + 1:2 * p + 2, :]], axis=1)
        aug = jnp.where(row_a == 0, pair_row(c_hi),
              jnp.where(row_a == 1, pair_row(c_mid),
              jnp.where(row_a == 2, pair_row(c_lo),
              jnp.where((row_a >= 3) & (row_a < 6), jnp.where(second, 0.0, 1.0),
              jnp.where((row_a >= 6) & (row_a < 9), jnp.where(second, 1.0, 0.0), 0.0)))))
        rhs_scr[p, LANES:LANES + FOX_AUG_ROWS, :] = aug.astype(BF16)
        rhs_scr[p, LANES + FOX_AUG_ROWS:FOX_K, :] = jnp.zeros((FOX_K - LANES - FOX_AUG_ROWS, 2 * tq), BF16)

    _softmax_init(m_scr, l_scr, acc_scr)
    key_j = lax.broadcasted_iota(I32, (tk, tq), 0)
    qry_i = q0 + lax.broadcasted_iota(I32, (tk, tq), 1)

    def tile(kt, masked, second_pass):
        k0 = pl.multiple_of(kt * tk, tk)
        for p in range(N_HEADS // 2):
            st = _dot(kaug_ref[0, pl.ds(k0, tk), p * FOX_K:(p + 1) * FOX_K], rhs_scr[p])
            for hh in range(2):
                h = 2 * p + hh
                s = st[:, hh * tq:(hh + 1) * tq]
                if masked:
                    s = jnp.where(k0 + key_j <= qry_i, s, NEG_INF)
                if second_pass:
                    _softmax_accumulate(s, m_scr[h], l_scr[h], acc_scr[h],
                                        vt_ref[0, kt, h * HEAD_DIM:(h + 1) * HEAD_DIM, :])
                else:
                    _softmax_max(s, m_scr[h])

    n_full = i * (tq // tk)

    def sweep(second_pass):
        def full_tile(kt, carry):
            tile(kt, False, second_pass)
            return carry

        lax.fori_loop(0, n_full, full_tile, 0)
        for d in range(tq // tk):
            tile(n_full + d, True, second_pass)

    bounds, safe = _logit_bounds(qt_ref, kn_ref[0], FOX_BIAS_SLACK)

    @pl.when(safe)
    def _():
        for h in range(N_HEADS):
            m_scr[h][...] = bounds[h]
        sweep(True)

    @pl.when(jnp.logical_not(safe))
    def _():
        sweep(False)
        sweep(True)

    _softmax_finish(o_ref, l_scr, acc_scr)


def _fox_call(qbt, aux_t, kaug3, vbt4):
    batch, _, seq = qbt.shape
    return pl.pallas_call(
        _fox_kernel,
        grid=(batch, seq // TQ_FOX),
        in_specs=[pl.BlockSpec((1, 512, TQ_FOX), lambda b, i: (b, 0, i)),
                  pl.BlockSpec((1, N_HEADS, TQ_FOX), lambda b, i: (b, 1, i)),
                  pl.BlockSpec((1, N_HEADS, seq), lambda b, i: (b, 2, 0)),
                  pl.BlockSpec((1, seq, (N_HEADS // 2) * FOX_K), lambda b, i: (b, 0, 0)),
                  pl.BlockSpec((1, seq // TK_FOX, 512, TK_FOX), lambda b, i: (b, 0, 0, 0))],
        out_specs=pl.BlockSpec((1, 512, TQ_FOX), lambda b, i: (b, 0, i)),
        out_shape=jax.ShapeDtypeStruct((batch, 512, seq), BF16),
        scratch_shapes=[pltpu.VMEM((N_HEADS // 2, FOX_K, 2 * TQ_FOX), BF16)] + _softmax_scratch(TQ_FOX),
        compiler_params=_cparams(("arbitrary", "arbitrary")),
        name="fox",
    )(qbt, aux_t, aux_t, kaug3, vbt4)


def _bit_transpose32(words):
    a = list(words)
    j, mask = 16, 0x0000FFFF
    while j:
        mask_i = int(np.array(mask, np.uint32).view(np.int32))
        k = 0
        while k < 32:
            t = (a[k] ^ lax.shift_right_logical(a[k + j], j)) & mask_i
            a[k] = a[k] ^ t
            a[k + j] = a[k + j] ^ lax.shift_left(t, j)
            k = (k + j + 1) & ~j
        j >>= 1
        mask = (mask ^ (mask << j)) & 0xFFFFFFFF
    return a


def _dsa_kernel(qat_ref, qit_ref, wit_ref, kn_ref, ki_ref, ka_ref, vat_ref, o_ref,
                aqi_scr, aqa_scr, sc_scr, bias_scr, plane_scr, alive_scr, *softmax_refs, seq):
    m_scr, l_scr, acc_scr = _softmax_split(softmax_refs)
    tq, tk = TQ_DSA, TK_DSA
    blk_per_tile = tk // RADIX_BLOCK
    n_blocks = seq // RADIX_BLOCK
    i = pl.program_id(1)
    q0 = i * tq
    nk = lax.shift_right_logical(q0 + tq + tk - 1, int(np.log2(tk)))
    n_rest = (seq - nk * tk).astype(F32)

    row_q = lax.broadcasted_iota(I32, (LANES, tq), 0)
    for h in range(N_HEADS):
        keep = (row_q < HEAD_DIM) if h % 2 == 0 else (row_q >= HEAD_DIM)
        rows = slice((h // 2) * LANES, (h // 2 + 1) * LANES)
        qi_blk = qit_ref[0, rows, :]
        qa_blk = qat_ref[0, rows, :]
        aqi_scr[:, h * tq:(h + 1) * tq] = jnp.where(keep, qi_blk, jnp.zeros_like(qi_blk))
        aqa_scr[:, h * tq:(h + 1) * tq] = jnp.where(keep, qa_blk, jnp.zeros_like(qa_blk))

    w = wit_ref[0]
    key_j = lax.broadcasted_iota(I32, (tk, tq), 0)
    qry_chunk = lax.shift_right_logical(q0 + lax.broadcasted_iota(I32, (tk, tq), 1), CHUNK_SHIFT)

    def admissible(k0):
        return lax.shift_right_logical(k0 + key_j, CHUNK_SHIFT) <= qry_chunk

    def p1(kt, carry):
        k0 = pl.multiple_of(kt * tk, tk)
        rel = _dot(ki_ref[0, pl.ds(k0, tk), :], aqi_scr[...])
        sc = w[0:1, :] * jnp.maximum(rel[:, 0:tq], 0.0)
        for h in range(1, N_HEADS):
            sc = sc + w[h:h + 1, :] * jnp.maximum(rel[:, h * tq:(h + 1) * tq], 0.0)
        sc = jnp.where(admissible(k0), sc, NEG_INF)
        sc = jnp.where(sc == 0.0, 0.0, sc)
        sc_scr[kt] = sc
        b = pltpu.bitcast(sc, I32)
        key = b ^ (lax.shift_right_arithmetic(b, 31) & 0x7FFFFFFF)
        for half in range(blk_per_tile):
            base = half * RADIX_BLOCK
            planes = _bit_transpose32([key[base + SUBLANES * m:base + SUBLANES * (m + 1), :] ^ _INT_MIN
                                       for m in range(32)])
            bl = kt * blk_per_tile + half
            for p in range(32):
                plane_scr[bl, p] = planes[p]
            alive_scr[bl] = jnp.full((SUBLANES, tq), -1, I32)
        return carry

    lax.fori_loop(0, nk, p1, 0)

    def clear_block(bl, carry):
        plane_scr[bl] = jnp.zeros((32, SUBLANES, tq), I32)
        alive_scr[bl] = jnp.zeros((SUBLANES, tq), I32)
        return carry

    lax.fori_loop(nk * blk_per_tile, n_blocks, clear_block, 0)

    sent_u = jnp.int32(_SENT_KEY ^ _INT_MIN)

    def bit_step(p, carry):
        k_rem, thr_bits, rest_alive = carry
        shift = 31 - p
        sent_bit = lax.shift_right_logical(sent_u, shift) & 1
        ones = [alive_scr[bl] & plane_scr[bl, p] for bl in range(n_blocks)]
        tot = lax.population_count(ones[0])
        for bl in range(1, n_blocks):
            tot = tot + lax.population_count(ones[bl])
        cnt = jnp.sum(tot.astype(F32), axis=0, keepdims=True)
        cnt = cnt + jnp.where((rest_alive != 0) & (sent_bit != 0), n_rest, 0.0)
        take1 = cnt >= k_rem
        for bl in range(n_blocks):
            alive_scr[bl] = jnp.where(take1, ones[bl], alive_scr[bl] & ~plane_scr[bl, p])
        k_rem = jnp.where(take1, k_rem, k_rem - cnt)
        thr_bits = thr_bits | jnp.where(take1, lax.shift_left(jnp.int32(1), shift), 0)
        rest_alive = jnp.where(take1 == (sent_bit != 0), rest_alive, 0)
        return k_rem, thr_bits, rest_alive

    kf = float(TOPK)
    _, thr_bits, _ = lax.fori_loop(
        0, 32, bit_step,
        (jnp.full((1, tq), kf, F32), jnp.zeros((1, tq), I32), jnp.ones((1, tq), I32)))
    thr_key = thr_bits ^ _INT_MIN
    thr0 = pltpu.bitcast(thr_key ^ (lax.shift_right_arithmetic(thr_key, 31) & 0x7FFFFFFF), F32)

    def rank_counts(thr):
        def body(kt, c):
            sc = sc_scr[kt]
            return (c[0] + jnp.sum(jnp.where(sc >= thr, 1.0, 0.0), axis=0, keepdims=True),
                    c[1] + jnp.sum(jnp.where(sc > thr, 1.0, 0.0), axis=0, keepdims=True))
        z = jnp.zeros((1, tq), F32)
        n_ge, n_gt = lax.fori_loop(0, nk, body, (z, z))
        return (n_ge + jnp.where(thr <= NEG_INF, n_rest, 0.0), n_gt + jnp.where(thr < NEG_INF, n_rest, 0.0))

    def misplaced(state):
        _, n_ge, n_gt = state
        return jnp.max(jnp.where((n_ge < kf) | (n_gt >= kf), 1.0, 0.0)) > 0.0

    def step_threshold(state):
        thr, n_ge, n_gt = state

        def body(kt, c):
            sc = sc_scr[kt]
            below = jnp.max(jnp.where(sc < thr, sc, -jnp.inf), axis=0, keepdims=True)
            above = jnp.min(jnp.where(sc > thr, sc, jnp.inf), axis=0, keepdims=True)
            return jnp.maximum(c[0], below), jnp.minimum(c[1], above)

        below, above = lax.fori_loop(0, nk, body, (jnp.full((1, tq), -jnp.inf, F32), jnp.full((1, tq), jnp.inf, F32)))
        has_rest = n_rest > 0.0
        below = jnp.where(has_rest & (thr > NEG_INF), jnp.maximum(below, NEG_INF), below)
        above = jnp.where(has_rest & (thr < NEG_INF), jnp.minimum(above, NEG_INF), above)
        thr = jnp.where(n_ge < kf, below, jnp.where(n_gt >= kf, above, thr))
        return (thr,) + rank_counts(thr)

    thr, _, n_gt = lax.while_loop(misplaced, step_threshold, (thr0,) + rank_counts(thr0))
    need = kf - n_gt

    tr = lax.broadcasted_iota(I32, (tk, tk), 0)
    tc = lax.broadcasted_iota(I32, (tk, tk), 1)
    tri = jnp.where(tc <= tr, 1.0, 0.0).astype(BF16)
    _softmax_init(m_scr, l_scr, acc_scr)

    def selection_bias(kt, tie):
        sc = sc_scr[kt]
        eq = sc == thr
        pref = _dot(tri, jnp.where(eq, 1.0, 0.0).astype(BF16)) + tie
        sel = (sc > thr) | (eq & (pref <= need))
        return jnp.where(sel & admissible(kt * tk), 0.0, NEG_INF), pref[tk - 1:tk, :]

    def logits(kt):
        return _dot(ka_ref[0, pl.ds(pl.multiple_of(kt * tk, tk), tk), :], aqa_scr[...])

    def accumulate(kt, bias):
        logit = logits(kt)
        v_t = vat_ref[0, kt]
        for h in range(N_HEADS):
            _softmax_accumulate(logit[:, h * tq:(h + 1) * tq] + bias, m_scr[h], l_scr[h], acc_scr[h], v_t)

    def p3_single(kt, tie):
        bias, tie = selection_bias(kt, tie)
        accumulate(kt, bias)
        return tie

    def p3_max(kt, tie):
        bias, tie = selection_bias(kt, tie)
        bias_scr[kt] = bias
        logit = logits(kt)
        for h in range(N_HEADS):
            _softmax_max(logit[:, h * tq:(h + 1) * tq] + bias, m_scr[h])
        return tie

    def p3_accumulate(kt, carry):
        accumulate(kt, bias_scr[kt])
        return carry

    kn2 = jnp.broadcast_to(kn_ref[0, 0:1, :], (N_HEADS, seq))
    bounds, safe = _logit_bounds(qat_ref, kn2, 0.0)
    no_tie = jnp.zeros((1, tq), F32)

    @pl.when(safe)
    def _():
        for h in range(N_HEADS):
            m_scr[h][...] = bounds[h]
        lax.fori_loop(0, nk, p3_single, no_tie)

    @pl.when(jnp.logical_not(safe))
    def _():
        lax.fori_loop(0, nk, p3_max, no_tie)
        lax.fori_loop(0, nk, p3_accumulate, 0)

    _softmax_finish(o_ref, l_scr, acc_scr)


def _dsa_call(qat, qit, aux_t, ki3, ka3, vat4):
    batch, _, seq = qat.shape
    qspec = pl.BlockSpec((1, 512, TQ_DSA), lambda b, i: (b, 0, i))
    kspec = pl.BlockSpec((1, seq, LANES), lambda b, i: (b, 0, 0))
    return pl.pallas_call(
        functools.partial(_dsa_kernel, seq=seq),
        grid=(batch, seq // TQ_DSA),
        in_specs=[qspec, qspec,
                  pl.BlockSpec((1, N_HEADS, TQ_DSA), lambda b, i: (b, 0, i)),
                  pl.BlockSpec((1, N_HEADS, seq), lambda b, i: (b, 3, 0)),
                  kspec, kspec,
                  pl.BlockSpec((1, seq // TK_DSA, HEAD_DIM, TK_DSA), lambda b, i: (b, 0, 0, 0))],
        out_specs=qspec,
        out_shape=jax.ShapeDtypeStruct((batch, 512, seq), BF16),
        scratch_shapes=[pltpu.VMEM((LANES, N_HEADS * TQ_DSA), BF16),
                        pltpu.VMEM((LANES, N_HEADS * TQ_DSA), BF16),
                        pltpu.VMEM((seq // TK_DSA, TK_DSA, TQ_DSA), F32),
                        pltpu.VMEM((seq // TK_DSA, TK_DSA, TQ_DSA), F32),
                        pltpu.VMEM((seq // RADIX_BLOCK, 32, SUBLANES, TQ_DSA), I32),
                        pltpu.VMEM((seq // RADIX_BLOCK, SUBLANES, TQ_DSA), I32)] + _softmax_scratch(TQ_DSA),
        compiler_params=_cparams(("arbitrary", "arbitrary")),
        name="dsa",
    )(qat, qit, aux_t, aux_t, ki3, ka3, vat4)


def _outproj_kernel(oa_ref, ob_ref, sga_ref, sgb_ref, x_ref, mod_ref, gffn_ref,
                    woa_ref, wob_ref, wo_ref, wr_hi_ref, wr_lo_ref, br_ref,
                    x1_ref, h2_ref, lg_ref):
    tm = TM_OUT
    ya = _dot(oa_ref[...], woa_ref[...])
    yb = _dot(ob_ref[...], wob_ref[...])
    merged = (sga_ref[...].astype(F32) * ya + sgb_ref[...].astype(F32) * yb).astype(BF16)
    mix = _dot(merged, wo_ref[...])
    gt1 = mod_ref[0, 2:3, :]
    sh2 = mod_ref[0, 3:4, :]
    sc2 = mod_ref[0, 4:5, :]
    x1 = x_ref[...] + gt1 * mix
    x1_ref[...] = x1
    var = jnp.mean(x1 * x1, axis=-1, keepdims=True)
    h2 = x1 * lax.rsqrt(var + RMS_EPS) * gffn_ref[...] * (1.0 + sc2) + sh2
    for c in range(D_MODEL // LANES):
        h2_ref[pl.ds(c, tm, stride=SUBLANES), :] = h2[:, c * LANES:(c + 1) * LANES]
    hi = h2.astype(BF16)
    lo = (h2 - hi.astype(F32)).astype(BF16)
    lg_ref[...] = (_dot(hi, wr_hi_ref[...]) + _dot(hi, wr_lo_ref[...]) + _dot(lo, wr_hi_ref[...])
                   + br_ref[...])


def _outproj_call(oa2, ob2, sga, sgb, x2, mod3, g_ffn, woa, wob, wo, wr_hi, wr_lo, br, seq):
    n = x2.shape[0]
    tpb = seq // TM_OUT
    row = lambda w: pl.BlockSpec((TM_OUT, w), lambda i: (i, 0))
    full = lambda a: pl.BlockSpec(a.shape, lambda i: (0,) * a.ndim)
    return pl.pallas_call(
        _outproj_kernel,
        grid=(n // TM_OUT,),
        in_specs=[row(512), row(512), row(1024), row(1024), row(D_MODEL),
                  pl.BlockSpec((1, 6, D_MODEL), lambda i: (i // tpb, 0, 0)),
                  full(g_ffn), full(woa), full(wob), full(wo), full(wr_hi), full(wr_lo), full(br)],
        out_specs=[row(D_MODEL),
                   pl.BlockSpec((TM_OUT * SUBLANES, LANES), lambda i: (i, 0)),
                   row(LANES)],
        out_shape=[jax.ShapeDtypeStruct((n, D_MODEL), F32),
                   jax.ShapeDtypeStruct((n * SUBLANES, LANES), F32),
                   jax.ShapeDtypeStruct((n, LANES), F32)],
        compiler_params=_cparams(("arbitrary",)),
        name="outproj",
    )(oa2, ob2, sga, sgb, x2, mod3, g_ffn, woa, wob, wo, wr_hi, wr_lo, br)


def _route_kernel(lg_ref, rt_ref):
    tm = TM_ROUTE
    lg = lg_ref[...]
    lane = lax.broadcasted_iota(I32, (tm, LANES), 1)
    big = jnp.int32(LANES)
    ninf = -jnp.inf
    gmask = (lane >= N_EXPERTS) & (lane < N_EXPERTS + N_GROUPS)
    g = jnp.where(gmask, lg, ninf)
    gmax = jnp.max(g, axis=-1, keepdims=True)
    grp = jnp.min(jnp.where(g == gmax, lane - N_EXPERTS, big), axis=-1, keepdims=True)
    p_grp = 1.0 / jnp.sum(jnp.where(gmask, jnp.exp(lg - gmax), 0.0), axis=-1, keepdims=True)
    lo = grp * EXPERTS_PER_GROUP
    emask = (lane >= lo) & (lane < lo + EXPERTS_PER_GROUP)
    ev = jnp.where(emask, lg, ninf)
    v0 = jnp.max(ev, axis=-1, keepdims=True)
    i0 = jnp.min(jnp.where(emask & (ev == v0), lane, big), axis=-1, keepdims=True)
    rest = emask & (lane != i0)
    ev1 = jnp.where(rest, lg, ninf)
    v1 = jnp.max(ev1, axis=-1, keepdims=True)
    i1 = jnp.min(jnp.where(rest & (ev1 == v1), lane, big), axis=-1, keepdims=True)
    e1 = jnp.exp(v1 - v0)
    w0 = p_grp / (1.0 + e1)
    w1 = p_grp * e1 / (1.0 + e1)
    rt_ref[...] = jnp.where(lane == 0, i0.astype(F32),
                            jnp.where(lane == 1, i1.astype(F32),
                                      jnp.where(lane == 2, w0, jnp.where(lane == 3, w1, 0.0))))


def _route_call(lg):
    n = lg.shape[0]
    spec = pl.BlockSpec((TM_ROUTE, LANES), lambda i: (i, 0))
    return pl.pallas_call(
        _route_kernel, grid=(n // TM_ROUTE,), in_specs=[spec], out_specs=spec,
        out_shape=jax.ShapeDtypeStruct((n, LANES), F32),
        compiler_params=_cparams(("arbitrary",)), name="route",
    )(lg)


def _expert_onehots(rt, tm):
    lane = lax.broadcasted_iota(I32, (tm, LANES), 1)
    lane_f = lane.astype(F32)
    e0 = jnp.sum(jnp.where(lane == 0, rt, 0.0), axis=-1, keepdims=True)
    e1 = jnp.sum(jnp.where(lane == 1, rt, 0.0), axis=-1, keepdims=True)
    return lane, lane_f == e0, lane_f == e1


def _rank_kernel(rt_ref, rk_ref, cnt_ref, carry_scr):
    tm = TM_ROUTE

    @pl.when(pl.program_id(0) == 0)
    def _():
        carry_scr[...] = jnp.zeros_like(carry_scr)

    lane, is0, is1 = _expert_onehots(rt_ref[...], tm)
    hits = jnp.where(is0, 1.0, 0.0) + jnp.where(is1, 1.0, 0.0)
    rr = lax.broadcasted_iota(I32, (tm, tm), 0)
    cc = lax.broadcasted_iota(I32, (tm, tm), 1)
    before = jnp.where(cc < rr, 1.0, 0.0).astype(BF16)
    seen = _dot(before, hits.astype(BF16)) + carry_scr[...]
    r0 = jnp.sum(jnp.where(is0, seen, 0.0), axis=-1, keepdims=True)
    r1 = jnp.sum(jnp.where(is1, seen, 0.0), axis=-1, keepdims=True)
    rk_ref[...] = jnp.where(lane == 0, r0, jnp.where(lane == 1, r1, 0.0))
    carry_scr[...] = carry_scr[...] + jnp.sum(hits, axis=0, keepdims=True)
    cnt_ref[...] = jnp.broadcast_to(carry_scr[...], cnt_ref.shape)


def _rank_call(rt):
    n = rt.shape[0]
    spec = pl.BlockSpec((TM_ROUTE, LANES), lambda i: (i, 0))
    return pl.pallas_call(
        _rank_kernel, grid=(n // TM_ROUTE,), in_specs=[spec],
        out_specs=[spec, pl.BlockSpec((SUBLANES, LANES), lambda i: (0, 0))],
        out_shape=[jax.ShapeDtypeStruct((n, LANES), F32), jax.ShapeDtypeStruct((SUBLANES, LANES), F32)],
        scratch_shapes=[pltpu.VMEM((1, LANES), F32)],
        compiler_params=_cparams(("arbitrary",)), name="rank",
    )(rt)


def _dest_kernel(rt_ref, rk_ref, cnt_ref, dd_ref, be_ref, nu_ref):
    tm = TM_ROUTE
    lane, is0, is1 = _expert_onehots(rt_ref[...], tm)
    blocks = jnp.floor((cnt_ref[...] + (BLK_E - 1)) * (1.0 / BLK_E))
    er = lax.broadcasted_iota(I32, (LANES, LANES), 0)
    ec = lax.broadcasted_iota(I32, (LANES, LANES), 1)
    upto = jnp.where(er <= ec, 1.0, 0.0).astype(BF16)
    bend = _dot(blocks.astype(BF16), upto)
    pstart = (bend[0:1, :] - blocks[0:1, :]) * BLK_E
    rk = rk_ref[...]
    r0 = jnp.sum(jnp.where(lane == 0, rk, 0.0), axis=-1, keepdims=True)
    r1 = jnp.sum(jnp.where(lane == 1, rk, 0.0), axis=-1, keepdims=True)
    d0 = jnp.sum(jnp.where(is0, pstart, 0.0), axis=-1, keepdims=True) + r0
    d1 = jnp.sum(jnp.where(is1, pstart, 0.0), axis=-1, keepdims=True) + r1
    dd_ref[...] = jnp.where(lane == 0, d0, jnp.where(lane == 1, d1, 0.0)).astype(I32)
    nb = be_ref.shape[0]
    blk = lax.broadcasted_iota(I32, (nb, LANES), 0).astype(F32)
    lane_b = lax.broadcasted_iota(I32, (nb, LANES), 1)
    ended = jnp.where((lane_b < N_EXPERTS) & (bend[0:1, :] <= blk), 1.0, 0.0)
    be = jnp.minimum(jnp.sum(ended, axis=-1, keepdims=True), float(N_EXPERTS - 1))
    be_ref[...] = jnp.broadcast_to(be, be_ref.shape).astype(I32)
    lane_c = lax.broadcasted_iota(I32, (SUBLANES, LANES), 1)
    used = jnp.sum(jnp.where(lane_c == N_EXPERTS - 1, bend, 0.0), axis=-1, keepdims=True)
    nu_ref[...] = jnp.broadcast_to(used, nu_ref.shape).astype(I32)


def _dest_call(rt, rk, cnt, n_blocks):
    n = rt.shape[0]
    nb_pad = -(-n_blocks // SUBLANES) * SUBLANES
    spec = pl.BlockSpec((TM_ROUTE, LANES), lambda i: (i, 0))
    const = lambda rows: pl.BlockSpec((rows, LANES), lambda i: (0, 0))
    return pl.pallas_call(
        _dest_kernel, grid=(n // TM_ROUTE,), in_specs=[spec, spec, const(SUBLANES)],
        out_specs=[spec, const(nb_pad), const(SUBLANES)],
        out_shape=[jax.ShapeDtypeStruct((n, LANES), I32), jax.ShapeDtypeStruct((nb_pad, LANES), I32),
                   jax.ShapeDtypeStruct((SUBLANES, LANES), I32)],
        compiler_params=_cparams(("arbitrary",)), name="dest",
    )(rt, rk, cnt)


def _slab_copy(src, src_row, dst, dst_row, sem):
    return pltpu.make_async_copy(src.at[pl.ds(pl.multiple_of(src_row * SUBLANES, SUBLANES), SUBLANES)],
                                 dst.at[pl.ds(pl.multiple_of(dst_row * SUBLANES, SUBLANES), SUBLANES)], sem)


def _dispatch_kernel(dest_ref, h2_ref, xs_in_hbm, xs_hbm, sem):
    del xs_in_hbm
    tm = TM_DISPATCH
    base = pl.program_id(0) * tm

    def issue(r, _):
        for k in range(2):
            _slab_copy(h2_ref, r, xs_hbm, dest_ref[2 * (base + r) + k], sem).start()
        return 0

    lax.fori_loop(0, tm, issue, 0, unroll=GATHER_UNROLL)
    for _ in range(2):
        pltpu.make_async_copy(h2_ref, xs_hbm.at[pl.ds(0, tm * SUBLANES)], sem).wait()


def _dispatch_call(dest, h2s, xs_zero):
    n = h2s.shape[0] // SUBLANES
    anyspec = pl.BlockSpec(memory_space=pl.ANY)
    return pl.pallas_call(
        _dispatch_kernel,
        grid_spec=pltpu.PrefetchScalarGridSpec(
            num_scalar_prefetch=1, grid=(n // TM_DISPATCH,),
            in_specs=[pl.BlockSpec((TM_DISPATCH * SUBLANES, LANES), lambda i, d: (i, 0)), anyspec],
            out_specs=anyspec,
            scratch_shapes=[pltpu.SemaphoreType.DMA(())]),
        out_shape=jax.ShapeDtypeStruct(xs_zero.shape, F32),
        input_output_aliases={2: 0},
        compiler_params=_cparams(("arbitrary",)),
        name="dispatch",
    )(dest, h2s, xs_zero)


def _experts_kernel(be_ref, nused_ref, xs_ref, wg_ref, wu_ref, wd_ref, yb_ref, x_scr, wg_scr, wu_scr, wd_scr):
    i = pl.program_id(0)

    @pl.when(i >= nused_ref[0])
    def _():
        yb_ref[...] = jnp.zeros_like(yb_ref)

    @pl.when(i < nused_ref[0])
    def _():
        prev = be_ref[jnp.maximum(i - 1, 0)]

        @pl.when((i == 0) | (be_ref[i] != prev))
        def _():
            wg_scr[...] = wg_ref[0].astype(BF16)
            wu_scr[...] = wu_ref[0].astype(BF16)
            wd_scr[...] = wd_ref[0].astype(BF16)

        for c in range(D_MODEL // LANES):
            x_scr[:, c * LANES:(c + 1) * LANES] = xs_ref[pl.ds(c, BLK_E, stride=SUBLANES), :].astype(BF16)
        xb = x_scr[...]
        g = _dot(xb, wg_scr[...])
        u = _dot(xb, wu_scr[...])
        hid = (g * _sigmoid(g) * u).astype(BF16)
        y = _dot(hid, wd_scr[...])
        for c in range(D_MODEL // LANES):
            yb_ref[pl.ds(c, BLK_E, stride=SUBLANES), :] = y[:, c * LANES:(c + 1) * LANES]


def _experts_call(block_e, n_used, xs, w_gate, w_up, w_down):
    n_blocks = block_e.shape[0]
    slab = pl.BlockSpec((BLK_E * SUBLANES, LANES), lambda i, be, nu: (i, 0))
    return pl.pallas_call(
        _experts_kernel,
        grid_spec=pltpu.PrefetchScalarGridSpec(
            num_scalar_prefetch=2, grid=(n_blocks,),
            in_specs=[slab,
                      pl.BlockSpec((1, D_MODEL, D_EXPERT), lambda i, be, nu: (be[i], 0, 0)),
                      pl.BlockSpec((1, D_MODEL, D_EXPERT), lambda i, be, nu: (be[i], 0, 0)),
                      pl.BlockSpec((1, D_EXPERT, D_MODEL), lambda i, be, nu: (be[i], 0, 0))],
            out_specs=slab,
            scratch_shapes=[pltpu.VMEM((BLK_E, D_MODEL), BF16),
                            pltpu.VMEM((D_MODEL, D_EXPERT), BF16),
                            pltpu.VMEM((D_MODEL, D_EXPERT), BF16),
                            pltpu.VMEM((D_EXPERT, D_MODEL), BF16)]),
        out_shape=jax.ShapeDtypeStruct(xs.shape, F32),
        compiler_params=_cparams(("arbitrary",)),
        name="experts",
    )(block_e, n_used, xs, w_gate, w_up, w_down)


def _final_kernel(dest_ref, x1_ref, rt_ref, mod_ref, gfin_ref, yb_hbm, o_ref, g_scr, sem):
    tm = TM_FINAL
    step = pl.program_id(0)
    slot = step % 2
    slot_rows = 2 * tm * SUBLANES

    def gather(s, into):
        def issue(r, _):
            tok = s * tm + r
            for k in range(2):
                _slab_copy(yb_hbm, dest_ref[2 * tok + k], g_scr.at[into], k * tm + r, sem.at[into]).start()
            return 0
        lax.fori_loop(0, tm, issue, 0, unroll=GATHER_UNROLL)

    @pl.when(step == 0)
    def _():
        gather(0, 0)

    @pl.when(step + 1 < pl.num_programs(0))
    def _():
        gather(step + 1, 1 - slot)

    g_now = g_scr.at[slot]
    pltpu.make_async_copy(yb_hbm.at[pl.ds(0, slot_rows)], g_now, sem.at[slot]).wait()

    rt = rt_ref[...]
    lane = lax.broadcasted_iota(I32, (tm, LANES), 1)
    gw0 = jnp.sum(jnp.where(lane == 2, rt, 0.0), axis=-1, keepdims=True)
    gw1 = jnp.sum(jnp.where(lane == 3, rt, 0.0), axis=-1, keepdims=True)
    gt2 = mod_ref[0, 5:6, :]
    x1 = x1_ref[...]
    cols = []
    for c in range(D_MODEL // LANES):
        y0 = g_now[pl.ds(c, tm, stride=SUBLANES), :]
        y1 = g_now[pl.ds(tm * SUBLANES + c, tm, stride=SUBLANES), :]
        y = gw0 * y0 + gw1 * y1
        cols.append(x1[:, c * LANES:(c + 1) * LANES] + gt2[:, c * LANES:(c + 1) * LANES] * y)
    x2 = jnp.concatenate(cols, axis=1)
    var = jnp.mean(x2 * x2, axis=-1, keepdims=True)
    o_ref[...] = x2 * lax.rsqrt(var + RMS_EPS) * gfin_ref[...]


def _final_call(dest, x1, rt, mod3, g_final, yb, seq):
    n = x1.shape[0]
    tpb = seq // TM_FINAL
    return pl.pallas_call(
        _final_kernel,
        grid_spec=pltpu.PrefetchScalarGridSpec(
            num_scalar_prefetch=1, grid=(n // TM_FINAL,),
            in_specs=[pl.BlockSpec((TM_FINAL, D_MODEL), lambda i, d: (i, 0)),
                      pl.BlockSpec((TM_FINAL, LANES), lambda i, d: (i, 0)),
                      pl.BlockSpec((1, 6, D_MODEL), lambda i, d: (i // tpb, 0, 0)),
                      pl.BlockSpec((1, D_MODEL), lambda i, d: (0, 0)),
                      pl.BlockSpec(memory_space=pl.ANY)],
            out_specs=pl.BlockSpec((TM_FINAL, D_MODEL), lambda i, d: (i, 0)),
            scratch_shapes=[pltpu.VMEM((2, 2 * TM_FINAL * SUBLANES, LANES), F32),
                            pltpu.SemaphoreType.DMA((2,))]),
        out_shape=jax.ShapeDtypeStruct((n, D_MODEL), F32),
        compiler_params=_cparams(("arbitrary",)),
        name="final",
    )(dest, x1, rt, mod3, g_final, yb)


def _permute_w_in(w):
    o = np.cumsum([0, 512, 64, 64, 512, 64, 8, 512, 512, 512, 8, 1024, 1024])
    qa, ka, va, qi, ki, wi, qb, kb, vb, fb, ga, gb = [w[:, o[k]:o[k + 1]] for k in range(12)]
    aux = jnp.concatenate([wi, fb, jnp.zeros((w.shape[0], LANES - 2 * N_HEADS), w.dtype)], axis=1)
    return jnp.concatenate([qa, qi, qb, kb, vb, ga, gb, ka, ka, va, va, ki, ki, aux], axis=1).astype(BF16)


def _layer(x2, pos2, mod3, batch, seq, g_mix, w_in, b_forget, w_out_a, w_out_b, w_out, g_ffn,
           w_group, b_group, w_router, b_router, w_e_gate, w_e_up, w_e_down, g_final):
    n = x2.shape[0]
    inv_freq = ROPE_THETA ** (-jnp.arange(0, ROPE_DIM, 2, dtype=F32) / ROPE_DIM)
    jj = np.arange(LANES) % HEAD_DIM
    invf = jnp.where(jj < ROPE_DIM, inv_freq[jj % (ROPE_DIM // 2)], 0.0)[None, :].astype(F32)
    fbias = jnp.zeros((1, LANES), F32).at[0, N_HEADS:2 * N_HEADS].set(b_forget.astype(F32))

    (qa, qi, qb, kaug, vb, sga, sgb, ka2, va2, ki2, aux) = _inproj_call(
        x2, pos2, mod3, g_mix.reshape(1, D_MODEL), _permute_w_in(w_in), invf, fbias, batch, seq)

    r3 = lambda a: a.reshape(batch, seq, a.shape[-1])
    tr = lambda a: jnp.swapaxes(r3(a), 1, 2)
    tr_tiles = lambda a, tk: jnp.swapaxes(a.reshape(batch, seq // tk, tk, a.shape[-1]), 2, 3)
    aux_t = tr(aux[:, :4 * N_HEADS])
    obt = _fox_call(tr(qb), aux_t, r3(kaug), tr_tiles(vb, TK_FOX))
    oat = _dsa_call(tr(qa), tr(qi), aux_t, r3(ki2), r3(ka2), tr_tiles(va2[:, :HEAD_DIM], TK_DSA))
    oa = jnp.swapaxes(oat, 1, 2).reshape(n, 512)
    ob = jnp.swapaxes(obt, 1, 2).reshape(n, 512)

    w_rt = jnp.concatenate([w_router, w_group, jnp.zeros((D_MODEL, LANES - N_EXPERTS - N_GROUPS), F32)], axis=1)
    wr_hi = w_rt.astype(BF16)
    wr_lo = (w_rt - wr_hi.astype(F32)).astype(BF16)
    br = jnp.concatenate([b_router, b_group, jnp.zeros((LANES - N_EXPERTS - N_GROUPS,), F32)])[None, :]
    x1, h2s, lg = _outproj_call(oa, ob, sga, sgb, x2, mod3,
                                g_ffn.reshape(1, D_MODEL), w_out_a.astype(BF16), w_out_b.astype(BF16),
                                w_out.astype(BF16), wr_hi, wr_lo, br, seq)
    rt = _route_call(lg)

    n_slots = 2 * n + N_EXPERTS * BLK_E
    n_blocks = n_slots // BLK_E
    rk, cnt = _rank_call(rt)
    dd, be, nu = _dest_call(rt, rk, cnt, n_blocks)
    dest = dd[:, 0:2].reshape(-1)
    block_e = be[:n_blocks, 0]
    n_used = nu[0, 0:1]

    xs = _dispatch_call(dest, h2s, jnp.zeros((n_slots * SUBLANES, LANES), F32))
    yb = _experts_call(block_e, n_used, xs, w_e_gate, w_e_up, w_e_down)
    return _final_call(dest, x1, rt, mod3, g_final.reshape(1, D_MODEL), yb, seq)


def kernel(x, c, positions, w_mod, b_mod, g_mix, w_in, b_forget, w_out_a, w_out_b, w_out, g_ffn, w_group,
           b_group, w_router, b_router, w_e_gate, w_e_up, w_e_down, g_final):
    batch, seq, d = x.shape
    depth = w_mod.shape[0]
    assert depth == 1 and d == D_MODEL, "kernel fuses the final norm into the single layer"
    n = batch * seq
    c8 = jnp.zeros((8, d), F32).at[:batch].set(c)
    mod = _mod_call(c8, w_mod[0], b_mod[0][None, :])
    mod3 = mod[:batch].reshape(batch, 6, d)
    out = _layer(x.reshape(n, d), positions.reshape(n, 1), mod3, batch, seq, g_mix[0], w_in[0], b_forget[0],
                 w_out_a[0], w_out_b[0], w_out[0], g_ffn[0], w_group[0], b_group[0], w_router[0], b_router[0],
                 w_e_gate[0], w_e_up[0], w_e_down[0], g_final)
    return out.reshape(batch, seq, d)
```

```python
import functools

import numpy as np
import jax
import jax.numpy as jnp
from jax import lax
from jax.experimental import pallas as pl
from jax.experimental.pallas import tpu as pltpu

F32 = jnp.float32
BF16 = jnp.bfloat16
I32 = jnp.int32

D_MODEL = 1024
HEAD_DIM = 64
N_HEADS = 8
CHUNK_SHIFT = 6
TOPK = 256
ROPE_DIM = 16
ROPE_THETA = 500000.0
N_GROUPS = 4
EXPERTS_PER_GROUP = 8
N_EXPERTS = 32
D_EXPERT = 512
RMS_EPS = 1e-6
NEG_INF = -1e30
ATTN_SCALE = HEAD_DIM ** -0.5
IDX_SCALE = HEAD_DIM ** -0.5
LOG2E = 1.4426950408889634

LANES = 128
SUBLANES = 8
VMEM_LIMIT = 56 * 1024 * 1024

TM_IN = 512
TQ_FOX = 512
TK_FOX = 512
TQ_DSA = 256
TK_DSA = 512
TM_OUT = 512
TM_ROUTE = 1024
TM_DISPATCH = 1024
BLK_E = 256
TM_FINAL = 256
GATHER_UNROLL = 8

RADIX_BLOCK = 32 * SUBLANES
BOUND_MARGIN = 1.02
BOUND_SAFE = 40.0
FOX_BIAS_SLACK = 1.0
FOX_K = 256
FOX_AUG_ROWS = 16

_SEG_QA, _SEG_QI, _SEG_QB, _SEG_KB, _SEG_VB = 0, 512, 1024, 1536, 2048
_SEG_GA, _SEG_GB = 2560, 3584
_SEG_KA, _SEG_VA, _SEG_KI, _SEG_AUX = 4608, 4736, 4864, 4992
_NC_IN = 5120

_SENT_KEY = int(np.array(NEG_INF, np.float32).view(np.int32) ^ 0x7FFFFFFF)
_INT_MIN = -(2 ** 31)


def _cparams(sem):
    return pltpu.CompilerParams(dimension_semantics=sem, vmem_limit_bytes=VMEM_LIMIT)


def _dot(a, b):
    return jnp.dot(a, b, preferred_element_type=F32)


def _sigmoid(x):
    return 1.0 / (1.0 + jnp.exp(-x))


def _mod_kernel(c_ref, w_ref, b_ref, o_ref):
    c = c_ref[...]
    ca = c * _sigmoid(c)
    o_ref[...] = _dot(ca.astype(BF16), w_ref[...].astype(BF16)) + b_ref[...]


def _mod_call(c8, w_mod, b_mod):
    n_out = w_mod.shape[1]
    tn = 1024
    return pl.pallas_call(
        _mod_kernel,
        grid=(n_out // tn,),
        in_specs=[pl.BlockSpec((8, D_MODEL), lambda j: (0, 0)),
                  pl.BlockSpec((D_MODEL, tn), lambda j: (0, j)),
                  pl.BlockSpec((1, tn), lambda j: (0, j))],
        out_specs=pl.BlockSpec((8, tn), lambda j: (0, j)),
        out_shape=jax.ShapeDtypeStruct((8, n_out), F32),
        compiler_params=_cparams(("arbitrary",)),
        name="mod",
    )(c8, w_mod, b_mod)


def _inproj_kernel(x_ref, pos_ref, mod_ref, g_ref, w_ref, invf_ref, fbias_ref, sel_ref, ones_ref,
                   qa_ref, qi_ref, qb_ref, kb_ref, vb_ref, sga_ref, sgb_ref,
                   ka_ref, va_ref, ki_ref, aux_ref, h_scr, carry_scr):
    tm = TM_IN
    x = x_ref[...]
    var = jnp.mean(x * x, axis=-1, keepdims=True)
    tf = x * lax.rsqrt(var + RMS_EPS) * g_ref[...]
    sh = mod_ref[0, 0:1, :]
    sc = mod_ref[0, 1:2, :]
    h_scr[...] = (tf * (1.0 + sc) + sh).astype(BF16)
    hb = h_scr[...]

    lane = lax.broadcasted_iota(I32, (tm, LANES), 1)
    j = lane & (HEAD_DIM - 1)
    ang = pos_ref[...].astype(F32) * invf_ref[...]
    cs = jnp.cos(ang)
    sn = jnp.sin(ang)
    coef_next = jnp.where(j < ROPE_DIM // 2, -sn, 0.0)
    coef_prev = jnp.where((j >= ROPE_DIM // 2) & (j < ROPE_DIM), sn, 0.0)

    def rope(tc):
        return (tc * cs + pltpu.roll(tc, LANES - ROPE_DIM // 2, 1) * coef_next
                + pltpu.roll(tc, ROPE_DIM // 2, 1) * coef_prev)

    def seg(off, width):
        return _dot(hb, w_ref[:, off:off + width])

    t = seg(_SEG_QA, 512)
    for c in range(4):
        qa_ref[:, c * LANES:(c + 1) * LANES] = (
            rope(t[:, c * LANES:(c + 1) * LANES]) * (ATTN_SCALE * LOG2E)).astype(BF16)
    t = seg(_SEG_QI, 512)
    for c in range(4):
        qi_ref[:, c * LANES:(c + 1) * LANES] = (rope(t[:, c * LANES:(c + 1) * LANES]) * IDX_SCALE).astype(BF16)
    qb_ref[...] = (seg(_SEG_QB, 512) * (ATTN_SCALE * LOG2E)).astype(BF16)
    vb_ref[...] = seg(_SEG_VB, 512).astype(BF16)
    sga_ref[...] = _sigmoid(seg(_SEG_GA, 1024)).astype(BF16)
    sgb_ref[...] = _sigmoid(seg(_SEG_GB, 1024)).astype(BF16)
    ka = rope(seg(_SEG_KA, LANES))
    ka_ref[...] = ka.astype(BF16)
    va_ref[...] = seg(_SEG_VA, LANES).astype(BF16)
    ki_ref[...] = rope(seg(_SEG_KI, LANES)).astype(BF16)

    z = seg(_SEG_AUX, LANES)
    zf = z + fbias_ref[...]
    logf = jnp.minimum(zf, 0.0) - jnp.log(1.0 + jnp.exp(-jnp.abs(zf)))
    is_f = (lane >= N_HEADS) & (lane < 2 * N_HEADS)
    logf = jnp.where(is_f, logf, 0.0)

    def split3(v):
        hi = v.astype(BF16)
        r1 = v - hi.astype(F32)
        mid = r1.astype(BF16)
        return hi, mid, (r1 - mid.astype(F32)).astype(BF16)

    rr = lax.broadcasted_iota(I32, (tm, tm), 0)
    cc = lax.broadcasted_iota(I32, (tm, tm), 1)
    tri = jnp.where(cc <= rr, 1.0, 0.0).astype(BF16)
    p_hi, p_mid, p_lo = split3(logf)

    @pl.when(pl.program_id(1) == 0)
    def _():
        carry_scr[...] = jnp.zeros_like(carry_scr)

    parts = _dot(tri, jnp.concatenate([p_hi, p_mid, p_lo], axis=1))
    cum = parts[:, 0:LANES] + parts[:, LANES:2 * LANES] + parts[:, 2 * LANES:3 * LANES] + carry_scr[...]
    carry_scr[...] = cum[tm - 1:tm, :]
    cum2 = cum * LOG2E
    aux = jnp.where(lane < N_HEADS, z * (N_HEADS ** -0.5), jnp.where(is_f, cum2, 0.0))

    kb = seg(_SEG_KB, 512)
    first = lane < HEAD_DIM

    def half_norms(blk):
        sq = blk * blk
        return (jnp.sum(jnp.where(first, sq, 0.0), axis=-1, keepdims=True),
                jnp.sum(jnp.where(first, 0.0, sq), axis=-1, keepdims=True))

    for p in range(N_HEADS // 2):
        n_even, n_odd = half_norms(kb[:, p * LANES:(p + 1) * LANES])
        aux = jnp.where(lane == 2 * N_HEADS + 2 * p, n_even, jnp.where(lane == 2 * N_HEADS + 2 * p + 1, n_odd, aux))
    aux_ref[...] = jnp.where(lane == 3 * N_HEADS, half_norms(ka)[0], aux)

    c_hi, c_mid, c_lo = split3(cum2)
    kaug = _dot(jnp.concatenate([c_hi, c_mid, c_lo], axis=1), sel_ref[...]) + ones_ref[...]
    for p in range(N_HEADS // 2):
        kb_ref[:, p * FOX_K:p * FOX_K + LANES] = kb[:, p * LANES:(p + 1) * LANES].astype(BF16)
        kb_ref[:, p * FOX_K + LANES:(p + 1) * FOX_K] = kaug[:, p * LANES:(p + 1) * LANES].astype(BF16)


def _fox_routing_constants():
    sel = np.zeros((3, LANES, (N_HEADS // 2) * LANES), np.float32)
    ones = np.zeros((1, (N_HEADS // 2) * LANES), np.float32)
    for h in range(N_HEADS):
        for t in range(3):
            sel[t, N_HEADS + h, (h // 2) * LANES + 3 + 3 * (h % 2) + t] = -1.0
    for p in range(N_HEADS // 2):
        ones[0, p * LANES:p * LANES + 3] = 1.0
    return jnp.asarray(sel.reshape(3 * LANES, -1), BF16), jnp.asarray(ones, F32)


def _inproj_call(x2, pos2, mod3, g_mix, w_in_p, invf, fbias, batch, seq):
    n = x2.shape[0]
    tpb = seq // TM_IN
    row = lambda b, s: (b * tpb + s, 0)
    bf = lambda w: jax.ShapeDtypeStruct((n, w), BF16)
    sel, ones = _fox_routing_constants()
    out_shapes = [bf(512), bf(512), bf(512), bf((N_HEADS // 2) * FOX_K), bf(512), bf(1024), bf(1024),
                  bf(LANES), bf(LANES), bf(LANES), jax.ShapeDtypeStruct((n, LANES), F32)]
    out_specs = [pl.BlockSpec((TM_IN, s.shape[1]), row) for s in out_shapes]
    return pl.pallas_call(
        _inproj_kernel,
        grid=(batch, tpb),
        in_specs=[pl.BlockSpec((TM_IN, D_MODEL), row),
                  pl.BlockSpec((TM_IN, 1), row),
                  pl.BlockSpec((1, 6, D_MODEL), lambda b, s: (b, 0, 0)),
                  pl.BlockSpec((1, D_MODEL), lambda b, s: (0, 0)),
                  pl.BlockSpec((D_MODEL, _NC_IN), lambda b, s: (0, 0)),
                  pl.BlockSpec((1, LANES), lambda b, s: (0, 0)),
                  pl.BlockSpec((1, LANES), lambda b, s: (0, 0)),
                  pl.BlockSpec(sel.shape, lambda b, s: (0, 0)),
                  pl.BlockSpec(ones.shape, lambda b, s: (0, 0))],
        out_specs=out_specs,
        out_shape=out_shapes,
        scratch_shapes=[pltpu.VMEM((TM_IN, D_MODEL), BF16), pltpu.VMEM((1, LANES), F32)],
        compiler_params=_cparams(("arbitrary", "arbitrary")),
        name="inproj",
    )(x2, pos2, mod3, g_mix, w_in_p, invf, fbias, sel, ones)


def _softmax_max(s, m_ref):
    m_ref[...] = jnp.maximum(m_ref[...], jnp.max(s, axis=0, keepdims=True))


def _softmax_accumulate(s, m_ref, l_ref, acc_ref, v_t):
    p = jnp.exp2(s - m_ref[...])
    l_ref[...] = l_ref[...] + jnp.sum(p, axis=0, keepdims=True)
    acc_ref[...] = acc_ref[...] + _dot(v_t, p.astype(BF16))


def _softmax_scratch(tq):
    return ([pltpu.VMEM((1, tq), F32) for _ in range(2 * N_HEADS)]
            + [pltpu.VMEM((HEAD_DIM, tq), F32) for _ in range(N_HEADS)])


def _softmax_split(refs):
    return refs[:N_HEADS], refs[N_HEADS:2 * N_HEADS], refs[2 * N_HEADS:3 * N_HEADS]


def _softmax_init(m_refs, l_refs, acc_refs):
    for h in range(N_HEADS):
        m_refs[h][...] = jnp.full(m_refs[h].shape, NEG_INF, F32)
        l_refs[h][...] = jnp.zeros(l_refs[h].shape, F32)
        acc_refs[h][...] = jnp.zeros(acc_refs[h].shape, F32)


def _softmax_finish(o_ref, l_refs, acc_refs):
    for h in range(N_HEADS):
        o_ref[0, h * HEAD_DIM:(h + 1) * HEAD_DIM, :] = (acc_refs[h][...] * (1.0 / l_refs[h][...])).astype(BF16)


def _logit_bounds(qt_ref, kn2, slack):
    k_max = jnp.sqrt(jnp.max(kn2, axis=-1, keepdims=True))
    bounds = []
    worst = jnp.float32(0.0)
    for h in range(N_HEADS):
        qh = qt_ref[0, h * HEAD_DIM:(h + 1) * HEAD_DIM, :].astype(F32)
        q_norm = jnp.sqrt(jnp.sum(qh * qh, axis=0, keepdims=True))
        b = q_norm * k_max[h:h + 1, :] * BOUND_MARGIN + slack
        bounds.append(b)
        worst = jnp.maximum(worst, jnp.max(b))
    return bounds, worst <= BOUND_SAFE


def _fox_kernel(qt_ref, cumt_ref, kn_ref, kaug_ref, vt_ref, o_ref, rhs_scr, *softmax_refs):
    m_scr, l_scr, acc_scr = _softmax_split(softmax_refs)
    tq, tk = TQ_FOX, TK_FOX
    i = pl.program_id(1)
    q0 = i * tq

    cq = cumt_ref[0]
    c_hi = cq.astype(BF16).astype(F32)
    c_r = cq - c_hi
    c_mid = c_r.astype(BF16).astype(F32)
    c_lo = c_r - c_mid
    row_q = lax.broadcasted_iota(I32, (LANES, tq), 0)
    row_a = lax.broadcasted_iota(I32, (FOX_AUG_ROWS, 2 * tq), 0)
    second = lax.broadcasted_iota(I32, (FOX_AUG_ROWS, 2 * tq), 1) >= tq
    for p in range(N_HEADS // 2):
        qp = qt_ref[0, p * LANES:(p + 1) * LANES, :]
        zq = jnp.zeros_like(qp)
        rhs_scr[p, 0:LANES, 0:tq] = jnp.where(row_q < HEAD_DIM, qp, zq)
        rhs_scr[p, 0:LANES, tq:2 * tq] = jnp.where(row_q >= HEAD_DIM, qp, zq)
        pair_row = lambda a: jnp.concatenate([a[2 * p:2 * p + 1, :], a[2 * p + 1:2 * p + 2, :]], axis=1)
        aug = jnp.where(row_a == 0, pair_row(c_hi),
              jnp.where(row_a == 1, pair_row(c_mid),
              jnp.where(row_a == 2, pair_row(c_lo),
              jnp.where((row_a >= 3) & (row_a < 6), jnp.where(second, 0.0, 1.0),
              jnp.where((row_a >= 6) & (row_a < 9), jnp.where(second, 1.0, 0.0), 0.0)))))
        rhs_scr[p, LANES:LANES + FOX_AUG_ROWS, :] = aug.astype(BF16)
        rhs_scr[p, LANES + FOX_AUG_ROWS:FOX_K, :] = jnp.zeros((FOX_K - LANES - FOX_AUG_ROWS, 2 * tq), BF16)

    _softmax_init(m_scr, l_scr, acc_scr)
    key_j = lax.broadcasted_iota(I32, (tk, tq), 0)
    qry_i = q0 + lax.broadcasted_iota(I32, (tk, tq), 1)

    def tile(kt, masked, second_pass):
        k0 = pl.multiple_of(kt * tk, tk)
        for p in range(N_HEADS // 2):
            st = _dot(kaug_ref[0, pl.ds(k0, tk), p * FOX_K:(p + 1) * FOX_K], rhs_scr[p])
            for hh in range(2):
                h = 2 * p + hh
                s = st[:, hh * tq:(hh + 1) * tq]
                if masked:
                    s = jnp.where(k0 + key_j <= qry_i, s, NEG_INF)
                if second_pass:
                    _softmax_accumulate(s, m_scr[h], l_scr[h], acc_scr[h],
                                        vt_ref[0, kt, h * HEAD_DIM:(h + 1) * HEAD_DIM, :])
                else:
                    _softmax_max(s, m_scr[h])

    n_full = i * (tq // tk)

    def sweep(second_pass):
        def full_tile(kt, carry):
            tile(kt, False, second_pass)
            return carry

        lax.fori_loop(0, n_full, full_tile, 0)
        for d in range(tq // tk):
            tile(n_full + d, True, second_pass)

    bounds, safe = _logit_bounds(qt_ref, kn_ref[0], FOX_BIAS_SLACK)

    @pl.when(safe)
    def _():
        for h in range(N_HEADS):
            m_scr[h][...] = bounds[h]
        sweep(True)

    @pl.when(jnp.logical_not(safe))
    def _():
        sweep(False)
        sweep(True)

    _softmax_finish(o_ref, l_scr, acc_scr)


def _fox_call(qbt, aux_t, kaug3, vbt4):
    batch, _, seq = qbt.shape
    return pl.pallas_call(
        _fox_kernel,
        grid=(batch, seq // TQ_FOX),
        in_specs=[pl.BlockSpec((1, 512, TQ_FOX), lambda b, i: (b, 0, i)),
                  pl.BlockSpec((1, N_HEADS, TQ_FOX), lambda b, i: (b, 1, i)),
                  pl.BlockSpec((1, N_HEADS, seq), lambda b, i: (b, 2, 0)),
                  pl.BlockSpec((1, seq, (N_HEADS // 2) * FOX_K), lambda b, i: (b, 0, 0)),
                  pl.BlockSpec((1, seq // TK_FOX, 512, TK_FOX), lambda b, i: (b, 0, 0, 0))],
        out_specs=pl.BlockSpec((1, 512, TQ_FOX), lambda b, i: (b, 0, i)),
        out_shape=jax.ShapeDtypeStruct((batch, 512, seq), BF16),
        scratch_shapes=[pltpu.VMEM((N_HEADS // 2, FOX_K, 2 * TQ_FOX), BF16)] + _softmax_scratch(TQ_FOX),
        compiler_params=_cparams(("arbitrary", "arbitrary")),
        name="fox",
    )(qbt, aux_t, aux_t, kaug3, vbt4)


def _bit_transpose32(words):
    a = list(words)
    j, mask = 16, 0x0000FFFF
    while j:
        mask_i = int(np.array(mask, np.uint32).view(np.int32))
        k = 0
        while k < 32:
            t = (a[k] ^ lax.shift_right_logical(a[k + j], j)) & mask_i
            a[k] = a[k] ^ t
            a[k + j] = a[k + j] ^ lax.shift_left(t, j)
            k = (k + j + 1) & ~j
        j >>= 1
        mask = (mask ^ (mask << j)) & 0xFFFFFFFF
    return a


def _dsa_kernel(qat_ref, qit_ref, wit_ref, kn_ref, ki_ref, ka_ref, vat_ref, o_ref,
                aqi_scr, aqa_scr, sc_scr, bias_scr, plane_scr, alive_scr, *softmax_refs, seq):
    m_scr, l_scr, acc_scr = _softmax_split(softmax_refs)
    tq, tk = TQ_DSA, TK_DSA
    blk_per_tile = tk // RADIX_BLOCK
    n_blocks = seq // RADIX_BLOCK
    i = pl.program_id(1)
    q0 = i * tq
    nk = lax.shift_right_logical(q0 + tq + tk - 1, int(np.log2(tk)))
    n_rest = (seq - nk * tk).astype(F32)

    row_q = lax.broadcasted_iota(I32, (LANES, tq), 0)
    for h in range(N_HEADS):
        keep = (row_q < HEAD_DIM) if h % 2 == 0 else (row_q >= HEAD_DIM)
        rows = slice((h // 2) * LANES, (h // 2 + 1) * LANES)
        qi_blk = qit_ref[0, rows, :]
        qa_blk = qat_ref[0, rows, :]
        aqi_scr[:, h * tq:(h + 1) * tq] = jnp.where(keep, qi_blk, jnp.zeros_like(qi_blk))
        aqa_scr[:, h * tq:(h + 1) * tq] = jnp.where(keep, qa_blk, jnp.zeros_like(qa_blk))

    w = wit_ref[0]
    key_j = lax.broadcasted_iota(I32, (tk, tq), 0)
    qry_chunk = lax.shift_right_logical(q0 + lax.broadcasted_iota(I32, (tk, tq), 1), CHUNK_SHIFT)

    def admissible(k0):
        return lax.shift_right_logical(k0 + key_j, CHUNK_SHIFT) <= qry_chunk

    def p1(kt, carry):
        k0 = pl.multiple_of(kt * tk, tk)
        rel = _dot(ki_ref[0, pl.ds(k0, tk), :], aqi_scr[...])
        sc = w[0:1, :] * jnp.maximum(rel[:, 0:tq], 0.0)
        for h in range(1, N_HEADS):
            sc = sc + w[h:h + 1, :] * jnp.maximum(rel[:, h * tq:(h + 1) * tq], 0.0)
        sc = jnp.where(admissible(k0), sc, NEG_INF)
        sc = jnp.where(sc == 0.0, 0.0, sc)
        sc_scr[kt] = sc
        b = pltpu.bitcast(sc, I32)
        key = b ^ (lax.shift_right_arithmetic(b, 31) & 0x7FFFFFFF)
        for half in range(blk_per_tile):
            base = half * RADIX_BLOCK
            planes = _bit_transpose32([key[base + SUBLANES * m:base + SUBLANES * (m + 1), :] ^ _INT_MIN
                                       for m in range(32)])
            bl = kt * blk_per_tile + half
            for p in range(32):
                plane_scr[bl, p] = planes[p]
            alive_scr[bl] = jnp.full((SUBLANES, tq), -1, I32)
        return carry

    lax.fori_loop(0, nk, p1, 0)

    def clear_block(bl, carry):
        plane_scr[bl] = jnp.zeros((32, SUBLANES, tq), I32)
        alive_scr[bl] = jnp.zeros((SUBLANES, tq), I32)
        return carry

    lax.fori_loop(nk * blk_per_tile, n_blocks, clear_block, 0)

    sent_u = jnp.int32(_SENT_KEY ^ _INT_MIN)

    def bit_step(p, carry):
        k_rem, thr_bits, rest_alive = carry
        shift = 31 - p
        sent_bit = lax.shift_right_logical(sent_u, shift) & 1
        ones = [alive_scr[bl] & plane_scr[bl, p] for bl in range(n_blocks)]
        tot = lax.population_count(ones[0])
        for bl in range(1, n_blocks):
            tot = tot + lax.population_count(ones[bl])
        cnt = jnp.sum(tot.astype(F32), axis=0, keepdims=True)
        cnt = cnt + jnp.where((rest_alive != 0) & (sent_bit != 0), n_rest, 0.0)
        take1 = cnt >= k_rem
        for bl in range(n_blocks):
            alive_scr[bl] = jnp.where(take1, ones[bl], alive_scr[bl] & ~plane_scr[bl, p])
        k_rem = jnp.where(take1, k_rem, k_rem - cnt)
        thr_bits = thr_bits | jnp.where(take1, lax.shift_left(jnp.int32(1), shift), 0)
        rest_alive = jnp.where(take1 == (sent_bit != 0), rest_alive, 0)
        return k_rem, thr_bits, rest_alive

    kf = float(TOPK)
    _, thr_bits, _ = lax.fori_loop(
        0, 32, bit_step,
        (jnp.full((1, tq), kf, F32), jnp.zeros((1, tq), I32), jnp.ones((1, tq), I32)))
    thr_key = thr_bits ^ _INT_MIN
    thr0 = pltpu.bitcast(thr_key ^ (lax.shift_right_arithmetic(thr_key, 31) & 0x7FFFFFFF), F32)

    def rank_counts(thr):
        def body(kt, c):
            sc = sc_scr[kt]
            return (c[0] + jnp.sum(jnp.where(sc >= thr, 1.0, 0.0), axis=0, keepdims=True),
                    c[1] + jnp.sum(jnp.where(sc > thr, 1.0, 0.0), axis=0, keepdims=True))
        z = jnp.zeros((1, tq), F32)
        n_ge, n_gt = lax.fori_loop(0, nk, body, (z, z))
        return (n_ge + jnp.where(thr <= NEG_INF, n_rest, 0.0), n_gt + jnp.where(thr < NEG_INF, n_rest, 0.0))

    def misplaced(state):
        _, n_ge, n_gt = state
        return jnp.max(jnp.where((n_ge < kf) | (n_gt >= kf), 1.0, 0.0)) > 0.0

    def step_threshold(state):
        thr, n_ge, n_gt = state

        def body(kt, c):
            sc = sc_scr[kt]
            below = jnp.max(jnp.where(sc < thr, sc, -jnp.inf), axis=0, keepdims=True)
            above = jnp.min(jnp.where(sc > thr, sc, jnp.inf), axis=0, keepdims=True)
            return jnp.maximum(c[0], below), jnp.minimum(c[1], above)

        below, above = lax.fori_loop(0, nk, body, (jnp.full((1, tq), -jnp.inf, F32), jnp.full((1, tq), jnp.inf, F32)))
        has_rest = n_rest > 0.0
        below = jnp.where(has_rest & (thr > NEG_INF), jnp.maximum(below, NEG_INF), below)
        above = jnp.where(has_rest & (thr < NEG_INF), jnp.minimum(above, NEG_INF), above)
        thr = jnp.where(n_ge < kf, below, jnp.where(n_gt >= kf, above, thr))
        return (thr,) + rank_counts(thr)

    thr, _, n_gt = lax.while_loop(misplaced, step_threshold, (thr0,) + rank_counts(thr0))
    need = kf - n_gt

    tr = lax.broadcasted_iota(I32, (tk, tk), 0)
    tc = lax.broadcasted_iota(I32, (tk, tk), 1)
    tri = jnp.where(tc <= tr, 1.0, 0.0).astype(BF16)
    _softmax_init(m_scr, l_scr, acc_scr)

    def selection_bias(kt, tie):
        sc = sc_scr[kt]
        eq = sc == thr
        pref = _dot(tri, jnp.where(eq, 1.0, 0.0).astype(BF16)) + tie
        sel = (sc > thr) | (eq & (pref <= need))
        return jnp.where(sel & admissible(kt * tk), 0.0, NEG_INF), pref[tk - 1:tk, :]

    def logits(kt):
        return _dot(ka_ref[0, pl.ds(pl.multiple_of(kt * tk, tk), tk), :], aqa_scr[...])

    def accumulate(kt, bias):
        logit = logits(kt)
        v_t = vat_ref[0, kt]
        for h in range(N_HEADS):
            _softmax_accumulate(logit[:, h * tq:(h + 1) * tq] + bias, m_scr[h], l_scr[h], acc_scr[h], v_t)

    def p3_single(kt, tie):
        bias, tie = selection_bias(kt, tie)
        accumulate(kt, bias)
        return tie

    def p3_max(kt, tie):
        bias, tie = selection_bias(kt, tie)
        bias_scr[kt] = bias
        logit = logits(kt)
        for h in range(N_HEADS):
            _softmax_max(logit[:, h * tq:(h + 1) * tq] + bias, m_scr[h])
        return tie

    def p3_accumulate(kt, carry):
        accumulate(kt, bias_scr[kt])
        return carry

    kn2 = jnp.broadcast_to(kn_ref[0, 0:1, :], (N_HEADS, seq))
    bounds, safe = _logit_bounds(qat_ref, kn2, 0.0)
    no_tie = jnp.zeros((1, tq), F32)

    @pl.when(safe)
    def _():
        for h in range(N_HEADS):
            m_scr[h][...] = bounds[h]
        lax.fori_loop(0, nk, p3_single, no_tie)

    @pl.when(jnp.logical_not(safe))
    def _():
        lax.fori_loop(0, nk, p3_max, no_tie)
        lax.fori_loop(0, nk, p3_accumulate, 0)

    _softmax_finish(o_ref, l_scr, acc_scr)


def _dsa_call(qat, qit, aux_t, ki3, ka3, vat4):
    batch, _, seq = qat.shape
    qspec = pl.BlockSpec((1, 512, TQ_DSA), lambda b, i: (b, 0, i))
    kspec = pl.BlockSpec((1, seq, LANES), lambda b, i: (b, 0, 0))
    return pl.pallas_call(
        functools.partial(_dsa_kernel, seq=seq),
        grid=(batch, seq // TQ_DSA),
        in_specs=[qspec, qspec,
                  pl.BlockSpec((1, N_HEADS, TQ_DSA), lambda b, i: (b, 0, i)),
                  pl.BlockSpec((1, N_HEADS, seq), lambda b, i: (b, 3, 0)),
                  kspec, kspec,
                  pl.BlockSpec((1, seq // TK_DSA, HEAD_DIM, TK_DSA), lambda b, i: (b, 0, 0, 0))],
        out_specs=qspec,
        out_shape=jax.ShapeDtypeStruct((batch, 512, seq), BF16),
        scratch_shapes=[pltpu.VMEM((LANES, N_HEADS * TQ_DSA), BF16),
                        pltpu.VMEM((LANES, N_HEADS * TQ_DSA), BF16),
                        pltpu.VMEM((seq // TK_DSA, TK_DSA, TQ_DSA), F32),
                        pltpu.VMEM((seq // TK_DSA, TK_DSA, TQ_DSA), F32),
                        pltpu.VMEM((seq // RADIX_BLOCK, 32, SUBLANES, TQ_DSA), I32),
                        pltpu.VMEM((seq // RADIX_BLOCK, SUBLANES, TQ_DSA), I32)] + _softmax_scratch(TQ_DSA),
        compiler_params=_cparams(("arbitrary", "arbitrary")),
        name="dsa",
    )(qat, qit, aux_t, aux_t, ki3, ka3, vat4)


def _outproj_kernel(oa_ref, ob_ref, sga_ref, sgb_ref, x_ref, mod_ref, gffn_ref,
                    woa_ref, wob_ref, wo_ref, wr_hi_ref, wr_lo_ref, br_ref,
                    x1_ref, h2_ref, lg_ref):
    tm = TM_OUT
    ya = _dot(oa_ref[...], woa_ref[...])
    yb = _dot(ob_ref[...], wob_ref[...])
    merged = (sga_ref[...].astype(F32) * ya + sgb_ref[...].astype(F32) * yb).astype(BF16)
    mix = _dot(merged, wo_ref[...])
    gt1 = mod_ref[0, 2:3, :]
    sh2 = mod_ref[0, 3:4, :]
    sc2 = mod_ref[0, 4:5, :]
    x1 = x_ref[...] + gt1 * mix
    x1_ref[...] = x1
    var = jnp.mean(x1 * x1, axis=-1, keepdims=True)
    h2 = x1 * lax.rsqrt(var + RMS_EPS) * gffn_ref[...] * (1.0 + sc2) + sh2
    for c in range(D_MODEL // LANES):
        h2_ref[pl.ds(c, tm, stride=SUBLANES), :] = h2[:, c * LANES:(c + 1) * LANES]
    hi = h2.astype(BF16)
    lo = (h2 - hi.astype(F32)).astype(BF16)
    lg_ref[...] = (_dot(hi, wr_hi_ref[...]) + _dot(hi, wr_lo_ref[...]) + _dot(lo, wr_hi_ref[...])
                   + br_ref[...])


def _outproj_call(oa2, ob2, sga, sgb, x2, mod3, g_ffn, woa, wob, wo, wr_hi, wr_lo, br, seq):
    n = x2.shape[0]
    tpb = seq // TM_OUT
    row = lambda w: pl.BlockSpec((TM_OUT, w), lambda i: (i, 0))
    full = lambda a: pl.BlockSpec(a.shape, lambda i: (0,) * a.ndim)
    return pl.pallas_call(
        _outproj_kernel,
        grid=(n // TM_OUT,),
        in_specs=[row(512), row(512), row(1024), row(1024), row(D_MODEL),
                  pl.BlockSpec((1, 6, D_MODEL), lambda i: (i // tpb, 0, 0)),
                  full(g_ffn), full(woa), full(wob), full(wo), full(wr_hi), full(wr_lo), full(br)],
        out_specs=[row(D_MODEL),
                   pl.BlockSpec((TM_OUT * SUBLANES, LANES), lambda i: (i, 0)),
                   row(LANES)],
        out_shape=[jax.ShapeDtypeStruct((n, D_MODEL), F32),
                   jax.ShapeDtypeStruct((n * SUBLANES, LANES), F32),
                   jax.ShapeDtypeStruct((n, LANES), F32)],
        compiler_params=_cparams(("arbitrary",)),
        name="outproj",
    )(oa2, ob2, sga, sgb, x2, mod3, g_ffn, woa, wob, wo, wr_hi, wr_lo, br)


def _route_kernel(lg_ref, rt_ref):
    tm = TM_ROUTE
    lg = lg_ref[...]
    lane = lax.broadcasted_iota(I32, (tm, LANES), 1)
    big = jnp.int32(LANES)
    ninf = -jnp.inf
    gmask = (lane >= N_EXPERTS) & (lane < N_EXPERTS + N_GROUPS)
    g = jnp.where(gmask, lg, ninf)
    gmax = jnp.max(g, axis=-1, keepdims=True)
    grp = jnp.min(jnp.where(g == gmax, lane - N_EXPERTS, big), axis=-1, keepdims=True)
    p_grp = 1.0 / jnp.sum(jnp.where(gmask, jnp.exp(lg - gmax), 0.0), axis=-1, keepdims=True)
    lo = grp * EXPERTS_PER_GROUP
    emask = (lane >= lo) & (lane < lo + EXPERTS_PER_GROUP)
    ev = jnp.where(emask, lg, ninf)
    v0 = jnp.max(ev, axis=-1, keepdims=True)
    i0 = jnp.min(jnp.where(emask & (ev == v0), lane, big), axis=-1, keepdims=True)
    rest = emask & (lane != i0)
    ev1 = jnp.where(rest, lg, ninf)
    v1 = jnp.max(ev1, axis=-1, keepdims=True)
    i1 = jnp.min(jnp.where(rest & (ev1 == v1), lane, big), axis=-1, keepdims=True)
    e1 = jnp.exp(v1 - v0)
    w0 = p_grp / (1.0 + e1)
    w1 = p_grp * e1 / (1.0 + e1)
    rt_ref[...] = jnp.where(lane == 0, i0.astype(F32),
                            jnp.where(lane == 1, i1.astype(F32),
                                      jnp.where(lane == 2, w0, jnp.where(lane == 3, w1, 0.0))))


def _route_call(lg):
    n = lg.shape[0]
    spec = pl.BlockSpec((TM_ROUTE, LANES), lambda i: (i, 0))
    return pl.pallas_call(
        _route_kernel, grid=(n // TM_ROUTE,), in_specs=[spec], out_specs=spec,
        out_shape=jax.ShapeDtypeStruct((n, LANES), F32),
        compiler_params=_cparams(("arbitrary",)), name="route",
    )(lg)


def _expert_onehots(rt, tm):
    lane = lax.broadcasted_iota(I32, (tm, LANES), 1)
    lane_f = lane.astype(F32)
    e0 = jnp.sum(jnp.where(lane == 0, rt, 0.0), axis=-1, keepdims=True)
    e1 = jnp.sum(jnp.where(lane == 1, rt, 0.0), axis=-1, keepdims=True)
    return lane, lane_f == e0, lane_f == e1


def _rank_kernel(rt_ref, rk_ref, cnt_ref, carry_scr):
    tm = TM_ROUTE

    @pl.when(pl.program_id(0) == 0)
    def _():
        carry_scr[...] = jnp.zeros_like(carry_scr)

    lane, is0, is1 = _expert_onehots(rt_ref[...], tm)
    hits = jnp.where(is0, 1.0, 0.0) + jnp.where(is1, 1.0, 0.0)
    rr = lax.broadcasted_iota(I32, (tm, tm), 0)
    cc = lax.broadcasted_iota(I32, (tm, tm), 1)
    before = jnp.where(cc < rr, 1.0, 0.0).astype(BF16)
    seen = _dot(before, hits.astype(BF16)) + carry_scr[...]
    r0 = jnp.sum(jnp.where(is0, seen, 0.0), axis=-1, keepdims=True)
    r1 = jnp.sum(jnp.where(is1, seen, 0.0), axis=-1, keepdims=True)
    rk_ref[...] = jnp.where(lane == 0, r0, jnp.where(lane == 1, r1, 0.0))
    carry_scr[...] = carry_scr[...] + jnp.sum(hits, axis=0, keepdims=True)
    cnt_ref[...] = jnp.broadcast_to(carry_scr[...], cnt_ref.shape)


def _rank_call(rt):
    n = rt.shape[0]
    spec = pl.BlockSpec((TM_ROUTE, LANES), lambda i: (i, 0))
    return pl.pallas_call(
        _rank_kernel, grid=(n // TM_ROUTE,), in_specs=[spec],
        out_specs=[spec, pl.BlockSpec((SUBLANES, LANES), lambda i: (0, 0))],
        out_shape=[jax.ShapeDtypeStruct((n, LANES), F32), jax.ShapeDtypeStruct((SUBLANES, LANES), F32)],
        scratch_shapes=[pltpu.VMEM((1, LANES), F32)],
        compiler_params=_cparams(("arbitrary",)), name="rank",
    )(rt)


def _dest_kernel(rt_ref, rk_ref, cnt_ref, dd_ref, be_ref, nu_ref):
    tm = TM_ROUTE
    lane, is0, is1 = _expert_onehots(rt_ref[...], tm)
    blocks = jnp.floor((cnt_ref[...] + (BLK_E - 1)) * (1.0 / BLK_E))
    er = lax.broadcasted_iota(I32, (LANES, LANES), 0)
    ec = lax.broadcasted_iota(I32, (LANES, LANES), 1)
    upto = jnp.where(er <= ec, 1.0, 0.0).astype(BF16)
    bend = _dot(blocks.astype(BF16), upto)
    pstart = (bend[0:1, :] - blocks[0:1, :]) * BLK_E
    rk = rk_ref[...]
    r0 = jnp.sum(jnp.where(lane == 0, rk, 0.0), axis=-1, keepdims=True)
    r1 = jnp.sum(jnp.where(lane == 1, rk, 0.0), axis=-1, keepdims=True)
    d0 = jnp.sum(jnp.where(is0, pstart, 0.0), axis=-1, keepdims=True) + r0
    d1 = jnp.sum(jnp.where(is1, pstart, 0.0), axis=-1, keepdims=True) + r1
    dd_ref[...] = jnp.where(lane == 0, d0, jnp.where(lane == 1, d1, 0.0)).astype(I32)
    nb = be_ref.shape[0]
    blk = lax.broadcasted_iota(I32, (nb, LANES), 0).astype(F32)
    lane_b = lax.broadcasted_iota(I32, (nb, LANES), 1)
    ended = jnp.where((lane_b < N_EXPERTS) & (bend[0:1, :] <= blk), 1.0, 0.0)
    be = jnp.minimum(jnp.sum(ended, axis=-1, keepdims=True), float(N_EXPERTS - 1))
    be_ref[...] = jnp.broadcast_to(be, be_ref.shape).astype(I32)
    lane_c = lax.broadcasted_iota(I32, (SUBLANES, LANES), 1)
    used = jnp.sum(jnp.where(lane_c == N_EXPERTS - 1, bend, 0.0), axis=-1, keepdims=True)
    nu_ref[...] = jnp.broadcast_to(used, nu_ref.shape).astype(I32)


def _dest_call(rt, rk, cnt, n_blocks):
    n = rt.shape[0]
    nb_pad = -(-n_blocks // SUBLANES) * SUBLANES
    spec = pl.BlockSpec((TM_ROUTE, LANES), lambda i: (i, 0))
    const = lambda rows: pl.BlockSpec((rows, LANES), lambda i: (0, 0))
    return pl.pallas_call(
        _dest_kernel, grid=(n // TM_ROUTE,), in_specs=[spec, spec, const(SUBLANES)],
        out_specs=[spec, const(nb_pad), const(SUBLANES)],
        out_shape=[jax.ShapeDtypeStruct((n, LANES), I32), jax.ShapeDtypeStruct((nb_pad, LANES), I32),
                   jax.ShapeDtypeStruct((SUBLANES, LANES), I32)],
        compiler_params=_cparams(("arbitrary",)), name="dest",
    )(rt, rk, cnt)


def _slab_copy(src, src_row, dst, dst_row, sem):
    return pltpu.make_async_copy(src.at[pl.ds(pl.multiple_of(src_row * SUBLANES, SUBLANES), SUBLANES)],
                                 dst.at[pl.ds(pl.multiple_of(dst_row * SUBLANES, SUBLANES), SUBLANES)], sem)


def _dispatch_kernel(dest_ref, h2_ref, xs_in_hbm, xs_hbm, sem):
    del xs_in_hbm
    tm = TM_DISPATCH
    base = pl.program_id(0) * tm

    def issue(r, _):
        for k in range(2):
            _slab_copy(h2_ref, r, xs_hbm, dest_ref[2 * (base + r) + k], sem).start(priority=k)
        return 0

    lax.fori_loop(0, tm, issue, 0, unroll=GATHER_UNROLL)
    for _ in range(2):
        pltpu.make_async_copy(h2_ref, xs_hbm.at[pl.ds(0, tm * SUBLANES)], sem).wait()


def _dispatch_call(dest, h2s, xs_zero):
    n = h2s.shape[0] // SUBLANES
    anyspec = pl.BlockSpec(memory_space=pl.ANY)
    return pl.pallas_call(
        _dispatch_kernel,
        grid_spec=pltpu.PrefetchScalarGridSpec(
            num_scalar_prefetch=1, grid=(n // TM_DISPATCH,),
            in_specs=[pl.BlockSpec((TM_DISPATCH * SUBLANES, LANES), lambda i, d: (i, 0)), anyspec],
            out_specs=anyspec,
            scratch_shapes=[pltpu.SemaphoreType.DMA(())]),
        out_shape=jax.ShapeDtypeStruct(xs_zero.shape, F32),
        input_output_aliases={2: 0},
        compiler_params=_cparams(("arbitrary",)),
        name="dispatch",
    )(dest, h2s, xs_zero)


def _experts_kernel(be_ref, nused_ref, xs_ref, wg_ref, wu_ref, wd_ref, yb_ref, x_scr, wg_scr, wu_scr, wd_scr):
    i = pl.program_id(0)

    @pl.when(i >= nused_ref[0])
    def _():
        yb_ref[...] = jnp.zeros_like(yb_ref)

    @pl.when(i < nused_ref[0])
    def _():
        prev = be_ref[jnp.maximum(i - 1, 0)]

        @pl.when((i == 0) | (be_ref[i] != prev))
        def _():
            wg_scr[...] = wg_ref[0].astype(BF16)
            wu_scr[...] = wu_ref[0].astype(BF16)
            wd_scr[...] = wd_ref[0].astype(BF16)

        for c in range(D_MODEL // LANES):
            x_scr[:, c * LANES:(c + 1) * LANES] = xs_ref[pl.ds(c, BLK_E, stride=SUBLANES), :].astype(BF16)
        xb = x_scr[...]
        g = _dot(xb, wg_scr[...])
        u = _dot(xb, wu_scr[...])
        hid = (g * _sigmoid(g) * u).astype(BF16)
        y = _dot(hid, wd_scr[...])
        for c in range(D_MODEL // LANES):
            yb_ref[pl.ds(c, BLK_E, stride=SUBLANES), :] = y[:, c * LANES:(c + 1) * LANES]


def _experts_call(block_e, n_used, xs, w_gate, w_up, w_down):
    n_blocks = block_e.shape[0]
    slab = pl.BlockSpec((BLK_E * SUBLANES, LANES), lambda i, be, nu: (i, 0))
    return pl.pallas_call(
        _experts_kernel,
        grid_spec=pltpu.PrefetchScalarGridSpec(
            num_scalar_prefetch=2, grid=(n_blocks,),
            in_specs=[slab,
                      pl.BlockSpec((1, D_MODEL, D_EXPERT), lambda i, be, nu: (be[i], 0, 0)),
                      pl.BlockSpec((1, D_MODEL, D_EXPERT), lambda i, be, nu: (be[i], 0, 0)),
                      pl.BlockSpec((1, D_EXPERT, D_MODEL), lambda i, be, nu: (be[i], 0, 0))],
            out_specs=slab,
            scratch_shapes=[pltpu.VMEM((BLK_E, D_MODEL), BF16),
                            pltpu.VMEM((D_MODEL, D_EXPERT), BF16),
                            pltpu.VMEM((D_MODEL, D_EXPERT), BF16),
                            pltpu.VMEM((D_EXPERT, D_MODEL), BF16)]),
        out_shape=jax.ShapeDtypeStruct(xs.shape, F32),
        compiler_params=_cparams(("arbitrary",)),
        name="experts",
    )(block_e, n_used, xs, w_gate, w_up, w_down)


def _final_kernel(dest_ref, x1_ref, rt_ref, mod_ref, gfin_ref, yb_hbm, o_ref, g_scr, sem):
    tm = TM_FINAL
    step = pl.program_id(0)
    slot = step % 2
    slot_rows = 2 * tm * SUBLANES

    def gather(s, into):
        def issue(r, _):
            tok = s * tm + r
            for k in range(2):
                _slab_copy(yb_hbm, dest_ref[2 * tok + k], g_scr.at[into], k * tm + r,
                           sem.at[into]).start(priority=k)
            return 0
        lax.fori_loop(0, tm, issue, 0, unroll=GATHER_UNROLL)

    @pl.when(step == 0)
    def _():
        gather(0, 0)

    @pl.when(step + 1 < pl.num_programs(0))
    def _():
        gather(step + 1, 1 - slot)

    g_now = g_scr.at[slot]
    pltpu.make_async_copy(yb_hbm.at[pl.ds(0, slot_rows)], g_now, sem.at[slot]).wait()

    rt = rt_ref[...]
    lane = lax.broadcasted_iota(I32, (tm, LANES), 1)
    gw0 = jnp.sum(jnp.where(lane == 2, rt, 0.0), axis=-1, keepdims=True)
    gw1 = jnp.sum(jnp.where(lane == 3, rt, 0.0), axis=-1, keepdims=True)
    gt2 = mod_ref[0, 5:6, :]
    x1 = x1_ref[...]
    cols = []
    for c in range(D_MODEL // LANES):
        y0 = g_now[pl.ds(c, tm, stride=SUBLANES), :]
        y1 = g_now[pl.ds(tm * SUBLANES + c, tm, stride=SUBLANES), :]
        y = gw0 * y0 + gw1 * y1
        cols.append(x1[:, c * LANES:(c + 1) * LANES] + gt2[:, c * LANES:(c + 1) * LANES] * y)
    x2 = jnp.concatenate(cols, axis=1)
    var = jnp.mean(x2 * x2, axis=-1, keepdims=True)
    o_ref[...] = x2 * lax.rsqrt(var + RMS_EPS) * gfin_ref[...]


def _final_call(dest, x1, rt, mod3, g_final, yb, seq):
    n = x1.shape[0]
    tpb = seq // TM_FINAL
    return pl.pallas_call(
        _final_kernel,
        grid_spec=pltpu.PrefetchScalarGridSpec(
            num_scalar_prefetch=1, grid=(n // TM_FINAL,),
            in_specs=[pl.BlockSpec((TM_FINAL, D_MODEL), lambda i, d: (i, 0)),
                      pl.BlockSpec((TM_FINAL, LANES), lambda i, d: (i, 0)),
                      pl.BlockSpec((1, 6, D_MODEL), lambda i, d: (i // tpb, 0, 0)),
                      pl.BlockSpec((1, D_MODEL), lambda i, d: (0, 0)),
                      pl.BlockSpec(memory_space=pl.ANY)],
            out_specs=pl.BlockSpec((TM_FINAL, D_MODEL), lambda i, d: (i, 0)),
            scratch_shapes=[pltpu.VMEM((2, 2 * TM_FINAL * SUBLANES, LANES), F32),
                            pltpu.SemaphoreType.DMA((2,))]),
        out_shape=jax.ShapeDtypeStruct((n, D_MODEL), F32),
        compiler_params=_cparams(("arbitrary",)),
        name="final",
    )(dest, x1, rt, mod3, g_final, yb)


def _permute_w_in(w):
    o = np.cumsum([0, 512, 64, 64, 512, 64, 8, 512, 512, 512, 8, 1024, 1024])
    qa, ka, va, qi, ki, wi, qb, kb, vb, fb, ga, gb = [w[:, o[k]:o[k + 1]] for k in range(12)]
    aux = jnp.concatenate([wi, fb, jnp.zeros((w.shape[0], LANES - 2 * N_HEADS), w.dtype)], axis=1)
    return jnp.concatenate([qa, qi, qb, kb, vb, ga, gb, ka, ka, va, va, ki, ki, aux], axis=1).astype(BF16)


def _layer(x2, pos2, mod3, batch, seq, g_mix, w_in, b_forget, w_out_a, w_out_b, w_out, g_ffn,
           w_group, b_group, w_router, b_router, w_e_gate, w_e_up, w_e_down, g_final):
    n = x2.shape[0]
    inv_freq = ROPE_THETA ** (-jnp.arange(0, ROPE_DIM, 2, dtype=F32) / ROPE_DIM)
    jj = np.arange(LANES) % HEAD_DIM
    invf = jnp.where(jj < ROPE_DIM, inv_freq[jj % (ROPE_DIM // 2)], 0.0)[None, :].astype(F32)
    fbias = jnp.zeros((1, LANES), F32).at[0, N_HEADS:2 * N_HEADS].set(b_forget.astype(F32))

    (qa, qi, qb, kaug, vb, sga, sgb, ka2, va2, ki2, aux) = _inproj_call(
        x2, pos2, mod3, g_mix.reshape(1, D_MODEL), _permute_w_in(w_in), invf, fbias, batch, seq)

    r3 = lambda a: a.reshape(batch, seq, a.shape[-1])
    tr = lambda a: jnp.swapaxes(r3(a), 1, 2)
    tr_tiles = lambda a, tk: jnp.swapaxes(a.reshape(batch, seq // tk, tk, a.shape[-1]), 2, 3)
    aux_t = tr(aux[:, :4 * N_HEADS])
    obt = _fox_call(tr(qb), aux_t, r3(kaug), tr_tiles(vb, TK_FOX))
    oat = _dsa_call(tr(qa), tr(qi), aux_t, r3(ki2), r3(ka2), tr_tiles(va2[:, :HEAD_DIM], TK_DSA))
    oa = jnp.swapaxes(oat, 1, 2).reshape(n, 512)
    ob = jnp.swapaxes(obt, 1, 2).reshape(n, 512)

    w_rt = jnp.concatenate([w_router, w_group, jnp.zeros((D_MODEL, LANES - N_EXPERTS - N_GROUPS), F32)], axis=1)
    wr_hi = w_rt.astype(BF16)
    wr_lo = (w_rt - wr_hi.astype(F32)).astype(BF16)
    br = jnp.concatenate([b_router, b_group, jnp.zeros((LANES - N_EXPERTS - N_GROUPS,), F32)])[None, :]
    x1, h2s, lg = _outproj_call(oa, ob, sga, sgb, x2, mod3,
                                g_ffn.reshape(1, D_MODEL), w_out_a.astype(BF16), w_out_b.astype(BF16),
                                w_out.astype(BF16), wr_hi, wr_lo, br, seq)
    rt = _route_call(lg)

    n_slots = 2 * n + N_EXPERTS * BLK_E
    n_blocks = n_slots // BLK_E
    rk, cnt = _rank_call(rt)
    dd, be, nu = _dest_call(rt, rk, cnt, n_blocks)
    dest = dd[:, 0:2].reshape(-1)
    block_e = be[:n_blocks, 0]
    n_used = nu[0, 0:1]

    xs = _dispatch_call(dest, h2s, jnp.zeros((n_slots * SUBLANES, LANES), F32))
    yb = _experts_call(block_e, n_used, xs, w_e_gate, w_e_up, w_e_down)
    return _final_call(dest, x1, rt, mod3, g_final.reshape(1, D_MODEL), yb, seq)


def kernel(x, c, positions, w_mod, b_mod, g_mix, w_in, b_forget, w_out_a, w_out_b, w_out, g_ffn, w_group,
           b_group, w_router, b_router, w_e_gate, w_e_up, w_e_down, g_final):
    batch, seq, d = x.shape
    depth = w_mod.shape[0]
    assert depth == 1 and d == D_MODEL, "kernel fuses the final norm into the single layer"
    n = batch * seq
    c8 = jnp.zeros((8, d), F32).at[:batch].set(c)
    mod = _mod_call(c8, w_mod[0], b_mod[0][None, :])
    mod3 = mod[:batch].reshape(batch, 6, d)
    out = _layer(x.reshape(n, d), positions.reshape(n, 1), mod3, batch, seq, g_mix[0], w_in[0], b_forget[0],
                 w_out_a[0], w_out_b[0], w_out[0], g_ffn[0], w_group[0], b_group[0], w_router[0], b_router[0],
                 w_e_gate[0], w_e_up[0], w_e_down[0], g_final)
    return out.reshape(batch, seq, d)
```

```python
import functools

import numpy as np
import jax
import jax.numpy as jnp
from jax import lax
from jax.experimental import pallas as pl
from jax.experimental.pallas import tpu as pltpu

F32 = jnp.float32
BF16 = jnp.bfloat16
I32 = jnp.int32

D_MODEL = 1024
HEAD_DIM = 64
N_HEADS = 8
CHUNK_SHIFT = 6
TOPK = 256
ROPE_DIM = 16
ROPE_THETA = 500000.0
N_GROUPS = 4
EXPERTS_PER_GROUP = 8
N_EXPERTS = 32
D_EXPERT = 512
RMS_EPS = 1e-6
NEG_INF = -1e30
ATTN_SCALE = HEAD_DIM ** -0.5
IDX_SCALE = HEAD_DIM ** -0.5
LOG2E = 1.4426950408889634

LANES = 128
SUBLANES = 8
VMEM_LIMIT = 56 * 1024 * 1024

TM_IN = 512
TQ_FOX = 512
TK_FOX = 512
TQ_DSA = 256
TK_DSA = 512
TM_OUT = 512
TM_ROUTE = 1024
TM_DISPATCH = 1024
BLK_E = 256
TM_FINAL = 512
GATHER_UNROLL = 8

RADIX_BLOCK = 32 * SUBLANES
BOUND_MARGIN = 1.02
BOUND_SAFE = 40.0
FOX_BIAS_SLACK = 1.0
FOX_K = 256
FOX_AUG_ROWS = 16

_SEG_QA, _SEG_QI, _SEG_QB, _SEG_KB, _SEG_VB = 0, 512, 1024, 1536, 2048
_SEG_GA, _SEG_GB = 2560, 3584
_SEG_KA, _SEG_VA, _SEG_KI, _SEG_AUX = 4608, 4736, 4864, 4992
_NC_IN = 5120

_SENT_KEY = int(np.array(NEG_INF, np.float32).view(np.int32) ^ 0x7FFFFFFF)
_INT_MIN = -(2 ** 31)


def _cparams(sem):
    return pltpu.CompilerParams(dimension_semantics=sem, vmem_limit_bytes=VMEM_LIMIT)


def _dot(a, b):
    return jnp.dot(a, b, preferred_element_type=F32)


def _sigmoid(x):
    return 1.0 / (1.0 + jnp.exp(-x))


def _mod_kernel(c_ref, w_ref, b_ref, o_ref):
    c = c_ref[...]
    ca = c * _sigmoid(c)
    o_ref[...] = _dot(ca.astype(BF16), w_ref[...].astype(BF16)) + b_ref[...]


def _mod_call(c8, w_mod, b_mod):
    n_out = w_mod.shape[1]
    tn = 1024
    return pl.pallas_call(
        _mod_kernel,
        grid=(n_out // tn,),
        in_specs=[pl.BlockSpec((8, D_MODEL), lambda j: (0, 0)),
                  pl.BlockSpec((D_MODEL, tn), lambda j: (0, j)),
                  pl.BlockSpec((1, tn), lambda j: (0, j))],
        out_specs=pl.BlockSpec((8, tn), lambda j: (0, j)),
        out_shape=jax.ShapeDtypeStruct((8, n_out), F32),
        compiler_params=_cparams(("arbitrary",)),
        name="mod",
    )(c8, w_mod, b_mod)


def _inproj_kernel(x_ref, pos_ref, mod_ref, g_ref, w_ref, invf_ref, fbias_ref, sel_ref, ones_ref,
                   qa_ref, qi_ref, qb_ref, kb_ref, vb_ref, sga_ref, sgb_ref,
                   ka_ref, va_ref, ki_ref, aux_ref, h_scr, carry_scr):
    tm = TM_IN
    x = x_ref[...]
    var = jnp.mean(x * x, axis=-1, keepdims=True)
    tf = x * lax.rsqrt(var + RMS_EPS) * g_ref[...]
    sh = mod_ref[0, 0:1, :]
    sc = mod_ref[0, 1:2, :]
    h_scr[...] = (tf * (1.0 + sc) + sh).astype(BF16)
    hb = h_scr[...]

    lane = lax.broadcasted_iota(I32, (tm, LANES), 1)
    j = lane & (HEAD_DIM - 1)
    ang = pos_ref[...].astype(F32) * invf_ref[...]
    cs = jnp.cos(ang)
    sn = jnp.sin(ang)
    coef_next = jnp.where(j < ROPE_DIM // 2, -sn, 0.0)
    coef_prev = jnp.where((j >= ROPE_DIM // 2) & (j < ROPE_DIM), sn, 0.0)

    def rope(tc):
        return (tc * cs + pltpu.roll(tc, LANES - ROPE_DIM // 2, 1) * coef_next
                + pltpu.roll(tc, ROPE_DIM // 2, 1) * coef_prev)

    def seg(off, width):
        return _dot(hb, w_ref[:, off:off + width])

    t = seg(_SEG_QA, 512)
    for c in range(4):
        qa_ref[:, c * LANES:(c + 1) * LANES] = (
            rope(t[:, c * LANES:(c + 1) * LANES]) * (ATTN_SCALE * LOG2E)).astype(BF16)
    t = seg(_SEG_QI, 512)
    for c in range(4):
        qi_ref[:, c * LANES:(c + 1) * LANES] = (rope(t[:, c * LANES:(c + 1) * LANES]) * IDX_SCALE).astype(BF16)
    qb_ref[...] = (seg(_SEG_QB, 512) * (ATTN_SCALE * LOG2E)).astype(BF16)
    vb_ref[...] = seg(_SEG_VB, 512).astype(BF16)
    sga_ref[...] = _sigmoid(seg(_SEG_GA, 1024)).astype(BF16)
    sgb_ref[...] = _sigmoid(seg(_SEG_GB, 1024)).astype(BF16)
    ka = rope(seg(_SEG_KA, LANES))
    ka_ref[...] = ka.astype(BF16)
    va_ref[...] = seg(_SEG_VA, LANES).astype(BF16)
    ki_ref[...] = rope(seg(_SEG_KI, LANES)).astype(BF16)

    z = seg(_SEG_AUX, LANES)
    zf = z + fbias_ref[...]
    logf = jnp.minimum(zf, 0.0) - jnp.log(1.0 + jnp.exp(-jnp.abs(zf)))
    is_f = (lane >= N_HEADS) & (lane < 2 * N_HEADS)
    logf = jnp.where(is_f, logf, 0.0)

    def split3(v):
        hi = v.astype(BF16)
        r1 = v - hi.astype(F32)
        mid = r1.astype(BF16)
        return hi, mid, (r1 - mid.astype(F32)).astype(BF16)

    rr = lax.broadcasted_iota(I32, (tm, tm), 0)
    cc = lax.broadcasted_iota(I32, (tm, tm), 1)
    tri = jnp.where(cc <= rr, 1.0, 0.0).astype(BF16)
    p_hi, p_mid, p_lo = split3(logf)

    @pl.when(pl.program_id(1) == 0)
    def _():
        carry_scr[...] = jnp.zeros_like(carry_scr)

    parts = _dot(tri, jnp.concatenate([p_hi, p_mid, p_lo], axis=1))
    cum = parts[:, 0:LANES] + parts[:, LANES:2 * LANES] + parts[:, 2 * LANES:3 * LANES] + carry_scr[...]
    carry_scr[...] = cum[tm - 1:tm, :]
    cum2 = cum * LOG2E
    aux = jnp.where(lane < N_HEADS, z * (N_HEADS ** -0.5), jnp.where(is_f, cum2, 0.0))

    kb = seg(_SEG_KB, 512)
    first = lane < HEAD_DIM

    def half_norms(blk):
        sq = blk * blk
        return (jnp.sum(jnp.where(first, sq, 0.0), axis=-1, keepdims=True),
                jnp.sum(jnp.where(first, 0.0, sq), axis=-1, keepdims=True))

    for p in range(N_HEADS // 2):
        n_even, n_odd = half_norms(kb[:, p * LANES:(p + 1) * LANES])
        aux = jnp.where(lane == 2 * N_HEADS + 2 * p, n_even, jnp.where(lane == 2 * N_HEADS + 2 * p + 1, n_odd, aux))
    aux_ref[...] = jnp.where(lane == 3 * N_HEADS, half_norms(ka)[0], aux)

    c_hi, c_mid, c_lo = split3(cum2)
    kaug = _dot(jnp.concatenate([c_hi, c_mid, c_lo], axis=1), sel_ref[...]) + ones_ref[...]
    for p in range(N_HEADS // 2):
        kb_ref[:, p * FOX_K:p * FOX_K + LANES] = kb[:, p * LANES:(p + 1) * LANES].astype(BF16)
        kb_ref[:, p * FOX_K + LANES:(p + 1) * FOX_K] = kaug[:, p * LANES:(p + 1) * LANES].astype(BF16)


def _fox_routing_constants():
    sel = np.zeros((3, LANES, (N_HEADS // 2) * LANES), np.float32)
    ones = np.zeros((1, (N_HEADS // 2) * LANES), np.float32)
    for h in range(N_HEADS):
        for t in range(3):
            sel[t, N_HEADS + h, (h // 2) * LANES + 3 + 3 * (h % 2) + t] = -1.0
    for p in range(N_HEADS // 2):
        ones[0, p * LANES:p * LANES + 3] = 1.0
    return jnp.asarray(sel.reshape(3 * LANES, -1), BF16), jnp.asarray(ones, F32)


def _inproj_call(x2, pos2, mod3, g_mix, w_in_p, invf, fbias, batch, seq):
    n = x2.shape[0]
    tpb = seq // TM_IN
    row = lambda b, s: (b * tpb + s, 0)
    bf = lambda w: jax.ShapeDtypeStruct((n, w), BF16)
    sel, ones = _fox_routing_constants()
    out_shapes = [bf(512), bf(512), bf(512), bf((N_HEADS // 2) * FOX_K), bf(512), bf(1024), bf(1024),
                  bf(LANES), bf(LANES), bf(LANES), jax.ShapeDtypeStruct((n, LANES), F32)]
    out_specs = [pl.BlockSpec((TM_IN, s.shape[1]), row) for s in out_shapes]
    return pl.pallas_call(
        _inproj_kernel,
        grid=(batch, tpb),
        in_specs=[pl.BlockSpec((TM_IN, D_MODEL), row),
                  pl.BlockSpec((TM_IN, 1), row),
                  pl.BlockSpec((1, 6, D_MODEL), lambda b, s: (b, 0, 0)),
                  pl.BlockSpec((1, D_MODEL), lambda b, s: (0, 0)),
                  pl.BlockSpec((D_MODEL, _NC_IN), lambda b, s: (0, 0)),
                  pl.BlockSpec((1, LANES), lambda b, s: (0, 0)),
                  pl.BlockSpec((1, LANES), lambda b, s: (0, 0)),
                  pl.BlockSpec(sel.shape, lambda b, s: (0, 0)),
                  pl.BlockSpec(ones.shape, lambda b, s: (0, 0))],
        out_specs=out_specs,
        out_shape=out_shapes,
        scratch_shapes=[pltpu.VMEM((TM_IN, D_MODEL), BF16), pltpu.VMEM((1, LANES), F32)],
        compiler_params=_cparams(("arbitrary", "arbitrary")),
        name="inproj",
    )(x2, pos2, mod3, g_mix, w_in_p, invf, fbias, sel, ones)


def _softmax_max(s, m_ref):
    m_ref[...] = jnp.maximum(m_ref[...], jnp.max(s, axis=0, keepdims=True))


def _softmax_accumulate(s, m_ref, l_ref, acc_ref, v_t):
    p = jnp.exp2(s - m_ref[...])
    l_ref[...] = l_ref[...] + jnp.sum(p, axis=0, keepdims=True)
    acc_ref[...] = acc_ref[...] + _dot(v_t, p.astype(BF16))


def _softmax_scratch(tq):
    return ([pltpu.VMEM((1, tq), F32) for _ in range(2 * N_HEADS)]
            + [pltpu.VMEM((HEAD_DIM, tq), F32) for _ in range(N_HEADS)])


def _softmax_split(refs):
    return refs[:N_HEADS], refs[N_HEADS:2 * N_HEADS], refs[2 * N_HEADS:3 * N_HEADS]


def _softmax_init(m_refs, l_refs, acc_refs):
    for h in range(N_HEADS):
        m_refs[h][...] = jnp.full(m_refs[h].shape, NEG_INF, F32)
        l_refs[h][...] = jnp.zeros(l_refs[h].shape, F32)
        acc_refs[h][...] = jnp.zeros(acc_refs[h].shape, F32)


def _softmax_finish(o_ref, l_refs, acc_refs):
    for h in range(N_HEADS):
        o_ref[0, h * HEAD_DIM:(h + 1) * HEAD_DIM, :] = (acc_refs[h][...] * (1.0 / l_refs[h][...])).astype(BF16)


def _logit_bounds(qt_ref, kn2, slack):
    k_max = jnp.sqrt(jnp.max(kn2, axis=-1, keepdims=True))
    bounds = []
    worst = jnp.float32(0.0)
    for h in range(N_HEADS):
        qh = qt_ref[0, h * HEAD_DIM:(h + 1) * HEAD_DIM, :].astype(F32)
        q_norm = jnp.sqrt(jnp.sum(qh * qh, axis=0, keepdims=True))
        b = q_norm * k_max[h:h + 1, :] * BOUND_MARGIN + slack
        bounds.append(b)
        worst = jnp.maximum(worst, jnp.max(b))
    return bounds, worst <= BOUND_SAFE


def _fox_kernel(qt_ref, cumt_ref, kn_ref, kaug_ref, vt_ref, o_ref, rhs_scr, *softmax_refs):
    m_scr, l_scr, acc_scr = _softmax_split(softmax_refs)
    tq, tk = TQ_FOX, TK_FOX
    i = pl.program_id(1)
    q0 = i * tq

    cq = cumt_ref[0]
    c_hi = cq.astype(BF16).astype(F32)
    c_r = cq - c_hi
    c_mid = c_r.astype(BF16).astype(F32)
    c_lo = c_r - c_mid
    row_q = lax.broadcasted_iota(I32, (LANES, tq), 0)
    row_a = lax.broadcasted_iota(I32, (FOX_AUG_ROWS, 2 * tq), 0)
    second = lax.broadcasted_iota(I32, (FOX_AUG_ROWS, 2 * tq), 1) >= tq
    for p in range(N_HEADS // 2):
        qp = qt_ref[0, p * LANES:(p + 1) * LANES, :]
        zq = jnp.zeros_like(qp)
        rhs_scr[p, 0:LANES, 0:tq] = jnp.where(row_q < HEAD_DIM, qp, zq)
        rhs_scr[p, 0:LANES, tq:2 * tq] = jnp.where(row_q >= HEAD_DIM, qp, zq)
        pair_row = lambda a: jnp.concatenate([a[2 * p:2 * p + 1, :], a[2 * p + 1:2 * p + 2, :]], axis=1)
        aug = jnp.where(row_a == 0, pair_row(c_hi),
              jnp.where(row_a == 1, pair_row(c_mid),
              jnp.where(row_a == 2, pair_row(c_lo),
              jnp.where((row_a >= 3) & (row_a < 6), jnp.where(second, 0.0, 1.0),
              jnp.where((row_a >= 6) & (row_a < 9), jnp.where(second, 1.0, 0.0), 0.0)))))
        rhs_scr[p, LANES:LANES + FOX_AUG_ROWS, :] = aug.astype(BF16)
        rhs_scr[p, LANES + FOX_AUG_ROWS:FOX_K, :] = jnp.zeros((FOX_K - LANES - FOX_AUG_ROWS, 2 * tq), BF16)

    _softmax_init(m_scr, l_scr, acc_scr)
    key_j = lax.broadcasted_iota(I32, (tk, tq), 0)
    qry_i = q0 + lax.broadcasted_iota(I32, (tk, tq), 1)

    def tile(kt, masked, second_pass):
        k0 = pl.multiple_of(kt * tk, tk)
        for p in range(N_HEADS // 2):
            st = _dot(kaug_ref[0, pl.ds(k0, tk), p * FOX_K:(p + 1) * FOX_K], rhs_scr[p])
            for hh in range(2):
                h = 2 * p + hh
                s = st[:, hh * tq:(hh + 1) * tq]
                if masked:
                    s = jnp.where(k0 + key_j <= qry_i, s, NEG_INF)
                if second_pass:
                    _softmax_accumulate(s, m_scr[h], l_scr[h], acc_scr[h],
                                        vt_ref[0, kt, h * HEAD_DIM:(h + 1) * HEAD_DIM, :])
                else:
                    _softmax_max(s, m_scr[h])

    n_full = i * (tq // tk)

    def sweep(second_pass):
        def full_tile(kt, carry):
            tile(kt, False, second_pass)
            return carry

        lax.fori_loop(0, n_full, full_tile, 0)
        for d in range(tq // tk):
            tile(n_full + d, True, second_pass)

    bounds, safe = _logit_bounds(qt_ref, kn_ref[0], FOX_BIAS_SLACK)

    @pl.when(safe)
    def _():
        for h in range(N_HEADS):
            m_scr[h][...] = bounds[h]
        sweep(True)

    @pl.when(jnp.logical_not(safe))
    def _():
        sweep(False)
        sweep(True)

    _softmax_finish(o_ref, l_scr, acc_scr)


def _fox_call(qbt, aux_t, kaug3, vbt4):
    batch, _, seq = qbt.shape
    return pl.pallas_call(
        _fox_kernel,
        grid=(batch, seq // TQ_FOX),
        in_specs=[pl.BlockSpec((1, 512, TQ_FOX), lambda b, i: (b, 0, i)),
                  pl.BlockSpec((1, N_HEADS, TQ_FOX), lambda b, i: (b, 1, i)),
                  pl.BlockSpec((1, N_HEADS, seq), lambda b, i: (b, 2, 0)),
                  pl.BlockSpec((1, seq, (N_HEADS // 2) * FOX_K), lambda b, i: (b, 0, 0)),
                  pl.BlockSpec((1, seq // TK_FOX, 512, TK_FOX), lambda b, i: (b, 0, 0, 0))],
        out_specs=pl.BlockSpec((1, 512, TQ_FOX), lambda b, i: (b, 0, i)),
        out_shape=jax.ShapeDtypeStruct((batch, 512, seq), BF16),
        scratch_shapes=[pltpu.VMEM((N_HEADS // 2, FOX_K, 2 * TQ_FOX), BF16)] + _softmax_scratch(TQ_FOX),
        compiler_params=_cparams(("arbitrary", "arbitrary")),
        name="fox",
    )(qbt, aux_t, aux_t, kaug3, vbt4)


def _bit_transpose32(words):
    a = list(words)
    j, mask = 16, 0x0000FFFF
    while j:
        mask_i = int(np.array(mask, np.uint32).view(np.int32))
        k = 0
        while k < 32:
            t = (a[k] ^ lax.shift_right_logical(a[k + j], j)) & mask_i
            a[k] = a[k] ^ t
            a[k + j] = a[k + j] ^ lax.shift_left(t, j)
            k = (k + j + 1) & ~j
        j >>= 1
        mask = (mask ^ (mask << j)) & 0xFFFFFFFF
    return a


def _dsa_kernel(qat_ref, qit_ref, wit_ref, kn_ref, ki_ref, ka_ref, vat_ref, o_ref,
                aqi_scr, aqa_scr, sc_scr, bias_scr, plane_scr, alive_scr, *softmax_refs, seq):
    m_scr, l_scr, acc_scr = _softmax_split(softmax_refs)
    tq, tk = TQ_DSA, TK_DSA
    blk_per_tile = tk // RADIX_BLOCK
    n_blocks = seq // RADIX_BLOCK
    i = pl.program_id(1)
    q0 = i * tq
    nk = lax.shift_right_logical(q0 + tq + tk - 1, int(np.log2(tk)))
    n_rest = (seq - nk * tk).astype(F32)

    row_q = lax.broadcasted_iota(I32, (LANES, tq), 0)
    for h in range(N_HEADS):
        keep = (row_q < HEAD_DIM) if h % 2 == 0 else (row_q >= HEAD_DIM)
        rows = slice((h // 2) * LANES, (h // 2 + 1) * LANES)
        qi_blk = qit_ref[0, rows, :]
        qa_blk = qat_ref[0, rows, :]
        aqi_scr[:, h * tq:(h + 1) * tq] = jnp.where(keep, qi_blk, jnp.zeros_like(qi_blk))
        aqa_scr[:, h * tq:(h + 1) * tq] = jnp.where(keep, qa_blk, jnp.zeros_like(qa_blk))

    w = wit_ref[0]
    key_j = lax.broadcasted_iota(I32, (tk, tq), 0)
    qry_chunk = lax.shift_right_logical(q0 + lax.broadcasted_iota(I32, (tk, tq), 1), CHUNK_SHIFT)

    def admissible(k0):
        return lax.shift_right_logical(k0 + key_j, CHUNK_SHIFT) <= qry_chunk

    def p1(kt, carry, masked):
        k0 = pl.multiple_of(kt * tk, tk)
        rel = _dot(ki_ref[0, pl.ds(k0, tk), :], aqi_scr[...])
        sc = w[0:1, :] * jnp.maximum(rel[:, 0:tq], 0.0)
        for h in range(1, N_HEADS):
            sc = sc + w[h:h + 1, :] * jnp.maximum(rel[:, h * tq:(h + 1) * tq], 0.0)
        if masked:
            sc = jnp.where(admissible(k0), sc, NEG_INF)
        sc = jnp.where(sc == 0.0, 0.0, sc)
        sc_scr[kt] = sc
        b = pltpu.bitcast(sc, I32)
        key = b ^ (lax.shift_right_arithmetic(b, 31) & 0x7FFFFFFF)
        for half in range(blk_per_tile):
            base = half * RADIX_BLOCK
            planes = _bit_transpose32([key[base + SUBLANES * m:base + SUBLANES * (m + 1), :] ^ _INT_MIN
                                       for m in range(32)])
            bl = kt * blk_per_tile + half
            for p in range(32):
                plane_scr[bl, p] = planes[p]
            alive_scr[bl] = jnp.full((SUBLANES, tq), -1, I32)
        return carry

    n_unmasked = lax.shift_right_logical(q0, int(np.log2(tk)))
    lax.fori_loop(0, n_unmasked, functools.partial(p1, masked=False), 0)
    lax.fori_loop(n_unmasked, nk, functools.partial(p1, masked=True), 0)

    def clear_block(bl, carry):
        plane_scr[bl] = jnp.zeros((32, SUBLANES, tq), I32)
        alive_scr[bl] = jnp.zeros((SUBLANES, tq), I32)
        return carry

    lax.fori_loop(nk * blk_per_tile, n_blocks, clear_block, 0)

    sent_u = jnp.int32(_SENT_KEY ^ _INT_MIN)

    def bit_step(p, carry):
        k_rem, thr_bits, rest_alive = carry
        shift = 31 - p
        sent_bit = lax.shift_right_logical(sent_u, shift) & 1
        ones = [alive_scr[bl] & plane_scr[bl, p] for bl in range(n_blocks)]
        tot = lax.population_count(ones[0])
        for bl in range(1, n_blocks):
            tot = tot + lax.population_count(ones[bl])
        cnt = jnp.sum(tot.astype(F32), axis=0, keepdims=True)
        cnt = cnt + jnp.where((rest_alive != 0) & (sent_bit != 0), n_rest, 0.0)
        take1 = cnt >= k_rem
        for bl in range(n_blocks):
            alive_scr[bl] = jnp.where(take1, ones[bl], alive_scr[bl] & ~plane_scr[bl, p])
        k_rem = jnp.where(take1, k_rem, k_rem - cnt)
        thr_bits = thr_bits | jnp.where(take1, lax.shift_left(jnp.int32(1), shift), 0)
        rest_alive = jnp.where(take1 == (sent_bit != 0), rest_alive, 0)
        return k_rem, thr_bits, rest_alive

    kf = float(TOPK)
    _, thr_bits, _ = lax.fori_loop(
        0, 32, bit_step,
        (jnp.full((1, tq), kf, F32), jnp.zeros((1, tq), I32), jnp.ones((1, tq), I32)))
    thr_key = thr_bits ^ _INT_MIN
    thr0 = pltpu.bitcast(thr_key ^ (lax.shift_right_arithmetic(thr_key, 31) & 0x7FFFFFFF), F32)

    def rank_counts(thr):
        def body(kt, c):
            sc = sc_scr[kt]
            return (c[0] + jnp.sum(jnp.where(sc >= thr, 1.0, 0.0), axis=0, keepdims=True),
                    c[1] + jnp.sum(jnp.where(sc > thr, 1.0, 0.0), axis=0, keepdims=True))
        z = jnp.zeros((1, tq), F32)
        n_ge, n_gt = lax.fori_loop(0, nk, body, (z, z))
        return (n_ge + jnp.where(thr <= NEG_INF, n_rest, 0.0), n_gt + jnp.where(thr < NEG_INF, n_rest, 0.0))

    def misplaced(state):
        _, n_ge, n_gt = state
        return jnp.max(jnp.where((n_ge < kf) | (n_gt >= kf), 1.0, 0.0)) > 0.0

    def step_threshold(state):
        thr, n_ge, n_gt = state

        def body(kt, c):
            sc = sc_scr[kt]
            below = jnp.max(jnp.where(sc < thr, sc, -jnp.inf), axis=0, keepdims=True)
            above = jnp.min(jnp.where(sc > thr, sc, jnp.inf), axis=0, keepdims=True)
            return jnp.maximum(c[0], below), jnp.minimum(c[1], above)

        below, above = lax.fori_loop(0, nk, body, (jnp.full((1, tq), -jnp.inf, F32), jnp.full((1, tq), jnp.inf, F32)))
        has_rest = n_rest > 0.0
        below = jnp.where(has_rest & (thr > NEG_INF), jnp.maximum(below, NEG_INF), below)
        above = jnp.where(has_rest & (thr < NEG_INF), jnp.minimum(above, NEG_INF), above)
        thr = jnp.where(n_ge < kf, below, jnp.where(n_gt >= kf, above, thr))
        return (thr,) + rank_counts(thr)

    thr, n_ge, n_gt = lax.while_loop(misplaced, step_threshold, (thr0,) + rank_counts(thr0))
    need = kf - n_gt
    tie_free = jnp.min(jnp.where(n_ge == kf, 1.0, 0.0)) > 0.0

    tr = lax.broadcasted_iota(I32, (tk, tk), 0)
    tc = lax.broadcasted_iota(I32, (tk, tk), 1)
    tri = jnp.where(tc <= tr, 1.0, 0.0).astype(BF16)
    _softmax_init(m_scr, l_scr, acc_scr)

    def selection_bias(kt, tie):
        sc = sc_scr[kt]
        eq = sc == thr
        pref = _dot(tri, jnp.where(eq, 1.0, 0.0).astype(BF16)) + tie
        sel = (sc > thr) | (eq & (pref <= need))
        return jnp.where(sel & admissible(kt * tk), 0.0, NEG_INF), pref[tk - 1:tk, :]

    def logits(kt):
        return _dot(ka_ref[0, pl.ds(pl.multiple_of(kt * tk, tk), tk), :], aqa_scr[...])

    def accumulate(kt, bias):
        logit = logits(kt)
        v_t = vat_ref[0, kt]
        for h in range(N_HEADS):
            _softmax_accumulate(logit[:, h * tq:(h + 1) * tq] + bias, m_scr[h], l_scr[h], acc_scr[h], v_t)

    def p3_single(kt, tie):
        bias, tie = selection_bias(kt, tie)
        accumulate(kt, bias)
        return tie

    def p3_single_tie_free(kt, carry):
        accumulate(kt, jnp.where(sc_scr[kt] >= thr, 0.0, NEG_INF))
        return carry

    def p3_max(kt, tie):
        bias, tie = selection_bias(kt, tie)
        bias_scr[kt] = bias
        logit = logits(kt)
        for h in range(N_HEADS):
            _softmax_max(logit[:, h * tq:(h + 1) * tq] + bias, m_scr[h])
        return tie

    def p3_accumulate(kt, carry):
        accumulate(kt, bias_scr[kt])
        return carry

    kn2 = jnp.broadcast_to(kn_ref[0, 0:1, :], (N_HEADS, seq))
    bounds, safe = _logit_bounds(qat_ref, kn2, 0.0)
    no_tie = jnp.zeros((1, tq), F32)

    @pl.when(safe)
    def _():
        for h in range(N_HEADS):
            m_scr[h][...] = bounds[h]

    @pl.when(safe & tie_free)
    def _():
        lax.fori_loop(0, nk, p3_single_tie_free, 0)

    @pl.when(safe & jnp.logical_not(tie_free))
    def _():
        lax.fori_loop(0, nk, p3_single, no_tie)

    @pl.when(jnp.logical_not(safe))
    def _():
        lax.fori_loop(0, nk, p3_max, no_tie)
        lax.fori_loop(0, nk, p3_accumulate, 0)

    _softmax_finish(o_ref, l_scr, acc_scr)


def _dsa_call(qat, qit, aux_t, ki3, ka3, vat4):
    batch, _, seq = qat.shape
    qspec = pl.BlockSpec((1, 512, TQ_DSA), lambda b, i: (b, 0, i))
    kspec = pl.BlockSpec((1, seq, LANES), lambda b, i: (b, 0, 0))
    return pl.pallas_call(
        functools.partial(_dsa_kernel, seq=seq),
        grid=(batch, seq // TQ_DSA),
        in_specs=[qspec, qspec,
                  pl.BlockSpec((1, N_HEADS, TQ_DSA), lambda b, i: (b, 0, i)),
                  pl.BlockSpec((1, N_HEADS, seq), lambda b, i: (b, 3, 0)),
                  kspec, kspec,
                  pl.BlockSpec((1, seq // TK_DSA, HEAD_DIM, TK_DSA), lambda b, i: (b, 0, 0, 0))],
        out_specs=qspec,
        out_shape=jax.ShapeDtypeStruct((batch, 512, seq), BF16),
        scratch_shapes=[pltpu.VMEM((LANES, N_HEADS * TQ_DSA), BF16),
                        pltpu.VMEM((LANES, N_HEADS * TQ_DSA), BF16),
                        pltpu.VMEM((seq // TK_DSA, TK_DSA, TQ_DSA), F32),
                        pltpu.VMEM((seq // TK_DSA, TK_DSA, TQ_DSA), F32),
                        pltpu.VMEM((seq // RADIX_BLOCK, 32, SUBLANES, TQ_DSA), I32),
                        pltpu.VMEM((seq // RADIX_BLOCK, SUBLANES, TQ_DSA), I32)] + _softmax_scratch(TQ_DSA),
        compiler_params=_cparams(("arbitrary", "arbitrary")),
        name="dsa",
    )(qat, qit, aux_t, aux_t, ki3, ka3, vat4)


def _outproj_kernel(oa_ref, ob_ref, sga_ref, sgb_ref, x_ref, mod_ref, gffn_ref,
                    woa_ref, wob_ref, wo_ref, wr_hi_ref, wr_lo_ref, br_ref,
                    x1_ref, h2_ref, lg_ref):
    tm = TM_OUT
    ya = _dot(oa_ref[...], woa_ref[...])
    yb = _dot(ob_ref[...], wob_ref[...])
    merged = (sga_ref[...].astype(F32) * ya + sgb_ref[...].astype(F32) * yb).astype(BF16)
    mix = _dot(merged, wo_ref[...])
    gt1 = mod_ref[0, 2:3, :]
    sh2 = mod_ref[0, 3:4, :]
    sc2 = mod_ref[0, 4:5, :]
    x1 = x_ref[...] + gt1 * mix
    x1_ref[...] = x1
    var = jnp.mean(x1 * x1, axis=-1, keepdims=True)
    h2 = x1 * lax.rsqrt(var + RMS_EPS) * gffn_ref[...] * (1.0 + sc2) + sh2
    for c in range(D_MODEL // LANES):
        h2_ref[pl.ds(c, tm, stride=SUBLANES), :] = h2[:, c * LANES:(c + 1) * LANES]
    hi = h2.astype(BF16)
    lo = (h2 - hi.astype(F32)).astype(BF16)
    lg_ref[...] = (_dot(hi, wr_hi_ref[...]) + _dot(hi, wr_lo_ref[...]) + _dot(lo, wr_hi_ref[...])
                   + br_ref[...])


def _outproj_call(oa2, ob2, sga, sgb, x2, mod3, g_ffn, woa, wob, wo, wr_hi, wr_lo, br, seq):
    n = x2.shape[0]
    tpb = seq // TM_OUT
    row = lambda w: pl.BlockSpec((TM_OUT, w), lambda i: (i, 0))
    full = lambda a: pl.BlockSpec(a.shape, lambda i: (0,) * a.ndim)
    return pl.pallas_call(
        _outproj_kernel,
        grid=(n // TM_OUT,),
        in_specs=[row(512), row(512), row(1024), row(1024), row(D_MODEL),
                  pl.BlockSpec((1, 6, D_MODEL), lambda i: (i // tpb, 0, 0)),
                  full(g_ffn), full(woa), full(wob), full(wo), full(wr_hi), full(wr_lo), full(br)],
        out_specs=[row(D_MODEL),
                   pl.BlockSpec((TM_OUT * SUBLANES, LANES), lambda i: (i, 0)),
                   row(LANES)],
        out_shape=[jax.ShapeDtypeStruct((n, D_MODEL), F32),
                   jax.ShapeDtypeStruct((n * SUBLANES, LANES), F32),
                   jax.ShapeDtypeStruct((n, LANES), F32)],
        compiler_params=_cparams(("arbitrary",)),
        name="outproj",
    )(oa2, ob2, sga, sgb, x2, mod3, g_ffn, woa, wob, wo, wr_hi, wr_lo, br)


def _route_kernel(lg_ref, rt_ref):
    tm = TM_ROUTE
    lg = lg_ref[...]
    lane = lax.broadcasted_iota(I32, (tm, LANES), 1)
    big = jnp.int32(LANES)
    ninf = -jnp.inf
    gmask = (lane >= N_EXPERTS) & (lane < N_EXPERTS + N_GROUPS)
    g = jnp.where(gmask, lg, ninf)
    gmax = jnp.max(g, axis=-1, keepdims=True)
    grp = jnp.min(jnp.where(g == gmax, lane - N_EXPERTS, big), axis=-1, keepdims=True)
    p_grp = 1.0 / jnp.sum(jnp.where(gmask, jnp.exp(lg - gmax), 0.0), axis=-1, keepdims=True)
    lo = grp * EXPERTS_PER_GROUP
    emask = (lane >= lo) & (lane < lo + EXPERTS_PER_GROUP)
    ev = jnp.where(emask, lg, ninf)
    v0 = jnp.max(ev, axis=-1, keepdims=True)
    i0 = jnp.min(jnp.where(emask & (ev == v0), lane, big), axis=-1, keepdims=True)
    rest = emask & (lane != i0)
    ev1 = jnp.where(rest, lg, ninf)
    v1 = jnp.max(ev1, axis=-1, keepdims=True)
    i1 = jnp.min(jnp.where(rest & (ev1 == v1), lane, big), axis=-1, keepdims=True)
    e1 = jnp.exp(v1 - v0)
    w0 = p_grp / (1.0 + e1)
    w1 = p_grp * e1 / (1.0 + e1)
    rt_ref[...] = jnp.where(lane == 0, i0.astype(F32),
                            jnp.where(lane == 1, i1.astype(F32),
                                      jnp.where(lane == 2, w0, jnp.where(lane == 3, w1, 0.0))))


def _route_call(lg):
    n = lg.shape[0]
    spec = pl.BlockSpec((TM_ROUTE, LANES), lambda i: (i, 0))
    return pl.pallas_call(
        _route_kernel, grid=(n // TM_ROUTE,), in_specs=[spec], out_specs=spec,
        out_shape=jax.ShapeDtypeStruct((n, LANES), F32),
        compiler_params=_cparams(("arbitrary",)), name="route",
    )(lg)


def _expert_onehots(rt, tm):
    lane = lax.broadcasted_iota(I32, (tm, LANES), 1)
    lane_f = lane.astype(F32)
    e0 = jnp.sum(jnp.where(lane == 0, rt, 0.0), axis=-1, keepdims=True)
    e1 = jnp.sum(jnp.where(lane == 1, rt, 0.0), axis=-1, keepdims=True)
    return lane, lane_f == e0, lane_f == e1


def _rank_kernel(rt_ref, rk_ref, cnt_ref, carry_scr):
    tm = TM_ROUTE

    @pl.when(pl.program_id(0) == 0)
    def _():
        carry_scr[...] = jnp.zeros_like(carry_scr)

    lane, is0, is1 = _expert_onehots(rt_ref[...], tm)
    hits = jnp.where(is0, 1.0, 0.0) + jnp.where(is1, 1.0, 0.0)
    rr = lax.broadcasted_iota(I32, (tm, tm), 0)
    cc = lax.broadcasted_iota(I32, (tm, tm), 1)
    before = jnp.where(cc < rr, 1.0, 0.0).astype(BF16)
    seen = _dot(before, hits.astype(BF16)) + carry_scr[...]
    r0 = jnp.sum(jnp.where(is0, seen, 0.0), axis=-1, keepdims=True)
    r1 = jnp.sum(jnp.where(is1, seen, 0.0), axis=-1, keepdims=True)
    rk_ref[...] = jnp.where(lane == 0, r0, jnp.where(lane == 1, r1, 0.0))
    carry_scr[...] = carry_scr[...] + jnp.sum(hits, axis=0, keepdims=True)
    cnt_ref[...] = jnp.broadcast_to(carry_scr[...], cnt_ref.shape)


def _rank_call(rt):
    n = rt.shape[0]
    spec = pl.BlockSpec((TM_ROUTE, LANES), lambda i: (i, 0))
    return pl.pallas_call(
        _rank_kernel, grid=(n // TM_ROUTE,), in_specs=[spec],
        out_specs=[spec, pl.BlockSpec((SUBLANES, LANES), lambda i: (0, 0))],
        out_shape=[jax.ShapeDtypeStruct((n, LANES), F32), jax.ShapeDtypeStruct((SUBLANES, LANES), F32)],
        scratch_shapes=[pltpu.VMEM((1, LANES), F32)],
        compiler_params=_cparams(("arbitrary",)), name="rank",
    )(rt)


def _dest_kernel(rt_ref, rk_ref, cnt_ref, dd_ref, be_ref, nu_ref):
    tm = TM_ROUTE
    lane, is0, is1 = _expert_onehots(rt_ref[...], tm)
    blocks = jnp.floor((cnt_ref[...] + (BLK_E - 1)) * (1.0 / BLK_E))
    er = lax.broadcasted_iota(I32, (LANES, LANES), 0)
    ec = lax.broadcasted_iota(I32, (LANES, LANES), 1)
    upto = jnp.where(er <= ec, 1.0, 0.0).astype(BF16)
    bend = _dot(blocks.astype(BF16), upto)
    pstart = (bend[0:1, :] - blocks[0:1, :]) * BLK_E
    rk = rk_ref[...]
    r0 = jnp.sum(jnp.where(lane == 0, rk, 0.0), axis=-1, keepdims=True)
    r1 = jnp.sum(jnp.where(lane == 1, rk, 0.0), axis=-1, keepdims=True)
    d0 = jnp.sum(jnp.where(is0, pstart, 0.0), axis=-1, keepdims=True) + r0
    d1 = jnp.sum(jnp.where(is1, pstart, 0.0), axis=-1, keepdims=True) + r1
    dd_ref[...] = jnp.where(lane == 0, d0, jnp.where(lane == 1, d1, 0.0)).astype(I32)
    nb = be_ref.shape[0]
    blk = lax.broadcasted_iota(I32, (nb, LANES), 0).astype(F32)
    lane_b = lax.broadcasted_iota(I32, (nb, LANES), 1)
    ended = jnp.where((lane_b < N_EXPERTS) & (bend[0:1, :] <= blk), 1.0, 0.0)
    be = jnp.minimum(jnp.sum(ended, axis=-1, keepdims=True), float(N_EXPERTS - 1))
    be_ref[...] = jnp.broadcast_to(be, be_ref.shape).astype(I32)
    lane_c = lax.broadcasted_iota(I32, (SUBLANES, LANES), 1)
    used = jnp.sum(jnp.where(lane_c == N_EXPERTS - 1, bend, 0.0), axis=-1, keepdims=True)
    nu_ref[...] = jnp.broadcast_to(used, nu_ref.shape).astype(I32)


def _dest_call(rt, rk, cnt, n_blocks):
    n = rt.shape[0]
    nb_pad = -(-n_blocks // SUBLANES) * SUBLANES
    spec = pl.BlockSpec((TM_ROUTE, LANES), lambda i: (i, 0))
    const = lambda rows: pl.BlockSpec((rows, LANES), lambda i: (0, 0))
    return pl.pallas_call(
        _dest_kernel, grid=(n // TM_ROUTE,), in_specs=[spec, spec, const(SUBLANES)],
        out_specs=[spec, const(nb_pad), const(SUBLANES)],
        out_shape=[jax.ShapeDtypeStruct((n, LANES), I32), jax.ShapeDtypeStruct((nb_pad, LANES), I32),
                   jax.ShapeDtypeStruct((SUBLANES, LANES), I32)],
        compiler_params=_cparams(("arbitrary",)), name="dest",
    )(rt, rk, cnt)


def _slab_copy(src, src_row, dst, dst_row, sem):
    return pltpu.make_async_copy(src.at[pl.ds(pl.multiple_of(src_row * SUBLANES, SUBLANES), SUBLANES)],
                                 dst.at[pl.ds(pl.multiple_of(dst_row * SUBLANES, SUBLANES), SUBLANES)], sem)


def _dispatch_kernel(dest_ref, h2_ref, xs_in_hbm, xs_hbm, sem):
    del xs_in_hbm
    tm = TM_DISPATCH
    base = pl.program_id(0) * tm

    def issue(r, _):
        for k in range(2):
            _slab_copy(h2_ref, r, xs_hbm, dest_ref[2 * (base + r) + k], sem).start(priority=k)
        return 0

    lax.fori_loop(0, tm, issue, 0, unroll=GATHER_UNROLL)
    for _ in range(2):
        pltpu.make_async_copy(h2_ref, xs_hbm.at[pl.ds(0, tm * SUBLANES)], sem).wait()


def _dispatch_call(dest, h2s, xs_zero):
    n = h2s.shape[0] // SUBLANES
    anyspec = pl.BlockSpec(memory_space=pl.ANY)
    return pl.pallas_call(
        _dispatch_kernel,
        grid_spec=pltpu.PrefetchScalarGridSpec(
            num_scalar_prefetch=1, grid=(n // TM_DISPATCH,),
            in_specs=[pl.BlockSpec((TM_DISPATCH * SUBLANES, LANES), lambda i, d: (i, 0)), anyspec],
            out_specs=anyspec,
            scratch_shapes=[pltpu.SemaphoreType.DMA(())]),
        out_shape=jax.ShapeDtypeStruct(xs_zero.shape, F32),
        input_output_aliases={2: 0},
        compiler_params=_cparams(("arbitrary",)),
        name="dispatch",
    )(dest, h2s, xs_zero)


def _experts_kernel(be_ref, nused_ref, xs_ref, wg_ref, wu_ref, wd_ref, yb_ref, x_scr, wg_scr, wu_scr, wd_scr):
    i = pl.program_id(0)

    @pl.when(i >= nused_ref[0])
    def _():
        yb_ref[...] = jnp.zeros_like(yb_ref)

    @pl.when(i < nused_ref[0])
    def _():
        prev = be_ref[jnp.maximum(i - 1, 0)]

        @pl.when((i == 0) | (be_ref[i] != prev))
        def _():
            wg_scr[...] = wg_ref[0].astype(BF16)
            wu_scr[...] = wu_ref[0].astype(BF16)
            wd_scr[...] = wd_ref[0].astype(BF16)

        for c in range(D_MODEL // LANES):
            x_scr[:, c * LANES:(c + 1) * LANES] = xs_ref[pl.ds(c, BLK_E, stride=SUBLANES), :].astype(BF16)
        xb = x_scr[...]
        g = _dot(xb, wg_scr[...])
        u = _dot(xb, wu_scr[...])
        hid = (g * _sigmoid(g) * u).astype(BF16)
        y = _dot(hid, wd_scr[...])
        for c in range(D_MODEL // LANES):
            yb_ref[pl.ds(c, BLK_E, stride=SUBLANES), :] = y[:, c * LANES:(c + 1) * LANES]


def _experts_call(block_e, n_used, xs, w_gate, w_up, w_down):
    n_blocks = block_e.shape[0]
    slab = pl.BlockSpec((BLK_E * SUBLANES, LANES), lambda i, be, nu: (i, 0))
    return pl.pallas_call(
        _experts_kernel,
        grid_spec=pltpu.PrefetchScalarGridSpec(
            num_scalar_prefetch=2, grid=(n_blocks,),
            in_specs=[slab,
                      pl.BlockSpec((1, D_MODEL, D_EXPERT), lambda i, be, nu: (be[i], 0, 0)),
                      pl.BlockSpec((1, D_MODEL, D_EXPERT), lambda i, be, nu: (be[i], 0, 0)),
                      pl.BlockSpec((1, D_EXPERT, D_MODEL), lambda i, be, nu: (be[i], 0, 0))],
            out_specs=slab,
            scratch_shapes=[pltpu.VMEM((BLK_E, D_MODEL), BF16),
                            pltpu.VMEM((D_MODEL, D_EXPERT), BF16),
                            pltpu.VMEM((D_MODEL, D_EXPERT), BF16),
                            pltpu.VMEM((D_EXPERT, D_MODEL), BF16)]),
        out_shape=jax.ShapeDtypeStruct(xs.shape, F32),
        compiler_params=_cparams(("arbitrary",)),
        name="experts",
    )(block_e, n_used, xs, w_gate, w_up, w_down)


def _final_kernel(dest_ref, x1_ref, rt_ref, mod_ref, gfin_ref, yb_hbm, o_ref, g_scr, sem):
    tm = TM_FINAL
    step = pl.program_id(0)
    slot = step % 2
    slot_rows = 2 * tm * SUBLANES

    def gather(s, into):
        def issue(r, _):
            tok = s * tm + r
            for k in range(2):
                _slab_copy(yb_hbm, dest_ref[2 * tok + k], g_scr.at[into], k * tm + r,
                           sem.at[into]).start(priority=k)
            return 0
        lax.fori_loop(0, tm, issue, 0, unroll=GATHER_UNROLL)

    @pl.when(step == 0)
    def _():
        gather(0, 0)

    @pl.when(step + 1 < pl.num_programs(0))
    def _():
        gather(step + 1, 1 - slot)

    g_now = g_scr.at[slot]
    pltpu.make_async_copy(yb_hbm.at[pl.ds(0, slot_rows)], g_now, sem.at[slot]).wait()

    rt = rt_ref[...]
    lane = lax.broadcasted_iota(I32, (tm, LANES), 1)
    gw0 = jnp.sum(jnp.where(lane == 2, rt, 0.0), axis=-1, keepdims=True)
    gw1 = jnp.sum(jnp.where(lane == 3, rt, 0.0), axis=-1, keepdims=True)
    gt2 = mod_ref[0, 5:6, :]
    x1 = x1_ref[...]
    cols = []
    for c in range(D_MODEL // LANES):
        y0 = g_now[pl.ds(c, tm, stride=SUBLANES), :]
        y1 = g_now[pl.ds(tm * SUBLANES + c, tm, stride=SUBLANES), :]
        y = gw0 * y0 + gw1 * y1
        cols.append(x1[:, c * LANES:(c + 1) * LANES] + gt2[:, c * LANES:(c + 1) * LANES] * y)
    x2 = jnp.concatenate(cols, axis=1)
    var = jnp.mean(x2 * x2, axis=-1, keepdims=True)
    o_ref[...] = x2 * lax.rsqrt(var + RMS_EPS) * gfin_ref[...]


def _final_call(dest, x1, rt, mod3, g_final, yb, seq):
    n = x1.shape[0]
    tpb = seq // TM_FINAL
    return pl.pallas_call(
        _final_kernel,
        grid_spec=pltpu.PrefetchScalarGridSpec(
            num_scalar_prefetch=1, grid=(n // TM_FINAL,),
            in_specs=[pl.BlockSpec((TM_FINAL, D_MODEL), lambda i, d: (i, 0)),
                      pl.BlockSpec((TM_FINAL, LANES), lambda i, d: (i, 0)),
                      pl.BlockSpec((1, 6, D_MODEL), lambda i, d: (i // tpb, 0, 0)),
                      pl.BlockSpec((1, D_MODEL), lambda i, d: (0, 0)),
                      pl.BlockSpec(memory_space=pl.ANY)],
            out_specs=pl.BlockSpec((TM_FINAL, D_MODEL), lambda i, d: (i, 0)),
            scratch_shapes=[pltpu.VMEM((2, 2 * TM_FINAL * SUBLANES, LANES), F32),
                            pltpu.SemaphoreType.DMA((2,))]),
        out_shape=jax.ShapeDtypeStruct((n, D_MODEL), F32),
        compiler_params=_cparams(("arbitrary",)),
        name="final",
    )(dest, x1, rt, mod3, g_final, yb)


def _permute_w_in(w):
    o = np.cumsum([0, 512, 64, 64, 512, 64, 8, 512, 512, 512, 8, 1024, 1024])
    qa, ka, va, qi, ki, wi, qb, kb, vb, fb, ga, gb = [w[:, o[k]:o[k + 1]] for k in range(12)]
    aux = jnp.concatenate([wi, fb, jnp.zeros((w.shape[0], LANES - 2 * N_HEADS), w.dtype)], axis=1)
    return jnp.concatenate([qa, qi, qb, kb, vb, ga, gb, ka, ka, va, va, ki, ki, aux], axis=1).astype(BF16)


def _layer(x2, pos2, mod3, batch, seq, g_mix, w_in, b_forget, w_out_a, w_out_b, w_out, g_ffn,
           w_group, b_group, w_router, b_router, w_e_gate, w_e_up, w_e_down, g_final):
    n = x2.shape[0]
    inv_freq = ROPE_THETA ** (-jnp.arange(0, ROPE_DIM, 2, dtype=F32) / ROPE_DIM)
    jj = np.arange(LANES) % HEAD_DIM
    invf = jnp.where(jj < ROPE_DIM, inv_freq[jj % (ROPE_DIM // 2)], 0.0)[None, :].astype(F32)
    fbias = jnp.zeros((1, LANES), F32).at[0, N_HEADS:2 * N_HEADS].set(b_forget.astype(F32))

    (qa, qi, qb, kaug, vb, sga, sgb, ka2, va2, ki2, aux) = _inproj_call(
        x2, pos2, mod3, g_mix.reshape(1, D_MODEL), _permute_w_in(w_in), invf, fbias, batch, seq)

    r3 = lambda a: a.reshape(batch, seq, a.shape[-1])
    tr = lambda a: jnp.swapaxes(r3(a), 1, 2)
    tr_tiles = lambda a, tk: jnp.swapaxes(a.reshape(batch, seq // tk, tk, a.shape[-1]), 2, 3)
    aux_t = tr(aux[:, :4 * N_HEADS])
    obt = _fox_call(tr(qb), aux_t, r3(kaug), tr_tiles(vb, TK_FOX))
    oat = _dsa_call(tr(qa), tr(qi), aux_t, r3(ki2), r3(ka2), tr_tiles(va2[:, :HEAD_DIM], TK_DSA))
    oa = jnp.swapaxes(oat, 1, 2).reshape(n, 512)
    ob = jnp.swapaxes(obt, 1, 2).reshape(n, 512)

    w_rt = jnp.concatenate([w_router, w_group, jnp.zeros((D_MODEL, LANES - N_EXPERTS - N_GROUPS), F32)], axis=1)
    wr_hi = w_rt.astype(BF16)
    wr_lo = (w_rt - wr_hi.astype(F32)).astype(BF16)
    br = jnp.concatenate([b_router, b_group, jnp.zeros((LANES - N_EXPERTS - N_GROUPS,), F32)])[None, :]
    x1, h2s, lg = _outproj_call(oa, ob, sga, sgb, x2, mod3,
                                g_ffn.reshape(1, D_MODEL), w_out_a.astype(BF16), w_out_b.astype(BF16),
                                w_out.astype(BF16), wr_hi, wr_lo, br, seq)
    rt = _route_call(lg)

    n_slots = 2 * n + N_EXPERTS * BLK_E
    n_blocks = n_slots // BLK_E
    rk, cnt = _rank_call(rt)
    dd, be, nu = _dest_call(rt, rk, cnt, n_blocks)
    dest = dd[:, 0:2].reshape(-1)
    block_e = be[:n_blocks, 0]
    n_used = nu[0, 0:1]

    xs = _dispatch_call(dest, h2s, jnp.zeros((n_slots * SUBLANES, LANES), F32))
    yb = _experts_call(block_e, n_used, xs, w_e_gate, w_e_up, w_e_down)
    return _final_call(dest, x1, rt, mod3, g_final.reshape(1, D_MODEL), yb, seq)


def kernel(x, c, positions, w_mod, b_mod, g_mix, w_in, b_forget, w_out_a, w_out_b, w_out, g_ffn, w_group,
           b_group, w_router, b_router, w_e_gate, w_e_up, w_e_down, g_final):
    batch, seq, d = x.shape
    depth = w_mod.shape[0]
    assert depth == 1 and d == D_MODEL, "kernel fuses the final norm into the single layer"
    n = batch * seq
    c8 = jnp.zeros((8, d), F32).at[:batch].set(c)
    mod = _mod_call(c8, w_mod[0], b_mod[0][None, :])
    mod3 = mod[:batch].reshape(batch, 6, d)
    out = _layer(x.reshape(n, d), positions.reshape(n, 1), mod3, batch, seq, g_mix[0], w_in[0], b_forget[0],
                 w_out_a[0], w_out_b[0], w_out[0], g_ffn[0], w_group[0], b_group[0], w_router[0], b_router[0],
                 w_e_gate[0], w_e_up[0], w_e_down[0], g_final)
    return out.reshape(batch, seq, d)
```

```python
import functools

import numpy as np
import jax
import jax.numpy as jnp
from jax import lax
from jax.experimental import pallas as pl
from jax.experimental.pallas import tpu as pltpu

F32 = jnp.float32
BF16 = jnp.bfloat16
I32 = jnp.int32

D_MODEL = 1024
HEAD_DIM = 64
N_HEADS = 8
CHUNK_SHIFT = 6
TOPK = 256
ROPE_DIM = 16
ROPE_THETA = 500000.0
N_GROUPS = 4
EXPERTS_PER_GROUP = 8
N_EXPERTS = 32
D_EXPERT = 512
RMS_EPS = 1e-6
NEG_INF = -1e30
ATTN_SCALE = HEAD_DIM ** -0.5
IDX_SCALE = HEAD_DIM ** -0.5
LOG2E = 1.4426950408889634

LANES = 128
SUBLANES = 8
VMEM_LIMIT = 56 * 1024 * 1024

TM_IN = 512
TQ_FOX = 512
TK_FOX = 512
TQ_DSA = 256
TK_DSA = 512
TM_OUT = 512
TM_ROUTE = 1024
BLK_E = 256
TM_FINAL = 512
GATHER_UNROLL = 8

RADIX_BLOCK = 32 * SUBLANES
BOUND_MARGIN = 1.02
BOUND_SAFE = 40.0
FOX_BIAS_SLACK = 1.0
FOX_K = 256
FOX_AUG_ROWS = 16

_SEG_QA, _SEG_QI, _SEG_QB, _SEG_KB, _SEG_VB = 0, 512, 1024, 1536, 2048
_SEG_GA, _SEG_GB = 2560, 3584
_SEG_KA, _SEG_VA, _SEG_KI, _SEG_AUX = 4608, 4736, 4864, 4992
_NC_IN = 5120

_SENT_KEY = int(np.array(NEG_INF, np.float32).view(np.int32) ^ 0x7FFFFFFF)
_INT_MIN = -(2 ** 31)


def _cparams(sem):
    return pltpu.CompilerParams(dimension_semantics=sem, vmem_limit_bytes=VMEM_LIMIT)


def _dot(a, b):
    return jnp.dot(a, b, preferred_element_type=F32)


def _sigmoid(x):
    return 1.0 / (1.0 + jnp.exp(-x))


def _mod_kernel(c_ref, w_ref, b_ref, o_ref):
    c = c_ref[...]
    ca = c * _sigmoid(c)
    o_ref[...] = _dot(ca.astype(BF16), w_ref[...].astype(BF16)) + b_ref[...]


def _mod_call(c8, w_mod, b_mod):
    n_out = w_mod.shape[1]
    tn = 1024
    return pl.pallas_call(
        _mod_kernel,
        grid=(n_out // tn,),
        in_specs=[pl.BlockSpec((8, D_MODEL), lambda j: (0, 0)),
                  pl.BlockSpec((D_MODEL, tn), lambda j: (0, j)),
                  pl.BlockSpec((1, tn), lambda j: (0, j))],
        out_specs=pl.BlockSpec((8, tn), lambda j: (0, j)),
        out_shape=jax.ShapeDtypeStruct((8, n_out), F32),
        compiler_params=_cparams(("arbitrary",)),
        name="mod",
    )(c8, w_mod, b_mod)


def _inproj_kernel(x_ref, pos_ref, mod_ref, g_ref, w_ref, invf_ref, fbias_ref, sel_ref, ones_ref,
                   qa_ref, qi_ref, qb_ref, kb_ref, vb_ref, sga_ref, sgb_ref,
                   ka_ref, va_ref, ki_ref, aux_ref, h_scr, carry_scr):
    tm = TM_IN
    x = x_ref[...]
    var = jnp.mean(x * x, axis=-1, keepdims=True)
    tf = x * lax.rsqrt(var + RMS_EPS) * g_ref[...]
    sh = mod_ref[0, 0:1, :]
    sc = mod_ref[0, 1:2, :]
    h_scr[...] = (tf * (1.0 + sc) + sh).astype(BF16)
    hb = h_scr[...]

    lane = lax.broadcasted_iota(I32, (tm, LANES), 1)
    j = lane & (HEAD_DIM - 1)
    ang = pos_ref[...].astype(F32) * invf_ref[...]
    cs = jnp.cos(ang)
    sn = jnp.sin(ang)
    coef_next = jnp.where(j < ROPE_DIM // 2, -sn, 0.0)
    coef_prev = jnp.where((j >= ROPE_DIM // 2) & (j < ROPE_DIM), sn, 0.0)

    def rope(tc):
        return (tc * cs + pltpu.roll(tc, LANES - ROPE_DIM // 2, 1) * coef_next
                + pltpu.roll(tc, ROPE_DIM // 2, 1) * coef_prev)

    def seg(off, width):
        return _dot(hb, w_ref[:, off:off + width])

    t = seg(_SEG_QA, 512)
    for c in range(4):
        qa_ref[:, c * LANES:(c + 1) * LANES] = (
            rope(t[:, c * LANES:(c + 1) * LANES]) * (ATTN_SCALE * LOG2E)).astype(BF16)
    t = seg(_SEG_QI, 512)
    for c in range(4):
        qi_ref[:, c * LANES:(c + 1) * LANES] = (rope(t[:, c * LANES:(c + 1) * LANES]) * IDX_SCALE).astype(BF16)
    qb_ref[...] = (seg(_SEG_QB, 512) * (ATTN_SCALE * LOG2E)).astype(BF16)
    vb_ref[...] = seg(_SEG_VB, 512).astype(BF16)
    sga_ref[...] = _sigmoid(seg(_SEG_GA, 1024)).astype(BF16)
    sgb_ref[...] = _sigmoid(seg(_SEG_GB, 1024)).astype(BF16)
    ka = rope(seg(_SEG_KA, LANES))
    ka_ref[...] = ka.astype(BF16)
    va_ref[...] = seg(_SEG_VA, LANES).astype(BF16)
    ki_ref[...] = rope(seg(_SEG_KI, LANES)).astype(BF16)

    z = seg(_SEG_AUX, LANES)
    zf = z + fbias_ref[...]
    logf = jnp.minimum(zf, 0.0) - jnp.log(1.0 + jnp.exp(-jnp.abs(zf)))
    is_f = (lane >= N_HEADS) & (lane < 2 * N_HEADS)
    logf = jnp.where(is_f, logf, 0.0)

    def split3(v):
        hi = v.astype(BF16)
        r1 = v - hi.astype(F32)
        mid = r1.astype(BF16)
        return hi, mid, (r1 - mid.astype(F32)).astype(BF16)

    rr = lax.broadcasted_iota(I32, (tm, tm), 0)
    cc = lax.broadcasted_iota(I32, (tm, tm), 1)
    tri = jnp.where(cc <= rr, 1.0, 0.0).astype(BF16)
    p_hi, p_mid, p_lo = split3(logf)

    @pl.when(pl.program_id(1) == 0)
    def _():
        carry_scr[...] = jnp.zeros_like(carry_scr)

    parts = _dot(tri, jnp.concatenate([p_hi, p_mid, p_lo], axis=1))
    cum = parts[:, 0:LANES] + parts[:, LANES:2 * LANES] + parts[:, 2 * LANES:3 * LANES] + carry_scr[...]
    carry_scr[...] = cum[tm - 1:tm, :]
    cum2 = cum * LOG2E
    aux = jnp.where(lane < N_HEADS, z * (N_HEADS ** -0.5), jnp.where(is_f, cum2, 0.0))

    kb = seg(_SEG_KB, 512)
    first = lane < HEAD_DIM

    def half_norms(blk):
        sq = blk * blk
        return (jnp.sum(jnp.where(first, sq, 0.0), axis=-1, keepdims=True),
                jnp.sum(jnp.where(first, 0.0, sq), axis=-1, keepdims=True))

    for p in range(N_HEADS // 2):
        n_even, n_odd = half_norms(kb[:, p * LANES:(p + 1) * LANES])
        aux = jnp.where(lane == 2 * N_HEADS + 2 * p, n_even, jnp.where(lane == 2 * N_HEADS + 2 * p + 1, n_odd, aux))
    aux_ref[...] = jnp.where(lane == 3 * N_HEADS, half_norms(ka)[0], aux)

    c_hi, c_mid, c_lo = split3(cum2)
    kaug = _dot(jnp.concatenate([c_hi, c_mid, c_lo], axis=1), sel_ref[...]) + ones_ref[...]
    for p in range(N_HEADS // 2):
        kb_ref[:, p * FOX_K:p * FOX_K + LANES] = kb[:, p * LANES:(p + 1) * LANES].astype(BF16)
        kb_ref[:, p * FOX_K + LANES:(p + 1) * FOX_K] = kaug[:, p * LANES:(p + 1) * LANES].astype(BF16)


def _fox_routing_constants():
    sel = np.zeros((3, LANES, (N_HEADS // 2) * LANES), np.float32)
    ones = np.zeros((1, (N_HEADS // 2) * LANES), np.float32)
    for h in range(N_HEADS):
        for t in range(3):
            sel[t, N_HEADS + h, (h // 2) * LANES + 3 + 3 * (h % 2) + t] = -1.0
    for p in range(N_HEADS // 2):
        ones[0, p * LANES:p * LANES + 3] = 1.0
    return jnp.asarray(sel.reshape(3 * LANES, -1), BF16), jnp.asarray(ones, F32)


def _inproj_call(x2, pos2, mod3, g_mix, w_in_p, invf, fbias, batch, seq):
    n = x2.shape[0]
    tpb = seq // TM_IN
    row = lambda b, s: (b * tpb + s, 0)
    bf = lambda w: jax.ShapeDtypeStruct((n, w), BF16)
    sel, ones = _fox_routing_constants()
    out_shapes = [bf(512), bf(512), bf(512), bf((N_HEADS // 2) * FOX_K), bf(512), bf(1024), bf(1024),
                  bf(LANES), bf(LANES), bf(LANES), jax.ShapeDtypeStruct((n, LANES), F32)]
    out_specs = [pl.BlockSpec((TM_IN, s.shape[1]), row) for s in out_shapes]
    return pl.pallas_call(
        _inproj_kernel,
        grid=(batch, tpb),
        in_specs=[pl.BlockSpec((TM_IN, D_MODEL), row),
                  pl.BlockSpec((TM_IN, 1), row),
                  pl.BlockSpec((1, 6, D_MODEL), lambda b, s: (b, 0, 0)),
                  pl.BlockSpec((1, D_MODEL), lambda b, s: (0, 0)),
                  pl.BlockSpec((D_MODEL, _NC_IN), lambda b, s: (0, 0)),
                  pl.BlockSpec((1, LANES), lambda b, s: (0, 0)),
                  pl.BlockSpec((1, LANES), lambda b, s: (0, 0)),
                  pl.BlockSpec(sel.shape, lambda b, s: (0, 0)),
                  pl.BlockSpec(ones.shape, lambda b, s: (0, 0))],
        out_specs=out_specs,
        out_shape=out_shapes,
        scratch_shapes=[pltpu.VMEM((TM_IN, D_MODEL), BF16), pltpu.VMEM((1, LANES), F32)],
        compiler_params=_cparams(("arbitrary", "arbitrary")),
        name="inproj",
    )(x2, pos2, mod3, g_mix, w_in_p, invf, fbias, sel, ones)


def _softmax_max(s, m_ref):
    m_ref[...] = jnp.maximum(m_ref[...], jnp.max(s, axis=0, keepdims=True))


def _softmax_accumulate(s, m_ref, l_ref, acc_ref, v_t):
    p = jnp.exp2(s - m_ref[...])
    l_ref[...] = l_ref[...] + jnp.sum(p, axis=0, keepdims=True)
    acc_ref[...] = acc_ref[...] + _dot(v_t, p.astype(BF16))


def _softmax_scratch(tq):
    return ([pltpu.VMEM((1, tq), F32) for _ in range(2 * N_HEADS)]
            + [pltpu.VMEM((HEAD_DIM, tq), F32) for _ in range(N_HEADS)])


def _softmax_split(refs):
    return refs[:N_HEADS], refs[N_HEADS:2 * N_HEADS], refs[2 * N_HEADS:3 * N_HEADS]


def _softmax_init(m_refs, l_refs, acc_refs):
    for h in range(N_HEADS):
        m_refs[h][...] = jnp.full(m_refs[h].shape, NEG_INF, F32)
        l_refs[h][...] = jnp.zeros(l_refs[h].shape, F32)
        acc_refs[h][...] = jnp.zeros(acc_refs[h].shape, F32)


def _softmax_finish(o_ref, l_refs, acc_refs):
    for h in range(N_HEADS):
        o_ref[0, h * HEAD_DIM:(h + 1) * HEAD_DIM, :] = (acc_refs[h][...] * (1.0 / l_refs[h][...])).astype(BF16)


def _logit_bounds(qt_ref, kn2, slack):
    k_max = jnp.sqrt(jnp.max(kn2, axis=-1, keepdims=True))
    bounds = []
    worst = jnp.float32(0.0)
    for h in range(N_HEADS):
        qh = qt_ref[0, h * HEAD_DIM:(h + 1) * HEAD_DIM, :].astype(F32)
        q_norm = jnp.sqrt(jnp.sum(qh * qh, axis=0, keepdims=True))
        b = q_norm * k_max[h:h + 1, :] * BOUND_MARGIN + slack
        bounds.append(b)
        worst = jnp.maximum(worst, jnp.max(b))
    return bounds, worst <= BOUND_SAFE


def _fox_kernel(qt_ref, cumt_ref, kn_ref, kaug_ref, vt_ref, o_ref, rhs_scr, *softmax_refs):
    m_scr, l_scr, acc_scr = _softmax_split(softmax_refs)
    tq, tk = TQ_FOX, TK_FOX
    i = pl.program_id(1)
    q0 = i * tq

    cq = cumt_ref[0]
    c_hi = cq.astype(BF16).astype(F32)
    c_r = cq - c_hi
    c_mid = c_r.astype(BF16).astype(F32)
    c_lo = c_r - c_mid
    row_q = lax.broadcasted_iota(I32, (LANES, tq), 0)
    row_a = lax.broadcasted_iota(I32, (FOX_AUG_ROWS, 2 * tq), 0)
    second = lax.broadcasted_iota(I32, (FOX_AUG_ROWS, 2 * tq), 1) >= tq
    for p in range(N_HEADS // 2):
        qp = qt_ref[0, p * LANES:(p + 1) * LANES, :]
        zq = jnp.zeros_like(qp)
        rhs_scr[p, 0:LANES, 0:tq] = jnp.where(row_q < HEAD_DIM, qp, zq)
        rhs_scr[p, 0:LANES, tq:2 * tq] = jnp.where(row_q >= HEAD_DIM, qp, zq)
        pair_row = lambda a: jnp.concatenate([a[2 * p:2 * p + 1, :], a[2 * p + 1:2 * p + 2, :]], axis=1)
        aug = jnp.where(row_a == 0, pair_row(c_hi),
              jnp.where(row_a == 1, pair_row(c_mid),
              jnp.where(row_a == 2, pair_row(c_lo),
              jnp.where((row_a >= 3) & (row_a < 6), jnp.where(second, 0.0, 1.0),
              jnp.where((row_a >= 6) & (row_a < 9), jnp.where(second, 1.0, 0.0), 0.0)))))
        rhs_scr[p, LANES:LANES + FOX_AUG_ROWS, :] = aug.astype(BF16)
        rhs_scr[p, LANES + FOX_AUG_ROWS:FOX_K, :] = jnp.zeros((FOX_K - LANES - FOX_AUG_ROWS, 2 * tq), BF16)

    _softmax_init(m_scr, l_scr, acc_scr)
    key_j = lax.broadcasted_iota(I32, (tk, tq), 0)
    qry_i = q0 + lax.broadcasted_iota(I32, (tk, tq), 1)

    def tile(kt, masked, second_pass):
        k0 = pl.multiple_of(kt * tk, tk)
        for p in range(N_HEADS // 2):
            st = _dot(kaug_ref[0, pl.ds(k0, tk), p * FOX_K:(p + 1) * FOX_K], rhs_scr[p])
            for hh in range(2):
                h = 2 * p + hh
                s = st[:, hh * tq:(hh + 1) * tq]
                if masked:
                    s = jnp.where(k0 + key_j <= qry_i, s, NEG_INF)
                if second_pass:
                    _softmax_accumulate(s, m_scr[h], l_scr[h], acc_scr[h],
                                        vt_ref[0, kt, h * HEAD_DIM:(h + 1) * HEAD_DIM, :])
                else:
                    _softmax_max(s, m_scr[h])

    n_full = i * (tq // tk)

    def sweep(second_pass):
        def full_tile(kt, carry):
            tile(kt, False, second_pass)
            return carry

        lax.fori_loop(0, n_full, full_tile, 0)
        for d in range(tq // tk):
            tile(n_full + d, True, second_pass)

    bounds, safe = _logit_bounds(qt_ref, kn_ref[0], FOX_BIAS_SLACK)

    @pl.when(safe)
    def _():
        for h in range(N_HEADS):
            m_scr[h][...] = bounds[h]
        sweep(True)

    @pl.when(jnp.logical_not(safe))
    def _():
        sweep(False)
        sweep(True)

    _softmax_finish(o_ref, l_scr, acc_scr)


def _fox_call(qbt, aux_t, kaug3, vbt4):
    batch, _, seq = qbt.shape
    return pl.pallas_call(
        _fox_kernel,
        grid=(batch, seq // TQ_FOX),
        in_specs=[pl.BlockSpec((1, 512, TQ_FOX), lambda b, i: (b, 0, i)),
                  pl.BlockSpec((1, N_HEADS, TQ_FOX), lambda b, i: (b, 1, i)),
                  pl.BlockSpec((1, N_HEADS, seq), lambda b, i: (b, 2, 0)),
                  pl.BlockSpec((1, seq, (N_HEADS // 2) * FOX_K), lambda b, i: (b, 0, 0)),
                  pl.BlockSpec((1, seq // TK_FOX, 512, TK_FOX), lambda b, i: (b, 0, 0, 0))],
        out_specs=pl.BlockSpec((1, 512, TQ_FOX), lambda b, i: (b, 0, i)),
        out_shape=jax.ShapeDtypeStruct((batch, 512, seq), BF16),
        scratch_shapes=[pltpu.VMEM((N_HEADS // 2, FOX_K, 2 * TQ_FOX), BF16)] + _softmax_scratch(TQ_FOX),
        compiler_params=_cparams(("arbitrary", "arbitrary")),
        name="fox",
    )(qbt, aux_t, aux_t, kaug3, vbt4)


def _bit_transpose32(words):
    a = list(words)
    j, mask = 16, 0x0000FFFF
    while j:
        mask_i = int(np.array(mask, np.uint32).view(np.int32))
        k = 0
        while k < 32:
            t = (a[k] ^ lax.shift_right_logical(a[k + j], j)) & mask_i
            a[k] = a[k] ^ t
            a[k + j] = a[k + j] ^ lax.shift_left(t, j)
            k = (k + j + 1) & ~j
        j >>= 1
        mask = (mask ^ (mask << j)) & 0xFFFFFFFF
    return a


def _dsa_kernel(qat_ref, qit_ref, wit_ref, kn_ref, ki_ref, ka_ref, vat_ref, o_ref,
                aqi_scr, aqa_scr, sc_scr, bias_scr, plane_scr, alive_scr, *softmax_refs, seq):
    m_scr, l_scr, acc_scr = _softmax_split(softmax_refs)
    tq, tk = TQ_DSA, TK_DSA
    blk_per_tile = tk // RADIX_BLOCK
    n_blocks = seq // RADIX_BLOCK
    i = pl.program_id(1)
    q0 = i * tq
    nk = lax.shift_right_logical(q0 + tq + tk - 1, int(np.log2(tk)))
    n_rest = (seq - nk * tk).astype(F32)

    row_q = lax.broadcasted_iota(I32, (LANES, tq), 0)
    for h in range(N_HEADS):
        keep = (row_q < HEAD_DIM) if h % 2 == 0 else (row_q >= HEAD_DIM)
        rows = slice((h // 2) * LANES, (h // 2 + 1) * LANES)
        qi_blk = qit_ref[0, rows, :]
        qa_blk = qat_ref[0, rows, :]
        aqi_scr[:, h * tq:(h + 1) * tq] = jnp.where(keep, qi_blk, jnp.zeros_like(qi_blk))
        aqa_scr[:, h * tq:(h + 1) * tq] = jnp.where(keep, qa_blk, jnp.zeros_like(qa_blk))

    w = wit_ref[0]
    key_j = lax.broadcasted_iota(I32, (tk, tq), 0)
    qry_chunk = lax.shift_right_logical(q0 + lax.broadcasted_iota(I32, (tk, tq), 1), CHUNK_SHIFT)

    def admissible(k0):
        return lax.shift_right_logical(k0 + key_j, CHUNK_SHIFT) <= qry_chunk

    def p1(kt, carry, masked):
        k0 = pl.multiple_of(kt * tk, tk)
        rel = _dot(ki_ref[0, pl.ds(k0, tk), :], aqi_scr[...])
        sc = w[0:1, :] * jnp.maximum(rel[:, 0:tq], 0.0)
        for h in range(1, N_HEADS):
            sc = sc + w[h:h + 1, :] * jnp.maximum(rel[:, h * tq:(h + 1) * tq], 0.0)
        if masked:
            sc = jnp.where(admissible(k0), sc, NEG_INF)
        sc = jnp.where(sc == 0.0, 0.0, sc)
        sc_scr[kt] = sc
        b = pltpu.bitcast(sc, I32)
        key = b ^ (lax.shift_right_arithmetic(b, 31) & 0x7FFFFFFF)
        for half in range(blk_per_tile):
            base = half * RADIX_BLOCK
            planes = _bit_transpose32([key[base + SUBLANES * m:base + SUBLANES * (m + 1), :] ^ _INT_MIN
                                       for m in range(32)])
            bl = kt * blk_per_tile + half
            for p in range(32):
                plane_scr[bl, p] = planes[p]
            alive_scr[bl] = jnp.full((SUBLANES, tq), -1, I32)
        return carry

    n_unmasked = lax.shift_right_logical(q0, int(np.log2(tk)))
    lax.fori_loop(0, n_unmasked, functools.partial(p1, masked=False), 0)
    lax.fori_loop(n_unmasked, nk, functools.partial(p1, masked=True), 0)

    def clear_block(bl, carry):
        plane_scr[bl] = jnp.zeros((32, SUBLANES, tq), I32)
        alive_scr[bl] = jnp.zeros((SUBLANES, tq), I32)
        return carry

    lax.fori_loop(nk * blk_per_tile, n_blocks, clear_block, 0)

    sent_u = jnp.int32(_SENT_KEY ^ _INT_MIN)

    def bit_step(p, carry):
        k_rem, thr_bits, rest_alive = carry
        shift = 31 - p
        sent_bit = lax.shift_right_logical(sent_u, shift) & 1
        ones = [alive_scr[bl] & plane_scr[bl, p] for bl in range(n_blocks)]
        tot = lax.population_count(ones[0])
        for bl in range(1, n_blocks):
            tot = tot + lax.population_count(ones[bl])
        cnt = jnp.sum(tot.astype(F32), axis=0, keepdims=True)
        cnt = cnt + jnp.where((rest_alive != 0) & (sent_bit != 0), n_rest, 0.0)
        take1 = cnt >= k_rem
        for bl in range(n_blocks):
            alive_scr[bl] = jnp.where(take1, ones[bl], alive_scr[bl] & ~plane_scr[bl, p])
        k_rem = jnp.where(take1, k_rem, k_rem - cnt)
        thr_bits = thr_bits | jnp.where(take1, lax.shift_left(jnp.int32(1), shift), 0)
        rest_alive = jnp.where(take1 == (sent_bit != 0), rest_alive, 0)
        return k_rem, thr_bits, rest_alive

    kf = float(TOPK)
    _, thr_bits, _ = lax.fori_loop(
        0, 32, bit_step,
        (jnp.full((1, tq), kf, F32), jnp.zeros((1, tq), I32), jnp.ones((1, tq), I32)))
    thr_key = thr_bits ^ _INT_MIN
    thr0 = pltpu.bitcast(thr_key ^ (lax.shift_right_arithmetic(thr_key, 31) & 0x7FFFFFFF), F32)

    def rank_counts(thr):
        def body(kt, c):
            sc = sc_scr[kt]
            return (c[0] + jnp.sum(jnp.where(sc >= thr, 1.0, 0.0), axis=0, keepdims=True),
                    c[1] + jnp.sum(jnp.where(sc > thr, 1.0, 0.0), axis=0, keepdims=True))
        z = jnp.zeros((1, tq), F32)
        n_ge, n_gt = lax.fori_loop(0, nk, body, (z, z))
        return (n_ge + jnp.where(thr <= NEG_INF, n_rest, 0.0), n_gt + jnp.where(thr < NEG_INF, n_rest, 0.0))

    def misplaced(state):
        _, n_ge, n_gt = state
        return jnp.max(jnp.where((n_ge < kf) | (n_gt >= kf), 1.0, 0.0)) > 0.0

    def step_threshold(state):
        thr, n_ge, n_gt = state

        def body(kt, c):
            sc = sc_scr[kt]
            below = jnp.max(jnp.where(sc < thr, sc, -jnp.inf), axis=0, keepdims=True)
            above = jnp.min(jnp.where(sc > thr, sc, jnp.inf), axis=0, keepdims=True)
            return jnp.maximum(c[0], below), jnp.minimum(c[1], above)

        below, above = lax.fori_loop(0, nk, body, (jnp.full((1, tq), -jnp.inf, F32), jnp.full((1, tq), jnp.inf, F32)))
        has_rest = n_rest > 0.0
        below = jnp.where(has_rest & (thr > NEG_INF), jnp.maximum(below, NEG_INF), below)
        above = jnp.where(has_rest & (thr < NEG_INF), jnp.minimum(above, NEG_INF), above)
        thr = jnp.where(n_ge < kf, below, jnp.where(n_gt >= kf, above, thr))
        return (thr,) + rank_counts(thr)

    thr, n_ge, n_gt = lax.while_loop(misplaced, step_threshold, (thr0,) + rank_counts(thr0))
    need = kf - n_gt
    tie_free = jnp.min(jnp.where(n_ge == kf, 1.0, 0.0)) > 0.0

    tr = lax.broadcasted_iota(I32, (tk, tk), 0)
    tc = lax.broadcasted_iota(I32, (tk, tk), 1)
    tri = jnp.where(tc <= tr, 1.0, 0.0).astype(BF16)
    _softmax_init(m_scr, l_scr, acc_scr)

    def selection_bias(kt, tie):
        sc = sc_scr[kt]
        eq = sc == thr
        pref = _dot(tri, jnp.where(eq, 1.0, 0.0).astype(BF16)) + tie
        sel = (sc > thr) | (eq & (pref <= need))
        return jnp.where(sel & admissible(kt * tk), 0.0, NEG_INF), pref[tk - 1:tk, :]

    def logits(kt):
        return _dot(ka_ref[0, pl.ds(pl.multiple_of(kt * tk, tk), tk), :], aqa_scr[...])

    def accumulate(kt, bias):
        logit = logits(kt)
        v_t = vat_ref[0, kt]
        for h in range(N_HEADS):
            _softmax_accumulate(logit[:, h * tq:(h + 1) * tq] + bias, m_scr[h], l_scr[h], acc_scr[h], v_t)

    def p3_single(kt, tie):
        bias, tie = selection_bias(kt, tie)
        accumulate(kt, bias)
        return tie

    def p3_single_tie_free(kt, carry):
        accumulate(kt, jnp.where(sc_scr[kt] >= thr, 0.0, NEG_INF))
        return carry

    def p3_max(kt, tie):
        bias, tie = selection_bias(kt, tie)
        bias_scr[kt] = bias
        logit = logits(kt)
        for h in range(N_HEADS):
            _softmax_max(logit[:, h * tq:(h + 1) * tq] + bias, m_scr[h])
        return tie

    def p3_accumulate(kt, carry):
        accumulate(kt, bias_scr[kt])
        return carry

    kn2 = jnp.broadcast_to(kn_ref[0, 0:1, :], (N_HEADS, seq))
    bounds, safe = _logit_bounds(qat_ref, kn2, 0.0)
    no_tie = jnp.zeros((1, tq), F32)

    @pl.when(safe)
    def _():
        for h in range(N_HEADS):
            m_scr[h][...] = bounds[h]

    @pl.when(safe & tie_free)
    def _():
        lax.fori_loop(0, nk, p3_single_tie_free, 0)

    @pl.when(safe & jnp.logical_not(tie_free))
    def _():
        lax.fori_loop(0, nk, p3_single, no_tie)

    @pl.when(jnp.logical_not(safe))
    def _():
        lax.fori_loop(0, nk, p3_max, no_tie)
        lax.fori_loop(0, nk, p3_accumulate, 0)

    _softmax_finish(o_ref, l_scr, acc_scr)


def _dsa_call(qat, qit, aux_t, ki3, ka3, vat4):
    batch, _, seq = qat.shape
    qspec = pl.BlockSpec((1, 512, TQ_DSA), lambda b, i: (b, 0, i))
    kspec = pl.BlockSpec((1, seq, LANES), lambda b, i: (b, 0, 0))
    return pl.pallas_call(
        functools.partial(_dsa_kernel, seq=seq),
        grid=(batch, seq // TQ_DSA),
        in_specs=[qspec, qspec,
                  pl.BlockSpec((1, N_HEADS, TQ_DSA), lambda b, i: (b, 0, i)),
                  pl.BlockSpec((1, N_HEADS, seq), lambda b, i: (b, 3, 0)),
                  kspec, kspec,
                  pl.BlockSpec((1, seq // TK_DSA, HEAD_DIM, TK_DSA), lambda b, i: (b, 0, 0, 0))],
        out_specs=qspec,
        out_shape=jax.ShapeDtypeStruct((batch, 512, seq), BF16),
        scratch_shapes=[pltpu.VMEM((LANES, N_HEADS * TQ_DSA), BF16),
                        pltpu.VMEM((LANES, N_HEADS * TQ_DSA), BF16),
                        pltpu.VMEM((seq // TK_DSA, TK_DSA, TQ_DSA), F32),
                        pltpu.VMEM((seq // TK_DSA, TK_DSA, TQ_DSA), F32),
                        pltpu.VMEM((seq // RADIX_BLOCK, 32, SUBLANES, TQ_DSA), I32),
                        pltpu.VMEM((seq // RADIX_BLOCK, SUBLANES, TQ_DSA), I32)] + _softmax_scratch(TQ_DSA),
        compiler_params=_cparams(("arbitrary", "arbitrary")),
        name="dsa",
    )(qat, qit, aux_t, aux_t, ki3, ka3, vat4)


def _outproj_kernel(oa_ref, ob_ref, sga_ref, sgb_ref, x_ref, mod_ref, gffn_ref,
                    woa_ref, wob_ref, wo_ref, wr_hi_ref, wr_lo_ref, br_ref,
                    x1_ref, h2_ref, lg_ref):
    tm = TM_OUT
    ya = _dot(oa_ref[...], woa_ref[...])
    yb = _dot(ob_ref[...], wob_ref[...])
    merged = (sga_ref[...].astype(F32) * ya + sgb_ref[...].astype(F32) * yb).astype(BF16)
    mix = _dot(merged, wo_ref[...])
    gt1 = mod_ref[0, 2:3, :]
    sh2 = mod_ref[0, 3:4, :]
    sc2 = mod_ref[0, 4:5, :]
    x1 = x_ref[...] + gt1 * mix
    x1_ref[...] = x1
    var = jnp.mean(x1 * x1, axis=-1, keepdims=True)
    h2 = x1 * lax.rsqrt(var + RMS_EPS) * gffn_ref[...] * (1.0 + sc2) + sh2
    for c in range(D_MODEL // LANES):
        h2_ref[pl.ds(c, tm, stride=SUBLANES), :] = h2[:, c * LANES:(c + 1) * LANES]
    hi = h2.astype(BF16)
    lo = (h2 - hi.astype(F32)).astype(BF16)
    lg_ref[...] = (_dot(hi, wr_hi_ref[...]) + _dot(hi, wr_lo_ref[...]) + _dot(lo, wr_hi_ref[...])
                   + br_ref[...])


def _outproj_call(oa2, ob2, sga, sgb, x2, mod3, g_ffn, woa, wob, wo, wr_hi, wr_lo, br, seq):
    n = x2.shape[0]
    tpb = seq // TM_OUT
    row = lambda w: pl.BlockSpec((TM_OUT, w), lambda i: (i, 0))
    full = lambda a: pl.BlockSpec(a.shape, lambda i: (0,) * a.ndim)
    return pl.pallas_call(
        _outproj_kernel,
        grid=(n // TM_OUT,),
        in_specs=[row(512), row(512), row(1024), row(1024), row(D_MODEL),
                  pl.BlockSpec((1, 6, D_MODEL), lambda i: (i // tpb, 0, 0)),
                  full(g_ffn), full(woa), full(wob), full(wo), full(wr_hi), full(wr_lo), full(br)],
        out_specs=[row(D_MODEL),
                   pl.BlockSpec((TM_OUT * SUBLANES, LANES), lambda i: (i, 0)),
                   row(LANES)],
        out_shape=[jax.ShapeDtypeStruct((n, D_MODEL), F32),
                   jax.ShapeDtypeStruct((n * SUBLANES, LANES), F32),
                   jax.ShapeDtypeStruct((n, LANES), F32)],
        compiler_params=_cparams(("arbitrary",)),
        name="outproj",
    )(oa2, ob2, sga, sgb, x2, mod3, g_ffn, woa, wob, wo, wr_hi, wr_lo, br)


def _route_kernel(lg_ref, rt_ref):
    tm = TM_ROUTE
    lg = lg_ref[...]
    lane = lax.broadcasted_iota(I32, (tm, LANES), 1)
    big = jnp.int32(LANES)
    ninf = -jnp.inf
    gmask = (lane >= N_EXPERTS) & (lane < N_EXPERTS + N_GROUPS)
    g = jnp.where(gmask, lg, ninf)
    gmax = jnp.max(g, axis=-1, keepdims=True)
    grp = jnp.min(jnp.where(g == gmax, lane - N_EXPERTS, big), axis=-1, keepdims=True)
    p_grp = 1.0 / jnp.sum(jnp.where(gmask, jnp.exp(lg - gmax), 0.0), axis=-1, keepdims=True)
    lo = grp * EXPERTS_PER_GROUP
    emask = (lane >= lo) & (lane < lo + EXPERTS_PER_GROUP)
    ev = jnp.where(emask, lg, ninf)
    v0 = jnp.max(ev, axis=-1, keepdims=True)
    i0 = jnp.min(jnp.where(emask & (ev == v0), lane, big), axis=-1, keepdims=True)
    rest = emask & (lane != i0)
    ev1 = jnp.where(rest, lg, ninf)
    v1 = jnp.max(ev1, axis=-1, keepdims=True)
    i1 = jnp.min(jnp.where(rest & (ev1 == v1), lane, big), axis=-1, keepdims=True)
    e1 = jnp.exp(v1 - v0)
    w0 = p_grp / (1.0 + e1)
    w1 = p_grp * e1 / (1.0 + e1)
    rt_ref[...] = jnp.where(lane == 0, i0.astype(F32),
                            jnp.where(lane == 1, i1.astype(F32),
                                      jnp.where(lane == 2, w0, jnp.where(lane == 3, w1, 0.0))))


def _route_call(lg):
    n = lg.shape[0]
    spec = pl.BlockSpec((TM_ROUTE, LANES), lambda i: (i, 0))
    return pl.pallas_call(
        _route_kernel, grid=(n // TM_ROUTE,), in_specs=[spec], out_specs=spec,
        out_shape=jax.ShapeDtypeStruct((n, LANES), F32),
        compiler_params=_cparams(("arbitrary",)), name="route",
    )(lg)


def _expert_onehots(rt, tm):
    lane = lax.broadcasted_iota(I32, (tm, LANES), 1)
    lane_f = lane.astype(F32)
    e0 = jnp.sum(jnp.where(lane == 0, rt, 0.0), axis=-1, keepdims=True)
    e1 = jnp.sum(jnp.where(lane == 1, rt, 0.0), axis=-1, keepdims=True)
    return lane, lane_f == e0, lane_f == e1


def _rank_kernel(rt_ref, rk_ref, cnt_ref, carry_scr):
    tm = TM_ROUTE

    @pl.when(pl.program_id(0) == 0)
    def _():
        carry_scr[...] = jnp.zeros_like(carry_scr)

    lane, is0, is1 = _expert_onehots(rt_ref[...], tm)
    hits = jnp.where(is0, 1.0, 0.0) + jnp.where(is1, 1.0, 0.0)
    rr = lax.broadcasted_iota(I32, (tm, tm), 0)
    cc = lax.broadcasted_iota(I32, (tm, tm), 1)
    before = jnp.where(cc < rr, 1.0, 0.0).astype(BF16)
    seen = _dot(before, hits.astype(BF16)) + carry_scr[...]
    r0 = jnp.sum(jnp.where(is0, seen, 0.0), axis=-1, keepdims=True)
    r1 = jnp.sum(jnp.where(is1, seen, 0.0), axis=-1, keepdims=True)
    rk_ref[...] = jnp.where(lane == 0, r0, jnp.where(lane == 1, r1, 0.0))
    carry_scr[...] = carry_scr[...] + jnp.sum(hits, axis=0, keepdims=True)
    cnt_ref[...] = jnp.broadcast_to(carry_scr[...], cnt_ref.shape)


def _rank_call(rt):
    n = rt.shape[0]
    spec = pl.BlockSpec((TM_ROUTE, LANES), lambda i: (i, 0))
    return pl.pallas_call(
        _rank_kernel, grid=(n // TM_ROUTE,), in_specs=[spec],
        out_specs=[spec, pl.BlockSpec((SUBLANES, LANES), lambda i: (0, 0))],
        out_shape=[jax.ShapeDtypeStruct((n, LANES), F32), jax.ShapeDtypeStruct((SUBLANES, LANES), F32)],
        scratch_shapes=[pltpu.VMEM((1, LANES), F32)],
        compiler_params=_cparams(("arbitrary",)), name="rank",
    )(rt)


def _dest_kernel(rt_ref, rk_ref, cnt_ref, dd_ref, be_ref, nu_ref, seg_ref):
    tm = TM_ROUTE
    lane, is0, is1 = _expert_onehots(rt_ref[...], tm)
    blocks = jnp.floor((cnt_ref[...] + (BLK_E - 1)) * (1.0 / BLK_E))
    er = lax.broadcasted_iota(I32, (LANES, LANES), 0)
    ec = lax.broadcasted_iota(I32, (LANES, LANES), 1)
    upto = jnp.where(er <= ec, 1.0, 0.0).astype(BF16)
    bend = _dot(blocks.astype(BF16), upto)
    pstart = (bend[0:1, :] - blocks[0:1, :]) * BLK_E
    rk = rk_ref[...]
    r0 = jnp.sum(jnp.where(lane == 0, rk, 0.0), axis=-1, keepdims=True)
    r1 = jnp.sum(jnp.where(lane == 1, rk, 0.0), axis=-1, keepdims=True)
    d0 = jnp.sum(jnp.where(is0, pstart, 0.0), axis=-1, keepdims=True) + r0
    d1 = jnp.sum(jnp.where(is1, pstart, 0.0), axis=-1, keepdims=True) + r1
    dd_ref[...] = jnp.where(lane == 0, d0, jnp.where(lane == 1, d1, 0.0)).astype(I32)
    nb = be_ref.shape[0]
    blk = lax.broadcasted_iota(I32, (nb, LANES), 0).astype(F32)
    lane_b = lax.broadcasted_iota(I32, (nb, LANES), 1)
    ended = jnp.where((lane_b < N_EXPERTS) & (bend[0:1, :] <= blk), 1.0, 0.0)
    be = jnp.minimum(jnp.sum(ended, axis=-1, keepdims=True), float(N_EXPERTS - 1))
    be_ref[...] = jnp.broadcast_to(be, be_ref.shape).astype(I32)
    lane_c = lax.broadcasted_iota(I32, (SUBLANES, LANES), 1)
    used = jnp.sum(jnp.where(lane_c == N_EXPERTS - 1, bend, 0.0), axis=-1, keepdims=True)
    nu_ref[...] = jnp.broadcast_to(used, nu_ref.shape).astype(I32)
    row_c = lax.broadcasted_iota(I32, (SUBLANES, LANES), 0)
    seg_ref[...] = jnp.where(row_c == 0, pstart + cnt_ref[...], bend * BLK_E).astype(I32)


def _dest_call(rt, rk, cnt, n_blocks):
    n = rt.shape[0]
    nb_pad = -(-n_blocks // SUBLANES) * SUBLANES
    spec = pl.BlockSpec((TM_ROUTE, LANES), lambda i: (i, 0))
    const = lambda rows: pl.BlockSpec((rows, LANES), lambda i: (0, 0))
    return pl.pallas_call(
        _dest_kernel, grid=(n // TM_ROUTE,), in_specs=[spec, spec, const(SUBLANES)],
        out_specs=[spec, const(nb_pad), const(SUBLANES), const(SUBLANES)],
        out_shape=[jax.ShapeDtypeStruct((n, LANES), I32), jax.ShapeDtypeStruct((nb_pad, LANES), I32),
                   jax.ShapeDtypeStruct((SUBLANES, LANES), I32), jax.ShapeDtypeStruct((SUBLANES, LANES), I32)],
        compiler_params=_cparams(("arbitrary",)), name="dest",
    )(rt, rk, cnt)


def _slab_copy(src, src_row, dst, dst_row, sem):
    return pltpu.make_async_copy(src.at[pl.ds(pl.multiple_of(src_row * SUBLANES, SUBLANES), SUBLANES)],
                                 dst.at[pl.ds(pl.multiple_of(dst_row * SUBLANES, SUBLANES), SUBLANES)], sem)


def _invert_kernel(dest_ref, first_pad_ref, seg_end_ref, tok_ref, *, n_tokens, n_slots):
    def scatter(t, carry):
        tok_ref[dest_ref[2 * t]] = t
        tok_ref[dest_ref[2 * t + 1]] = t
        return carry

    lax.fori_loop(0, n_tokens, scatter, 0, unroll=GATHER_UNROLL)

    def clear(s, carry):
        tok_ref[s] = 0
        return carry

    for e in range(N_EXPERTS):
        lax.fori_loop(first_pad_ref[e], seg_end_ref[e], clear, 0)
    lax.fori_loop(seg_end_ref[N_EXPERTS - 1], n_slots, clear, 0)


def _invert_call(dest, first_pad, seg_end, n_slots):
    n_tokens = dest.shape[0] // 2
    return pl.pallas_call(
        functools.partial(_invert_kernel, n_tokens=n_tokens, n_slots=n_slots),
        grid_spec=pltpu.PrefetchScalarGridSpec(
            num_scalar_prefetch=3, grid=(1,), in_specs=[],
            out_specs=pl.BlockSpec(memory_space=pltpu.SMEM)),
        out_shape=jax.ShapeDtypeStruct((n_slots,), I32),
        compiler_params=_cparams(("arbitrary",)),
        name="invert",
    )(dest, first_pad, seg_end)


def _experts_kernel(be_ref, nused_ref, tok_ref, h2_hbm, wg_ref, wu_ref, wd_ref, yb_ref,
                    xg_scr, sem, x_scr, wg_scr, wu_scr, wd_scr):
    i = pl.program_id(0)
    slot = i % 2

    def gather(b, into):
        def issue(j, carry):
            for k in range(2):
                r = 2 * j + k
                _slab_copy(h2_hbm, tok_ref[b * BLK_E + r], xg_scr.at[into], r, sem.at[into]).start(priority=k)
            return carry
        lax.fori_loop(0, BLK_E // 2, issue, 0, unroll=GATHER_UNROLL)

    @pl.when((i == 0) & (nused_ref[0] > 0))
    def _():
        gather(0, 0)

    @pl.when(i + 1 < nused_ref[0])
    def _():
        gather(i + 1, 1 - slot)

    @pl.when(i >= nused_ref[0])
    def _():
        yb_ref[...] = jnp.zeros_like(yb_ref)

    @pl.when(i < nused_ref[0])
    def _():
        prev = be_ref[jnp.maximum(i - 1, 0)]

        @pl.when((i == 0) | (be_ref[i] != prev))
        def _():
            wg_scr[...] = wg_ref[0].astype(BF16)
            wu_scr[...] = wu_ref[0].astype(BF16)
            wd_scr[...] = wd_ref[0].astype(BF16)

        xs_ref = xg_scr.at[slot]
        pltpu.make_async_copy(h2_hbm.at[pl.ds(0, BLK_E * SUBLANES)], xs_ref, sem.at[slot]).wait()
        for c in range(D_MODEL // LANES):
            x_scr[:, c * LANES:(c + 1) * LANES] = xs_ref[pl.ds(c, BLK_E, stride=SUBLANES), :].astype(BF16)
        xb = x_scr[...]
        g = _dot(xb, wg_scr[...])
        u = _dot(xb, wu_scr[...])
        hid = (g * _sigmoid(g) * u).astype(BF16)
        y = _dot(hid, wd_scr[...])
        for c in range(D_MODEL // LANES):
            yb_ref[pl.ds(c, BLK_E, stride=SUBLANES), :] = y[:, c * LANES:(c + 1) * LANES]


def _experts_call(block_e, n_used, tok, h2s, w_gate, w_up, w_down):
    n_blocks = block_e.shape[0]
    slab = pl.BlockSpec((BLK_E * SUBLANES, LANES), lambda i, be, nu, tk: (i, 0))
    weight = lambda shape: pl.BlockSpec((1,) + shape, lambda i, be, nu, tk: (be[i], 0, 0))
    return pl.pallas_call(
        _experts_kernel,
        grid_spec=pltpu.PrefetchScalarGridSpec(
            num_scalar_prefetch=3, grid=(n_blocks,),
            in_specs=[pl.BlockSpec(memory_space=pl.ANY),
                      weight((D_MODEL, D_EXPERT)), weight((D_MODEL, D_EXPERT)), weight((D_EXPERT, D_MODEL))],
            out_specs=slab,
            scratch_shapes=[pltpu.VMEM((2, BLK_E * SUBLANES, LANES), F32),
                            pltpu.SemaphoreType.DMA((2,)),
                            pltpu.VMEM((BLK_E, D_MODEL), BF16),
                            pltpu.VMEM((D_MODEL, D_EXPERT), BF16),
                            pltpu.VMEM((D_MODEL, D_EXPERT), BF16),
                            pltpu.VMEM((D_EXPERT, D_MODEL), BF16)]),
        out_shape=jax.ShapeDtypeStruct((n_blocks * BLK_E * SUBLANES, LANES), F32),
        compiler_params=_cparams(("arbitrary",)),
        name="experts",
    )(block_e, n_used, tok, h2s, w_gate, w_up, w_down)


def _final_kernel(dest_ref, x1_ref, rt_ref, mod_ref, gfin_ref, yb_hbm, o_ref, g_scr, sem):
    tm = TM_FINAL
    step = pl.program_id(0)
    slot = step % 2
    slot_rows = 2 * tm * SUBLANES

    def gather(s, into):
        def issue(r, _):
            tok = s * tm + r
            for k in range(2):
                _slab_copy(yb_hbm, dest_ref[2 * tok + k], g_scr.at[into], k * tm + r,
                           sem.at[into]).start(priority=k)
            return 0
        lax.fori_loop(0, tm, issue, 0, unroll=GATHER_UNROLL)

    @pl.when(step == 0)
    def _():
        gather(0, 0)

    @pl.when(step + 1 < pl.num_programs(0))
    def _():
        gather(step + 1, 1 - slot)

    g_now = g_scr.at[slot]
    pltpu.make_async_copy(yb_hbm.at[pl.ds(0, slot_rows)], g_now, sem.at[slot]).wait()

    rt = rt_ref[...]
    lane = lax.broadcasted_iota(I32, (tm, LANES), 1)
    gw0 = jnp.sum(jnp.where(lane == 2, rt, 0.0), axis=-1, keepdims=True)
    gw1 = jnp.sum(jnp.where(lane == 3, rt, 0.0), axis=-1, keepdims=True)
    gt2 = mod_ref[0, 5:6, :]
    x1 = x1_ref[...]
    cols = []
    for c in range(D_MODEL // LANES):
        y0 = g_now[pl.ds(c, tm, stride=SUBLANES), :]
        y1 = g_now[pl.ds(tm * SUBLANES + c, tm, stride=SUBLANES), :]
        y = gw0 * y0 + gw1 * y1
        cols.append(x1[:, c * LANES:(c + 1) * LANES] + gt2[:, c * LANES:(c + 1) * LANES] * y)
    x2 = jnp.concatenate(cols, axis=1)
    var = jnp.mean(x2 * x2, axis=-1, keepdims=True)
    o_ref[...] = x2 * lax.rsqrt(var + RMS_EPS) * gfin_ref[...]


def _final_call(dest, x1, rt, mod3, g_final, yb, seq):
    n = x1.shape[0]
    tpb = seq // TM_FINAL
    return pl.pallas_call(
        _final_kernel,
        grid_spec=pltpu.PrefetchScalarGridSpec(
            num_scalar_prefetch=1, grid=(n // TM_FINAL,),
            in_specs=[pl.BlockSpec((TM_FINAL, D_MODEL), lambda i, d: (i, 0)),
                      pl.BlockSpec((TM_FINAL, LANES), lambda i, d: (i, 0)),
                      pl.BlockSpec((1, 6, D_MODEL), lambda i, d: (i // tpb, 0, 0)),
                      pl.BlockSpec((1, D_MODEL), lambda i, d: (0, 0)),
                      pl.BlockSpec(memory_space=pl.ANY)],
            out_specs=pl.BlockSpec((TM_FINAL, D_MODEL), lambda i, d: (i, 0)),
            scratch_shapes=[pltpu.VMEM((2, 2 * TM_FINAL * SUBLANES, LANES), F32),
                            pltpu.SemaphoreType.DMA((2,))]),
        out_shape=jax.ShapeDtypeStruct((n, D_MODEL), F32),
        compiler_params=_cparams(("arbitrary",)),
        name="final",
    )(dest, x1, rt, mod3, g_final, yb)


def _permute_w_in(w):
    o = np.cumsum([0, 512, 64, 64, 512, 64, 8, 512, 512, 512, 8, 1024, 1024])
    qa, ka, va, qi, ki, wi, qb, kb, vb, fb, ga, gb = [w[:, o[k]:o[k + 1]] for k in range(12)]
    aux = jnp.concatenate([wi, fb, jnp.zeros((w.shape[0], LANES - 2 * N_HEADS), w.dtype)], axis=1)
    return jnp.concatenate([qa, qi, qb, kb, vb, ga, gb, ka, ka, va, va, ki, ki, aux], axis=1).astype(BF16)


def _layer(x2, pos2, mod3, batch, seq, g_mix, w_in, b_forget, w_out_a, w_out_b, w_out, g_ffn,
           w_group, b_group, w_router, b_router, w_e_gate, w_e_up, w_e_down, g_final):
    n = x2.shape[0]
    inv_freq = ROPE_THETA ** (-jnp.arange(0, ROPE_DIM, 2, dtype=F32) / ROPE_DIM)
    jj = np.arange(LANES) % HEAD_DIM
    invf = jnp.where(jj < ROPE_DIM, inv_freq[jj % (ROPE_DIM // 2)], 0.0)[None, :].astype(F32)
    fbias = jnp.zeros((1, LANES), F32).at[0, N_HEADS:2 * N_HEADS].set(b_forget.astype(F32))

    (qa, qi, qb, kaug, vb, sga, sgb, ka2, va2, ki2, aux) = _inproj_call(
        x2, pos2, mod3, g_mix.reshape(1, D_MODEL), _permute_w_in(w_in), invf, fbias, batch, seq)

    r3 = lambda a: a.reshape(batch, seq, a.shape[-1])
    tr = lambda a: jnp.swapaxes(r3(a), 1, 2)
    tr_tiles = lambda a, tk: jnp.swapaxes(a.reshape(batch, seq // tk, tk, a.shape[-1]), 2, 3)
    aux_t = tr(aux[:, :4 * N_HEADS])
    obt = _fox_call(tr(qb), aux_t, r3(kaug), tr_tiles(vb, TK_FOX))
    oat = _dsa_call(tr(qa), tr(qi), aux_t, r3(ki2), r3(ka2), tr_tiles(va2[:, :HEAD_DIM], TK_DSA))
    oa = jnp.swapaxes(oat, 1, 2).reshape(n, 512)
    ob = jnp.swapaxes(obt, 1, 2).reshape(n, 512)

    w_rt = jnp.concatenate([w_router, w_group, jnp.zeros((D_MODEL, LANES - N_EXPERTS - N_GROUPS), F32)], axis=1)
    wr_hi = w_rt.astype(BF16)
    wr_lo = (w_rt - wr_hi.astype(F32)).astype(BF16)
    br = jnp.concatenate([b_router, b_group, jnp.zeros((LANES - N_EXPERTS - N_GROUPS,), F32)])[None, :]
    x1, h2s, lg = _outproj_call(oa, ob, sga, sgb, x2, mod3,
                                g_ffn.reshape(1, D_MODEL), w_out_a.astype(BF16), w_out_b.astype(BF16),
                                w_out.astype(BF16), wr_hi, wr_lo, br, seq)
    rt = _route_call(lg)

    n_slots = 2 * n + N_EXPERTS * BLK_E
    n_blocks = n_slots // BLK_E
    rk, cnt = _rank_call(rt)
    dd, be, nu, seg = _dest_call(rt, rk, cnt, n_blocks)
    dest = dd[:, 0:2].reshape(-1)
    block_e = be[:n_blocks, 0]
    n_used = nu[0, 0:1]
    tok = _invert_call(dest, seg[0, :N_EXPERTS], seg[1, :N_EXPERTS], n_slots)
    yb = _experts_call(block_e, n_used, tok, h2s, w_e_gate, w_e_up, w_e_down)
    return _final_call(dest, x1, rt, mod3, g_final.reshape(1, D_MODEL), yb, seq)


def kernel(x, c, positions, w_mod, b_mod, g_mix, w_in, b_forget, w_out_a, w_out_b, w_out, g_ffn, w_group,
           b_group, w_router, b_router, w_e_gate, w_e_up, w_e_down, g_final):
    batch, seq, d = x.shape
    depth = w_mod.shape[0]
    assert depth == 1 and d == D_MODEL, "kernel fuses the final norm into the single layer"
    n = batch * seq
    c8 = jnp.zeros((8, d), F32).at[:batch].set(c)
    mod = _mod_call(c8, w_mod[0], b_mod[0][None, :])
    mod3 = mod[:batch].reshape(batch, 6, d)
    out = _layer(x.reshape(n, d), positions.reshape(n, 1), mod3, batch, seq, g_mix[0], w_in[0], b_forget[0],
                 w_out_a[0], w_out_b[0], w_out[0], g_ffn[0], w_group[0], b_group[0], w_router[0], b_router[0],
                 w_e_gate[0], w_e_up[0], w_e_down[0], g_final)
    return out.reshape(batch, seq, d)
```

```python
import functools

import numpy as np
import jax
import jax.numpy as jnp
from jax import lax
from jax.experimental import pallas as pl
from jax.experimental.pallas import tpu as pltpu

F32 = jnp.float32
BF16 = jnp.bfloat16
I32 = jnp.int32

D_MODEL = 1024
HEAD_DIM = 64
N_HEADS = 8
CHUNK_SHIFT = 6
TOPK = 256
ROPE_DIM = 16
ROPE_THETA = 500000.0
N_GROUPS = 4
EXPERTS_PER_GROUP = 8
N_EXPERTS = 32
D_EXPERT = 512
RMS_EPS = 1e-6
NEG_INF = -1e30
ATTN_SCALE = HEAD_DIM ** -0.5
IDX_SCALE = HEAD_DIM ** -0.5
LOG2E = 1.4426950408889634

LANES = 128
SUBLANES = 8
VMEM_LIMIT = 56 * 1024 * 1024

TM_IN = 512
TQ_FOX = 512
TK_FOX = 512
TQ_DSA = 256
TK_DSA = 512
TM_OUT = 512
TM_ROUTE = 1024
TM_DISPATCH = 1024
BLK_E = 256
TM_FINAL = 512
GATHER_UNROLL = 8

RADIX_BLOCK = 32 * SUBLANES
BOUND_MARGIN = 1.02
BOUND_SAFE = 40.0
FOX_BIAS_SLACK = 1.0
FOX_K = 256
FOX_AUG_ROWS = 16

_SEG_QA, _SEG_QI, _SEG_QB, _SEG_KB, _SEG_VB = 0, 512, 1024, 1536, 2048
_SEG_GA, _SEG_GB = 2560, 3584
_SEG_KA, _SEG_VA, _SEG_KI, _SEG_AUX = 4608, 4736, 4864, 4992
_NC_IN = 5120

_SENT_KEY = int(np.array(NEG_INF, np.float32).view(np.int32) ^ 0x7FFFFFFF)
_INT_MIN = -(2 ** 31)


def _cparams(sem):
    return pltpu.CompilerParams(dimension_semantics=sem, vmem_limit_bytes=VMEM_LIMIT)


def _dot(a, b):
    return jnp.dot(a, b, preferred_element_type=F32)


def _sigmoid(x):
    return 1.0 / (1.0 + jnp.exp(-x))


def _mod_kernel(c_ref, w_ref, b_ref, o_ref):
    c = c_ref[...]
    ca = c * _sigmoid(c)
    o_ref[...] = _dot(ca.astype(BF16), w_ref[...].astype(BF16)) + b_ref[...]


def _mod_call(c8, w_mod, b_mod):
    n_out = w_mod.shape[1]
    tn = 1024
    return pl.pallas_call(
        _mod_kernel,
        grid=(n_out // tn,),
        in_specs=[pl.BlockSpec((8, D_MODEL), lambda j: (0, 0)),
                  pl.BlockSpec((D_MODEL, tn), lambda j: (0, j)),
                  pl.BlockSpec((1, tn), lambda j: (0, j))],
        out_specs=pl.BlockSpec((8, tn), lambda j: (0, j)),
        out_shape=jax.ShapeDtypeStruct((8, n_out), F32),
        compiler_params=_cparams(("arbitrary",)),
        name="mod",
    )(c8, w_mod, b_mod)


def _inproj_kernel(x_ref, pos_ref, mod_ref, g_ref, w_ref, invf_ref, fbias_ref, sel_ref, ones_ref,
                   qa_ref, qi_ref, qb_ref, kb_ref, vb_ref, sga_ref, sgb_ref,
                   ka_ref, va_ref, ki_ref, aux_ref, h_scr, carry_scr):
    tm = TM_IN
    x = x_ref[...]
    var = jnp.mean(x * x, axis=-1, keepdims=True)
    tf = x * lax.rsqrt(var + RMS_EPS) * g_ref[...]
    sh = mod_ref[0, 0:1, :]
    sc = mod_ref[0, 1:2, :]
    h_scr[...] = (tf * (1.0 + sc) + sh).astype(BF16)
    hb = h_scr[...]

    lane = lax.broadcasted_iota(I32, (tm, LANES), 1)
    j = lane & (HEAD_DIM - 1)
    ang = pos_ref[...].astype(F32) * invf_ref[...]
    cs = jnp.cos(ang)
    sn = jnp.sin(ang)
    coef_next = jnp.where(j < ROPE_DIM // 2, -sn, 0.0)
    coef_prev = jnp.where((j >= ROPE_DIM // 2) & (j < ROPE_DIM), sn, 0.0)

    def rope(tc):
        return (tc * cs + pltpu.roll(tc, LANES - ROPE_DIM // 2, 1) * coef_next
                + pltpu.roll(tc, ROPE_DIM // 2, 1) * coef_prev)

    def seg(off, width):
        return _dot(hb, w_ref[:, off:off + width])

    t = seg(_SEG_QA, 512)
    for c in range(4):
        qa_ref[:, c * LANES:(c + 1) * LANES] = (
            rope(t[:, c * LANES:(c + 1) * LANES]) * (ATTN_SCALE * LOG2E)).astype(BF16)
    t = seg(_SEG_QI, 512)
    for c in range(4):
        qi_ref[:, c * LANES:(c + 1) * LANES] = (rope(t[:, c * LANES:(c + 1) * LANES]) * IDX_SCALE).astype(BF16)
    qb_ref[...] = (seg(_SEG_QB, 512) * (ATTN_SCALE * LOG2E)).astype(BF16)
    vb_ref[...] = seg(_SEG_VB, 512).astype(BF16)
    sga_ref[...] = _sigmoid(seg(_SEG_GA, 1024)).astype(BF16)
    sgb_ref[...] = _sigmoid(seg(_SEG_GB, 1024)).astype(BF16)
    ka = rope(seg(_SEG_KA, LANES))
    ka_ref[...] = ka.astype(BF16)
    va_ref[...] = seg(_SEG_VA, LANES).astype(BF16)
    ki_ref[...] = rope(seg(_SEG_KI, LANES)).astype(BF16)

    z = seg(_SEG_AUX, LANES)
    zf = z + fbias_ref[...]
    logf = jnp.minimum(zf, 0.0) - jnp.log(1.0 + jnp.exp(-jnp.abs(zf)))
    is_f = (lane >= N_HEADS) & (lane < 2 * N_HEADS)
    logf = jnp.where(is_f, logf, 0.0)

    def split3(v):
        hi = v.astype(BF16)
        r1 = v - hi.astype(F32)
        mid = r1.astype(BF16)
        return hi, mid, (r1 - mid.astype(F32)).astype(BF16)

    rr = lax.broadcasted_iota(I32, (tm, tm), 0)
    cc = lax.broadcasted_iota(I32, (tm, tm), 1)
    tri = jnp.where(cc <= rr, 1.0, 0.0).astype(BF16)
    p_hi, p_mid, p_lo = split3(logf)

    @pl.when(pl.program_id(1) == 0)
    def _():
        carry_scr[...] = jnp.zeros_like(carry_scr)

    parts = _dot(tri, jnp.concatenate([p_hi, p_mid, p_lo], axis=1))
    cum = parts[:, 0:LANES] + parts[:, LANES:2 * LANES] + parts[:, 2 * LANES:3 * LANES] + carry_scr[...]
    carry_scr[...] = cum[tm - 1:tm, :]
    cum2 = cum * LOG2E
    aux = jnp.where(lane < N_HEADS, z * (N_HEADS ** -0.5), jnp.where(is_f, cum2, 0.0))

    kb = seg(_SEG_KB, 512)
    first = lane < HEAD_DIM

    def half_norms(blk):
        sq = blk * blk
        return (jnp.sum(jnp.where(first, sq, 0.0), axis=-1, keepdims=True),
                jnp.sum(jnp.where(first, 0.0, sq), axis=-1, keepdims=True))

    for p in range(N_HEADS // 2):
        n_even, n_odd = half_norms(kb[:, p * LANES:(p + 1) * LANES])
        aux = jnp.where(lane == 2 * N_HEADS + 2 * p, n_even, jnp.where(lane == 2 * N_HEADS + 2 * p + 1, n_odd, aux))
    aux_ref[...] = jnp.where(lane == 3 * N_HEADS, half_norms(ka)[0], aux)

    c_hi, c_mid, c_lo = split3(cum2)
    kaug = _dot(jnp.concatenate([c_hi, c_mid, c_lo], axis=1), sel_ref[...]) + ones_ref[...]
    for p in range(N_HEADS // 2):
        kb_ref[:, p * FOX_K:p * FOX_K + LANES] = kb[:, p * LANES:(p + 1) * LANES].astype(BF16)
        kb_ref[:, p * FOX_K + LANES:(p + 1) * FOX_K] = kaug[:, p * LANES:(p + 1) * LANES].astype(BF16)


def _fox_routing_constants():
    sel = np.zeros((3, LANES, (N_HEADS // 2) * LANES), np.float32)
    ones = np.zeros((1, (N_HEADS // 2) * LANES), np.float32)
    for h in range(N_HEADS):
        for t in range(3):
            sel[t, N_HEADS + h, (h // 2) * LANES + 3 + 3 * (h % 2) + t] = -1.0
    for p in range(N_HEADS // 2):
        ones[0, p * LANES:p * LANES + 3] = 1.0
    return jnp.asarray(sel.reshape(3 * LANES, -1), BF16), jnp.asarray(ones, F32)


def _inproj_call(x2, pos2, mod3, g_mix, w_in_p, invf, fbias, batch, seq):
    n = x2.shape[0]
    tpb = seq // TM_IN
    row = lambda b, s: (b * tpb + s, 0)
    bf = lambda w: jax.ShapeDtypeStruct((n, w), BF16)
    sel, ones = _fox_routing_constants()
    out_shapes = [bf(512), bf(512), bf(512), bf((N_HEADS // 2) * FOX_K), bf(512), bf(1024), bf(1024),
                  bf(LANES), bf(LANES), bf(LANES), jax.ShapeDtypeStruct((n, LANES), F32)]
    out_specs = [pl.BlockSpec((TM_IN, s.shape[1]), row) for s in out_shapes]
    return pl.pallas_call(
        _inproj_kernel,
        grid=(batch, tpb),
        in_specs=[pl.BlockSpec((TM_IN, D_MODEL), row),
                  pl.BlockSpec((TM_IN, 1), row),
                  pl.BlockSpec((1, 6, D_MODEL), lambda b, s: (b, 0, 0)),
                  pl.BlockSpec((1, D_MODEL), lambda b, s: (0, 0)),
                  pl.BlockSpec((D_MODEL, _NC_IN), lambda b, s: (0, 0)),
                  pl.BlockSpec((1, LANES), lambda b, s: (0, 0)),
                  pl.BlockSpec((1, LANES), lambda b, s: (0, 0)),
                  pl.BlockSpec(sel.shape, lambda b, s: (0, 0)),
                  pl.BlockSpec(ones.shape, lambda b, s: (0, 0))],
        out_specs=out_specs,
        out_shape=out_shapes,
        scratch_shapes=[pltpu.VMEM((TM_IN, D_MODEL), BF16), pltpu.VMEM((1, LANES), F32)],
        compiler_params=_cparams(("arbitrary", "arbitrary")),
        name="inproj",
    )(x2, pos2, mod3, g_mix, w_in_p, invf, fbias, sel, ones)


def _softmax_max(s, m_ref):
    m_ref[...] = jnp.maximum(m_ref[...], jnp.max(s, axis=0, keepdims=True))


def _softmax_accumulate(s, m_ref, l_ref, acc_ref, v_t):
    p = jnp.exp2(s - m_ref[...])
    l_ref[...] = l_ref[...] + jnp.sum(p, axis=0, keepdims=True)
    acc_ref[...] = acc_ref[...] + _dot(v_t, p.astype(BF16))


def _softmax_scratch(tq):
    return ([pltpu.VMEM((1, tq), F32) for _ in range(2 * N_HEADS)]
            + [pltpu.VMEM((HEAD_DIM, tq), F32) for _ in range(N_HEADS)])


def _softmax_split(refs):
    return refs[:N_HEADS], refs[N_HEADS:2 * N_HEADS], refs[2 * N_HEADS:3 * N_HEADS]


def _softmax_init(m_refs, l_refs, acc_refs):
    for h in range(N_HEADS):
        m_refs[h][...] = jnp.full(m_refs[h].shape, NEG_INF, F32)
        l_refs[h][...] = jnp.zeros(l_refs[h].shape, F32)
        acc_refs[h][...] = jnp.zeros(acc_refs[h].shape, F32)


def _softmax_finish(o_ref, l_refs, acc_refs):
    for h in range(N_HEADS):
        o_ref[0, h * HEAD_DIM:(h + 1) * HEAD_DIM, :] = (acc_refs[h][...] * (1.0 / l_refs[h][...])).astype(BF16)


def _logit_bounds(qt_ref, kn2, slack):
    k_max = jnp.sqrt(jnp.max(kn2, axis=-1, keepdims=True))
    bounds = []
    worst = jnp.float32(0.0)
    for h in range(N_HEADS):
        qh = qt_ref[0, h * HEAD_DIM:(h + 1) * HEAD_DIM, :].astype(F32)
        q_norm = jnp.sqrt(jnp.sum(qh * qh, axis=0, keepdims=True))
        b = q_norm * k_max[h:h + 1, :] * BOUND_MARGIN + slack
        bounds.append(b)
        worst = jnp.maximum(worst, jnp.max(b))
    return bounds, worst <= BOUND_SAFE


def _fox_kernel(qt_ref, cumt_ref, kn_ref, kaug_ref, vt_ref, o_ref, rhs_scr, *softmax_refs):
    m_scr, l_scr, acc_scr = _softmax_split(softmax_refs)
    tq, tk = TQ_FOX, TK_FOX
    i = pl.program_id(1)
    q0 = i * tq

    cq = cumt_ref[0]
    c_hi = cq.astype(BF16).astype(F32)
    c_r = cq - c_hi
    c_mid = c_r.astype(BF16).astype(F32)
    c_lo = c_r - c_mid
    row_q = lax.broadcasted_iota(I32, (LANES, tq), 0)
    row_a = lax.broadcasted_iota(I32, (FOX_AUG_ROWS, 2 * tq), 0)
    second = lax.broadcasted_iota(I32, (FOX_AUG_ROWS, 2 * tq), 1) >= tq
    for p in range(N_HEADS // 2):
        qp = qt_ref[0, p * LANES:(p + 1) * LANES, :]
        zq = jnp.zeros_like(qp)
        rhs_scr[p, 0:LANES, 0:tq] = jnp.where(row_q < HEAD_DIM, qp, zq)
        rhs_scr[p, 0:LANES, tq:2 * tq] = jnp.where(row_q >= HEAD_DIM, qp, zq)
        pair_row = lambda a: jnp.concatenate([a[2 * p:2 * p + 1, :], a[2 * p + 1:2 * p + 2, :]], axis=1)
        aug = jnp.where(row_a == 0, pair_row(c_hi),
              jnp.where(row_a == 1, pair_row(c_mid),
              jnp.where(row_a == 2, pair_row(c_lo),
              jnp.where((row_a >= 3) & (row_a < 6), jnp.where(second, 0.0, 1.0),
              jnp.where((row_a >= 6) & (row_a < 9), jnp.where(second, 1.0, 0.0), 0.0)))))
        rhs_scr[p, LANES:LANES + FOX_AUG_ROWS, :] = aug.astype(BF16)
        rhs_scr[p, LANES + FOX_AUG_ROWS:FOX_K, :] = jnp.zeros((FOX_K - LANES - FOX_AUG_ROWS, 2 * tq), BF16)

    _softmax_init(m_scr, l_scr, acc_scr)
    key_j = lax.broadcasted_iota(I32, (tk, tq), 0)
    qry_i = q0 + lax.broadcasted_iota(I32, (tk, tq), 1)

    def tile(kt, masked, second_pass):
        k0 = pl.multiple_of(kt * tk, tk)
        for p in range(N_HEADS // 2):
            st = _dot(kaug_ref[0, pl.ds(k0, tk), p * FOX_K:(p + 1) * FOX_K], rhs_scr[p])
            for hh in range(2):
                h = 2 * p + hh
                s = st[:, hh * tq:(hh + 1) * tq]
                if masked:
                    s = jnp.where(k0 + key_j <= qry_i, s, NEG_INF)
                if second_pass:
                    _softmax_accumulate(s, m_scr[h], l_scr[h], acc_scr[h],
                                        vt_ref[0, kt, h * HEAD_DIM:(h + 1) * HEAD_DIM, :])
                else:
                    _softmax_max(s, m_scr[h])

    n_full = i * (tq // tk)

    def sweep(second_pass):
        def full_tile(kt, carry):
            tile(kt, False, second_pass)
            return carry

        lax.fori_loop(0, n_full, full_tile, 0)
        for d in range(tq // tk):
            tile(n_full + d, True, second_pass)

    bounds, safe = _logit_bounds(qt_ref, kn_ref[0], FOX_BIAS_SLACK)

    @pl.when(safe)
    def _():
        for h in range(N_HEADS):
            m_scr[h][...] = bounds[h]
        sweep(True)

    @pl.when(jnp.logical_not(safe))
    def _():
        sweep(False)
        sweep(True)

    _softmax_finish(o_ref, l_scr, acc_scr)


def _fox_call(qbt, aux_t, kaug3, vbt4):
    batch, _, seq = qbt.shape
    return pl.pallas_call(
        _fox_kernel,
        grid=(batch, seq // TQ_FOX),
        in_specs=[pl.BlockSpec((1, 512, TQ_FOX), lambda b, i: (b, 0, i)),
                  pl.BlockSpec((1, N_HEADS, TQ_FOX), lambda b, i: (b, 1, i)),
                  pl.BlockSpec((1, N_HEADS, seq), lambda b, i: (b, 2, 0)),
                  pl.BlockSpec((1, seq, (N_HEADS // 2) * FOX_K), lambda b, i: (b, 0, 0)),
                  pl.BlockSpec((1, seq // TK_FOX, 512, TK_FOX), lambda b, i: (b, 0, 0, 0))],
        out_specs=pl.BlockSpec((1, 512, TQ_FOX), lambda b, i: (b, 0, i)),
        out_shape=jax.ShapeDtypeStruct((batch, 512, seq), BF16),
        scratch_shapes=[pltpu.VMEM((N_HEADS // 2, FOX_K, 2 * TQ_FOX), BF16)] + _softmax_scratch(TQ_FOX),
        compiler_params=_cparams(("arbitrary", "arbitrary")),
        name="fox",
    )(qbt, aux_t, aux_t, kaug3, vbt4)


def _bit_transpose32(words):
    a = list(words)
    j, mask = 16, 0x0000FFFF
    while j:
        mask_i = int(np.array(mask, np.uint32).view(np.int32))
        k = 0
        while k < 32:
            t = (a[k] ^ lax.shift_right_logical(a[k + j], j)) & mask_i
            a[k] = a[k] ^ t
            a[k + j] = a[k + j] ^ lax.shift_left(t, j)
            k = (k + j + 1) & ~j
        j >>= 1
        mask = (mask ^ (mask << j)) & 0xFFFFFFFF
    return a


def _dsa_kernel(qat_ref, qit_ref, wit_ref, kn_ref, ki_ref, ka_ref, vat_ref, o_ref,
                aqi_scr, aqa_scr, sc_scr, bias_scr, plane_scr, alive_scr, *softmax_refs, seq):
    m_scr, l_scr, acc_scr = _softmax_split(softmax_refs)
    tq, tk = TQ_DSA, TK_DSA
    blk_per_tile = tk // RADIX_BLOCK
    n_blocks = seq // RADIX_BLOCK
    i = pl.program_id(1)
    q0 = i * tq
    nk = lax.shift_right_logical(q0 + tq + tk - 1, int(np.log2(tk)))
    n_rest = (seq - nk * tk).astype(F32)

    row_q = lax.broadcasted_iota(I32, (LANES, tq), 0)
    for h in range(N_HEADS):
        keep = (row_q < HEAD_DIM) if h % 2 == 0 else (row_q >= HEAD_DIM)
        rows = slice((h // 2) * LANES, (h // 2 + 1) * LANES)
        qi_blk = qit_ref[0, rows, :]
        qa_blk = qat_ref[0, rows, :]
        aqi_scr[:, h * tq:(h + 1) * tq] = jnp.where(keep, qi_blk, jnp.zeros_like(qi_blk))
        aqa_scr[:, h * tq:(h + 1) * tq] = jnp.where(keep, qa_blk, jnp.zeros_like(qa_blk))

    w = wit_ref[0]
    key_j = lax.broadcasted_iota(I32, (tk, tq), 0)
    qry_chunk = lax.shift_right_logical(q0 + lax.broadcasted_iota(I32, (tk, tq), 1), CHUNK_SHIFT)

    def admissible(k0):
        return lax.shift_right_logical(k0 + key_j, CHUNK_SHIFT) <= qry_chunk

    def p1(kt, carry, masked):
        k0 = pl.multiple_of(kt * tk, tk)
        rel = _dot(ki_ref[0, pl.ds(k0, tk), :], aqi_scr[...])
        sc = w[0:1, :] * jnp.maximum(rel[:, 0:tq], 0.0)
        for h in range(1, N_HEADS):
            sc = sc + w[h:h + 1, :] * jnp.maximum(rel[:, h * tq:(h + 1) * tq], 0.0)
        if masked:
            sc = jnp.where(admissible(k0), sc, NEG_INF)
        sc = jnp.where(sc == 0.0, 0.0, sc)
        sc_scr[kt] = sc
        b = pltpu.bitcast(sc, I32)
        key = b ^ (lax.shift_right_arithmetic(b, 31) & 0x7FFFFFFF)
        for half in range(blk_per_tile):
            base = half * RADIX_BLOCK
            planes = _bit_transpose32([key[base + SUBLANES * m:base + SUBLANES * (m + 1), :] ^ _INT_MIN
                                       for m in range(32)])
            bl = kt * blk_per_tile + half
            for p in range(32):
                plane_scr[bl, p] = planes[p]
            alive_scr[bl] = jnp.full((SUBLANES, tq), -1, I32)
        return carry

    n_unmasked = lax.shift_right_logical(q0, int(np.log2(tk)))
    lax.fori_loop(0, n_unmasked, functools.partial(p1, masked=False), 0)
    lax.fori_loop(n_unmasked, nk, functools.partial(p1, masked=True), 0)

    def clear_block(bl, carry):
        plane_scr[bl] = jnp.zeros((32, SUBLANES, tq), I32)
        alive_scr[bl] = jnp.zeros((SUBLANES, tq), I32)
        return carry

    lax.fori_loop(nk * blk_per_tile, n_blocks, clear_block, 0)

    sent_u = jnp.int32(_SENT_KEY ^ _INT_MIN)

    def bit_step(p, carry):
        k_rem, thr_bits, rest_alive = carry
        shift = 31 - p
        sent_bit = lax.shift_right_logical(sent_u, shift) & 1
        ones = [alive_scr[bl] & plane_scr[bl, p] for bl in range(n_blocks)]
        tot = lax.population_count(ones[0])
        for bl in range(1, n_blocks):
            tot = tot + lax.population_count(ones[bl])
        cnt = jnp.sum(tot.astype(F32), axis=0, keepdims=True)
        cnt = cnt + jnp.where((rest_alive != 0) & (sent_bit != 0), n_rest, 0.0)
        take1 = cnt >= k_rem
        for bl in range(n_blocks):
            alive_scr[bl] = jnp.where(take1, ones[bl], alive_scr[bl] & ~plane_scr[bl, p])
        k_rem = jnp.where(take1, k_rem, k_rem - cnt)
        thr_bits = thr_bits | jnp.where(take1, lax.shift_left(jnp.int32(1), shift), 0)
        rest_alive = jnp.where(take1 == (sent_bit != 0), rest_alive, 0)
        return k_rem, thr_bits, rest_alive

    kf = float(TOPK)
    _, thr_bits, _ = lax.fori_loop(
        0, 32, bit_step,
        (jnp.full((1, tq), kf, F32), jnp.zeros((1, tq), I32), jnp.ones((1, tq), I32)))
    thr_key = thr_bits ^ _INT_MIN
    thr0 = pltpu.bitcast(thr_key ^ (lax.shift_right_arithmetic(thr_key, 31) & 0x7FFFFFFF), F32)

    def rank_counts(thr):
        def body(kt, c):
            sc = sc_scr[kt]
            return (c[0] + jnp.sum(jnp.where(sc >= thr, 1.0, 0.0), axis=0, keepdims=True),
                    c[1] + jnp.sum(jnp.where(sc > thr, 1.0, 0.0), axis=0, keepdims=True))
        z = jnp.zeros((1, tq), F32)
        n_ge, n_gt = lax.fori_loop(0, nk, body, (z, z))
        return (n_ge + jnp.where(thr <= NEG_INF, n_rest, 0.0), n_gt + jnp.where(thr < NEG_INF, n_rest, 0.0))

    def misplaced(state):
        _, n_ge, n_gt = state
        return jnp.max(jnp.where((n_ge < kf) | (n_gt >= kf), 1.0, 0.0)) > 0.0

    def step_threshold(state):
        thr, n_ge, n_gt = state

        def body(kt, c):
            sc = sc_scr[kt]
            below = jnp.max(jnp.where(sc < thr, sc, -jnp.inf), axis=0, keepdims=True)
            above = jnp.min(jnp.where(sc > thr, sc, jnp.inf), axis=0, keepdims=True)
            return jnp.maximum(c[0], below), jnp.minimum(c[1], above)

        below, above = lax.fori_loop(0, nk, body, (jnp.full((1, tq), -jnp.inf, F32), jnp.full((1, tq), jnp.inf, F32)))
        has_rest = n_rest > 0.0
        below = jnp.where(has_rest & (thr > NEG_INF), jnp.maximum(below, NEG_INF), below)
        above = jnp.where(has_rest & (thr < NEG_INF), jnp.minimum(above, NEG_INF), above)
        thr = jnp.where(n_ge < kf, below, jnp.where(n_gt >= kf, above, thr))
        return (thr,) + rank_counts(thr)

    thr, n_ge, n_gt = lax.while_loop(misplaced, step_threshold, (thr0,) + rank_counts(thr0))
    need = kf - n_gt
    tie_free = jnp.min(jnp.where(n_ge == kf, 1.0, 0.0)) > 0.0

    tr = lax.broadcasted_iota(I32, (tk, tk), 0)
    tc = lax.broadcasted_iota(I32, (tk, tk), 1)
    tri = jnp.where(tc <= tr, 1.0, 0.0).astype(BF16)
    _softmax_init(m_scr, l_scr, acc_scr)

    def selection_bias(kt, tie):
        sc = sc_scr[kt]
        eq = sc == thr
        pref = _dot(tri, jnp.where(eq, 1.0, 0.0).astype(BF16)) + tie
        sel = (sc > thr) | (eq & (pref <= need))
        return jnp.where(sel & admissible(kt * tk), 0.0, NEG_INF), pref[tk - 1:tk, :]

    def logits(kt):
        return _dot(ka_ref[0, pl.ds(pl.multiple_of(kt * tk, tk), tk), :], aqa_scr[...])

    def accumulate(kt, bias):
        logit = logits(kt)
        v_t = vat_ref[0, kt]
        for h in range(N_HEADS):
            _softmax_accumulate(logit[:, h * tq:(h + 1) * tq] + bias, m_scr[h], l_scr[h], acc_scr[h], v_t)

    def p3_single(kt, tie):
        bias, tie = selection_bias(kt, tie)
        accumulate(kt, bias)
        return tie

    def p3_single_tie_free(kt, carry):
        accumulate(kt, jnp.where(sc_scr[kt] >= thr, 0.0, NEG_INF))
        return carry

    def p3_max(kt, tie):
        bias, tie = selection_bias(kt, tie)
        bias_scr[kt] = bias
        logit = logits(kt)
        for h in range(N_HEADS):
            _softmax_max(logit[:, h * tq:(h + 1) * tq] + bias, m_scr[h])
        return tie

    def p3_accumulate(kt, carry):
        accumulate(kt, bias_scr[kt])
        return carry

    kn2 = jnp.broadcast_to(kn_ref[0, 0:1, :], (N_HEADS, seq))
    bounds, safe = _logit_bounds(qat_ref, kn2, 0.0)
    no_tie = jnp.zeros((1, tq), F32)

    @pl.when(safe)
    def _():
        for h in range(N_HEADS):
            m_scr[h][...] = bounds[h]

    @pl.when(safe & tie_free)
    def _():
        lax.fori_loop(0, nk, p3_single_tie_free, 0)

    @pl.when(safe & jnp.logical_not(tie_free))
    def _():
        lax.fori_loop(0, nk, p3_single, no_tie)

    @pl.when(jnp.logical_not(safe))
    def _():
        lax.fori_loop(0, nk, p3_max, no_tie)
        lax.fori_loop(0, nk, p3_accumulate, 0)

    _softmax_finish(o_ref, l_scr, acc_scr)


def _dsa_call(qat, qit, aux_t, ki3, ka3, vat4):
    batch, _, seq = qat.shape
    qspec = pl.BlockSpec((1, 512, TQ_DSA), lambda b, i: (b, 0, i))
    kspec = pl.BlockSpec((1, seq, LANES), lambda b, i: (b, 0, 0))
    return pl.pallas_call(
        functools.partial(_dsa_kernel, seq=seq),
        grid=(batch, seq // TQ_DSA),
        in_specs=[qspec, qspec,
                  pl.BlockSpec((1, N_HEADS, TQ_DSA), lambda b, i: (b, 0, i)),
                  pl.BlockSpec((1, N_HEADS, seq), lambda b, i: (b, 3, 0)),
                  kspec, kspec,
                  pl.BlockSpec((1, seq // TK_DSA, HEAD_DIM, TK_DSA), lambda b, i: (b, 0, 0, 0))],
        out_specs=qspec,
        out_shape=jax.ShapeDtypeStruct((batch, 512, seq), BF16),
        scratch_shapes=[pltpu.VMEM((LANES, N_HEADS * TQ_DSA), BF16),
                        pltpu.VMEM((LANES, N_HEADS * TQ_DSA), BF16),
                        pltpu.VMEM((seq // TK_DSA, TK_DSA, TQ_DSA), F32),
                        pltpu.VMEM((seq // TK_DSA, TK_DSA, TQ_DSA), F32),
                        pltpu.VMEM((seq // RADIX_BLOCK, 32, SUBLANES, TQ_DSA), I32),
                        pltpu.VMEM((seq // RADIX_BLOCK, SUBLANES, TQ_DSA), I32)] + _softmax_scratch(TQ_DSA),
        compiler_params=_cparams(("arbitrary", "arbitrary")),
        name="dsa",
    )(qat, qit, aux_t, aux_t, ki3, ka3, vat4)


def _outproj_kernel(oa_ref, ob_ref, sga_ref, sgb_ref, x_ref, mod_ref, gffn_ref,
                    woa_ref, wob_ref, wo_ref, wr_hi_ref, wr_lo_ref, br_ref,
                    x1_ref, h2_ref, lg_ref):
    tm = TM_OUT
    ya = _dot(oa_ref[...], woa_ref[...])
    yb = _dot(ob_ref[...], wob_ref[...])
    merged = (sga_ref[...].astype(F32) * ya + sgb_ref[...].astype(F32) * yb).astype(BF16)
    mix = _dot(merged, wo_ref[...])
    gt1 = mod_ref[0, 2:3, :]
    sh2 = mod_ref[0, 3:4, :]
    sc2 = mod_ref[0, 4:5, :]
    x1 = x_ref[...] + gt1 * mix
    x1_ref[...] = x1
    var = jnp.mean(x1 * x1, axis=-1, keepdims=True)
    h2 = x1 * lax.rsqrt(var + RMS_EPS) * gffn_ref[...] * (1.0 + sc2) + sh2
    for c in range(D_MODEL // LANES):
        h2_ref[pl.ds(c, tm, stride=SUBLANES), :] = h2[:, c * LANES:(c + 1) * LANES]
    hi = h2.astype(BF16)
    lo = (h2 - hi.astype(F32)).astype(BF16)
    lg_ref[...] = (_dot(hi, wr_hi_ref[...]) + _dot(hi, wr_lo_ref[...]) + _dot(lo, wr_hi_ref[...])
                   + br_ref[...])


def _outproj_call(oa2, ob2, sga, sgb, x2, mod3, g_ffn, woa, wob, wo, wr_hi, wr_lo, br, seq):
    n = x2.shape[0]
    tpb = seq // TM_OUT
    row = lambda w: pl.BlockSpec((TM_OUT, w), lambda i: (i, 0))
    full = lambda a: pl.BlockSpec(a.shape, lambda i: (0,) * a.ndim)
    return pl.pallas_call(
        _outproj_kernel,
        grid=(n // TM_OUT,),
        in_specs=[row(512), row(512), row(1024), row(1024), row(D_MODEL),
                  pl.BlockSpec((1, 6, D_MODEL), lambda i: (i // tpb, 0, 0)),
                  full(g_ffn), full(woa), full(wob), full(wo), full(wr_hi), full(wr_lo), full(br)],
        out_specs=[row(D_MODEL),
                   pl.BlockSpec((TM_OUT * SUBLANES, LANES), lambda i: (i, 0)),
                   row(LANES)],
        out_shape=[jax.ShapeDtypeStruct((n, D_MODEL), F32),
                   jax.ShapeDtypeStruct((n * SUBLANES, LANES), F32),
                   jax.ShapeDtypeStruct((n, LANES), F32)],
        compiler_params=_cparams(("arbitrary",)),
        name="outproj",
    )(oa2, ob2, sga, sgb, x2, mod3, g_ffn, woa, wob, wo, wr_hi, wr_lo, br)


def _route_kernel(lg_ref, rt_ref):
    tm = TM_ROUTE
    lg = lg_ref[...]
    lane = lax.broadcasted_iota(I32, (tm, LANES), 1)
    big = jnp.int32(LANES)
    ninf = -jnp.inf
    gmask = (lane >= N_EXPERTS) & (lane < N_EXPERTS + N_GROUPS)
    g = jnp.where(gmask, lg, ninf)
    gmax = jnp.max(g, axis=-1, keepdims=True)
    grp = jnp.min(jnp.where(g == gmax, lane - N_EXPERTS, big), axis=-1, keepdims=True)
    p_grp = 1.0 / jnp.sum(jnp.where(gmask, jnp.exp(lg - gmax), 0.0), axis=-1, keepdims=True)
    lo = grp * EXPERTS_PER_GROUP
    emask = (lane >= lo) & (lane < lo + EXPERTS_PER_GROUP)
    ev = jnp.where(emask, lg, ninf)
    v0 = jnp.max(ev, axis=-1, keepdims=True)
    i0 = jnp.min(jnp.where(emask & (ev == v0), lane, big), axis=-1, keepdims=True)
    rest = emask & (lane != i0)
    ev1 = jnp.where(rest, lg, ninf)
    v1 = jnp.max(ev1, axis=-1, keepdims=True)
    i1 = jnp.min(jnp.where(rest & (ev1 == v1), lane, big), axis=-1, keepdims=True)
    e1 = jnp.exp(v1 - v0)
    w0 = p_grp / (1.0 + e1)
    w1 = p_grp * e1 / (1.0 + e1)
    rt_ref[...] = jnp.where(lane == 0, i0.astype(F32),
                            jnp.where(lane == 1, i1.astype(F32),
                                      jnp.where(lane == 2, w0, jnp.where(lane == 3, w1, 0.0))))


def _route_call(lg):
    n = lg.shape[0]
    spec = pl.BlockSpec((TM_ROUTE, LANES), lambda i: (i, 0))
    return pl.pallas_call(
        _route_kernel, grid=(n // TM_ROUTE,), in_specs=[spec], out_specs=spec,
        out_shape=jax.ShapeDtypeStruct((n, LANES), F32),
        compiler_params=_cparams(("arbitrary",)), name="route",
    )(lg)


def _expert_onehots(rt, tm):
    lane = lax.broadcasted_iota(I32, (tm, LANES), 1)
    lane_f = lane.astype(F32)
    e0 = jnp.sum(jnp.where(lane == 0, rt, 0.0), axis=-1, keepdims=True)
    e1 = jnp.sum(jnp.where(lane == 1, rt, 0.0), axis=-1, keepdims=True)
    return lane, lane_f == e0, lane_f == e1


def _rank_kernel(rt_ref, rk_ref, cnt_ref, carry_scr):
    tm = TM_ROUTE

    @pl.when(pl.program_id(0) == 0)
    def _():
        carry_scr[...] = jnp.zeros_like(carry_scr)

    lane, is0, is1 = _expert_onehots(rt_ref[...], tm)
    hits = jnp.where(is0, 1.0, 0.0) + jnp.where(is1, 1.0, 0.0)
    rr = lax.broadcasted_iota(I32, (tm, tm), 0)
    cc = lax.broadcasted_iota(I32, (tm, tm), 1)
    before = jnp.where(cc < rr, 1.0, 0.0).astype(BF16)
    seen = _dot(before, hits.astype(BF16)) + carry_scr[...]
    r0 = jnp.sum(jnp.where(is0, seen, 0.0), axis=-1, keepdims=True)
    r1 = jnp.sum(jnp.where(is1, seen, 0.0), axis=-1, keepdims=True)
    rk_ref[...] = jnp.where(lane == 0, r0, jnp.where(lane == 1, r1, 0.0))
    carry_scr[...] = carry_scr[...] + jnp.sum(hits, axis=0, keepdims=True)
    cnt_ref[...] = jnp.broadcast_to(carry_scr[...], cnt_ref.shape)


def _rank_call(rt):
    n = rt.shape[0]
    spec = pl.BlockSpec((TM_ROUTE, LANES), lambda i: (i, 0))
    return pl.pallas_call(
        _rank_kernel, grid=(n // TM_ROUTE,), in_specs=[spec],
        out_specs=[spec, pl.BlockSpec((SUBLANES, LANES), lambda i: (0, 0))],
        out_shape=[jax.ShapeDtypeStruct((n, LANES), F32), jax.ShapeDtypeStruct((SUBLANES, LANES), F32)],
        scratch_shapes=[pltpu.VMEM((1, LANES), F32)],
        compiler_params=_cparams(("arbitrary",)), name="rank",
    )(rt)


def _dest_kernel(rt_ref, rk_ref, cnt_ref, dd_ref, be_ref, nu_ref, end_ref):
    tm = TM_ROUTE
    lane, is0, is1 = _expert_onehots(rt_ref[...], tm)
    blocks = jnp.floor((cnt_ref[...] + (BLK_E - 1)) * (1.0 / BLK_E))
    er = lax.broadcasted_iota(I32, (LANES, LANES), 0)
    ec = lax.broadcasted_iota(I32, (LANES, LANES), 1)
    upto = jnp.where(er <= ec, 1.0, 0.0).astype(BF16)
    bend = _dot(blocks.astype(BF16), upto)
    pstart = (bend[0:1, :] - blocks[0:1, :]) * BLK_E
    rk = rk_ref[...]
    r0 = jnp.sum(jnp.where(lane == 0, rk, 0.0), axis=-1, keepdims=True)
    r1 = jnp.sum(jnp.where(lane == 1, rk, 0.0), axis=-1, keepdims=True)
    d0 = jnp.sum(jnp.where(is0, pstart, 0.0), axis=-1, keepdims=True) + r0
    d1 = jnp.sum(jnp.where(is1, pstart, 0.0), axis=-1, keepdims=True) + r1
    dd_ref[...] = jnp.where(lane == 0, d0, jnp.where(lane == 1, d1, 0.0)).astype(I32)
    nb = be_ref.shape[0]
    blk = lax.broadcasted_iota(I32, (nb, LANES), 0).astype(F32)
    lane_b = lax.broadcasted_iota(I32, (nb, LANES), 1)
    ended = jnp.where((lane_b < N_EXPERTS) & (bend[0:1, :] <= blk), 1.0, 0.0)
    be = jnp.minimum(jnp.sum(ended, axis=-1, keepdims=True), float(N_EXPERTS - 1))
    be_ref[...] = jnp.broadcast_to(be, be_ref.shape).astype(I32)
    lane_c = lax.broadcasted_iota(I32, (SUBLANES, LANES), 1)
    used = jnp.sum(jnp.where(lane_c == N_EXPERTS - 1, bend, 0.0), axis=-1, keepdims=True)
    nu_ref[...] = jnp.broadcast_to(used, nu_ref.shape).astype(I32)
    end_ref[...] = (bend * BLK_E).astype(I32)


def _dest_call(rt, rk, cnt, n_blocks):
    n = rt.shape[0]
    nb_pad = -(-n_blocks // SUBLANES) * SUBLANES
    spec = pl.BlockSpec((TM_ROUTE, LANES), lambda i: (i, 0))
    const = lambda rows: pl.BlockSpec((rows, LANES), lambda i: (0, 0))
    return pl.pallas_call(
        _dest_kernel, grid=(n // TM_ROUTE,), in_specs=[spec, spec, const(SUBLANES)],
        out_specs=[spec, const(nb_pad), const(SUBLANES), const(SUBLANES)],
        out_shape=[jax.ShapeDtypeStruct((n, LANES), I32), jax.ShapeDtypeStruct((nb_pad, LANES), I32),
                   jax.ShapeDtypeStruct((SUBLANES, LANES), I32), jax.ShapeDtypeStruct((SUBLANES, LANES), I32)],
        compiler_params=_cparams(("arbitrary",)), name="dest",
    )(rt, rk, cnt)


def _slab_copy(src, src_row, dst, dst_row, sem):
    return pltpu.make_async_copy(src.at[pl.ds(pl.multiple_of(src_row * SUBLANES, SUBLANES), SUBLANES)],
                                 dst.at[pl.ds(pl.multiple_of(dst_row * SUBLANES, SUBLANES), SUBLANES)], sem)


def _dispatch_kernel(dest_ref, seg_end_ref, h2_ref, xs_hbm, zero_scr, sem, zsem):
    tm = TM_DISPATCH
    base = pl.program_id(0) * tm

    @pl.when(pl.program_id(0) == 0)
    def _():
        zero_scr[...] = jnp.zeros_like(zero_scr)
        clears = [pltpu.make_async_copy(
            zero_scr, xs_hbm.at[pl.ds(pl.multiple_of(jnp.maximum(seg_end_ref[e] - BLK_E, 0) * SUBLANES, SUBLANES),
                                      BLK_E * SUBLANES)], zsem) for e in range(N_EXPERTS)]
        for c in clears:
            c.start()
        for c in clears:
            c.wait()
        n_blocks = xs_hbm.shape[0] // (BLK_E * SUBLANES)
        first_free = lax.shift_right_logical(seg_end_ref[N_EXPERTS - 1], int(np.log2(BLK_E)))
        for b in range(N_EXPERTS):
            @pl.when(first_free + b < n_blocks)
            def _():
                row = pl.multiple_of((first_free + b) * (BLK_E * SUBLANES), BLK_E * SUBLANES)
                tail = pltpu.make_async_copy(zero_scr, xs_hbm.at[pl.ds(row, BLK_E * SUBLANES)], zsem)
                tail.start()
                tail.wait()

    def issue(r, _):
        for k in range(2):
            _slab_copy(h2_ref, r, xs_hbm, dest_ref[2 * (base + r) + k], sem).start(priority=k)
        return 0

    lax.fori_loop(0, tm, issue, 0, unroll=GATHER_UNROLL)
    for _ in range(2):
        pltpu.make_async_copy(h2_ref, xs_hbm.at[pl.ds(0, tm * SUBLANES)], sem).wait()


def _dispatch_call(dest, seg_end, h2s, n_slots):
    n = h2s.shape[0] // SUBLANES
    return pl.pallas_call(
        _dispatch_kernel,
        grid_spec=pltpu.PrefetchScalarGridSpec(
            num_scalar_prefetch=2, grid=(n // TM_DISPATCH,),
            in_specs=[pl.BlockSpec((TM_DISPATCH * SUBLANES, LANES), lambda i, d, e: (i, 0))],
            out_specs=pl.BlockSpec(memory_space=pl.ANY),
            scratch_shapes=[pltpu.VMEM((BLK_E * SUBLANES, LANES), F32),
                            pltpu.SemaphoreType.DMA(()), pltpu.SemaphoreType.DMA(())]),
        out_shape=jax.ShapeDtypeStruct((n_slots * SUBLANES, LANES), F32),
        compiler_params=_cparams(("arbitrary",)),
        name="dispatch",
    )(dest, seg_end, h2s)


def _experts_kernel(be_ref, nused_ref, xs_ref, wg_ref, wu_ref, wd_ref, yb_ref, x_scr, wg_scr, wu_scr, wd_scr):
    i = pl.program_id(0)

    @pl.when(i >= nused_ref[0])
    def _():
        yb_ref[...] = jnp.zeros_like(yb_ref)

    @pl.when(i < nused_ref[0])
    def _():
        prev = be_ref[jnp.maximum(i - 1, 0)]

        @pl.when((i == 0) | (be_ref[i] != prev))
        def _():
            wg_scr[...] = wg_ref[0].astype(BF16)
            wu_scr[...] = wu_ref[0].astype(BF16)
            wd_scr[...] = wd_ref[0].astype(BF16)

        for c in range(D_MODEL // LANES):
            x_scr[:, c * LANES:(c + 1) * LANES] = xs_ref[pl.ds(c, BLK_E, stride=SUBLANES), :].astype(BF16)
        xb = x_scr[...]
        g = _dot(xb, wg_scr[...])
        u = _dot(xb, wu_scr[...])
        hid = (g * _sigmoid(g) * u).astype(BF16)
        y = _dot(hid, wd_scr[...])
        for c in range(D_MODEL // LANES):
            yb_ref[pl.ds(c, BLK_E, stride=SUBLANES), :] = y[:, c * LANES:(c + 1) * LANES]


def _experts_call(block_e, n_used, xs, w_gate, w_up, w_down):
    n_blocks = block_e.shape[0]
    slab = pl.BlockSpec((BLK_E * SUBLANES, LANES), lambda i, be, nu: (i, 0))
    used_slab = pl.BlockSpec((BLK_E * SUBLANES, LANES), lambda i, be, nu: (jnp.minimum(i, nu[0] - 1), 0))
    return pl.pallas_call(
        _experts_kernel,
        grid_spec=pltpu.PrefetchScalarGridSpec(
            num_scalar_prefetch=2, grid=(n_blocks,),
            in_specs=[used_slab,
                      pl.BlockSpec((1, D_MODEL, D_EXPERT), lambda i, be, nu: (be[i], 0, 0)),
                      pl.BlockSpec((1, D_MODEL, D_EXPERT), lambda i, be, nu: (be[i], 0, 0)),
                      pl.BlockSpec((1, D_EXPERT, D_MODEL), lambda i, be, nu: (be[i], 0, 0))],
            out_specs=slab,
            scratch_shapes=[pltpu.VMEM((BLK_E, D_MODEL), BF16),
                            pltpu.VMEM((D_MODEL, D_EXPERT), BF16),
                            pltpu.VMEM((D_MODEL, D_EXPERT), BF16),
                            pltpu.VMEM((D_EXPERT, D_MODEL), BF16)]),
        out_shape=jax.ShapeDtypeStruct(xs.shape, F32),
        compiler_params=_cparams(("arbitrary",)),
        name="experts",
    )(block_e, n_used, xs, w_gate, w_up, w_down)


def _final_kernel(dest_ref, x1_ref, rt_ref, mod_ref, gfin_ref, yb_hbm, o_ref, g_scr, sem):
    tm = TM_FINAL
    step = pl.program_id(0)
    slot = step % 2
    slot_rows = 2 * tm * SUBLANES

    def gather(s, into):
        def issue(r, _):
            tok = s * tm + r
            for k in range(2):
                _slab_copy(yb_hbm, dest_ref[2 * tok + k], g_scr.at[into], k * tm + r,
                           sem.at[into]).start(priority=k)
            return 0
        lax.fori_loop(0, tm, issue, 0, unroll=GATHER_UNROLL)

    @pl.when(step == 0)
    def _():
        gather(0, 0)

    @pl.when(step + 1 < pl.num_programs(0))
    def _():
        gather(step + 1, 1 - slot)

    g_now = g_scr.at[slot]
    pltpu.make_async_copy(yb_hbm.at[pl.ds(0, slot_rows)], g_now, sem.at[slot]).wait()

    rt = rt_ref[...]
    lane = lax.broadcasted_iota(I32, (tm, LANES), 1)
    gw0 = jnp.sum(jnp.where(lane == 2, rt, 0.0), axis=-1, keepdims=True)
    gw1 = jnp.sum(jnp.where(lane == 3, rt, 0.0), axis=-1, keepdims=True)
    gt2 = mod_ref[0, 5:6, :]
    x1 = x1_ref[...]
    cols = []
    for c in range(D_MODEL // LANES):
        y0 = g_now[pl.ds(c, tm, stride=SUBLANES), :]
        y1 = g_now[pl.ds(tm * SUBLANES + c, tm, stride=SUBLANES), :]
        y = gw0 * y0 + gw1 * y1
        cols.append(x1[:, c * LANES:(c + 1) * LANES] + gt2[:, c * LANES:(c + 1) * LANES] * y)
    x2 = jnp.concatenate(cols, axis=1)
    var = jnp.mean(x2 * x2, axis=-1, keepdims=True)
    o_ref[...] = x2 * lax.rsqrt(var + RMS_EPS) * gfin_ref[...]


def _final_call(dest, x1, rt, mod3, g_final, yb, seq):
    n = x1.shape[0]
    tpb = seq // TM_FINAL
    return pl.pallas_call(
        _final_kernel,
        grid_spec=pltpu.PrefetchScalarGridSpec(
            num_scalar_prefetch=1, grid=(n // TM_FINAL,),
            in_specs=[pl.BlockSpec((TM_FINAL, D_MODEL), lambda i, d: (i, 0)),
                      pl.BlockSpec((TM_FINAL, LANES), lambda i, d: (i, 0)),
                      pl.BlockSpec((1, 6, D_MODEL), lambda i, d: (i // tpb, 0, 0)),
                      pl.BlockSpec((1, D_MODEL), lambda i, d: (0, 0)),
                      pl.BlockSpec(memory_space=pl.ANY)],
            out_specs=pl.BlockSpec((TM_FINAL, D_MODEL), lambda i, d: (i, 0)),
            scratch_shapes=[pltpu.VMEM((2, 2 * TM_FINAL * SUBLANES, LANES), F32),
                            pltpu.SemaphoreType.DMA((2,))]),
        out_shape=jax.ShapeDtypeStruct((n, D_MODEL), F32),
        compiler_params=_cparams(("arbitrary",)),
        name="final",
    )(dest, x1, rt, mod3, g_final, yb)


def _permute_w_in(w):
    o = np.cumsum([0, 512, 64, 64, 512, 64, 8, 512, 512, 512, 8, 1024, 1024])
    qa, ka, va, qi, ki, wi, qb, kb, vb, fb, ga, gb = [w[:, o[k]:o[k + 1]] for k in range(12)]
    aux = jnp.concatenate([wi, fb, jnp.zeros((w.shape[0], LANES - 2 * N_HEADS), w.dtype)], axis=1)
    return jnp.concatenate([qa, qi, qb, kb, vb, ga, gb, ka, ka, va, va, ki, ki, aux], axis=1).astype(BF16)


def _layer(x2, pos2, mod3, batch, seq, g_mix, w_in, b_forget, w_out_a, w_out_b, w_out, g_ffn,
           w_group, b_group, w_router, b_router, w_e_gate, w_e_up, w_e_down, g_final):
    n = x2.shape[0]
    inv_freq = ROPE_THETA ** (-jnp.arange(0, ROPE_DIM, 2, dtype=F32) / ROPE_DIM)
    jj = np.arange(LANES) % HEAD_DIM
    invf = jnp.where(jj < ROPE_DIM, inv_freq[jj % (ROPE_DIM // 2)], 0.0)[None, :].astype(F32)
    fbias = jnp.zeros((1, LANES), F32).at[0, N_HEADS:2 * N_HEADS].set(b_forget.astype(F32))

    (qa, qi, qb, kaug, vb, sga, sgb, ka2, va2, ki2, aux) = _inproj_call(
        x2, pos2, mod3, g_mix.reshape(1, D_MODEL), _permute_w_in(w_in), invf, fbias, batch, seq)

    r3 = lambda a: a.reshape(batch, seq, a.shape[-1])
    tr = lambda a: jnp.swapaxes(r3(a), 1, 2)
    tr_tiles = lambda a, tk: jnp.swapaxes(a.reshape(batch, seq // tk, tk, a.shape[-1]), 2, 3)
    aux_t = tr(aux[:, :4 * N_HEADS])
    obt = _fox_call(tr(qb), aux_t, r3(kaug), tr_tiles(vb, TK_FOX))
    oat = _dsa_call(tr(qa), tr(qi), aux_t, r3(ki2), r3(ka2), tr_tiles(va2[:, :HEAD_DIM], TK_DSA))
    oa = jnp.swapaxes(oat, 1, 2).reshape(n, 512)
    ob = jnp.swapaxes(obt, 1, 2).reshape(n, 512)

    w_rt = jnp.concatenate([w_router, w_group, jnp.zeros((D_MODEL, LANES - N_EXPERTS - N_GROUPS), F32)], axis=1)
    wr_hi = w_rt.astype(BF16)
    wr_lo = (w_rt - wr_hi.astype(F32)).astype(BF16)
    br = jnp.concatenate([b_router, b_group, jnp.zeros((LANES - N_EXPERTS - N_GROUPS,), F32)])[None, :]
    x1, h2s, lg = _outproj_call(oa, ob, sga, sgb, x2, mod3,
                                g_ffn.reshape(1, D_MODEL), w_out_a.astype(BF16), w_out_b.astype(BF16),
                                w_out.astype(BF16), wr_hi, wr_lo, br, seq)
    rt = _route_call(lg)

    n_slots = 2 * n + N_EXPERTS * BLK_E
    n_blocks = n_slots // BLK_E
    rk, cnt = _rank_call(rt)
    dd, be, nu, seg_end = _dest_call(rt, rk, cnt, n_blocks)
    dest = dd[:, 0:2].reshape(-1)
    block_e = be[:n_blocks, 0]
    n_used = nu[0, 0:1]

    xs = _dispatch_call(dest, seg_end[0, :N_EXPERTS], h2s, n_slots)
    yb = _experts_call(block_e, n_used, xs, w_e_gate, w_e_up, w_e_down)
    return _final_call(dest, x1, rt, mod3, g_final.reshape(1, D_MODEL), yb, seq)


def kernel(x, c, positions, w_mod, b_mod, g_mix, w_in, b_forget, w_out_a, w_out_b, w_out, g_ffn, w_group,
           b_group, w_router, b_router, w_e_gate, w_e_up, w_e_down, g_final):
    batch, seq, d = x.shape
    depth = w_mod.shape[0]
    assert depth == 1 and d == D_MODEL, "kernel fuses the final norm into the single layer"
    n = batch * seq
    c8 = jnp.zeros((8, d), F32).at[:batch].set(c)
    mod = _mod_call(c8, w_mod[0], b_mod[0][None, :])
    mod3 = mod[:batch].reshape(batch, 6, d)
    out = _layer(x.reshape(n, d), positions.reshape(n, 1), mod3, batch, seq, g_mix[0], w_in[0], b_forget[0],
                 w_out_a[0], w_out_b[0], w_out[0], g_ffn[0], w_group[0], b_group[0], w_router[0], b_router[0],
                 w_e_gate[0], w_e_up[0], w_e_down[0], g_final)
    return out.reshape(batch, seq, d)
```

```python
import functools

import numpy as np
import jax
import jax.numpy as jnp
from jax import lax
from jax.experimental import pallas as pl
from jax.experimental.pallas import tpu as pltpu

F32 = jnp.float32
BF16 = jnp.bfloat16
I32 = jnp.int32

D_MODEL = 1024
HEAD_DIM = 64
N_HEADS = 8
CHUNK_SHIFT = 6
TOPK = 256
ROPE_DIM = 16
ROPE_THETA = 500000.0
N_GROUPS = 4
EXPERTS_PER_GROUP = 8
N_EXPERTS = 32
D_EXPERT = 512
RMS_EPS = 1e-6
NEG_INF = -1e30
ATTN_SCALE = HEAD_DIM ** -0.5
IDX_SCALE = HEAD_DIM ** -0.5
LOG2E = 1.4426950408889634

LANES = 128
SUBLANES = 8
VMEM_LIMIT = 56 * 1024 * 1024

TM_IN = 512
TQ_FOX = 512
TK_FOX = 512
TQ_DSA = 256
TK_DSA = 512
TM_OUT = 512
TM_ROUTE = 1024
TM_DISPATCH = 1024
BLK_E = 256
TM_FINAL = 512
GATHER_UNROLL = 8

RADIX_BLOCK = 32 * SUBLANES
BOUND_MARGIN = 1.02
BOUND_SAFE = 40.0
FOX_BIAS_SLACK = 1.0
FOX_K = 256
FOX_AUG_ROWS = 16

_SEG_QA, _SEG_QI, _SEG_QB, _SEG_KB, _SEG_VB = 0, 512, 1024, 1536, 2048
_SEG_GA, _SEG_GB = 2560, 3584
_SEG_KA, _SEG_VA, _SEG_KI, _SEG_AUX = 4608, 4736, 4864, 4992
_NC_IN = 5120

_SENT_KEY = int(np.array(NEG_INF, np.float32).view(np.int32) ^ 0x7FFFFFFF)
_INT_MIN = -(2 ** 31)


def _cparams(sem):
    return pltpu.CompilerParams(dimension_semantics=sem, vmem_limit_bytes=VMEM_LIMIT)


def _dot(a, b):
    return jnp.dot(a, b, preferred_element_type=F32)


def _sigmoid(x):
    return 1.0 / (1.0 + jnp.exp(-x))


def _mod_kernel(c_ref, w_ref, b_ref, o_ref):
    c = c_ref[...]
    ca = c * _sigmoid(c)
    o_ref[...] = _dot(ca.astype(BF16), w_ref[...].astype(BF16)) + b_ref[...]


def _mod_call(c8, w_mod, b_mod):
    n_out = w_mod.shape[1]
    tn = 1024
    return pl.pallas_call(
        _mod_kernel,
        grid=(n_out // tn,),
        in_specs=[pl.BlockSpec((8, D_MODEL), lambda j: (0, 0)),
                  pl.BlockSpec((D_MODEL, tn), lambda j: (0, j)),
                  pl.BlockSpec((1, tn), lambda j: (0, j))],
        out_specs=pl.BlockSpec((8, tn), lambda j: (0, j)),
        out_shape=jax.ShapeDtypeStruct((8, n_out), F32),
        compiler_params=_cparams(("arbitrary",)),
        name="mod",
    )(c8, w_mod, b_mod)


def _inproj_kernel(x_ref, pos_ref, mod_ref, g_ref, w_ref, invf_ref, fbias_ref, sel_ref, ones_ref,
                   qa_ref, qi_ref, qb_ref, kb_ref, vb_ref, sga_ref, sgb_ref,
                   ka_ref, va_ref, ki_ref, aux_ref, h_scr, carry_scr):
    tm = TM_IN
    x = x_ref[...]
    var = jnp.mean(x * x, axis=-1, keepdims=True)
    tf = x * lax.rsqrt(var + RMS_EPS) * g_ref[...]
    sh = mod_ref[0, 0:1, :]
    sc = mod_ref[0, 1:2, :]
    h_scr[...] = (tf * (1.0 + sc) + sh).astype(BF16)
    hb = h_scr[...]

    lane = lax.broadcasted_iota(I32, (tm, LANES), 1)
    j = lane & (HEAD_DIM - 1)
    ang = pos_ref[...].astype(F32) * invf_ref[...]
    cs = jnp.cos(ang)
    sn = jnp.sin(ang)
    coef_next = jnp.where(j < ROPE_DIM // 2, -sn, 0.0)
    coef_prev = jnp.where((j >= ROPE_DIM // 2) & (j < ROPE_DIM), sn, 0.0)

    def rope(tc):
        return (tc * cs + pltpu.roll(tc, LANES - ROPE_DIM // 2, 1) * coef_next
                + pltpu.roll(tc, ROPE_DIM // 2, 1) * coef_prev)

    def seg(off, width):
        return _dot(hb, w_ref[:, off:off + width])

    t = seg(_SEG_QA, 512)
    for c in range(4):
        qa_ref[:, c * LANES:(c + 1) * LANES] = (
            rope(t[:, c * LANES:(c + 1) * LANES]) * (ATTN_SCALE * LOG2E)).astype(BF16)
    t = seg(_SEG_QI, 512)
    for c in range(4):
        qi_ref[:, c * LANES:(c + 1) * LANES] = (rope(t[:, c * LANES:(c + 1) * LANES]) * IDX_SCALE).astype(BF16)
    qb_ref[...] = (seg(_SEG_QB, 512) * (ATTN_SCALE * LOG2E)).astype(BF16)
    vb_ref[...] = seg(_SEG_VB, 512).astype(BF16)
    sga_ref[...] = _sigmoid(seg(_SEG_GA, 1024)).astype(BF16)
    sgb_ref[...] = _sigmoid(seg(_SEG_GB, 1024)).astype(BF16)
    ka = rope(seg(_SEG_KA, LANES))
    ka_ref[...] = ka.astype(BF16)
    va_ref[...] = seg(_SEG_VA, LANES).astype(BF16)
    ki_ref[...] = rope(seg(_SEG_KI, LANES)).astype(BF16)

    z = seg(_SEG_AUX, LANES)
    zf = z + fbias_ref[...]
    logf = jnp.minimum(zf, 0.0) - jnp.log(1.0 + jnp.exp(-jnp.abs(zf)))
    is_f = (lane >= N_HEADS) & (lane < 2 * N_HEADS)
    logf = jnp.where(is_f, logf, 0.0)

    def split3(v):
        hi = v.astype(BF16)
        r1 = v - hi.astype(F32)
        mid = r1.astype(BF16)
        return hi, mid, (r1 - mid.astype(F32)).astype(BF16)

    rr = lax.broadcasted_iota(I32, (tm, tm), 0)
    cc = lax.broadcasted_iota(I32, (tm, tm), 1)
    tri = jnp.where(cc <= rr, 1.0, 0.0).astype(BF16)
    p_hi, p_mid, p_lo = split3(logf)

    @pl.when(pl.program_id(1) == 0)
    def _():
        carry_scr[...] = jnp.zeros_like(carry_scr)

    parts = _dot(tri, jnp.concatenate([p_hi, p_mid, p_lo], axis=1))
    cum = parts[:, 0:LANES] + parts[:, LANES:2 * LANES] + parts[:, 2 * LANES:3 * LANES] + carry_scr[...]
    carry_scr[...] = cum[tm - 1:tm, :]
    cum2 = cum * LOG2E
    aux = jnp.where(lane < N_HEADS, z * (N_HEADS ** -0.5), jnp.where(is_f, cum2, 0.0))

    kb = seg(_SEG_KB, 512)
    first = lane < HEAD_DIM

    def half_norms(blk):
        sq = blk * blk
        return (jnp.sum(jnp.where(first, sq, 0.0), axis=-1, keepdims=True),
                jnp.sum(jnp.where(first, 0.0, sq), axis=-1, keepdims=True))

    for p in range(N_HEADS // 2):
        n_even, n_odd = half_norms(kb[:, p * LANES:(p + 1) * LANES])
        aux = jnp.where(lane == 2 * N_HEADS + 2 * p, n_even, jnp.where(lane == 2 * N_HEADS + 2 * p + 1, n_odd, aux))
    aux_ref[...] = jnp.where(lane == 3 * N_HEADS, half_norms(ka)[0], aux)

    c_hi, c_mid, c_lo = split3(cum2)
    kaug = _dot(jnp.concatenate([c_hi, c_mid, c_lo], axis=1), sel_ref[...]) + ones_ref[...]
    for p in range(N_HEADS // 2):
        kb_ref[:, p * FOX_K:p * FOX_K + LANES] = kb[:, p * LANES:(p + 1) * LANES].astype(BF16)
        kb_ref[:, p * FOX_K + LANES:(p + 1) * FOX_K] = kaug[:, p * LANES:(p + 1) * LANES].astype(BF16)


def _fox_routing_constants():
    sel = np.zeros((3, LANES, (N_HEADS // 2) * LANES), np.float32)
    ones = np.zeros((1, (N_HEADS // 2) * LANES), np.float32)
    for h in range(N_HEADS):
        for t in range(3):
            sel[t, N_HEADS + h, (h // 2) * LANES + 3 + 3 * (h % 2) + t] = -1.0
    for p in range(N_HEADS // 2):
        ones[0, p * LANES:p * LANES + 3] = 1.0
    return jnp.asarray(sel.reshape(3 * LANES, -1), BF16), jnp.asarray(ones, F32)


def _inproj_call(x2, pos2, mod3, g_mix, w_in_p, invf, fbias, batch, seq):
    n = x2.shape[0]
    tpb = seq // TM_IN
    row = lambda b, s: (b * tpb + s, 0)
    bf = lambda w: jax.ShapeDtypeStruct((n, w), BF16)
    sel, ones = _fox_routing_constants()
    out_shapes = [bf(512), bf(512), bf(512), bf((N_HEADS // 2) * FOX_K), bf(512), bf(1024), bf(1024),
                  bf(LANES), bf(LANES), bf(LANES), jax.ShapeDtypeStruct((n, LANES), F32)]
    out_specs = [pl.BlockSpec((TM_IN, s.shape[1]), row) for s in out_shapes]
    return pl.pallas_call(
        _inproj_kernel,
        grid=(batch, tpb),
        in_specs=[pl.BlockSpec((TM_IN, D_MODEL), row),
                  pl.BlockSpec((TM_IN, 1), row),
                  pl.BlockSpec((1, 6, D_MODEL), lambda b, s: (b, 0, 0)),
                  pl.BlockSpec((1, D_MODEL), lambda b, s: (0, 0)),
                  pl.BlockSpec((D_MODEL, _NC_IN), lambda b, s: (0, 0)),
                  pl.BlockSpec((1, LANES), lambda b, s: (0, 0)),
                  pl.BlockSpec((1, LANES), lambda b, s: (0, 0)),
                  pl.BlockSpec(sel.shape, lambda b, s: (0, 0)),
                  pl.BlockSpec(ones.shape, lambda b, s: (0, 0))],
        out_specs=out_specs,
        out_shape=out_shapes,
        scratch_shapes=[pltpu.VMEM((TM_IN, D_MODEL), BF16), pltpu.VMEM((1, LANES), F32)],
        compiler_params=_cparams(("arbitrary", "arbitrary")),
        name="inproj",
    )(x2, pos2, mod3, g_mix, w_in_p, invf, fbias, sel, ones)


def _softmax_max(s, m_ref):
    m_ref[...] = jnp.maximum(m_ref[...], jnp.max(s, axis=0, keepdims=True))


def _softmax_accumulate(s, m_ref, l_ref, acc_ref, v_t):
    p = jnp.exp2(s - m_ref[...])
    l_ref[...] = l_ref[...] + jnp.sum(p, axis=0, keepdims=True)
    acc_ref[...] = acc_ref[...] + _dot(v_t, p.astype(BF16))


def _softmax_scratch(tq):
    return ([pltpu.VMEM((1, tq), F32) for _ in range(2 * N_HEADS)]
            + [pltpu.VMEM((HEAD_DIM, tq), F32) for _ in range(N_HEADS)])


def _softmax_split(refs):
    return refs[:N_HEADS], refs[N_HEADS:2 * N_HEADS], refs[2 * N_HEADS:3 * N_HEADS]


def _softmax_init(m_refs, l_refs, acc_refs):
    for h in range(N_HEADS):
        m_refs[h][...] = jnp.full(m_refs[h].shape, NEG_INF, F32)
        l_refs[h][...] = jnp.zeros(l_refs[h].shape, F32)
        acc_refs[h][...] = jnp.zeros(acc_refs[h].shape, F32)


def _softmax_finish(o_ref, l_refs, acc_refs):
    for h in range(N_HEADS):
        o_ref[0, h * HEAD_DIM:(h + 1) * HEAD_DIM, :] = (acc_refs[h][...] * (1.0 / l_refs[h][...])).astype(BF16)


def _logit_bounds(qt_ref, kn2, slack):
    k_max = jnp.sqrt(jnp.max(kn2, axis=-1, keepdims=True))
    bounds = []
    worst = jnp.float32(0.0)
    for h in range(N_HEADS):
        qh = qt_ref[0, h * HEAD_DIM:(h + 1) * HEAD_DIM, :].astype(F32)
        q_norm = jnp.sqrt(jnp.sum(qh * qh, axis=0, keepdims=True))
        b = q_norm * k_max[h:h + 1, :] * BOUND_MARGIN + slack
        bounds.append(b)
        worst = jnp.maximum(worst, jnp.max(b))
    return bounds, worst <= BOUND_SAFE


def _fox_kernel(qt_ref, cumt_ref, kn_ref, kaug_ref, vt_ref, o_ref, rhs_scr, *softmax_refs):
    m_scr, l_scr, acc_scr = _softmax_split(softmax_refs)
    tq, tk = TQ_FOX, TK_FOX
    i = pl.program_id(1)
    q0 = i * tq

    cq = cumt_ref[0]
    c_hi = cq.astype(BF16).astype(F32)
    c_r = cq - c_hi
    c_mid = c_r.astype(BF16).astype(F32)
    c_lo = c_r - c_mid
    row_q = lax.broadcasted_iota(I32, (LANES, tq), 0)
    row_a = lax.broadcasted_iota(I32, (FOX_AUG_ROWS, 2 * tq), 0)
    second = lax.broadcasted_iota(I32, (FOX_AUG_ROWS, 2 * tq), 1) >= tq
    for p in range(N_HEADS // 2):
        qp = qt_ref[0, p * LANES:(p + 1) * LANES, :]
        zq = jnp.zeros_like(qp)
        rhs_scr[p, 0:LANES, 0:tq] = jnp.where(row_q < HEAD_DIM, qp, zq)
        rhs_scr[p, 0:LANES, tq:2 * tq] = jnp.where(row_q >= HEAD_DIM, qp, zq)
        pair_row = lambda a: jnp.concatenate([a[2 * p:2 * p + 1, :], a[2 * p + 1:2 * p + 2, :]], axis=1)
        aug = jnp.where(row_a == 0, pair_row(c_hi),
              jnp.where(row_a == 1, pair_row(c_mid),
              jnp.where(row_a == 2, pair_row(c_lo),
              jnp.where((row_a >= 3) & (row_a < 6), jnp.where(second, 0.0, 1.0),
              jnp.where((row_a >= 6) & (row_a < 9), jnp.where(second, 1.0, 0.0), 0.0)))))
        rhs_scr[p, LANES:LANES + FOX_AUG_ROWS, :] = aug.astype(BF16)
        rhs_scr[p, LANES + FOX_AUG_ROWS:FOX_K, :] = jnp.zeros((FOX_K - LANES - FOX_AUG_ROWS, 2 * tq), BF16)

    _softmax_init(m_scr, l_scr, acc_scr)
    key_j = lax.broadcasted_iota(I32, (tk, tq), 0)
    qry_i = q0 + lax.broadcasted_iota(I32, (tk, tq), 1)

    def tile(kt, masked, second_pass):
        k0 = pl.multiple_of(kt * tk, tk)
        for p in range(N_HEADS // 2):
            st = _dot(kaug_ref[0, pl.ds(k0, tk), p * FOX_K:(p + 1) * FOX_K], rhs_scr[p])
            for hh in range(2):
                h = 2 * p + hh
                s = st[:, hh * tq:(hh + 1) * tq]
                if masked:
                    s = jnp.where(k0 + key_j <= qry_i, s, NEG_INF)
                if second_pass:
                    _softmax_accumulate(s, m_scr[h], l_scr[h], acc_scr[h],
                                        vt_ref[0, kt, h * HEAD_DIM:(h + 1) * HEAD_DIM, :])
                else:
                    _softmax_max(s, m_scr[h])

    n_full = i * (tq // tk)

    def sweep(second_pass):
        def full_tile(kt, carry):
            tile(kt, False, second_pass)
            return carry

        lax.fori_loop(0, n_full, full_tile, 0)
        for d in range(tq // tk):
            tile(n_full + d, True, second_pass)

    bounds, safe = _logit_bounds(qt_ref, kn_ref[0], FOX_BIAS_SLACK)

    @pl.when(safe)
    def _():
        for h in range(N_HEADS):
            m_scr[h][...] = bounds[h]
        sweep(True)

    @pl.when(jnp.logical_not(safe))
    def _():
        sweep(False)
        sweep(True)

    _softmax_finish(o_ref, l_scr, acc_scr)


def _fox_call(qbt, aux_t, kaug3, vbt4):
    batch, _, seq = qbt.shape
    return pl.pallas_call(
        _fox_kernel,
        grid=(batch, seq // TQ_FOX),
        in_specs=[pl.BlockSpec((1, 512, TQ_FOX), lambda b, i: (b, 0, i)),
                  pl.BlockSpec((1, N_HEADS, TQ_FOX), lambda b, i: (b, 1, i)),
                  pl.BlockSpec((1, N_HEADS, seq), lambda b, i: (b, 2, 0)),
                  pl.BlockSpec((1, seq, (N_HEADS // 2) * FOX_K), lambda b, i: (b, 0, 0)),
                  pl.BlockSpec((1, seq // TK_FOX, 512, TK_FOX), lambda b, i: (b, 0, 0, 0))],
        out_specs=pl.BlockSpec((1, 512, TQ_FOX), lambda b, i: (b, 0, i)),
        out_shape=jax.ShapeDtypeStruct((batch, 512, seq), BF16),
        scratch_shapes=[pltpu.VMEM((N_HEADS // 2, FOX_K, 2 * TQ_FOX), BF16)] + _softmax_scratch(TQ_FOX),
        compiler_params=_cparams(("arbitrary", "arbitrary")),
        name="fox",
    )(qbt, aux_t, aux_t, kaug3, vbt4)


def _bit_transpose32(words):
    a = list(words)
    j, mask = 16, 0x0000FFFF
    while j:
        mask_i = int(np.array(mask, np.uint32).view(np.int32))
        k = 0
        while k < 32:
            t = (a[k] ^ lax.shift_right_logical(a[k + j], j)) & mask_i
            a[k] = a[k] ^ t
            a[k + j] = a[k + j] ^ lax.shift_left(t, j)
            k = (k + j + 1) & ~j
        j >>= 1
        mask = (mask ^ (mask << j)) & 0xFFFFFFFF
    return a


def _dsa_kernel(qat_ref, qit_ref, wit_ref, kn_ref, ki_ref, ka_ref, vat_ref, o_ref,
                aqi_scr, aqa_scr, sc_scr, bias_scr, plane_scr, alive_scr, thr_scr, *softmax_refs, seq):
    m_scr, l_scr, acc_scr = _softmax_split(softmax_refs)
    tq, tk = TQ_DSA, TK_DSA
    blk_per_tile = tk // RADIX_BLOCK
    n_blocks = seq // RADIX_BLOCK
    i = pl.program_id(1)
    q0 = i * tq
    nk = lax.shift_right_logical(q0 + tq + tk - 1, int(np.log2(tk)))
    n_rest = (seq - nk * tk).astype(F32)

    row_q = lax.broadcasted_iota(I32, (LANES, tq), 0)
    for h in range(N_HEADS):
        keep = (row_q < HEAD_DIM) if h % 2 == 0 else (row_q >= HEAD_DIM)
        rows = slice((h // 2) * LANES, (h // 2 + 1) * LANES)
        qi_blk = qit_ref[0, rows, :]
        qa_blk = qat_ref[0, rows, :]
        aqi_scr[:, h * tq:(h + 1) * tq] = jnp.where(keep, qi_blk, jnp.zeros_like(qi_blk))
        aqa_scr[:, h * tq:(h + 1) * tq] = jnp.where(keep, qa_blk, jnp.zeros_like(qa_blk))

    w = wit_ref[0]
    key_j = lax.broadcasted_iota(I32, (tk, tq), 0)
    qry_chunk = lax.shift_right_logical(q0 + lax.broadcasted_iota(I32, (tk, tq), 1), CHUNK_SHIFT)

    def admissible(k0):
        return lax.shift_right_logical(k0 + key_j, CHUNK_SHIFT) <= qry_chunk

    def p1(kt, carry, masked):
        k0 = pl.multiple_of(kt * tk, tk)
        rel = _dot(ki_ref[0, pl.ds(k0, tk), :], aqi_scr[...])
        sc = w[0:1, :] * jnp.maximum(rel[:, 0:tq], 0.0)
        for h in range(1, N_HEADS):
            sc = sc + w[h:h + 1, :] * jnp.maximum(rel[:, h * tq:(h + 1) * tq], 0.0)
        if masked:
            sc = jnp.where(admissible(k0), sc, NEG_INF)
        sc = jnp.where(sc == 0.0, 0.0, sc)
        sc_scr[kt] = sc
        b = pltpu.bitcast(sc, I32)
        key = b ^ (lax.shift_right_arithmetic(b, 31) & 0x7FFFFFFF)
        for half in range(blk_per_tile):
            base = half * RADIX_BLOCK
            planes = _bit_transpose32([key[base + SUBLANES * m:base + SUBLANES * (m + 1), :] ^ _INT_MIN
                                       for m in range(32)])
            bl = kt * blk_per_tile + half
            for p in range(32):
                plane_scr[bl, p] = planes[p]
            alive_scr[bl] = jnp.full((SUBLANES, tq), -1, I32)
        return carry

    n_unmasked = lax.shift_right_logical(q0, int(np.log2(tk)))
    lax.fori_loop(0, n_unmasked, functools.partial(p1, masked=False), 0)
    lax.fori_loop(n_unmasked, nk, functools.partial(p1, masked=True), 0)

    def clear_block(bl, carry):
        plane_scr[bl] = jnp.zeros((32, SUBLANES, tq), I32)
        alive_scr[bl] = jnp.zeros((SUBLANES, tq), I32)
        return carry

    n_held = nk * blk_per_tile
    half_blocks = n_blocks // 2
    fits_half = n_held <= half_blocks
    lax.fori_loop(n_held, jnp.where(fits_half, half_blocks, n_blocks), clear_block, 0)

    sent_u = jnp.int32(_SENT_KEY ^ _INT_MIN)
    kf = float(TOPK)

    def radix_select(n_walk):
        def bit_step(p, carry):
            k_rem, thr_bits, rest_alive = carry
            shift = 31 - p
            sent_bit = lax.shift_right_logical(sent_u, shift) & 1
            ones = [alive_scr[bl] & plane_scr[bl, p] for bl in range(n_walk)]
            tot = lax.population_count(ones[0])
            for bl in range(1, n_walk):
                tot = tot + lax.population_count(ones[bl])
            cnt = jnp.sum(tot.astype(F32), axis=0, keepdims=True)
            cnt = cnt + jnp.where((rest_alive != 0) & (sent_bit != 0), n_rest, 0.0)
            take1 = cnt >= k_rem
            for bl in range(n_walk):
                alive_scr[bl] = jnp.where(take1, ones[bl], alive_scr[bl] & ~plane_scr[bl, p])
            k_rem = jnp.where(take1, k_rem, k_rem - cnt)
            thr_bits = thr_bits | jnp.where(take1, lax.shift_left(jnp.int32(1), shift), 0)
            rest_alive = jnp.where(take1 == (sent_bit != 0), rest_alive, 0)
            return k_rem, thr_bits, rest_alive

        _, thr_bits, _ = lax.fori_loop(
            0, 32, bit_step,
            (jnp.full((1, tq), kf, F32), jnp.zeros((1, tq), I32), jnp.ones((1, tq), I32)))
        thr_scr[...] = thr_bits

    pl.when(fits_half)(lambda: radix_select(half_blocks))
    pl.when(jnp.logical_not(fits_half))(lambda: radix_select(n_blocks))
    thr_key = thr_scr[...] ^ _INT_MIN
    thr0 = pltpu.bitcast(thr_key ^ (lax.shift_right_arithmetic(thr_key, 31) & 0x7FFFFFFF), F32)

    def rank_counts(thr):
        def body(kt, c):
            sc = sc_scr[kt]
            return (c[0] + jnp.sum(jnp.where(sc >= thr, 1.0, 0.0), axis=0, keepdims=True),
                    c[1] + jnp.sum(jnp.where(sc > thr, 1.0, 0.0), axis=0, keepdims=True))
        z = jnp.zeros((1, tq), F32)
        n_ge, n_gt = lax.fori_loop(0, nk, body, (z, z))
        return (n_ge + jnp.where(thr <= NEG_INF, n_rest, 0.0), n_gt + jnp.where(thr < NEG_INF, n_rest, 0.0))

    def misplaced(state):
        _, n_ge, n_gt = state
        return jnp.max(jnp.where((n_ge < kf) | (n_gt >= kf), 1.0, 0.0)) > 0.0

    def step_threshold(state):
        thr, n_ge, n_gt = state

        def body(kt, c):
            sc = sc_scr[kt]
            below = jnp.max(jnp.where(sc < thr, sc, -jnp.inf), axis=0, keepdims=True)
            above = jnp.min(jnp.where(sc > thr, sc, jnp.inf), axis=0, keepdims=True)
            return jnp.maximum(c[0], below), jnp.minimum(c[1], above)

        below, above = lax.fori_loop(0, nk, body, (jnp.full((1, tq), -jnp.inf, F32), jnp.full((1, tq), jnp.inf, F32)))
        has_rest = n_rest > 0.0
        below = jnp.where(has_rest & (thr > NEG_INF), jnp.maximum(below, NEG_INF), below)
        above = jnp.where(has_rest & (thr < NEG_INF), jnp.minimum(above, NEG_INF), above)
        thr = jnp.where(n_ge < kf, below, jnp.where(n_gt >= kf, above, thr))
        return (thr,) + rank_counts(thr)

    thr, n_ge, n_gt = lax.while_loop(misplaced, step_threshold, (thr0,) + rank_counts(thr0))
    need = kf - n_gt
    tie_free = jnp.min(jnp.where(n_ge == kf, 1.0, 0.0)) > 0.0

    tr = lax.broadcasted_iota(I32, (tk, tk), 0)
    tc = lax.broadcasted_iota(I32, (tk, tk), 1)
    tri = jnp.where(tc <= tr, 1.0, 0.0).astype(BF16)
    _softmax_init(m_scr, l_scr, acc_scr)

    def selection_bias(kt, tie):
        sc = sc_scr[kt]
        eq = sc == thr
        pref = _dot(tri, jnp.where(eq, 1.0, 0.0).astype(BF16)) + tie
        sel = (sc > thr) | (eq & (pref <= need))
        return jnp.where(sel & admissible(kt * tk), 0.0, NEG_INF), pref[tk - 1:tk, :]

    def logits(kt):
        return _dot(ka_ref[0, pl.ds(pl.multiple_of(kt * tk, tk), tk), :], aqa_scr[...])

    def accumulate(kt, bias):
        logit = logits(kt)
        v_t = vat_ref[0, kt]
        for h in range(N_HEADS):
            _softmax_accumulate(logit[:, h * tq:(h + 1) * tq] + bias, m_scr[h], l_scr[h], acc_scr[h], v_t)

    def p3_single(kt, tie):
        bias, tie = selection_bias(kt, tie)
        accumulate(kt, bias)
        return tie

    def p3_single_tie_free(kt, carry):
        accumulate(kt, jnp.where(sc_scr[kt] >= thr, 0.0, NEG_INF))
        return carry

    def p3_max(kt, tie):
        bias, tie = selection_bias(kt, tie)
        bias_scr[kt] = bias
        logit = logits(kt)
        for h in range(N_HEADS):
            _softmax_max(logit[:, h * tq:(h + 1) * tq] + bias, m_scr[h])
        return tie

    def p3_accumulate(kt, carry):
        accumulate(kt, bias_scr[kt])
        return carry

    kn2 = jnp.broadcast_to(kn_ref[0, 0:1, :], (N_HEADS, seq))
    bounds, safe = _logit_bounds(qat_ref, kn2, 0.0)
    no_tie = jnp.zeros((1, tq), F32)

    @pl.when(safe)
    def _():
        for h in range(N_HEADS):
            m_scr[h][...] = bounds[h]

    @pl.when(safe & tie_free)
    def _():
        lax.fori_loop(0, nk, p3_single_tie_free, 0)

    @pl.when(safe & jnp.logical_not(tie_free))
    def _():
        lax.fori_loop(0, nk, p3_single, no_tie)

    @pl.when(jnp.logical_not(safe))
    def _():
        lax.fori_loop(0, nk, p3_max, no_tie)
        lax.fori_loop(0, nk, p3_accumulate, 0)

    _softmax_finish(o_ref, l_scr, acc_scr)


def _dsa_call(qat, qit, aux_t, ki3, ka3, vat4):
    batch, _, seq = qat.shape
    qspec = pl.BlockSpec((1, 512, TQ_DSA), lambda b, i: (b, 0, i))
    kspec = pl.BlockSpec((1, seq, LANES), lambda b, i: (b, 0, 0))
    return pl.pallas_call(
        functools.partial(_dsa_kernel, seq=seq),
        grid=(batch, seq // TQ_DSA),
        in_specs=[qspec, qspec,
                  pl.BlockSpec((1, N_HEADS, TQ_DSA), lambda b, i: (b, 0, i)),
                  pl.BlockSpec((1, N_HEADS, seq), lambda b, i: (b, 3, 0)),
                  kspec, kspec,
                  pl.BlockSpec((1, seq // TK_DSA, HEAD_DIM, TK_DSA), lambda b, i: (b, 0, 0, 0))],
        out_specs=qspec,
        out_shape=jax.ShapeDtypeStruct((batch, 512, seq), BF16),
        scratch_shapes=[pltpu.VMEM((LANES, N_HEADS * TQ_DSA), BF16),
                        pltpu.VMEM((LANES, N_HEADS * TQ_DSA), BF16),
                        pltpu.VMEM((seq // TK_DSA, TK_DSA, TQ_DSA), F32),
                        pltpu.VMEM((seq // TK_DSA, TK_DSA, TQ_DSA), F32),
                        pltpu.VMEM((seq // RADIX_BLOCK, 32, SUBLANES, TQ_DSA), I32),
                        pltpu.VMEM((seq // RADIX_BLOCK, SUBLANES, TQ_DSA), I32),
                        pltpu.VMEM((1, TQ_DSA), I32)] + _softmax_scratch(TQ_DSA),
        compiler_params=_cparams(("arbitrary", "arbitrary")),
        name="dsa",
    )(qat, qit, aux_t, aux_t, ki3, ka3, vat4)


def _outproj_kernel(oa_ref, ob_ref, sga_ref, sgb_ref, x_ref, mod_ref, gffn_ref,
                    woa_ref, wob_ref, wo_ref, wr_hi_ref, wr_lo_ref, br_ref,
                    x1_ref, h2_ref, lg_ref):
    tm = TM_OUT
    ya = _dot(oa_ref[...], woa_ref[...])
    yb = _dot(ob_ref[...], wob_ref[...])
    merged = (sga_ref[...].astype(F32) * ya + sgb_ref[...].astype(F32) * yb).astype(BF16)
    mix = _dot(merged, wo_ref[...])
    gt1 = mod_ref[0, 2:3, :]
    sh2 = mod_ref[0, 3:4, :]
    sc2 = mod_ref[0, 4:5, :]
    x1 = x_ref[...] + gt1 * mix
    x1_ref[...] = x1
    var = jnp.mean(x1 * x1, axis=-1, keepdims=True)
    h2 = x1 * lax.rsqrt(var + RMS_EPS) * gffn_ref[...] * (1.0 + sc2) + sh2
    for c in range(D_MODEL // LANES):
        h2_ref[pl.ds(c, tm, stride=SUBLANES), :] = h2[:, c * LANES:(c + 1) * LANES]
    hi = h2.astype(BF16)
    lo = (h2 - hi.astype(F32)).astype(BF16)
    lg_ref[...] = (_dot(hi, wr_hi_ref[...]) + _dot(hi, wr_lo_ref[...]) + _dot(lo, wr_hi_ref[...])
                   + br_ref[...])


def _outproj_call(oa2, ob2, sga, sgb, x2, mod3, g_ffn, woa, wob, wo, wr_hi, wr_lo, br, seq):
    n = x2.shape[0]
    tpb = seq // TM_OUT
    row = lambda w: pl.BlockSpec((TM_OUT, w), lambda i: (i, 0))
    full = lambda a: pl.BlockSpec(a.shape, lambda i: (0,) * a.ndim)
    return pl.pallas_call(
        _outproj_kernel,
        grid=(n // TM_OUT,),
        in_specs=[row(512), row(512), row(1024), row(1024), row(D_MODEL),
                  pl.BlockSpec((1, 6, D_MODEL), lambda i: (i // tpb, 0, 0)),
                  full(g_ffn), full(woa), full(wob), full(wo), full(wr_hi), full(wr_lo), full(br)],
        out_specs=[row(D_MODEL),
                   pl.BlockSpec((TM_OUT * SUBLANES, LANES), lambda i: (i, 0)),
                   row(LANES)],
        out_shape=[jax.ShapeDtypeStruct((n, D_MODEL), F32),
                   jax.ShapeDtypeStruct((n * SUBLANES, LANES), F32),
                   jax.ShapeDtypeStruct((n, LANES), F32)],
        compiler_params=_cparams(("arbitrary",)),
        name="outproj",
    )(oa2, ob2, sga, sgb, x2, mod3, g_ffn, woa, wob, wo, wr_hi, wr_lo, br)


def _route_rows(lg):
    tm = lg.shape[0]
    lane = lax.broadcasted_iota(I32, (tm, LANES), 1)
    big = jnp.int32(LANES)
    ninf = -jnp.inf
    gmask = (lane >= N_EXPERTS) & (lane < N_EXPERTS + N_GROUPS)
    g = jnp.where(gmask, lg, ninf)
    gmax = jnp.max(g, axis=-1, keepdims=True)
    grp = jnp.min(jnp.where(g == gmax, lane - N_EXPERTS, big), axis=-1, keepdims=True)
    p_grp = 1.0 / jnp.sum(jnp.where(gmask, jnp.exp(lg - gmax), 0.0), axis=-1, keepdims=True)
    lo = grp * EXPERTS_PER_GROUP
    emask = (lane >= lo) & (lane < lo + EXPERTS_PER_GROUP)
    ev = jnp.where(emask, lg, ninf)
    v0 = jnp.max(ev, axis=-1, keepdims=True)
    i0 = jnp.min(jnp.where(emask & (ev == v0), lane, big), axis=-1, keepdims=True)
    rest = emask & (lane != i0)
    ev1 = jnp.where(rest, lg, ninf)
    v1 = jnp.max(ev1, axis=-1, keepdims=True)
    i1 = jnp.min(jnp.where(rest & (ev1 == v1), lane, big), axis=-1, keepdims=True)
    e1 = jnp.exp(v1 - v0)
    w0 = p_grp / (1.0 + e1)
    w1 = p_grp * e1 / (1.0 + e1)
    return jnp.where(lane == 0, i0.astype(F32),
                     jnp.where(lane == 1, i1.astype(F32),
                               jnp.where(lane == 2, w0, jnp.where(lane == 3, w1, 0.0))))


def _expert_onehots(rt, tm):
    lane = lax.broadcasted_iota(I32, (tm, LANES), 1)
    lane_f = lane.astype(F32)
    e0 = jnp.sum(jnp.where(lane == 0, rt, 0.0), axis=-1, keepdims=True)
    e1 = jnp.sum(jnp.where(lane == 1, rt, 0.0), axis=-1, keepdims=True)
    return lane, lane_f == e0, lane_f == e1


def _rank_rows(rt, seen_before):
    tm = rt.shape[0]
    lane, is0, is1 = _expert_onehots(rt, tm)
    hits = jnp.where(is0, 1.0, 0.0) + jnp.where(is1, 1.0, 0.0)
    rr = lax.broadcasted_iota(I32, (tm, tm), 0)
    cc = lax.broadcasted_iota(I32, (tm, tm), 1)
    before = jnp.where(cc < rr, 1.0, 0.0).astype(BF16)
    seen = _dot(before, hits.astype(BF16)) + seen_before
    r0 = jnp.sum(jnp.where(is0, seen, 0.0), axis=-1, keepdims=True)
    r1 = jnp.sum(jnp.where(is1, seen, 0.0), axis=-1, keepdims=True)
    rk = jnp.where(lane == 0, r0, jnp.where(lane == 1, r1, 0.0))
    return rk, seen_before + jnp.sum(hits, axis=0, keepdims=True)


def _route_rank_kernel(lg_ref, rt_ref, rk_ref, cnt_ref, seen_scr):
    @pl.when(pl.program_id(0) == 0)
    def _():
        seen_scr[...] = jnp.zeros_like(seen_scr)

    rt = _route_rows(lg_ref[...])
    rt_ref[...] = rt
    rk, seen = _rank_rows(rt, seen_scr[...])
    rk_ref[...] = rk
    seen_scr[...] = seen
    cnt_ref[...] = jnp.broadcast_to(seen, cnt_ref.shape)


def _route_rank_call(lg):
    n = lg.shape[0]
    spec = pl.BlockSpec((TM_ROUTE, LANES), lambda i: (i, 0))
    return pl.pallas_call(
        _route_rank_kernel, grid=(n // TM_ROUTE,), in_specs=[spec],
        out_specs=[spec, spec, pl.BlockSpec((SUBLANES, LANES), lambda i: (0, 0))],
        out_shape=[jax.ShapeDtypeStruct((n, LANES), F32), jax.ShapeDtypeStruct((n, LANES), F32),
                   jax.ShapeDtypeStruct((SUBLANES, LANES), F32)],
        scratch_shapes=[pltpu.VMEM((1, LANES), F32)],
        compiler_params=_cparams(("arbitrary",)), name="route_rank",
    )(lg)


def _dest_kernel(rt_ref, rk_ref, cnt_ref, dd_ref, be_ref, nu_ref, end_ref):
    tm = TM_ROUTE
    lane, is0, is1 = _expert_onehots(rt_ref[...], tm)
    blocks = jnp.floor((cnt_ref[...] + (BLK_E - 1)) * (1.0 / BLK_E))
    er = lax.broadcasted_iota(I32, (LANES, LANES), 0)
    ec = lax.broadcasted_iota(I32, (LANES, LANES), 1)
    upto = jnp.where(er <= ec, 1.0, 0.0).astype(BF16)
    bend = _dot(blocks.astype(BF16), upto)
    pstart = (bend[0:1, :] - blocks[0:1, :]) * BLK_E
    rk = rk_ref[...]
    r0 = jnp.sum(jnp.where(lane == 0, rk, 0.0), axis=-1, keepdims=True)
    r1 = jnp.sum(jnp.where(lane == 1, rk, 0.0), axis=-1, keepdims=True)
    d0 = jnp.sum(jnp.where(is0, pstart, 0.0), axis=-1, keepdims=True) + r0
    d1 = jnp.sum(jnp.where(is1, pstart, 0.0), axis=-1, keepdims=True) + r1
    dd_ref[...] = jnp.where(lane == 0, d0, jnp.where(lane == 1, d1, 0.0)).astype(I32)
    nb = be_ref.shape[0]
    blk = lax.broadcasted_iota(I32, (nb, LANES), 0).astype(F32)
    lane_b = lax.broadcasted_iota(I32, (nb, LANES), 1)
    ended = jnp.where((lane_b < N_EXPERTS) & (bend[0:1, :] <= blk), 1.0, 0.0)
    be = jnp.minimum(jnp.sum(ended, axis=-1, keepdims=True), float(N_EXPERTS - 1))
    be_ref[...] = jnp.broadcast_to(be, be_ref.shape).astype(I32)
    lane_c = lax.broadcasted_iota(I32, (SUBLANES, LANES), 1)
    used = jnp.sum(jnp.where(lane_c == N_EXPERTS - 1, bend, 0.0), axis=-1, keepdims=True)
    nu_ref[...] = jnp.broadcast_to(used, nu_ref.shape).astype(I32)
    end_ref[...] = (bend * BLK_E).astype(I32)


def _dest_call(rt, rk, cnt, n_blocks):
    n = rt.shape[0]
    nb_pad = -(-n_blocks // SUBLANES) * SUBLANES
    spec = pl.BlockSpec((TM_ROUTE, LANES), lambda i: (i, 0))
    const = lambda rows: pl.BlockSpec((rows, LANES), lambda i: (0, 0))
    return pl.pallas_call(
        _dest_kernel, grid=(n // TM_ROUTE,), in_specs=[spec, spec, const(SUBLANES)],
        out_specs=[spec, const(nb_pad), const(SUBLANES), const(SUBLANES)],
        out_shape=[jax.ShapeDtypeStruct((n, LANES), I32), jax.ShapeDtypeStruct((nb_pad, LANES), I32),
                   jax.ShapeDtypeStruct((SUBLANES, LANES), I32), jax.ShapeDtypeStruct((SUBLANES, LANES), I32)],
        compiler_params=_cparams(("arbitrary",)), name="dest",
    )(rt, rk, cnt)


def _slab_copy(src, src_row, dst, dst_row, sem):
    return pltpu.make_async_copy(src.at[pl.ds(pl.multiple_of(src_row * SUBLANES, SUBLANES), SUBLANES)],
                                 dst.at[pl.ds(pl.multiple_of(dst_row * SUBLANES, SUBLANES), SUBLANES)], sem)


def _dispatch_kernel(dest_ref, seg_end_ref, h2_ref, xs_hbm, zero_scr, sem, zsem):
    tm = TM_DISPATCH
    base = pl.program_id(0) * tm

    @pl.when(pl.program_id(0) == 0)
    def _():
        zero_scr[...] = jnp.zeros_like(zero_scr)
        clears = [pltpu.make_async_copy(
            zero_scr, xs_hbm.at[pl.ds(pl.multiple_of(jnp.maximum(seg_end_ref[e] - BLK_E, 0) * SUBLANES, SUBLANES),
                                      BLK_E * SUBLANES)], zsem) for e in range(N_EXPERTS)]
        for c in clears:
            c.start()
        for c in clears:
            c.wait()
        n_blocks = xs_hbm.shape[0] // (BLK_E * SUBLANES)
        first_free = lax.shift_right_logical(seg_end_ref[N_EXPERTS - 1], int(np.log2(BLK_E)))
        def tail_clear(b):
            row = pl.multiple_of((first_free + b) * (BLK_E * SUBLANES), BLK_E * SUBLANES)
            return pltpu.make_async_copy(zero_scr, xs_hbm.at[pl.ds(row, BLK_E * SUBLANES)], zsem)

        for b in range(N_EXPERTS):
            pl.when(first_free + b < n_blocks)(lambda b=b: tail_clear(b).start())
        for b in range(N_EXPERTS):
            pl.when(first_free + b < n_blocks)(lambda b=b: tail_clear(b).wait())

    def issue(r, _):
        for k in range(2):
            _slab_copy(h2_ref, r, xs_hbm, dest_ref[2 * (base + r) + k], sem).start(priority=k)
        return 0

    lax.fori_loop(0, tm, issue, 0, unroll=GATHER_UNROLL)
    for _ in range(2):
        pltpu.make_async_copy(h2_ref, xs_hbm.at[pl.ds(0, tm * SUBLANES)], sem).wait()


def _dispatch_call(dest, seg_end, h2s, n_slots):
    n = h2s.shape[0] // SUBLANES
    return pl.pallas_call(
        _dispatch_kernel,
        grid_spec=pltpu.PrefetchScalarGridSpec(
            num_scalar_prefetch=2, grid=(n // TM_DISPATCH,),
            in_specs=[pl.BlockSpec((TM_DISPATCH * SUBLANES, LANES), lambda i, d, e: (i, 0))],
            out_specs=pl.BlockSpec(memory_space=pl.ANY),
            scratch_shapes=[pltpu.VMEM((BLK_E * SUBLANES, LANES), F32),
                            pltpu.SemaphoreType.DMA(()), pltpu.SemaphoreType.DMA(())]),
        out_shape=jax.ShapeDtypeStruct((n_slots * SUBLANES, LANES), F32),
        compiler_params=_cparams(("arbitrary",)),
        name="dispatch",
    )(dest, seg_end, h2s)


def _experts_kernel(be_ref, nused_ref, xs_ref, wg_ref, wu_ref, wd_ref, yb_ref, x_scr, wg_scr, wu_scr, wd_scr):
    i = pl.program_id(0)

    @pl.when(i >= nused_ref[0])
    def _():
        yb_ref[...] = jnp.zeros_like(yb_ref)

    @pl.when(i < nused_ref[0])
    def _():
        prev = be_ref[jnp.maximum(i - 1, 0)]

        @pl.when((i == 0) | (be_ref[i] != prev))
        def _():
            wg_scr[...] = wg_ref[0].astype(BF16)
            wu_scr[...] = wu_ref[0].astype(BF16)
            wd_scr[...] = wd_ref[0].astype(BF16)

        for c in range(D_MODEL // LANES):
            x_scr[:, c * LANES:(c + 1) * LANES] = xs_ref[pl.ds(c, BLK_E, stride=SUBLANES), :].astype(BF16)
        xb = x_scr[...]
        g = _dot(xb, wg_scr[...])
        u = _dot(xb, wu_scr[...])
        hid = (g * _sigmoid(g) * u).astype(BF16)
        y = _dot(hid, wd_scr[...])
        for c in range(D_MODEL // LANES):
            yb_ref[pl.ds(c, BLK_E, stride=SUBLANES), :] = y[:, c * LANES:(c + 1) * LANES]


def _experts_call(block_e, n_used, xs, w_gate, w_up, w_down):
    n_blocks = block_e.shape[0]
    slab = pl.BlockSpec((BLK_E * SUBLANES, LANES), lambda i, be, nu: (i, 0))
    used_slab = pl.BlockSpec((BLK_E * SUBLANES, LANES), lambda i, be, nu: (jnp.minimum(i, nu[0] - 1), 0))
    return pl.pallas_call(
        _experts_kernel,
        grid_spec=pltpu.PrefetchScalarGridSpec(
            num_scalar_prefetch=2, grid=(n_blocks,),
            in_specs=[used_slab,
                      pl.BlockSpec((1, D_MODEL, D_EXPERT), lambda i, be, nu: (be[i], 0, 0)),
                      pl.BlockSpec((1, D_MODEL, D_EXPERT), lambda i, be, nu: (be[i], 0, 0)),
                      pl.BlockSpec((1, D_EXPERT, D_MODEL), lambda i, be, nu: (be[i], 0, 0))],
            out_specs=slab,
            scratch_shapes=[pltpu.VMEM((BLK_E, D_MODEL), BF16),
                            pltpu.VMEM((D_MODEL, D_EXPERT), BF16),
                            pltpu.VMEM((D_MODEL, D_EXPERT), BF16),
                            pltpu.VMEM((D_EXPERT, D_MODEL), BF16)]),
        out_shape=jax.ShapeDtypeStruct(xs.shape, F32),
        compiler_params=_cparams(("arbitrary",)),
        name="experts",
    )(block_e, n_used, xs, w_gate, w_up, w_down)


def _final_kernel(dest_ref, x1_ref, rt_ref, mod_ref, gfin_ref, yb_hbm, o_ref, g_scr, sem):
    tm = TM_FINAL
    step = pl.program_id(0)
    slot = step % 2
    slot_rows = 2 * tm * SUBLANES

    def gather(s, into):
        def issue(r, _):
            tok = s * tm + r
            for k in range(2):
                _slab_copy(yb_hbm, dest_ref[2 * tok + k], g_scr.at[into], k * tm + r,
                           sem.at[into]).start(priority=k)
            return 0
        lax.fori_loop(0, tm, issue, 0, unroll=GATHER_UNROLL)

    @pl.when(step == 0)
    def _():
        gather(0, 0)

    @pl.when(step + 1 < pl.num_programs(0))
    def _():
        gather(step + 1, 1 - slot)

    g_now = g_scr.at[slot]
    pltpu.make_async_copy(yb_hbm.at[pl.ds(0, slot_rows)], g_now, sem.at[slot]).wait()

    rt = rt_ref[...]
    lane = lax.broadcasted_iota(I32, (tm, LANES), 1)
    gw0 = jnp.sum(jnp.where(lane == 2, rt, 0.0), axis=-1, keepdims=True)
    gw1 = jnp.sum(jnp.where(lane == 3, rt, 0.0), axis=-1, keepdims=True)
    gt2 = mod_ref[0, 5:6, :]
    x1 = x1_ref[...]
    cols = []
    for c in range(D_MODEL // LANES):
        y0 = g_now[pl.ds(c, tm, stride=SUBLANES), :]
        y1 = g_now[pl.ds(tm * SUBLANES + c, tm, stride=SUBLANES), :]
        y = gw0 * y0 + gw1 * y1
        cols.append(x1[:, c * LANES:(c + 1) * LANES] + gt2[:, c * LANES:(c + 1) * LANES] * y)
    x2 = jnp.concatenate(cols, axis=1)
    var = jnp.mean(x2 * x2, axis=-1, keepdims=True)
    o_ref[...] = x2 * lax.rsqrt(var + RMS_EPS) * gfin_ref[...]


def _final_call(dest, x1, rt, mod3, g_final, yb, seq):
    n = x1.shape[0]
    tpb = seq // TM_FINAL
    return pl.pallas_call(
        _final_kernel,
        grid_spec=pltpu.PrefetchScalarGridSpec(
            num_scalar_prefetch=1, grid=(n // TM_FINAL,),
            in_specs=[pl.BlockSpec((TM_FINAL, D_MODEL), lambda i, d: (i, 0)),
                      pl.BlockSpec((TM_FINAL, LANES), lambda i, d: (i, 0)),
                      pl.BlockSpec((1, 6, D_MODEL), lambda i, d: (i // tpb, 0, 0)),
                      pl.BlockSpec((1, D_MODEL), lambda i, d: (0, 0)),
                      pl.BlockSpec(memory_space=pl.ANY)],
            out_specs=pl.BlockSpec((TM_FINAL, D_MODEL), lambda i, d: (i, 0)),
            scratch_shapes=[pltpu.VMEM((2, 2 * TM_FINAL * SUBLANES, LANES), F32),
                            pltpu.SemaphoreType.DMA((2,))]),
        out_shape=jax.ShapeDtypeStruct((n, D_MODEL), F32),
        compiler_params=_cparams(("arbitrary",)),
        name="final",
    )(dest, x1, rt, mod3, g_final, yb)


def _permute_w_in(w):
    o = np.cumsum([0, 512, 64, 64, 512, 64, 8, 512, 512, 512, 8, 1024, 1024])
    qa, ka, va, qi, ki, wi, qb, kb, vb, fb, ga, gb = [w[:, o[k]:o[k + 1]] for k in range(12)]
    aux = jnp.concatenate([wi, fb, jnp.zeros((w.shape[0], LANES - 2 * N_HEADS), w.dtype)], axis=1)
    return jnp.concatenate([qa, qi, qb, kb, vb, ga, gb, ka, ka, va, va, ki, ki, aux], axis=1).astype(BF16)


def _layer(x2, pos2, mod3, batch, seq, g_mix, w_in, b_forget, w_out_a, w_out_b, w_out, g_ffn,
           w_group, b_group, w_router, b_router, w_e_gate, w_e_up, w_e_down, g_final):
    n = x2.shape[0]
    inv_freq = ROPE_THETA ** (-jnp.arange(0, ROPE_DIM, 2, dtype=F32) / ROPE_DIM)
    jj = np.arange(LANES) % HEAD_DIM
    invf = jnp.where(jj < ROPE_DIM, inv_freq[jj % (ROPE_DIM // 2)], 0.0)[None, :].astype(F32)
    fbias = jnp.zeros((1, LANES), F32).at[0, N_HEADS:2 * N_HEADS].set(b_forget.astype(F32))

    (qa, qi, qb, kaug, vb, sga, sgb, ka2, va2, ki2, aux) = _inproj_call(
        x2, pos2, mod3, g_mix.reshape(1, D_MODEL), _permute_w_in(w_in), invf, fbias, batch, seq)

    r3 = lambda a: a.reshape(batch, seq, a.shape[-1])
    tr = lambda a: jnp.swapaxes(r3(a), 1, 2)
    tr_tiles = lambda a, tk: jnp.swapaxes(a.reshape(batch, seq // tk, tk, a.shape[-1]), 2, 3)
    aux_t = tr(aux[:, :4 * N_HEADS])
    obt = _fox_call(tr(qb), aux_t, r3(kaug), tr_tiles(vb, TK_FOX))
    oat = _dsa_call(tr(qa), tr(qi), aux_t, r3(ki2), r3(ka2), tr_tiles(va2[:, :HEAD_DIM], TK_DSA))
    oa = jnp.swapaxes(oat, 1, 2).reshape(n, 512)
    ob = jnp.swapaxes(obt, 1, 2).reshape(n, 512)

    w_rt = jnp.concatenate([w_router, w_group, jnp.zeros((D_MODEL, LANES - N_EXPERTS - N_GROUPS), F32)], axis=1)
    wr_hi = w_rt.astype(BF16)
    wr_lo = (w_rt - wr_hi.astype(F32)).astype(BF16)
    br = jnp.concatenate([b_router, b_group, jnp.zeros((LANES - N_EXPERTS - N_GROUPS,), F32)])[None, :]
    x1, h2s, lg = _outproj_call(oa, ob, sga, sgb, x2, mod3,
                                g_ffn.reshape(1, D_MODEL), w_out_a.astype(BF16), w_out_b.astype(BF16),
                                w_out.astype(BF16), wr_hi, wr_lo, br, seq)
    rt, rk, cnt = _route_rank_call(lg)

    n_slots = 2 * n + N_EXPERTS * BLK_E
    n_blocks = n_slots // BLK_E
    dd, be, nu, seg_end = _dest_call(rt, rk, cnt, n_blocks)
    dest = dd[:, 0:2].reshape(-1)
    block_e = be[:n_blocks, 0]
    n_used = nu[0, 0:1]

    xs = _dispatch_call(dest, seg_end[0, :N_EXPERTS], h2s, n_slots)
    yb = _experts_call(block_e, n_used, xs, w_e_gate, w_e_up, w_e_down)
    return _final_call(dest, x1, rt, mod3, g_final.reshape(1, D_MODEL), yb, seq)


def kernel(x, c, positions, w_mod, b_mod, g_mix, w_in, b_forget, w_out_a, w_out_b, w_out, g_ffn, w_group,
           b_group, w_router, b_router, w_e_gate, w_e_up, w_e_down, g_final):
    batch, seq, d = x.shape
    depth = w_mod.shape[0]
    assert depth == 1 and d == D_MODEL, "kernel fuses the final norm into the single layer"
    n = batch * seq
    c8 = jnp.zeros((8, d), F32).at[:batch].set(c)
    mod = _mod_call(c8, w_mod[0], b_mod[0][None, :])
    mod3 = mod[:batch].reshape(batch, 6, d)
    out = _layer(x.reshape(n, d), positions.reshape(n, 1), mod3, batch, seq, g_mix[0], w_in[0], b_forget[0],
                 w_out_a[0], w_out_b[0], w_out[0], g_ffn[0], w_group[0], b_group[0], w_router[0], b_router[0],
                 w_e_gate[0], w_e_up[0], w_e_down[0], g_final)
    return out.reshape(batch, seq, d)
```

```python
import functools

import numpy as np
import jax
import jax.numpy as jnp
from jax import lax
from jax.experimental import pallas as pl
from jax.experimental.pallas import tpu as pltpu

F32 = jnp.float32
BF16 = jnp.bfloat16
I32 = jnp.int32

D_MODEL = 1024
HEAD_DIM = 64
N_HEADS = 8
CHUNK_SHIFT = 6
TOPK = 256
ROPE_DIM = 16
ROPE_THETA = 500000.0
N_GROUPS = 4
EXPERTS_PER_GROUP = 8
N_EXPERTS = 32
D_EXPERT = 512
RMS_EPS = 1e-6
NEG_INF = -1e30
ATTN_SCALE = HEAD_DIM ** -0.5
IDX_SCALE = HEAD_DIM ** -0.5
LOG2E = 1.4426950408889634

LANES = 128
SUBLANES = 8
VMEM_LIMIT = 56 * 1024 * 1024

TM_IN = 512
TQ_FOX = 512
TK_FOX = 512
TQ_DSA = 256
TK_DSA = 512
TM_OUT = 512
TM_ROUTE = 1024
TM_DISPATCH = 1024
BLK_E = 256
TM_FINAL = 512
GATHER_UNROLL = 8

RADIX_BLOCK = 32 * SUBLANES
BOUND_MARGIN = 1.02
BOUND_SAFE = 40.0
FOX_BIAS_SLACK = 1.0
FOX_K = 256
FOX_AUG_ROWS = 16

_SEG_KB, _SEG_GA, _SEG_GB, _SEG_KA, _SEG_KI, _SEG_AUX = 0, 512, 1536, 2560, 2688, 2816
_NC_IN = 2944
_ROW_QA, _ROW_QI, _ROW_QB, _ROW_VB, _ROW_VA = 0, 512, 1024, 1536, 2048
_NR_IN = 2048 + HEAD_DIM

_SENT_KEY = int(np.array(NEG_INF, np.float32).view(np.int32) ^ 0x7FFFFFFF)
_INT_MIN = -(2 ** 31)


def _cparams(sem):
    return pltpu.CompilerParams(dimension_semantics=sem, vmem_limit_bytes=VMEM_LIMIT)


def _dot(a, b):
    return jnp.dot(a, b, preferred_element_type=F32)


def _sigmoid(x):
    return 1.0 / (1.0 + jnp.exp(-x))


def _mod_kernel(c_ref, w_ref, b_ref, o_ref):
    c = c_ref[...]
    ca = c * _sigmoid(c)
    o_ref[...] = _dot(ca.astype(BF16), w_ref[...].astype(BF16)) + b_ref[...]


def _mod_call(c8, w_mod, b_mod):
    n_out = w_mod.shape[1]
    tn = 1024
    return pl.pallas_call(
        _mod_kernel,
        grid=(n_out // tn,),
        in_specs=[pl.BlockSpec((8, D_MODEL), lambda j: (0, 0)),
                  pl.BlockSpec((D_MODEL, tn), lambda j: (0, j)),
                  pl.BlockSpec((1, tn), lambda j: (0, j))],
        out_specs=pl.BlockSpec((8, tn), lambda j: (0, j)),
        out_shape=jax.ShapeDtypeStruct((8, n_out), F32),
        compiler_params=_cparams(("arbitrary",)),
        name="mod",
    )(c8, w_mod, b_mod)


def _inproj_kernel(x_ref, pos_ref, posr_ref, mod_ref, g_ref, w_ref, wt_ref, invf_ref, invf8_ref, fbias_ref,
                   sel_ref, ones_ref,
                   qat_ref, qit_ref, qbt_ref, kb_ref, vbt_ref, sga_ref, sgb_ref,
                   ka_ref, vat_ref, ki_ref, auxt_ref, h_scr, carry_scr):
    tm = TM_IN
    x = x_ref[...]
    var = jnp.mean(x * x, axis=-1, keepdims=True)
    tf = x * lax.rsqrt(var + RMS_EPS) * g_ref[...]
    sh = mod_ref[0, 0:1, :]
    sc = mod_ref[0, 1:2, :]
    h_scr[...] = (tf * (1.0 + sc) + sh).astype(BF16)
    hb = h_scr[...]

    lane = lax.broadcasted_iota(I32, (tm, LANES), 1)
    j = lane & (HEAD_DIM - 1)
    ang = pos_ref[...].astype(F32) * invf_ref[...]
    cs = jnp.cos(ang)
    sn = jnp.sin(ang)
    coef_next = jnp.where(j < ROPE_DIM // 2, -sn, 0.0)
    coef_prev = jnp.where((j >= ROPE_DIM // 2) & (j < ROPE_DIM), sn, 0.0)

    def rope(tc):
        return (tc * cs + pltpu.roll(tc, LANES - ROPE_DIM // 2, 1) * coef_next
                + pltpu.roll(tc, ROPE_DIM // 2, 1) * coef_prev)

    def seg(off, width):
        return _dot(hb, w_ref[:, off:off + width])

    def seg_t(row0, rows):
        return lax.dot_general(wt_ref[row0:row0 + rows, :], hb, (((1,), (1,)), ((), ())),
                               preferred_element_type=F32)

    ang8 = invf8_ref[:, 0:1] * posr_ref[0].astype(F32)
    cs8 = jnp.cos(ang8)
    sn8 = jnp.sin(ang8)
    half = ROPE_DIM // 2

    def rope_t(t, scale, out_ref):
        for h in range(N_HEADS):
            r0 = h * HEAD_DIM
            lo = t[r0:r0 + half, :]
            hi = t[r0 + half:r0 + ROPE_DIM, :]
            head = jnp.concatenate([lo * cs8 - hi * sn8, hi * cs8 + lo * sn8, t[r0 + ROPE_DIM:r0 + HEAD_DIM, :]],
                                   axis=0)
            out_ref[0, r0:r0 + HEAD_DIM, :] = (head * scale).astype(BF16)

    rope_t(seg_t(_ROW_QA, 512), ATTN_SCALE * LOG2E, qat_ref)
    rope_t(seg_t(_ROW_QI, 512), IDX_SCALE, qit_ref)
    qbt_ref[0] = (seg_t(_ROW_QB, 512) * (ATTN_SCALE * LOG2E)).astype(BF16)
    vbt_ref[0, 0] = seg_t(_ROW_VB, 512).astype(BF16)
    vat_ref[0, 0] = seg_t(_ROW_VA, HEAD_DIM).astype(BF16)
    sga_ref[...] = _sigmoid(seg(_SEG_GA, 1024)).astype(BF16)
    sgb_ref[...] = _sigmoid(seg(_SEG_GB, 1024)).astype(BF16)
    ka = rope(seg(_SEG_KA, LANES))
    ka_ref[...] = ka.astype(BF16)
    ki_ref[...] = rope(seg(_SEG_KI, LANES)).astype(BF16)

    z = seg(_SEG_AUX, LANES)
    zf = z + fbias_ref[...]
    logf = jnp.minimum(zf, 0.0) - jnp.log(1.0 + jnp.exp(-jnp.abs(zf)))
    is_f = (lane >= N_HEADS) & (lane < 2 * N_HEADS)
    logf = jnp.where(is_f, logf, 0.0)

    def split3(v):
        hi = v.astype(BF16)
        r1 = v - hi.astype(F32)
        mid = r1.astype(BF16)
        return hi, mid, (r1 - mid.astype(F32)).astype(BF16)

    rr = lax.broadcasted_iota(I32, (tm, tm), 0)
    cc = lax.broadcasted_iota(I32, (tm, tm), 1)
    tri = jnp.where(cc <= rr, 1.0, 0.0).astype(BF16)
    p_hi, p_mid, p_lo = split3(logf)

    @pl.when(pl.program_id(1) == 0)
    def _():
        carry_scr[...] = jnp.zeros_like(carry_scr)

    parts = _dot(tri, jnp.concatenate([p_hi, p_mid, p_lo], axis=1))
    cum = parts[:, 0:LANES] + parts[:, LANES:2 * LANES] + parts[:, 2 * LANES:3 * LANES] + carry_scr[...]
    carry_scr[...] = cum[tm - 1:tm, :]
    cum2 = cum * LOG2E
    aux = jnp.where(lane < N_HEADS, z * (N_HEADS ** -0.5), jnp.where(is_f, cum2, 0.0))

    kb = seg(_SEG_KB, 512)
    first = lane < HEAD_DIM

    def half_norms(blk):
        sq = blk * blk
        return (jnp.sum(jnp.where(first, sq, 0.0), axis=-1, keepdims=True),
                jnp.sum(jnp.where(first, 0.0, sq), axis=-1, keepdims=True))

    for p in range(N_HEADS // 2):
        n_even, n_odd = half_norms(kb[:, p * LANES:(p + 1) * LANES])
        aux = jnp.where(lane == 2 * N_HEADS + 2 * p, n_even, jnp.where(lane == 2 * N_HEADS + 2 * p + 1, n_odd, aux))
    aux = jnp.where(lane == 3 * N_HEADS, half_norms(ka)[0], aux)
    auxt_ref[0] = aux.T[0:4 * N_HEADS, :]

    c_hi, c_mid, c_lo = split3(cum2)
    kaug = _dot(jnp.concatenate([c_hi, c_mid, c_lo], axis=1), sel_ref[...]) + ones_ref[...]
    for p in range(N_HEADS // 2):
        kb_ref[:, p * FOX_K:p * FOX_K + LANES] = kb[:, p * LANES:(p + 1) * LANES].astype(BF16)
        kb_ref[:, p * FOX_K + LANES:(p + 1) * FOX_K] = kaug[:, p * LANES:(p + 1) * LANES].astype(BF16)


def _fox_routing_constants():
    sel = np.zeros((3, LANES, (N_HEADS // 2) * LANES), np.float32)
    ones = np.zeros((1, (N_HEADS // 2) * LANES), np.float32)
    for h in range(N_HEADS):
        for t in range(3):
            sel[t, N_HEADS + h, (h // 2) * LANES + 3 + 3 * (h % 2) + t] = -1.0
    for p in range(N_HEADS // 2):
        ones[0, p * LANES:p * LANES + 3] = 1.0
    return jnp.asarray(sel.reshape(3 * LANES, -1), BF16), jnp.asarray(ones, F32)


def _inproj_call(x2, pos2, pos_rows, mod3, g_mix, w_rows, w_t, invf, invf8, fbias, batch, seq):
    assert TM_IN == TK_FOX == TK_DSA, "value tiles are written per in-projection step"
    n = x2.shape[0]
    tpb = seq // TM_IN
    row = lambda b, s: (b * tpb + s, 0)
    rows = lambda w: (jax.ShapeDtypeStruct((n, w), BF16), pl.BlockSpec((TM_IN, w), row))
    chan = lambda c: (jax.ShapeDtypeStruct((batch, c, seq), BF16), pl.BlockSpec((1, c, TM_IN), lambda b, s: (b, 0, s)))
    tile = lambda c: (jax.ShapeDtypeStruct((batch, tpb, c, TM_IN), BF16),
                      pl.BlockSpec((1, 1, c, TM_IN), lambda b, s: (b, s, 0, 0)))
    aux_t = (jax.ShapeDtypeStruct((batch, 4 * N_HEADS, seq), F32),
             pl.BlockSpec((1, 4 * N_HEADS, TM_IN), lambda b, s: (b, 0, s)))
    outs = [chan(512), chan(512), chan(512), rows((N_HEADS // 2) * FOX_K), tile(512), rows(1024), rows(1024),
            rows(LANES), tile(HEAD_DIM), rows(LANES), aux_t]
    sel, ones = _fox_routing_constants()
    const = lambda a: pl.BlockSpec(a.shape, lambda b, s: (0,) * a.ndim)
    return pl.pallas_call(
        _inproj_kernel,
        grid=(batch, tpb),
        in_specs=[pl.BlockSpec((TM_IN, D_MODEL), row),
                  pl.BlockSpec((TM_IN, 1), row),
                  pl.BlockSpec((1, 1, TM_IN), lambda b, s: (b, 0, s)),
                  pl.BlockSpec((1, 6, D_MODEL), lambda b, s: (b, 0, 0)),
                  const(g_mix), const(w_rows), const(w_t), const(invf), const(invf8), const(fbias),
                  const(sel), const(ones)],
        out_specs=[o[1] for o in outs],
        out_shape=[o[0] for o in outs],
        scratch_shapes=[pltpu.VMEM((TM_IN, D_MODEL), BF16), pltpu.VMEM((1, LANES), F32)],
        compiler_params=_cparams(("arbitrary", "arbitrary")),
        name="inproj",
    )(x2, pos2, pos_rows, mod3, g_mix, w_rows, w_t, invf, invf8, fbias, sel, ones)


def _softmax_max(s, m_ref):
    m_ref[...] = jnp.maximum(m_ref[...], jnp.max(s, axis=0, keepdims=True))


def _softmax_accumulate(s, m_ref, l_ref, acc_ref, v_t):
    p = jnp.exp2(s - m_ref[...])
    l_ref[...] = l_ref[...] + jnp.sum(p, axis=0, keepdims=True)
    acc_ref[...] = acc_ref[...] + _dot(v_t, p.astype(BF16))


def _softmax_scratch(tq):
    return ([pltpu.VMEM((1, tq), F32) for _ in range(2 * N_HEADS)]
            + [pltpu.VMEM((HEAD_DIM, tq), F32) for _ in range(N_HEADS)])


def _softmax_split(refs):
    return refs[:N_HEADS], refs[N_HEADS:2 * N_HEADS], refs[2 * N_HEADS:3 * N_HEADS]


def _softmax_init(m_refs, l_refs, acc_refs):
    for h in range(N_HEADS):
        m_refs[h][...] = jnp.full(m_refs[h].shape, NEG_INF, F32)
        l_refs[h][...] = jnp.zeros(l_refs[h].shape, F32)
        acc_refs[h][...] = jnp.zeros(acc_refs[h].shape, F32)


def _softmax_finish(o_ref, l_refs, acc_refs):
    for h in range(N_HEADS):
        o_ref[0, h * HEAD_DIM:(h + 1) * HEAD_DIM, :] = (acc_refs[h][...] * (1.0 / l_refs[h][...])).astype(BF16)


def _logit_bounds(qt_ref, kn2, slack):
    k_max = jnp.sqrt(jnp.max(kn2, axis=-1, keepdims=True))
    bounds = []
    worst = jnp.float32(0.0)
    for h in range(N_HEADS):
        qh = qt_ref[0, h * HEAD_DIM:(h + 1) * HEAD_DIM, :].astype(F32)
        q_norm = jnp.sqrt(jnp.sum(qh * qh, axis=0, keepdims=True))
        b = q_norm * k_max[h:h + 1, :] * BOUND_MARGIN + slack
        bounds.append(b)
        worst = jnp.maximum(worst, jnp.max(b))
    return bounds, worst <= BOUND_SAFE


def _fox_kernel(qt_ref, cumt_ref, kn_ref, kaug_ref, vt_ref, o_ref, rhs_scr, *softmax_refs):
    m_scr, l_scr, acc_scr = _softmax_split(softmax_refs)
    tq, tk = TQ_FOX, TK_FOX
    i = pl.program_id(1)
    q0 = i * tq

    cq = cumt_ref[0]
    c_hi = cq.astype(BF16).astype(F32)
    c_r = cq - c_hi
    c_mid = c_r.astype(BF16).astype(F32)
    c_lo = c_r - c_mid
    row_q = lax.broadcasted_iota(I32, (LANES, tq), 0)
    row_a = lax.broadcasted_iota(I32, (FOX_AUG_ROWS, 2 * tq), 0)
    second = lax.broadcasted_iota(I32, (FOX_AUG_ROWS, 2 * tq), 1) >= tq
    for p in range(N_HEADS // 2):
        qp = qt_ref[0, p * LANES:(p + 1) * LANES, :]
        zq = jnp.zeros_like(qp)
        rhs_scr[p, 0:LANES, 0:tq] = jnp.where(row_q < HEAD_DIM, qp, zq)
        rhs_scr[p, 0:LANES, tq:2 * tq] = jnp.where(row_q >= HEAD_DIM, qp, zq)
        pair_row = lambda a: jnp.concatenate([a[2 * p:2 * p + 1, :], a[2 * p + 1:2 * p + 2, :]], axis=1)
        aug = jnp.where(row_a == 0, pair_row(c_hi),
              jnp.where(row_a == 1, pair_row(c_mid),
              jnp.where(row_a == 2, pair_row(c_lo),
              jnp.where((row_a >= 3) & (row_a < 6), jnp.where(second, 0.0, 1.0),
              jnp.where((row_a >= 6) & (row_a < 9), jnp.where(second, 1.0, 0.0), 0.0)))))
        rhs_scr[p, LANES:LANES + FOX_AUG_ROWS, :] = aug.astype(BF16)
        rhs_scr[p, LANES + FOX_AUG_ROWS:FOX_K, :] = jnp.zeros((FOX_K - LANES - FOX_AUG_ROWS, 2 * tq), BF16)

    _softmax_init(m_scr, l_scr, acc_scr)
    key_j = lax.broadcasted_iota(I32, (tk, tq), 0)
    qry_i = q0 + lax.broadcasted_iota(I32, (tk, tq), 1)

    def tile(kt, masked, second_pass):
        k0 = pl.multiple_of(kt * tk, tk)
        for p in range(N_HEADS // 2):
            st = _dot(kaug_ref[0, pl.ds(k0, tk), p * FOX_K:(p + 1) * FOX_K], rhs_scr[p])
            for hh in range(2):
                h = 2 * p + hh
                s = st[:, hh * tq:(hh + 1) * tq]
                if masked:
                    s = jnp.where(k0 + key_j <= qry_i, s, NEG_INF)
                if second_pass:
                    _softmax_accumulate(s, m_scr[h], l_scr[h], acc_scr[h],
                                        vt_ref[0, kt, h * HEAD_DIM:(h + 1) * HEAD_DIM, :])
                else:
                    _softmax_max(s, m_scr[h])

    n_full = i * (tq // tk)

    def sweep(second_pass):
        def full_tile(kt, carry):
            tile(kt, False, second_pass)
            return carry

        lax.fori_loop(0, n_full, full_tile, 0)
        for d in range(tq // tk):
            tile(n_full + d, True, second_pass)

    bounds, safe = _logit_bounds(qt_ref, kn_ref[0], FOX_BIAS_SLACK)

    @pl.when(safe)
    def _():
        for h in range(N_HEADS):
            m_scr[h][...] = bounds[h]
        sweep(True)

    @pl.when(jnp.logical_not(safe))
    def _():
        sweep(False)
        sweep(True)

    _softmax_finish(o_ref, l_scr, acc_scr)


def _fox_call(qbt, aux_t, kaug3, vbt4):
    batch, _, seq = qbt.shape
    return pl.pallas_call(
        _fox_kernel,
        grid=(batch, seq // TQ_FOX),
        in_specs=[pl.BlockSpec((1, 512, TQ_FOX), lambda b, i: (b, 0, i)),
                  pl.BlockSpec((1, N_HEADS, TQ_FOX), lambda b, i: (b, 1, i)),
                  pl.BlockSpec((1, N_HEADS, seq), lambda b, i: (b, 2, 0)),
                  pl.BlockSpec((1, seq, (N_HEADS // 2) * FOX_K), lambda b, i: (b, 0, 0)),
                  pl.BlockSpec((1, seq // TK_FOX, 512, TK_FOX), lambda b, i: (b, 0, 0, 0))],
        out_specs=pl.BlockSpec((1, 512, TQ_FOX), lambda b, i: (b, 0, i)),
        out_shape=jax.ShapeDtypeStruct((batch, 512, seq), BF16),
        scratch_shapes=[pltpu.VMEM((N_HEADS // 2, FOX_K, 2 * TQ_FOX), BF16)] + _softmax_scratch(TQ_FOX),
        compiler_params=_cparams(("arbitrary", "arbitrary")),
        name="fox",
    )(qbt, aux_t, aux_t, kaug3, vbt4)


def _bit_transpose32(words):
    a = list(words)
    j, mask = 16, 0x0000FFFF
    while j:
        mask_i = int(np.array(mask, np.uint32).view(np.int32))
        k = 0
        while k < 32:
            t = (a[k] ^ lax.shift_right_logical(a[k + j], j)) & mask_i
            a[k] = a[k] ^ t
            a[k + j] = a[k + j] ^ lax.shift_left(t, j)
            k = (k + j + 1) & ~j
        j >>= 1
        mask = (mask ^ (mask << j)) & 0xFFFFFFFF
    return a


def _dsa_kernel(qat_ref, qit_ref, wit_ref, kn_ref, ki_ref, ka_ref, vat_ref, o_ref,
                aqi_scr, aqa_scr, sc_scr, bias_scr, plane_scr, alive_scr, thr_scr, *softmax_refs, seq):
    m_scr, l_scr, acc_scr = _softmax_split(softmax_refs)
    tq, tk = TQ_DSA, TK_DSA
    blk_per_tile = tk // RADIX_BLOCK
    n_blocks = seq // RADIX_BLOCK
    i = pl.program_id(1)
    q0 = i * tq
    nk = lax.shift_right_logical(q0 + tq + tk - 1, int(np.log2(tk)))
    n_rest = (seq - nk * tk).astype(F32)

    row_q = lax.broadcasted_iota(I32, (LANES, tq), 0)
    for h in range(N_HEADS):
        keep = (row_q < HEAD_DIM) if h % 2 == 0 else (row_q >= HEAD_DIM)
        rows = slice((h // 2) * LANES, (h // 2 + 1) * LANES)
        qi_blk = qit_ref[0, rows, :]
        qa_blk = qat_ref[0, rows, :]
        aqi_scr[:, h * tq:(h + 1) * tq] = jnp.where(keep, qi_blk, jnp.zeros_like(qi_blk))
        aqa_scr[:, h * tq:(h + 1) * tq] = jnp.where(keep, qa_blk, jnp.zeros_like(qa_blk))

    w = wit_ref[0]
    key_j = lax.broadcasted_iota(I32, (tk, tq), 0)
    qry_chunk = lax.shift_right_logical(q0 + lax.broadcasted_iota(I32, (tk, tq), 1), CHUNK_SHIFT)

    def admissible(k0):
        return lax.shift_right_logical(k0 + key_j, CHUNK_SHIFT) <= qry_chunk

    def p1(kt, carry, masked):
        k0 = pl.multiple_of(kt * tk, tk)
        rel = _dot(ki_ref[0, pl.ds(k0, tk), :], aqi_scr[...])
        sc = w[0:1, :] * jnp.maximum(rel[:, 0:tq], 0.0)
        for h in range(1, N_HEADS):
            sc = sc + w[h:h + 1, :] * jnp.maximum(rel[:, h * tq:(h + 1) * tq], 0.0)
        if masked:
            sc = jnp.where(admissible(k0), sc, NEG_INF)
        sc = jnp.where(sc == 0.0, 0.0, sc)
        sc_scr[kt] = sc
        b = pltpu.bitcast(sc, I32)
        key = b ^ (lax.shift_right_arithmetic(b, 31) & 0x7FFFFFFF)
        for half in range(blk_per_tile):
            base = half * RADIX_BLOCK
            planes = _bit_transpose32([key[base + SUBLANES * m:base + SUBLANES * (m + 1), :] ^ _INT_MIN
                                       for m in range(32)])
            bl = kt * blk_per_tile + half
            for p in range(32):
                plane_scr[bl, p] = planes[p]
            alive_scr[bl] = jnp.full((SUBLANES, tq), -1, I32)
        return carry

    n_unmasked = lax.shift_right_logical(q0, int(np.log2(tk)))
    lax.fori_loop(0, n_unmasked, functools.partial(p1, masked=False), 0)
    lax.fori_loop(n_unmasked, nk, functools.partial(p1, masked=True), 0)

    def clear_block(bl, carry):
        plane_scr[bl] = jnp.zeros((32, SUBLANES, tq), I32)
        alive_scr[bl] = jnp.zeros((SUBLANES, tq), I32)
        return carry

    n_held = nk * blk_per_tile
    half_blocks = n_blocks // 2
    fits_half = n_held <= half_blocks
    lax.fori_loop(n_held, jnp.where(fits_half, half_blocks, n_blocks), clear_block, 0)

    sent_u = jnp.int32(_SENT_KEY ^ _INT_MIN)
    kf = float(TOPK)

    def radix_select(n_walk):
        def bit_step(p, carry):
            k_rem, thr_bits, rest_alive = carry
            shift = 31 - p
            sent_bit = lax.shift_right_logical(sent_u, shift) & 1
            ones = [alive_scr[bl] & plane_scr[bl, p] for bl in range(n_walk)]
            tot = lax.population_count(ones[0])
            for bl in range(1, n_walk):
                tot = tot + lax.population_count(ones[bl])
            cnt = jnp.sum(tot.astype(F32), axis=0, keepdims=True)
            cnt = cnt + jnp.where((rest_alive != 0) & (sent_bit != 0), n_rest, 0.0)
            take1 = cnt >= k_rem
            for bl in range(n_walk):
                alive_scr[bl] = jnp.where(take1, ones[bl], alive_scr[bl] & ~plane_scr[bl, p])
            k_rem = jnp.where(take1, k_rem, k_rem - cnt)
            thr_bits = thr_bits | jnp.where(take1, lax.shift_left(jnp.int32(1), shift), 0)
            rest_alive = jnp.where(take1 == (sent_bit != 0), rest_alive, 0)
            return k_rem, thr_bits, rest_alive

        _, thr_bits, _ = lax.fori_loop(
            0, 32, bit_step,
            (jnp.full((1, tq), kf, F32), jnp.zeros((1, tq), I32), jnp.ones((1, tq), I32)))
        thr_scr[...] = thr_bits

    pl.when(fits_half)(lambda: radix_select(half_blocks))
    pl.when(jnp.logical_not(fits_half))(lambda: radix_select(n_blocks))
    thr_key = thr_scr[...] ^ _INT_MIN
    thr0 = pltpu.bitcast(thr_key ^ (lax.shift_right_arithmetic(thr_key, 31) & 0x7FFFFFFF), F32)

    def rank_counts(thr):
        def body(kt, c):
            sc = sc_scr[kt]
            return (c[0] + jnp.sum(jnp.where(sc >= thr, 1.0, 0.0), axis=0, keepdims=True),
                    c[1] + jnp.sum(jnp.where(sc > thr, 1.0, 0.0), axis=0, keepdims=True))
        z = jnp.zeros((1, tq), F32)
        n_ge, n_gt = lax.fori_loop(0, nk, body, (z, z))
        return (n_ge + jnp.where(thr <= NEG_INF, n_rest, 0.0), n_gt + jnp.where(thr < NEG_INF, n_rest, 0.0))

    def misplaced(state):
        _, n_ge, n_gt = state
        return jnp.max(jnp.where((n_ge < kf) | (n_gt >= kf), 1.0, 0.0)) > 0.0

    def step_threshold(state):
        thr, n_ge, n_gt = state

        def body(kt, c):
            sc = sc_scr[kt]
            below = jnp.max(jnp.where(sc < thr, sc, -jnp.inf), axis=0, keepdims=True)
            above = jnp.min(jnp.where(sc > thr, sc, jnp.inf), axis=0, keepdims=True)
            return jnp.maximum(c[0], below), jnp.minimum(c[1], above)

        below, above = lax.fori_loop(0, nk, body, (jnp.full((1, tq), -jnp.inf, F32), jnp.full((1, tq), jnp.inf, F32)))
        has_rest = n_rest > 0.0
        below = jnp.where(has_rest & (thr > NEG_INF), jnp.maximum(below, NEG_INF), below)
        above = jnp.where(has_rest & (thr < NEG_INF), jnp.minimum(above, NEG_INF), above)
        thr = jnp.where(n_ge < kf, below, jnp.where(n_gt >= kf, above, thr))
        return (thr,) + rank_counts(thr)

    thr, n_ge, n_gt = lax.while_loop(misplaced, step_threshold, (thr0,) + rank_counts(thr0))
    need = kf - n_gt
    tie_free = jnp.min(jnp.where(n_ge == kf, 1.0, 0.0)) > 0.0

    tr = lax.broadcasted_iota(I32, (tk, tk), 0)
    tc = lax.broadcasted_iota(I32, (tk, tk), 1)
    tri = jnp.where(tc <= tr, 1.0, 0.0).astype(BF16)
    _softmax_init(m_scr, l_scr, acc_scr)

    def selection_bias(kt, tie):
        sc = sc_scr[kt]
        eq = sc == thr
        pref = _dot(tri, jnp.where(eq, 1.0, 0.0).astype(BF16)) + tie
        sel = (sc > thr) | (eq & (pref <= need))
        return jnp.where(sel & admissible(kt * tk), 0.0, NEG_INF), pref[tk - 1:tk, :]

    def logits(kt):
        return _dot(ka_ref[0, pl.ds(pl.multiple_of(kt * tk, tk), tk), :], aqa_scr[...])

    def accumulate(kt, bias):
        logit = logits(kt)
        v_t = vat_ref[0, kt]
        for h in range(N_HEADS):
            _softmax_accumulate(logit[:, h * tq:(h + 1) * tq] + bias, m_scr[h], l_scr[h], acc_scr[h], v_t)

    def p3_single(kt, tie):
        bias, tie = selection_bias(kt, tie)
        accumulate(kt, bias)
        return tie

    def p3_single_tie_free(kt, carry):
        accumulate(kt, jnp.where(sc_scr[kt] >= thr, 0.0, NEG_INF))
        return carry

    def p3_max(kt, tie):
        bias, tie = selection_bias(kt, tie)
        bias_scr[kt] = bias
        logit = logits(kt)
        for h in range(N_HEADS):
            _softmax_max(logit[:, h * tq:(h + 1) * tq] + bias, m_scr[h])
        return tie

    def p3_accumulate(kt, carry):
        accumulate(kt, bias_scr[kt])
        return carry

    kn2 = jnp.broadcast_to(kn_ref[0, 0:1, :], (N_HEADS, seq))
    bounds, safe = _logit_bounds(qat_ref, kn2, 0.0)
    no_tie = jnp.zeros((1, tq), F32)

    @pl.when(safe)
    def _():
        for h in range(N_HEADS):
            m_scr[h][...] = bounds[h]

    @pl.when(safe & tie_free)
    def _():
        lax.fori_loop(0, nk, p3_single_tie_free, 0)

    @pl.when(safe & jnp.logical_not(tie_free))
    def _():
        lax.fori_loop(0, nk, p3_single, no_tie)

    @pl.when(jnp.logical_not(safe))
    def _():
        lax.fori_loop(0, nk, p3_max, no_tie)
        lax.fori_loop(0, nk, p3_accumulate, 0)

    _softmax_finish(o_ref, l_scr, acc_scr)


def _dsa_call(qat, qit, aux_t, ki3, ka3, vat4):
    batch, _, seq = qat.shape
    qspec = pl.BlockSpec((1, 512, TQ_DSA), lambda b, i: (b, 0, i))
    kspec = pl.BlockSpec((1, seq, LANES), lambda b, i: (b, 0, 0))
    return pl.pallas_call(
        functools.partial(_dsa_kernel, seq=seq),
        grid=(batch, seq // TQ_DSA),
        in_specs=[qspec, qspec,
                  pl.BlockSpec((1, N_HEADS, TQ_DSA), lambda b, i: (b, 0, i)),
                  pl.BlockSpec((1, N_HEADS, seq), lambda b, i: (b, 3, 0)),
                  kspec, kspec,
                  pl.BlockSpec((1, seq // TK_DSA, HEAD_DIM, TK_DSA), lambda b, i: (b, 0, 0, 0))],
        out_specs=qspec,
        out_shape=jax.ShapeDtypeStruct((batch, 512, seq), BF16),
        scratch_shapes=[pltpu.VMEM((LANES, N_HEADS * TQ_DSA), BF16),
                        pltpu.VMEM((LANES, N_HEADS * TQ_DSA), BF16),
                        pltpu.VMEM((seq // TK_DSA, TK_DSA, TQ_DSA), F32),
                        pltpu.VMEM((seq // TK_DSA, TK_DSA, TQ_DSA), F32),
                        pltpu.VMEM((seq // RADIX_BLOCK, 32, SUBLANES, TQ_DSA), I32),
                        pltpu.VMEM((seq // RADIX_BLOCK, SUBLANES, TQ_DSA), I32),
                        pltpu.VMEM((1, TQ_DSA), I32)] + _softmax_scratch(TQ_DSA),
        compiler_params=_cparams(("arbitrary", "arbitrary")),
        name="dsa",
    )(qat, qit, aux_t, aux_t, ki3, ka3, vat4)


def _outproj_kernel(oa_ref, ob_ref, sga_ref, sgb_ref, x_ref, mod_ref, gffn_ref,
                    woa_ref, wob_ref, wo_ref, wr_hi_ref, wr_lo_ref, br_ref,
                    x1_ref, h2_ref, lg_ref):
    tm = TM_OUT
    tn_dot = lambda a, b: lax.dot_general(a, b, (((0,), (0,)), ((), ())), preferred_element_type=F32)
    ya = tn_dot(oa_ref[0], woa_ref[...])
    yb = tn_dot(ob_ref[0], wob_ref[...])
    merged = (sga_ref[...].astype(F32) * ya + sgb_ref[...].astype(F32) * yb).astype(BF16)
    mix = _dot(merged, wo_ref[...])
    gt1 = mod_ref[0, 2:3, :]
    sh2 = mod_ref[0, 3:4, :]
    sc2 = mod_ref[0, 4:5, :]
    x1 = x_ref[...] + gt1 * mix
    x1_ref[...] = x1
    var = jnp.mean(x1 * x1, axis=-1, keepdims=True)
    h2 = x1 * lax.rsqrt(var + RMS_EPS) * gffn_ref[...] * (1.0 + sc2) + sh2
    for c in range(D_MODEL // LANES):
        h2_ref[pl.ds(c, tm, stride=SUBLANES), :] = h2[:, c * LANES:(c + 1) * LANES]
    hi = h2.astype(BF16)
    lo = (h2 - hi.astype(F32)).astype(BF16)
    lg_ref[...] = (_dot(hi, wr_hi_ref[...]) + _dot(hi, wr_lo_ref[...]) + _dot(lo, wr_hi_ref[...])
                   + br_ref[...])


def _outproj_call(oa2, ob2, sga, sgb, x2, mod3, g_ffn, woa, wob, wo, wr_hi, wr_lo, br, seq):
    n = x2.shape[0]
    tpb = seq // TM_OUT
    row = lambda w: pl.BlockSpec((TM_OUT, w), lambda i: (i, 0))
    full = lambda a: pl.BlockSpec(a.shape, lambda i: (0,) * a.ndim)
    return pl.pallas_call(
        _outproj_kernel,
        grid=(n // TM_OUT,),
        in_specs=[pl.BlockSpec((1, 512, TM_OUT), lambda i: (i // tpb, 0, i % tpb)),
                  pl.BlockSpec((1, 512, TM_OUT), lambda i: (i // tpb, 0, i % tpb)),
                  row(1024), row(1024), row(D_MODEL),
                  pl.BlockSpec((1, 6, D_MODEL), lambda i: (i // tpb, 0, 0)),
                  full(g_ffn), full(woa), full(wob), full(wo), full(wr_hi), full(wr_lo), full(br)],
        out_specs=[row(D_MODEL),
                   pl.BlockSpec((TM_OUT * SUBLANES, LANES), lambda i: (i, 0)),
                   row(LANES)],
        out_shape=[jax.ShapeDtypeStruct((n, D_MODEL), F32),
                   jax.ShapeDtypeStruct((n * SUBLANES, LANES), F32),
                   jax.ShapeDtypeStruct((n, LANES), F32)],
        compiler_params=_cparams(("arbitrary",)),
        name="outproj",
    )(oa2, ob2, sga, sgb, x2, mod3, g_ffn, woa, wob, wo, wr_hi, wr_lo, br)


def _route_rows(lg):
    tm = lg.shape[0]
    lane = lax.broadcasted_iota(I32, (tm, LANES), 1)
    big = jnp.int32(LANES)
    ninf = -jnp.inf
    gmask = (lane >= N_EXPERTS) & (lane < N_EXPERTS + N_GROUPS)
    g = jnp.where(gmask, lg, ninf)
    gmax = jnp.max(g, axis=-1, keepdims=True)
    grp = jnp.min(jnp.where(g == gmax, lane - N_EXPERTS, big), axis=-1, keepdims=True)
    p_grp = 1.0 / jnp.sum(jnp.where(gmask, jnp.exp(lg - gmax), 0.0), axis=-1, keepdims=True)
    lo = grp * EXPERTS_PER_GROUP
    emask = (lane >= lo) & (lane < lo + EXPERTS_PER_GROUP)
    ev = jnp.where(emask, lg, ninf)
    v0 = jnp.max(ev, axis=-1, keepdims=True)
    i0 = jnp.min(jnp.where(emask & (ev == v0), lane, big), axis=-1, keepdims=True)
    rest = emask & (lane != i0)
    ev1 = jnp.where(rest, lg, ninf)
    v1 = jnp.max(ev1, axis=-1, keepdims=True)
    i1 = jnp.min(jnp.where(rest & (ev1 == v1), lane, big), axis=-1, keepdims=True)
    e1 = jnp.exp(v1 - v0)
    w0 = p_grp / (1.0 + e1)
    w1 = p_grp * e1 / (1.0 + e1)
    return jnp.where(lane == 0, i0.astype(F32),
                     jnp.where(lane == 1, i1.astype(F32),
                               jnp.where(lane == 2, w0, jnp.where(lane == 3, w1, 0.0))))


def _expert_onehots(rt, tm):
    lane = lax.broadcasted_iota(I32, (tm, LANES), 1)
    lane_f = lane.astype(F32)
    e0 = jnp.sum(jnp.where(lane == 0, rt, 0.0), axis=-1, keepdims=True)
    e1 = jnp.sum(jnp.where(lane == 1, rt, 0.0), axis=-1, keepdims=True)
    return lane, lane_f == e0, lane_f == e1


def _rank_rows(rt, seen_before):
    tm = rt.shape[0]
    lane, is0, is1 = _expert_onehots(rt, tm)
    hits = jnp.where(is0, 1.0, 0.0) + jnp.where(is1, 1.0, 0.0)
    rr = lax.broadcasted_iota(I32, (tm, tm), 0)
    cc = lax.broadcasted_iota(I32, (tm, tm), 1)
    before = jnp.where(cc < rr, 1.0, 0.0).astype(BF16)
    seen = _dot(before, hits.astype(BF16)) + seen_before
    r0 = jnp.sum(jnp.where(is0, seen, 0.0), axis=-1, keepdims=True)
    r1 = jnp.sum(jnp.where(is1, seen, 0.0), axis=-1, keepdims=True)
    rk = jnp.where(lane == 0, r0, jnp.where(lane == 1, r1, 0.0))
    return rk, seen_before + jnp.sum(hits, axis=0, keepdims=True)


def _route_rank_kernel(lg_ref, rt_ref, rk_ref, cnt_ref, seen_scr):
    @pl.when(pl.program_id(0) == 0)
    def _():
        seen_scr[...] = jnp.zeros_like(seen_scr)

    rt = _route_rows(lg_ref[...])
    rt_ref[...] = rt
    rk, seen = _rank_rows(rt, seen_scr[...])
    rk_ref[...] = rk
    seen_scr[...] = seen
    cnt_ref[...] = jnp.broadcast_to(seen, cnt_ref.shape)


def _route_rank_call(lg):
    n = lg.shape[0]
    spec = pl.BlockSpec((TM_ROUTE, LANES), lambda i: (i, 0))
    return pl.pallas_call(
        _route_rank_kernel, grid=(n // TM_ROUTE,), in_specs=[spec],
        out_specs=[spec, spec, pl.BlockSpec((SUBLANES, LANES), lambda i: (0, 0))],
        out_shape=[jax.ShapeDtypeStruct((n, LANES), F32), jax.ShapeDtypeStruct((n, LANES), F32),
                   jax.ShapeDtypeStruct((SUBLANES, LANES), F32)],
        scratch_shapes=[pltpu.VMEM((1, LANES), F32)],
        compiler_params=_cparams(("arbitrary",)), name="route_rank",
    )(lg)


def _dest_kernel(rt_ref, rk_ref, cnt_ref, dd_ref, be_ref, nu_ref, end_ref):
    tm = TM_ROUTE
    lane, is0, is1 = _expert_onehots(rt_ref[...], tm)
    blocks = jnp.floor((cnt_ref[...] + (BLK_E - 1)) * (1.0 / BLK_E))
    er = lax.broadcasted_iota(I32, (LANES, LANES), 0)
    ec = lax.broadcasted_iota(I32, (LANES, LANES), 1)
    upto = jnp.where(er <= ec, 1.0, 0.0).astype(BF16)
    bend = _dot(blocks.astype(BF16), upto)
    pstart = (bend[0:1, :] - blocks[0:1, :]) * BLK_E
    rk = rk_ref[...]
    r0 = jnp.sum(jnp.where(lane == 0, rk, 0.0), axis=-1, keepdims=True)
    r1 = jnp.sum(jnp.where(lane == 1, rk, 0.0), axis=-1, keepdims=True)
    d0 = jnp.sum(jnp.where(is0, pstart, 0.0), axis=-1, keepdims=True) + r0
    d1 = jnp.sum(jnp.where(is1, pstart, 0.0), axis=-1, keepdims=True) + r1
    dd_ref[...] = jnp.where(lane == 0, d0, jnp.where(lane == 1, d1, 0.0)).astype(I32)
    nb = be_ref.shape[0]
    blk = lax.broadcasted_iota(I32, (nb, LANES), 0).astype(F32)
    lane_b = lax.broadcasted_iota(I32, (nb, LANES), 1)
    ended = jnp.where((lane_b < N_EXPERTS) & (bend[0:1, :] <= blk), 1.0, 0.0)
    be = jnp.minimum(jnp.sum(ended, axis=-1, keepdims=True), float(N_EXPERTS - 1))
    be_ref[...] = jnp.broadcast_to(be, be_ref.shape).astype(I32)
    lane_c = lax.broadcasted_iota(I32, (SUBLANES, LANES), 1)
    used = jnp.sum(jnp.where(lane_c == N_EXPERTS - 1, bend, 0.0), axis=-1, keepdims=True)
    nu_ref[...] = jnp.broadcast_to(used, nu_ref.shape).astype(I32)
    end_ref[...] = (bend * BLK_E).astype(I32)


def _dest_call(rt, rk, cnt, n_blocks):
    n = rt.shape[0]
    nb_pad = -(-n_blocks // SUBLANES) * SUBLANES
    spec = pl.BlockSpec((TM_ROUTE, LANES), lambda i: (i, 0))
    const = lambda rows: pl.BlockSpec((rows, LANES), lambda i: (0, 0))
    return pl.pallas_call(
        _dest_kernel, grid=(n // TM_ROUTE,), in_specs=[spec, spec, const(SUBLANES)],
        out_specs=[spec, const(nb_pad), const(SUBLANES), const(SUBLANES)],
        out_shape=[jax.ShapeDtypeStruct((n, LANES), I32), jax.ShapeDtypeStruct((nb_pad, LANES), I32),
                   jax.ShapeDtypeStruct((SUBLANES, LANES), I32), jax.ShapeDtypeStruct((SUBLANES, LANES), I32)],
        compiler_params=_cparams(("arbitrary",)), name="dest",
    )(rt, rk, cnt)


def _slab_copy(src, src_row, dst, dst_row, sem):
    return pltpu.make_async_copy(src.at[pl.ds(pl.multiple_of(src_row * SUBLANES, SUBLANES), SUBLANES)],
                                 dst.at[pl.ds(pl.multiple_of(dst_row * SUBLANES, SUBLANES), SUBLANES)], sem)


def _dispatch_kernel(dest_ref, seg_end_ref, h2_ref, xs_hbm, zero_scr, sem, zsem):
    tm = TM_DISPATCH
    base = pl.program_id(0) * tm

    @pl.when(pl.program_id(0) == 0)
    def _():
        zero_scr[...] = jnp.zeros_like(zero_scr)
        clears = [pltpu.make_async_copy(
            zero_scr, xs_hbm.at[pl.ds(pl.multiple_of(jnp.maximum(seg_end_ref[e] - BLK_E, 0) * SUBLANES, SUBLANES),
                                      BLK_E * SUBLANES)], zsem) for e in range(N_EXPERTS)]
        for c in clears:
            c.start()
        for c in clears:
            c.wait()
        n_blocks = xs_hbm.shape[0] // (BLK_E * SUBLANES)
        first_free = lax.shift_right_logical(seg_end_ref[N_EXPERTS - 1], int(np.log2(BLK_E)))
        def tail_clear(b):
            row = pl.multiple_of((first_free + b) * (BLK_E * SUBLANES), BLK_E * SUBLANES)
            return pltpu.make_async_copy(zero_scr, xs_hbm.at[pl.ds(row, BLK_E * SUBLANES)], zsem)

        for b in range(N_EXPERTS):
            pl.when(first_free + b < n_blocks)(lambda b=b: tail_clear(b).start())
        for b in range(N_EXPERTS):
            pl.when(first_free + b < n_blocks)(lambda b=b: tail_clear(b).wait())

    def issue(r, _):
        for k in range(2):
            _slab_copy(h2_ref, r, xs_hbm, dest_ref[2 * (base + r) + k], sem).start(priority=k)
        return 0

    lax.fori_loop(0, tm, issue, 0, unroll=GATHER_UNROLL)
    for _ in range(2):
        pltpu.make_async_copy(h2_ref, xs_hbm.at[pl.ds(0, tm * SUBLANES)], sem).wait()


def _dispatch_call(dest, seg_end, h2s, n_slots):
    n = h2s.shape[0] // SUBLANES
    return pl.pallas_call(
        _dispatch_kernel,
        grid_spec=pltpu.PrefetchScalarGridSpec(
            num_scalar_prefetch=2, grid=(n // TM_DISPATCH,),
            in_specs=[pl.BlockSpec((TM_DISPATCH * SUBLANES, LANES), lambda i, d, e: (i, 0))],
            out_specs=pl.BlockSpec(memory_space=pl.ANY),
            scratch_shapes=[pltpu.VMEM((BLK_E * SUBLANES, LANES), F32),
                            pltpu.SemaphoreType.DMA(()), pltpu.SemaphoreType.DMA(())]),
        out_shape=jax.ShapeDtypeStruct((n_slots * SUBLANES, LANES), F32),
        compiler_params=_cparams(("arbitrary",)),
        name="dispatch",
    )(dest, seg_end, h2s)


def _experts_kernel(be_ref, nused_ref, xs_ref, wg_ref, wu_ref, wd_ref, yb_ref, x_scr, wg_scr, wu_scr, wd_scr):
    i = pl.program_id(0)

    @pl.when(i >= nused_ref[0])
    def _():
        yb_ref[...] = jnp.zeros_like(yb_ref)

    @pl.when(i < nused_ref[0])
    def _():
        prev = be_ref[jnp.maximum(i - 1, 0)]

        @pl.when((i == 0) | (be_ref[i] != prev))
        def _():
            wg_scr[...] = wg_ref[0].astype(BF16)
            wu_scr[...] = wu_ref[0].astype(BF16)
            wd_scr[...] = wd_ref[0].astype(BF16)

        for c in range(D_MODEL // LANES):
            x_scr[:, c * LANES:(c + 1) * LANES] = xs_ref[pl.ds(c, BLK_E, stride=SUBLANES), :].astype(BF16)
        xb = x_scr[...]
        g = _dot(xb, wg_scr[...])
        u = _dot(xb, wu_scr[...])
        hid = (g * _sigmoid(g) * u).astype(BF16)
        y = _dot(hid, wd_scr[...])
        for c in range(D_MODEL // LANES):
            yb_ref[pl.ds(c, BLK_E, stride=SUBLANES), :] = y[:, c * LANES:(c + 1) * LANES]


def _experts_call(block_e, n_used, xs, w_gate, w_up, w_down):
    n_blocks = block_e.shape[0]
    slab = pl.BlockSpec((BLK_E * SUBLANES, LANES), lambda i, be, nu: (i, 0))
    used_slab = pl.BlockSpec((BLK_E * SUBLANES, LANES), lambda i, be, nu: (jnp.minimum(i, nu[0] - 1), 0))
    return pl.pallas_call(
        _experts_kernel,
        grid_spec=pltpu.PrefetchScalarGridSpec(
            num_scalar_prefetch=2, grid=(n_blocks,),
            in_specs=[used_slab,
                      pl.BlockSpec((1, D_MODEL, D_EXPERT), lambda i, be, nu: (be[i], 0, 0)),
                      pl.BlockSpec((1, D_MODEL, D_EXPERT), lambda i, be, nu: (be[i], 0, 0)),
                      pl.BlockSpec((1, D_EXPERT, D_MODEL), lambda i, be, nu: (be[i], 0, 0))],
            out_specs=slab,
            scratch_shapes=[pltpu.VMEM((BLK_E, D_MODEL), BF16),
                            pltpu.VMEM((D_MODEL, D_EXPERT), BF16),
                            pltpu.VMEM((D_MODEL, D_EXPERT), BF16),
                            pltpu.VMEM((D_EXPERT, D_MODEL), BF16)]),
        out_shape=jax.ShapeDtypeStruct(xs.shape, F32),
        compiler_params=_cparams(("arbitrary",)),
        name="experts",
    )(block_e, n_used, xs, w_gate, w_up, w_down)


def _final_kernel(dest_ref, x1_ref, rt_ref, mod_ref, gfin_ref, yb_hbm, o_ref, g_scr, sem):
    tm = TM_FINAL
    step = pl.program_id(0)
    slot = step % 2
    slot_rows = 2 * tm * SUBLANES

    def gather(s, into):
        def issue(r, _):
            tok = s * tm + r
            for k in range(2):
                _slab_copy(yb_hbm, dest_ref[2 * tok + k], g_scr.at[into], k * tm + r,
                           sem.at[into]).start(priority=k)
            return 0
        lax.fori_loop(0, tm, issue, 0, unroll=GATHER_UNROLL)

    @pl.when(step == 0)
    def _():
        gather(0, 0)

    @pl.when(step + 1 < pl.num_programs(0))
    def _():
        gather(step + 1, 1 - slot)

    g_now = g_scr.at[slot]
    pltpu.make_async_copy(yb_hbm.at[pl.ds(0, slot_rows)], g_now, sem.at[slot]).wait()

    rt = rt_ref[...]
    lane = lax.broadcasted_iota(I32, (tm, LANES), 1)
    gw0 = jnp.sum(jnp.where(lane == 2, rt, 0.0), axis=-1, keepdims=True)
    gw1 = jnp.sum(jnp.where(lane == 3, rt, 0.0), axis=-1, keepdims=True)
    gt2 = mod_ref[0, 5:6, :]
    x1 = x1_ref[...]
    cols = []
    for c in range(D_MODEL // LANES):
        y0 = g_now[pl.ds(c, tm, stride=SUBLANES), :]
        y1 = g_now[pl.ds(tm * SUBLANES + c, tm, stride=SUBLANES), :]
        y = gw0 * y0 + gw1 * y1
        cols.append(x1[:, c * LANES:(c + 1) * LANES] + gt2[:, c * LANES:(c + 1) * LANES] * y)
    x2 = jnp.concatenate(cols, axis=1)
    var = jnp.mean(x2 * x2, axis=-1, keepdims=True)
    o_ref[...] = x2 * lax.rsqrt(var + RMS_EPS) * gfin_ref[...]


def _final_call(dest, x1, rt, mod3, g_final, yb, seq):
    n = x1.shape[0]
    tpb = seq // TM_FINAL
    return pl.pallas_call(
        _final_kernel,
        grid_spec=pltpu.PrefetchScalarGridSpec(
            num_scalar_prefetch=1, grid=(n // TM_FINAL,),
            in_specs=[pl.BlockSpec((TM_FINAL, D_MODEL), lambda i, d: (i, 0)),
                      pl.BlockSpec((TM_FINAL, LANES), lambda i, d: (i, 0)),
                      pl.BlockSpec((1, 6, D_MODEL), lambda i, d: (i // tpb, 0, 0)),
                      pl.BlockSpec((1, D_MODEL), lambda i, d: (0, 0)),
                      pl.BlockSpec(memory_space=pl.ANY)],
            out_specs=pl.BlockSpec((TM_FINAL, D_MODEL), lambda i, d: (i, 0)),
            scratch_shapes=[pltpu.VMEM((2, 2 * TM_FINAL * SUBLANES, LANES), F32),
                            pltpu.SemaphoreType.DMA((2,))]),
        out_shape=jax.ShapeDtypeStruct((n, D_MODEL), F32),
        compiler_params=_cparams(("arbitrary",)),
        name="final",
    )(dest, x1, rt, mod3, g_final, yb)


def _permute_w_in(w):
    o = np.cumsum([0, 512, 64, 64, 512, 64, 8, 512, 512, 512, 8, 1024, 1024])
    qa, ka, va, qi, ki, wi, qb, kb, vb, fb, ga, gb = [w[:, o[k]:o[k + 1]] for k in range(12)]
    aux = jnp.concatenate([wi, fb, jnp.zeros((w.shape[0], LANES - 2 * N_HEADS), w.dtype)], axis=1)
    w_rows = jnp.concatenate([kb, ga, gb, ka, ka, ki, ki, aux], axis=1).astype(BF16)
    w_t = jnp.concatenate([qa, qi, qb, vb, va], axis=1).T.astype(BF16)
    return w_rows, w_t


def _layer(x2, pos2, mod3, batch, seq, g_mix, w_in, b_forget, w_out_a, w_out_b, w_out, g_ffn,
           w_group, b_group, w_router, b_router, w_e_gate, w_e_up, w_e_down, g_final):
    n = x2.shape[0]
    inv_freq = ROPE_THETA ** (-jnp.arange(0, ROPE_DIM, 2, dtype=F32) / ROPE_DIM)
    jj = np.arange(LANES) % HEAD_DIM
    invf = jnp.where(jj < ROPE_DIM, inv_freq[jj % (ROPE_DIM // 2)], 0.0)[None, :].astype(F32)
    fbias = jnp.zeros((1, LANES), F32).at[0, N_HEADS:2 * N_HEADS].set(b_forget.astype(F32))

    invf8 = jnp.broadcast_to(inv_freq[:, None], (ROPE_DIM // 2, LANES)).astype(F32)
    w_rows, w_t = _permute_w_in(w_in)
    (qat, qit, qbt, kaug, vbt, sga, sgb, ka2, vat, ki2, aux_t) = _inproj_call(
        x2, pos2, pos2.reshape(batch, 1, seq), mod3, g_mix.reshape(1, D_MODEL), w_rows, w_t, invf, invf8, fbias,
        batch, seq)

    r3 = lambda a: a.reshape(batch, seq, a.shape[-1])
    obt = _fox_call(qbt, aux_t, r3(kaug), vbt)
    oat = _dsa_call(qat, qit, aux_t, r3(ki2), r3(ka2), vat)

    w_rt = jnp.concatenate([w_router, w_group, jnp.zeros((D_MODEL, LANES - N_EXPERTS - N_GROUPS), F32)], axis=1)
    wr_hi = w_rt.astype(BF16)
    wr_lo = (w_rt - wr_hi.astype(F32)).astype(BF16)
    br = jnp.concatenate([b_router, b_group, jnp.zeros((LANES - N_EXPERTS - N_GROUPS,), F32)])[None, :]
    x1, h2s, lg = _outproj_call(oat, obt, sga, sgb, x2, mod3,
                                g_ffn.reshape(1, D_MODEL), w_out_a.astype(BF16), w_out_b.astype(BF16),
                                w_out.astype(BF16), wr_hi, wr_lo, br, seq)
    rt, rk, cnt = _route_rank_call(lg)

    n_slots = 2 * n + N_EXPERTS * BLK_E
    n_blocks = n_slots // BLK_E
    dd, be, nu, seg_end = _dest_call(rt, rk, cnt, n_blocks)
    dest = dd[:, 0:2].reshape(-1)
    block_e = be[:n_blocks, 0]
    n_used = nu[0, 0:1]

    xs = _dispatch_call(dest, seg_end[0, :N_EXPERTS], h2s, n_slots)
    yb = _experts_call(block_e, n_used, xs, w_e_gate, w_e_up, w_e_down)
    return _final_call(dest, x1, rt, mod3, g_final.reshape(1, D_MODEL), yb, seq)


def kernel(x, c, positions, w_mod, b_mod, g_mix, w_in, b_forget, w_out_a, w_out_b, w_out, g_ffn, w_group,
           b_group, w_router, b_router, w_e_gate, w_e_up, w_e_down, g_final):
    batch, seq, d = x.shape
    depth = w_mod.shape[0]
    assert depth == 1 and d == D_MODEL, "kernel fuses the final norm into the single layer"
    n = batch * seq
    c8 = jnp.zeros((8, d), F32).at[:batch].set(c)
    mod = _mod_call(c8, w_mod[0], b_mod[0][None, :])
    mod3 = mod[:batch].reshape(batch, 6, d)
    out = _layer(x.reshape(n, d), positions.reshape(n, 1), mod3, batch, seq, g_mix[0], w_in[0], b_forget[0],
                 w_out_a[0], w_out_b[0], w_out[0], g_ffn[0], w_group[0], b_group[0], w_router[0], b_router[0],
                 w_e_gate[0], w_e_up[0], w_e_down[0], g_final)
    return out.reshape(batch, seq, d)
```

```python
import functools

import numpy as np
import jax
import jax.numpy as jnp
from jax import lax
from jax.experimental import pallas as pl
from jax.experimental.pallas import tpu as pltpu

F32 = jnp.float32
BF16 = jnp.bfloat16
I32 = jnp.int32

D_MODEL = 1024
HEAD_DIM = 64
N_HEADS = 8
CHUNK_SHIFT = 6
TOPK = 256
ROPE_DIM = 16
ROPE_THETA = 500000.0
N_GROUPS = 4
EXPERTS_PER_GROUP = 8
N_EXPERTS = 32
D_EXPERT = 512
RMS_EPS = 1e-6
NEG_INF = -1e30
ATTN_SCALE = HEAD_DIM ** -0.5
IDX_SCALE = HEAD_DIM ** -0.5
LOG2E = 1.4426950408889634

LANES = 128
SUBLANES = 8
VMEM_LIMIT = 56 * 1024 * 1024

TM_IN = 512
TQ_FOX = 512
TK_FOX = 512
TQ_DSA = 512
TK_DSA = 512
TM_OUT = 512
TM_ROUTE = 1024
TM_DISPATCH = 1024
BLK_E = 256
TM_FINAL = 512
GATHER_UNROLL = 8

RADIX_BLOCK = 32 * SUBLANES
BOUND_MARGIN = 1.02
BOUND_SAFE = 40.0
FOX_BIAS_SLACK = 1.0
FOX_K = 256
FOX_AUG_ROWS = 16

_SEG_KB, _SEG_GA, _SEG_GB, _SEG_KA, _SEG_KI, _SEG_AUX = 0, 512, 1536, 2560, 2688, 2816
_NC_IN = 2944
_ROW_QA, _ROW_QI, _ROW_QB, _ROW_VB, _ROW_VA = 0, 512, 1024, 1536, 2048
_NR_IN = 2048 + HEAD_DIM

_SENT_KEY = int(np.array(NEG_INF, np.float32).view(np.int32) ^ 0x7FFFFFFF)
_INT_MIN = -(2 ** 31)


def _cparams(sem):
    return pltpu.CompilerParams(dimension_semantics=sem, vmem_limit_bytes=VMEM_LIMIT)


def _dot(a, b):
    return jnp.dot(a, b, preferred_element_type=F32)


def _sigmoid(x):
    return 1.0 / (1.0 + jnp.exp(-x))


def _mod_kernel(c_ref, w_ref, b_ref, o_ref):
    c = c_ref[...]
    ca = c * _sigmoid(c)
    o_ref[...] = _dot(ca.astype(BF16), w_ref[...].astype(BF16)) + b_ref[...]


def _mod_call(c8, w_mod, b_mod):
    n_out = w_mod.shape[1]
    tn = 1024
    return pl.pallas_call(
        _mod_kernel,
        grid=(n_out // tn,),
        in_specs=[pl.BlockSpec((8, D_MODEL), lambda j: (0, 0)),
                  pl.BlockSpec((D_MODEL, tn), lambda j: (0, j)),
                  pl.BlockSpec((1, tn), lambda j: (0, j))],
        out_specs=pl.BlockSpec((8, tn), lambda j: (0, j)),
        out_shape=jax.ShapeDtypeStruct((8, n_out), F32),
        compiler_params=_cparams(("arbitrary",)),
        name="mod",
    )(c8, w_mod, b_mod)


def _inproj_kernel(x_ref, pos_ref, posr_ref, mod_ref, g_ref, w_ref, wt_ref, invf_ref, invf8_ref, fbias_ref,
                   sel_ref, ones_ref,
                   qat_ref, qit_ref, qbt_ref, kb_ref, vbt_ref, sga_ref, sgb_ref,
                   ka_ref, vat_ref, ki_ref, auxt_ref, h_scr, carry_scr):
    tm = TM_IN
    x = x_ref[...]
    var = jnp.mean(x * x, axis=-1, keepdims=True)
    tf = x * lax.rsqrt(var + RMS_EPS) * g_ref[...]
    sh = mod_ref[0, 0:1, :]
    sc = mod_ref[0, 1:2, :]
    h_scr[...] = (tf * (1.0 + sc) + sh).astype(BF16)
    hb = h_scr[...]

    lane = lax.broadcasted_iota(I32, (tm, LANES), 1)
    j = lane & (HEAD_DIM - 1)
    ang = pos_ref[...].astype(F32) * invf_ref[...]
    cs = jnp.cos(ang)
    sn = jnp.sin(ang)
    coef_next = jnp.where(j < ROPE_DIM // 2, -sn, 0.0)
    coef_prev = jnp.where((j >= ROPE_DIM // 2) & (j < ROPE_DIM), sn, 0.0)

    def rope(tc):
        return (tc * cs + pltpu.roll(tc, LANES - ROPE_DIM // 2, 1) * coef_next
                + pltpu.roll(tc, ROPE_DIM // 2, 1) * coef_prev)

    def seg(off, width):
        return _dot(hb, w_ref[:, off:off + width])

    def seg_t(row0, rows):
        return lax.dot_general(wt_ref[row0:row0 + rows, :], hb, (((1,), (1,)), ((), ())),
                               preferred_element_type=F32)

    ang8 = invf8_ref[:, 0:1] * posr_ref[0].astype(F32)
    cs8 = jnp.cos(ang8)
    sn8 = jnp.sin(ang8)
    half = ROPE_DIM // 2

    def rope_t(t, scale, out_ref):
        for h in range(N_HEADS):
            r0 = h * HEAD_DIM
            lo = t[r0:r0 + half, :]
            hi = t[r0 + half:r0 + ROPE_DIM, :]
            head = jnp.concatenate([lo * cs8 - hi * sn8, hi * cs8 + lo * sn8, t[r0 + ROPE_DIM:r0 + HEAD_DIM, :]],
                                   axis=0)
            out_ref[0, r0:r0 + HEAD_DIM, :] = (head * scale).astype(BF16)

    rope_t(seg_t(_ROW_QA, 512), ATTN_SCALE * LOG2E, qat_ref)
    rope_t(seg_t(_ROW_QI, 512), IDX_SCALE, qit_ref)
    qbt_ref[0] = (seg_t(_ROW_QB, 512) * (ATTN_SCALE * LOG2E)).astype(BF16)
    vbt_ref[0, 0] = seg_t(_ROW_VB, 512).astype(BF16)
    vat_ref[0, 0] = seg_t(_ROW_VA, HEAD_DIM).astype(BF16)
    sga_ref[...] = _sigmoid(seg(_SEG_GA, 1024)).astype(BF16)
    sgb_ref[...] = _sigmoid(seg(_SEG_GB, 1024)).astype(BF16)
    ka = rope(seg(_SEG_KA, LANES))
    ka_ref[...] = ka.astype(BF16)
    ki_ref[...] = rope(seg(_SEG_KI, LANES)).astype(BF16)

    z = seg(_SEG_AUX, LANES)
    zf = z + fbias_ref[...]
    logf = jnp.minimum(zf, 0.0) - jnp.log(1.0 + jnp.exp(-jnp.abs(zf)))
    is_f = (lane >= N_HEADS) & (lane < 2 * N_HEADS)
    logf = jnp.where(is_f, logf, 0.0)

    def split3(v):
        hi = v.astype(BF16)
        r1 = v - hi.astype(F32)
        mid = r1.astype(BF16)
        return hi, mid, (r1 - mid.astype(F32)).astype(BF16)

    rr = lax.broadcasted_iota(I32, (tm, tm), 0)
    cc = lax.broadcasted_iota(I32, (tm, tm), 1)
    tri = jnp.where(cc <= rr, 1.0, 0.0).astype(BF16)
    p_hi, p_mid, p_lo = split3(logf)

    @pl.when(pl.program_id(1) == 0)
    def _():
        carry_scr[...] = jnp.zeros_like(carry_scr)

    parts = _dot(tri, jnp.concatenate([p_hi, p_mid, p_lo], axis=1))
    cum = parts[:, 0:LANES] + parts[:, LANES:2 * LANES] + parts[:, 2 * LANES:3 * LANES] + carry_scr[...]
    carry_scr[...] = cum[tm - 1:tm, :]
    cum2 = cum * LOG2E
    aux = jnp.where(lane < N_HEADS, z * (N_HEADS ** -0.5), jnp.where(is_f, cum2, 0.0))

    kb = seg(_SEG_KB, 512)
    first = lane < HEAD_DIM

    def half_norms(blk):
        sq = blk * blk
        return (jnp.sum(jnp.where(first, sq, 0.0), axis=-1, keepdims=True),
                jnp.sum(jnp.where(first, 0.0, sq), axis=-1, keepdims=True))

    for p in range(N_HEADS // 2):
        n_even, n_odd = half_norms(kb[:, p * LANES:(p + 1) * LANES])
        aux = jnp.where(lane == 2 * N_HEADS + 2 * p, n_even, jnp.where(lane == 2 * N_HEADS + 2 * p + 1, n_odd, aux))
    aux = jnp.where(lane == 3 * N_HEADS, half_norms(ka)[0], aux)
    auxt_ref[0] = aux.T[0:4 * N_HEADS, :]

    c_hi, c_mid, c_lo = split3(cum2)
    kaug = _dot(jnp.concatenate([c_hi, c_mid, c_lo], axis=1), sel_ref[...]) + ones_ref[...]
    for p in range(N_HEADS // 2):
        kb_ref[:, p * FOX_K:p * FOX_K + LANES] = kb[:, p * LANES:(p + 1) * LANES].astype(BF16)
        kb_ref[:, p * FOX_K + LANES:(p + 1) * FOX_K] = kaug[:, p * LANES:(p + 1) * LANES].astype(BF16)


def _fox_routing_constants():
    sel = np.zeros((3, LANES, (N_HEADS // 2) * LANES), np.float32)
    ones = np.zeros((1, (N_HEADS // 2) * LANES), np.float32)
    for h in range(N_HEADS):
        for t in range(3):
            sel[t, N_HEADS + h, (h // 2) * LANES + 3 + 3 * (h % 2) + t] = -1.0
    for p in range(N_HEADS // 2):
        ones[0, p * LANES:p * LANES + 3] = 1.0
    return jnp.asarray(sel.reshape(3 * LANES, -1), BF16), jnp.asarray(ones, F32)


def _inproj_call(x2, pos2, pos_rows, mod3, g_mix, w_rows, w_t, invf, invf8, fbias, batch, seq):
    assert TM_IN == TK_FOX == TK_DSA, "value tiles are written per in-projection step"
    n = x2.shape[0]
    tpb = seq // TM_IN
    row = lambda b, s: (b * tpb + s, 0)
    rows = lambda w: (jax.ShapeDtypeStruct((n, w), BF16), pl.BlockSpec((TM_IN, w), row))
    chan = lambda c: (jax.ShapeDtypeStruct((batch, c, seq), BF16), pl.BlockSpec((1, c, TM_IN), lambda b, s: (b, 0, s)))
    tile = lambda c: (jax.ShapeDtypeStruct((batch, tpb, c, TM_IN), BF16),
                      pl.BlockSpec((1, 1, c, TM_IN), lambda b, s: (b, s, 0, 0)))
    aux_t = (jax.ShapeDtypeStruct((batch, 4 * N_HEADS, seq), F32),
             pl.BlockSpec((1, 4 * N_HEADS, TM_IN), lambda b, s: (b, 0, s)))
    outs = [chan(512), chan(512), chan(512), rows((N_HEADS // 2) * FOX_K), tile(512), rows(1024), rows(1024),
            rows(LANES), tile(HEAD_DIM), rows(LANES), aux_t]
    sel, ones = _fox_routing_constants()
    const = lambda a: pl.BlockSpec(a.shape, lambda b, s: (0,) * a.ndim)
    return pl.pallas_call(
        _inproj_kernel,
        grid=(batch, tpb),
        in_specs=[pl.BlockSpec((TM_IN, D_MODEL), row),
                  pl.BlockSpec((TM_IN, 1), row),
                  pl.BlockSpec((1, 1, TM_IN), lambda b, s: (b, 0, s)),
                  pl.BlockSpec((1, 6, D_MODEL), lambda b, s: (b, 0, 0)),
                  const(g_mix), const(w_rows), const(w_t), const(invf), const(invf8), const(fbias),
                  const(sel), const(ones)],
        out_specs=[o[1] for o in outs],
        out_shape=[o[0] for o in outs],
        scratch_shapes=[pltpu.VMEM((TM_IN, D_MODEL), BF16), pltpu.VMEM((1, LANES), F32)],
        compiler_params=_cparams(("arbitrary", "arbitrary")),
        name="inproj",
    )(x2, pos2, pos_rows, mod3, g_mix, w_rows, w_t, invf, invf8, fbias, sel, ones)


def _softmax_max(s, m_ref):
    m_ref[...] = jnp.maximum(m_ref[...], jnp.max(s, axis=0, keepdims=True))


def _softmax_accumulate(s, m_ref, l_ref, acc_ref, v_t):
    p = jnp.exp2(s - m_ref[...])
    l_ref[...] = l_ref[...] + jnp.sum(p, axis=0, keepdims=True)
    acc_ref[...] = acc_ref[...] + _dot(v_t, p.astype(BF16))


def _softmax_scratch(tq):
    return ([pltpu.VMEM((1, tq), F32) for _ in range(2 * N_HEADS)]
            + [pltpu.VMEM((HEAD_DIM, tq), F32) for _ in range(N_HEADS)])


def _softmax_split(refs):
    return refs[:N_HEADS], refs[N_HEADS:2 * N_HEADS], refs[2 * N_HEADS:3 * N_HEADS]


def _softmax_init(m_refs, l_refs, acc_refs):
    for h in range(N_HEADS):
        m_refs[h][...] = jnp.full(m_refs[h].shape, NEG_INF, F32)
        l_refs[h][...] = jnp.zeros(l_refs[h].shape, F32)
        acc_refs[h][...] = jnp.zeros(acc_refs[h].shape, F32)


def _softmax_finish(o_ref, l_refs, acc_refs):
    for h in range(N_HEADS):
        o_ref[0, h * HEAD_DIM:(h + 1) * HEAD_DIM, :] = (acc_refs[h][...] * (1.0 / l_refs[h][...])).astype(BF16)


def _logit_bounds(qt_ref, kn2, slack):
    k_max = jnp.sqrt(jnp.max(kn2, axis=-1, keepdims=True))
    bounds = []
    worst = jnp.float32(0.0)
    for h in range(N_HEADS):
        qh = qt_ref[0, h * HEAD_DIM:(h + 1) * HEAD_DIM, :].astype(F32)
        q_norm = jnp.sqrt(jnp.sum(qh * qh, axis=0, keepdims=True))
        b = q_norm * k_max[h:h + 1, :] * BOUND_MARGIN + slack
        bounds.append(b)
        worst = jnp.maximum(worst, jnp.max(b))
    return bounds, worst <= BOUND_SAFE


def _fox_kernel(qt_ref, cumt_ref, kn_ref, kaug_ref, vt_ref, o_ref, rhs_scr, *softmax_refs):
    m_scr, l_scr, acc_scr = _softmax_split(softmax_refs)
    tq, tk = TQ_FOX, TK_FOX
    i = pl.program_id(1)
    q0 = i * tq

    cq = cumt_ref[0]
    c_hi = cq.astype(BF16).astype(F32)
    c_r = cq - c_hi
    c_mid = c_r.astype(BF16).astype(F32)
    c_lo = c_r - c_mid
    row_q = lax.broadcasted_iota(I32, (LANES, tq), 0)
    row_a = lax.broadcasted_iota(I32, (FOX_AUG_ROWS, 2 * tq), 0)
    second = lax.broadcasted_iota(I32, (FOX_AUG_ROWS, 2 * tq), 1) >= tq
    for p in range(N_HEADS // 2):
        qp = qt_ref[0, p * LANES:(p + 1) * LANES, :]
        zq = jnp.zeros_like(qp)
        rhs_scr[p, 0:LANES, 0:tq] = jnp.where(row_q < HEAD_DIM, qp, zq)
        rhs_scr[p, 0:LANES, tq:2 * tq] = jnp.where(row_q >= HEAD_DIM, qp, zq)
        pair_row = lambda a: jnp.concatenate([a[2 * p:2 * p + 1, :], a[2 * p + 1:2 * p + 2, :]], axis=1)
        aug = jnp.where(row_a == 0, pair_row(c_hi),
              jnp.where(row_a == 1, pair_row(c_mid),
              jnp.where(row_a == 2, pair_row(c_lo),
              jnp.where((row_a >= 3) & (row_a < 6), jnp.where(second, 0.0, 1.0),
              jnp.where((row_a >= 6) & (row_a < 9), jnp.where(second, 1.0, 0.0), 0.0)))))
        rhs_scr[p, LANES:LANES + FOX_AUG_ROWS, :] = aug.astype(BF16)
        rhs_scr[p, LANES + FOX_AUG_ROWS:FOX_K, :] = jnp.zeros((FOX_K - LANES - FOX_AUG_ROWS, 2 * tq), BF16)

    _softmax_init(m_scr, l_scr, acc_scr)
    key_j = lax.broadcasted_iota(I32, (tk, tq), 0)
    qry_i = q0 + lax.broadcasted_iota(I32, (tk, tq), 1)

    def tile(kt, masked, second_pass):
        k0 = pl.multiple_of(kt * tk, tk)
        for p in range(N_HEADS // 2):
            st = _dot(kaug_ref[0, pl.ds(k0, tk), p * FOX_K:(p + 1) * FOX_K], rhs_scr[p])
            for hh in range(2):
                h = 2 * p + hh
                s = st[:, hh * tq:(hh + 1) * tq]
                if masked:
                    s = jnp.where(k0 + key_j <= qry_i, s, NEG_INF)
                if second_pass:
                    _softmax_accumulate(s, m_scr[h], l_scr[h], acc_scr[h],
                                        vt_ref[0, kt, h * HEAD_DIM:(h + 1) * HEAD_DIM, :])
                else:
                    _softmax_max(s, m_scr[h])

    n_full = i * (tq // tk)

    def sweep(second_pass):
        def full_tile(kt, carry):
            tile(kt, False, second_pass)
            return carry

        lax.fori_loop(0, n_full, full_tile, 0)
        for d in range(tq // tk):
            tile(n_full + d, True, second_pass)

    bounds, safe = _logit_bounds(qt_ref, kn_ref[0], FOX_BIAS_SLACK)

    @pl.when(safe)
    def _():
        for h in range(N_HEADS):
            m_scr[h][...] = bounds[h]
        sweep(True)

    @pl.when(jnp.logical_not(safe))
    def _():
        sweep(False)
        sweep(True)

    _softmax_finish(o_ref, l_scr, acc_scr)


def _fox_call(qbt, aux_t, kaug3, vbt4):
    batch, _, seq = qbt.shape
    return pl.pallas_call(
        _fox_kernel,
        grid=(batch, seq // TQ_FOX),
        in_specs=[pl.BlockSpec((1, 512, TQ_FOX), lambda b, i: (b, 0, i)),
                  pl.BlockSpec((1, N_HEADS, TQ_FOX), lambda b, i: (b, 1, i)),
                  pl.BlockSpec((1, N_HEADS, seq), lambda b, i: (b, 2, 0)),
                  pl.BlockSpec((1, seq, (N_HEADS // 2) * FOX_K), lambda b, i: (b, 0, 0)),
                  pl.BlockSpec((1, seq // TK_FOX, 512, TK_FOX), lambda b, i: (b, 0, 0, 0))],
        out_specs=pl.BlockSpec((1, 512, TQ_FOX), lambda b, i: (b, 0, i)),
        out_shape=jax.ShapeDtypeStruct((batch, 512, seq), BF16),
        scratch_shapes=[pltpu.VMEM((N_HEADS // 2, FOX_K, 2 * TQ_FOX), BF16)] + _softmax_scratch(TQ_FOX),
        compiler_params=_cparams(("arbitrary", "arbitrary")),
        name="fox",
    )(qbt, aux_t, aux_t, kaug3, vbt4)


def _bit_transpose32(words):
    a = list(words)
    j, mask = 16, 0x0000FFFF
    while j:
        mask_i = int(np.array(mask, np.uint32).view(np.int32))
        k = 0
        while k < 32:
            t = (a[k] ^ lax.shift_right_logical(a[k + j], j)) & mask_i
            a[k] = a[k] ^ t
            a[k + j] = a[k + j] ^ lax.shift_left(t, j)
            k = (k + j + 1) & ~j
        j >>= 1
        mask = (mask ^ (mask << j)) & 0xFFFFFFFF
    return a


def _dsa_kernel(qat_ref, qit_ref, wit_ref, kn_ref, ki_ref, ka_ref, vat_ref, o_ref,
                aqi_scr, aqa_scr, sc_scr, bias_scr, plane_scr, alive_scr, thr_scr, *softmax_refs, seq):
    m_scr, l_scr, acc_scr = _softmax_split(softmax_refs)
    tq, tk = TQ_DSA, TK_DSA
    blk_per_tile = tk // RADIX_BLOCK
    n_blocks = seq // RADIX_BLOCK
    i = pl.program_id(1)
    q0 = i * tq
    nk = lax.shift_right_logical(q0 + tq + tk - 1, int(np.log2(tk)))
    n_rest = (seq - nk * tk).astype(F32)

    row_q = lax.broadcasted_iota(I32, (LANES, tq), 0)
    for h in range(N_HEADS):
        keep = (row_q < HEAD_DIM) if h % 2 == 0 else (row_q >= HEAD_DIM)
        rows = slice((h // 2) * LANES, (h // 2 + 1) * LANES)
        qi_blk = qit_ref[0, rows, :]
        qa_blk = qat_ref[0, rows, :]
        aqi_scr[:, h * tq:(h + 1) * tq] = jnp.where(keep, qi_blk, jnp.zeros_like(qi_blk))
        aqa_scr[:, h * tq:(h + 1) * tq] = jnp.where(keep, qa_blk, jnp.zeros_like(qa_blk))

    w = wit_ref[0]
    key_j = lax.broadcasted_iota(I32, (tk, tq), 0)
    qry_chunk = lax.shift_right_logical(q0 + lax.broadcasted_iota(I32, (tk, tq), 1), CHUNK_SHIFT)

    def admissible(k0):
        return lax.shift_right_logical(k0 + key_j, CHUNK_SHIFT) <= qry_chunk

    def p1(kt, carry, masked):
        k0 = pl.multiple_of(kt * tk, tk)
        rel = _dot(ki_ref[0, pl.ds(k0, tk), :], aqi_scr[...])
        sc = w[0:1, :] * jnp.maximum(rel[:, 0:tq], 0.0)
        for h in range(1, N_HEADS):
            sc = sc + w[h:h + 1, :] * jnp.maximum(rel[:, h * tq:(h + 1) * tq], 0.0)
        if masked:
            sc = jnp.where(admissible(k0), sc, NEG_INF)
        sc = jnp.where(sc == 0.0, 0.0, sc)
        sc_scr[kt] = sc
        b = pltpu.bitcast(sc, I32)
        key = b ^ (lax.shift_right_arithmetic(b, 31) & 0x7FFFFFFF)
        for half in range(blk_per_tile):
            base = half * RADIX_BLOCK
            planes = _bit_transpose32([key[base + SUBLANES * m:base + SUBLANES * (m + 1), :] ^ _INT_MIN
                                       for m in range(32)])
            bl = kt * blk_per_tile + half
            for p in range(32):
                plane_scr[bl, p] = planes[p]
            alive_scr[bl] = jnp.full((SUBLANES, tq), -1, I32)
        return carry

    n_unmasked = lax.shift_right_logical(q0, int(np.log2(tk)))
    lax.fori_loop(0, n_unmasked, functools.partial(p1, masked=False), 0)
    lax.fori_loop(n_unmasked, nk, functools.partial(p1, masked=True), 0)

    def clear_block(bl, carry):
        plane_scr[bl] = jnp.zeros((32, SUBLANES, tq), I32)
        alive_scr[bl] = jnp.zeros((SUBLANES, tq), I32)
        return carry

    n_held = nk * blk_per_tile
    half_blocks = n_blocks // 2
    fits_half = n_held <= half_blocks
    lax.fori_loop(n_held, jnp.where(fits_half, half_blocks, n_blocks), clear_block, 0)

    sent_u = jnp.int32(_SENT_KEY ^ _INT_MIN)
    kf = float(TOPK)

    def radix_select(n_walk):
        def bit_step(p, carry):
            k_rem, thr_bits, rest_alive = carry
            shift = 31 - p
            sent_bit = lax.shift_right_logical(sent_u, shift) & 1
            ones = [alive_scr[bl] & plane_scr[bl, p] for bl in range(n_walk)]
            tot = lax.population_count(ones[0])
            for bl in range(1, n_walk):
                tot = tot + lax.population_count(ones[bl])
            cnt = jnp.sum(tot.astype(F32), axis=0, keepdims=True)
            cnt = cnt + jnp.where((rest_alive != 0) & (sent_bit != 0), n_rest, 0.0)
            take1 = cnt >= k_rem
            for bl in range(n_walk):
                alive_scr[bl] = jnp.where(take1, ones[bl], alive_scr[bl] & ~plane_scr[bl, p])
            k_rem = jnp.where(take1, k_rem, k_rem - cnt)
            thr_bits = thr_bits | jnp.where(take1, lax.shift_left(jnp.int32(1), shift), 0)
            rest_alive = jnp.where(take1 == (sent_bit != 0), rest_alive, 0)
            return k_rem, thr_bits, rest_alive

        _, thr_bits, _ = lax.fori_loop(
            0, 32, bit_step,
            (jnp.full((1, tq), kf, F32), jnp.zeros((1, tq), I32), jnp.ones((1, tq), I32)))
        thr_scr[...] = thr_bits

    pl.when(fits_half)(lambda: radix_select(half_blocks))
    pl.when(jnp.logical_not(fits_half))(lambda: radix_select(n_blocks))
    thr_key = thr_scr[...] ^ _INT_MIN
    thr0 = pltpu.bitcast(thr_key ^ (lax.shift_right_arithmetic(thr_key, 31) & 0x7FFFFFFF), F32)

    def rank_counts(thr):
        def body(kt, c):
            sc = sc_scr[kt]
            return (c[0] + jnp.sum(jnp.where(sc >= thr, 1.0, 0.0), axis=0, keepdims=True),
                    c[1] + jnp.sum(jnp.where(sc > thr, 1.0, 0.0), axis=0, keepdims=True))
        z = jnp.zeros((1, tq), F32)
        n_ge, n_gt = lax.fori_loop(0, nk, body, (z, z))
        return (n_ge + jnp.where(thr <= NEG_INF, n_rest, 0.0), n_gt + jnp.where(thr < NEG_INF, n_rest, 0.0))

    def misplaced(state):
        _, n_ge, n_gt = state
        return jnp.max(jnp.where((n_ge < kf) | (n_gt >= kf), 1.0, 0.0)) > 0.0

    def step_threshold(state):
        thr, n_ge, n_gt = state

        def body(kt, c):
            sc = sc_scr[kt]
            below = jnp.max(jnp.where(sc < thr, sc, -jnp.inf), axis=0, keepdims=True)
            above = jnp.min(jnp.where(sc > thr, sc, jnp.inf), axis=0, keepdims=True)
            return jnp.maximum(c[0], below), jnp.minimum(c[1], above)

        below, above = lax.fori_loop(0, nk, body, (jnp.full((1, tq), -jnp.inf, F32), jnp.full((1, tq), jnp.inf, F32)))
        has_rest = n_rest > 0.0
        below = jnp.where(has_rest & (thr > NEG_INF), jnp.maximum(below, NEG_INF), below)
        above = jnp.where(has_rest & (thr < NEG_INF), jnp.minimum(above, NEG_INF), above)
        thr = jnp.where(n_ge < kf, below, jnp.where(n_gt >= kf, above, thr))
        return (thr,) + rank_counts(thr)

    thr, n_ge, n_gt = lax.while_loop(misplaced, step_threshold, (thr0,) + rank_counts(thr0))
    need = kf - n_gt
    tie_free = jnp.min(jnp.where(n_ge == kf, 1.0, 0.0)) > 0.0

    tr = lax.broadcasted_iota(I32, (tk, tk), 0)
    tc = lax.broadcasted_iota(I32, (tk, tk), 1)
    tri = jnp.where(tc <= tr, 1.0, 0.0).astype(BF16)
    _softmax_init(m_scr, l_scr, acc_scr)

    def selection_bias(kt, tie):
        sc = sc_scr[kt]
        eq = sc == thr
        pref = _dot(tri, jnp.where(eq, 1.0, 0.0).astype(BF16)) + tie
        sel = (sc > thr) | (eq & (pref <= need))
        return jnp.where(sel & admissible(kt * tk), 0.0, NEG_INF), pref[tk - 1:tk, :]

    def logits(kt):
        return _dot(ka_ref[0, pl.ds(pl.multiple_of(kt * tk, tk), tk), :], aqa_scr[...])

    def accumulate(kt, bias):
        logit = logits(kt)
        v_t = vat_ref[0, kt]
        for h in range(N_HEADS):
            _softmax_accumulate(logit[:, h * tq:(h + 1) * tq] + bias, m_scr[h], l_scr[h], acc_scr[h], v_t)

    def p3_single(kt, tie):
        bias, tie = selection_bias(kt, tie)
        accumulate(kt, bias)
        return tie

    def p3_single_tie_free(kt, carry):
        accumulate(kt, jnp.where(sc_scr[kt] >= thr, 0.0, NEG_INF))
        return carry

    def p3_max(kt, tie):
        bias, tie = selection_bias(kt, tie)
        bias_scr[kt] = bias
        logit = logits(kt)
        for h in range(N_HEADS):
            _softmax_max(logit[:, h * tq:(h + 1) * tq] + bias, m_scr[h])
        return tie

    def p3_accumulate(kt, carry):
        accumulate(kt, bias_scr[kt])
        return carry

    kn2 = jnp.broadcast_to(kn_ref[0, 0:1, :], (N_HEADS, seq))
    bounds, safe = _logit_bounds(qat_ref, kn2, 0.0)
    no_tie = jnp.zeros((1, tq), F32)

    @pl.when(safe)
    def _():
        for h in range(N_HEADS):
            m_scr[h][...] = bounds[h]

    @pl.when(safe & tie_free)
    def _():
        lax.fori_loop(0, nk, p3_single_tie_free, 0)

    @pl.when(safe & jnp.logical_not(tie_free))
    def _():
        lax.fori_loop(0, nk, p3_single, no_tie)

    @pl.when(jnp.logical_not(safe))
    def _():
        lax.fori_loop(0, nk, p3_max, no_tie)
        lax.fori_loop(0, nk, p3_accumulate, 0)

    _softmax_finish(o_ref, l_scr, acc_scr)


def _dsa_call(qat, qit, aux_t, ki3, ka3, vat4):
    batch, _, seq = qat.shape
    qspec = pl.BlockSpec((1, 512, TQ_DSA), lambda b, i: (b, 0, i))
    kspec = pl.BlockSpec((1, seq, LANES), lambda b, i: (b, 0, 0))
    return pl.pallas_call(
        functools.partial(_dsa_kernel, seq=seq),
        grid=(batch, seq // TQ_DSA),
        in_specs=[qspec, qspec,
                  pl.BlockSpec((1, N_HEADS, TQ_DSA), lambda b, i: (b, 0, i)),
                  pl.BlockSpec((1, N_HEADS, seq), lambda b, i: (b, 3, 0)),
                  kspec, kspec,
                  pl.BlockSpec((1, seq // TK_DSA, HEAD_DIM, TK_DSA), lambda b, i: (b, 0, 0, 0))],
        out_specs=qspec,
        out_shape=jax.ShapeDtypeStruct((batch, 512, seq), BF16),
        scratch_shapes=[pltpu.VMEM((LANES, N_HEADS * TQ_DSA), BF16),
                        pltpu.VMEM((LANES, N_HEADS * TQ_DSA), BF16),
                        pltpu.VMEM((seq // TK_DSA, TK_DSA, TQ_DSA), F32),
                        pltpu.VMEM((seq // TK_DSA, TK_DSA, TQ_DSA), F32),
                        pltpu.VMEM((seq // RADIX_BLOCK, 32, SUBLANES, TQ_DSA), I32),
                        pltpu.VMEM((seq // RADIX_BLOCK, SUBLANES, TQ_DSA), I32),
                        pltpu.VMEM((1, TQ_DSA), I32)] + _softmax_scratch(TQ_DSA),
        compiler_params=_cparams(("arbitrary", "arbitrary")),
        name="dsa",
    )(qat, qit, aux_t, aux_t, ki3, ka3, vat4)


def _outproj_kernel(oa_ref, ob_ref, sga_ref, sgb_ref, x_ref, mod_ref, gffn_ref,
                    woa_ref, wob_ref, wo_ref, wr_hi_ref, wr_lo_ref, br_ref,
                    x1_ref, h2_ref, lg_ref):
    tm = TM_OUT
    tn_dot = lambda a, b: lax.dot_general(a, b, (((0,), (0,)), ((), ())), preferred_element_type=F32)
    ya = tn_dot(oa_ref[0], woa_ref[...])
    yb = tn_dot(ob_ref[0], wob_ref[...])
    merged = (sga_ref[...].astype(F32) * ya + sgb_ref[...].astype(F32) * yb).astype(BF16)
    mix = _dot(merged, wo_ref[...])
    gt1 = mod_ref[0, 2:3, :]
    sh2 = mod_ref[0, 3:4, :]
    sc2 = mod_ref[0, 4:5, :]
    x1 = x_ref[...] + gt1 * mix
    x1_ref[...] = x1
    var = jnp.mean(x1 * x1, axis=-1, keepdims=True)
    h2 = x1 * lax.rsqrt(var + RMS_EPS) * gffn_ref[...] * (1.0 + sc2) + sh2
    for c in range(D_MODEL // LANES):
        h2_ref[pl.ds(c, tm, stride=SUBLANES), :] = h2[:, c * LANES:(c + 1) * LANES]
    hi = h2.astype(BF16)
    lo = (h2 - hi.astype(F32)).astype(BF16)
    lg_ref[...] = (_dot(hi, wr_hi_ref[...]) + _dot(hi, wr_lo_ref[...]) + _dot(lo, wr_hi_ref[...])
                   + br_ref[...])


def _outproj_call(oa2, ob2, sga, sgb, x2, mod3, g_ffn, woa, wob, wo, wr_hi, wr_lo, br, seq):
    n = x2.shape[0]
    tpb = seq // TM_OUT
    row = lambda w: pl.BlockSpec((TM_OUT, w), lambda i: (i, 0))
    full = lambda a: pl.BlockSpec(a.shape, lambda i: (0,) * a.ndim)
    return pl.pallas_call(
        _outproj_kernel,
        grid=(n // TM_OUT,),
        in_specs=[pl.BlockSpec((1, 512, TM_OUT), lambda i: (i // tpb, 0, i % tpb)),
                  pl.BlockSpec((1, 512, TM_OUT), lambda i: (i // tpb, 0, i % tpb)),
                  row(1024), row(1024), row(D_MODEL),
                  pl.BlockSpec((1, 6, D_MODEL), lambda i: (i // tpb, 0, 0)),
                  full(g_ffn), full(woa), full(wob), full(wo), full(wr_hi), full(wr_lo), full(br)],
        out_specs=[row(D_MODEL),
                   pl.BlockSpec((TM_OUT * SUBLANES, LANES), lambda i: (i, 0)),
                   row(LANES)],
        out_shape=[jax.ShapeDtypeStruct((n, D_MODEL), F32),
                   jax.ShapeDtypeStruct((n * SUBLANES, LANES), F32),
                   jax.ShapeDtypeStruct((n, LANES), F32)],
        compiler_params=_cparams(("arbitrary",)),
        name="outproj",
    )(oa2, ob2, sga, sgb, x2, mod3, g_ffn, woa, wob, wo, wr_hi, wr_lo, br)


def _route_rows(lg):
    tm = lg.shape[0]
    lane = lax.broadcasted_iota(I32, (tm, LANES), 1)
    big = jnp.int32(LANES)
    ninf = -jnp.inf
    gmask = (lane >= N_EXPERTS) & (lane < N_EXPERTS + N_GROUPS)
    g = jnp.where(gmask, lg, ninf)
    gmax = jnp.max(g, axis=-1, keepdims=True)
    grp = jnp.min(jnp.where(g == gmax, lane - N_EXPERTS, big), axis=-1, keepdims=True)
    p_grp = 1.0 / jnp.sum(jnp.where(gmask, jnp.exp(lg - gmax), 0.0), axis=-1, keepdims=True)
    lo = grp * EXPERTS_PER_GROUP
    emask = (lane >= lo) & (lane < lo + EXPERTS_PER_GROUP)
    ev = jnp.where(emask, lg, ninf)
    v0 = jnp.max(ev, axis=-1, keepdims=True)
    i0 = jnp.min(jnp.where(emask & (ev == v0), lane, big), axis=-1, keepdims=True)
    rest = emask & (lane != i0)
    ev1 = jnp.where(rest, lg, ninf)
    v1 = jnp.max(ev1, axis=-1, keepdims=True)
    i1 = jnp.min(jnp.where(rest & (ev1 == v1), lane, big), axis=-1, keepdims=True)
    e1 = jnp.exp(v1 - v0)
    w0 = p_grp / (1.0 + e1)
    w1 = p_grp * e1 / (1.0 + e1)
    return jnp.where(lane == 0, i0.astype(F32),
                     jnp.where(lane == 1, i1.astype(F32),
                               jnp.where(lane == 2, w0, jnp.where(lane == 3, w1, 0.0))))


def _expert_onehots(rt, tm):
    lane = lax.broadcasted_iota(I32, (tm, LANES), 1)
    lane_f = lane.astype(F32)
    e0 = jnp.sum(jnp.where(lane == 0, rt, 0.0), axis=-1, keepdims=True)
    e1 = jnp.sum(jnp.where(lane == 1, rt, 0.0), axis=-1, keepdims=True)
    return lane, lane_f == e0, lane_f == e1


def _rank_rows(rt, seen_before):
    tm = rt.shape[0]
    lane, is0, is1 = _expert_onehots(rt, tm)
    hits = jnp.where(is0, 1.0, 0.0) + jnp.where(is1, 1.0, 0.0)
    rr = lax.broadcasted_iota(I32, (tm, tm), 0)
    cc = lax.broadcasted_iota(I32, (tm, tm), 1)
    before = jnp.where(cc < rr, 1.0, 0.0).astype(BF16)
    seen = _dot(before, hits.astype(BF16)) + seen_before
    r0 = jnp.sum(jnp.where(is0, seen, 0.0), axis=-1, keepdims=True)
    r1 = jnp.sum(jnp.where(is1, seen, 0.0), axis=-1, keepdims=True)
    rk = jnp.where(lane == 0, r0, jnp.where(lane == 1, r1, 0.0))
    return rk, seen_before + jnp.sum(hits, axis=0, keepdims=True)


def _route_rank_kernel(lg_ref, rt_ref, rk_ref, cnt_ref, seen_scr):
    @pl.when(pl.program_id(0) == 0)
    def _():
        seen_scr[...] = jnp.zeros_like(seen_scr)

    rt = _route_rows(lg_ref[...])
    rt_ref[...] = rt
    rk, seen = _rank_rows(rt, seen_scr[...])
    rk_ref[...] = rk
    seen_scr[...] = seen
    cnt_ref[...] = jnp.broadcast_to(seen, cnt_ref.shape)


def _route_rank_call(lg):
    n = lg.shape[0]
    spec = pl.BlockSpec((TM_ROUTE, LANES), lambda i: (i, 0))
    return pl.pallas_call(
        _route_rank_kernel, grid=(n // TM_ROUTE,), in_specs=[spec],
        out_specs=[spec, spec, pl.BlockSpec((SUBLANES, LANES), lambda i: (0, 0))],
        out_shape=[jax.ShapeDtypeStruct((n, LANES), F32), jax.ShapeDtypeStruct((n, LANES), F32),
                   jax.ShapeDtypeStruct((SUBLANES, LANES), F32)],
        scratch_shapes=[pltpu.VMEM((1, LANES), F32)],
        compiler_params=_cparams(("arbitrary",)), name="route_rank",
    )(lg)


def _dest_kernel(rt_ref, rk_ref, cnt_ref, dd_ref, be_ref, nu_ref, end_ref):
    tm = TM_ROUTE
    lane, is0, is1 = _expert_onehots(rt_ref[...], tm)
    blocks = jnp.floor((cnt_ref[...] + (BLK_E - 1)) * (1.0 / BLK_E))
    er = lax.broadcasted_iota(I32, (LANES, LANES), 0)
    ec = lax.broadcasted_iota(I32, (LANES, LANES), 1)
    upto = jnp.where(er <= ec, 1.0, 0.0).astype(BF16)
    bend = _dot(blocks.astype(BF16), upto)
    pstart = (bend[0:1, :] - blocks[0:1, :]) * BLK_E
    rk = rk_ref[...]
    r0 = jnp.sum(jnp.where(lane == 0, rk, 0.0), axis=-1, keepdims=True)
    r1 = jnp.sum(jnp.where(lane == 1, rk, 0.0), axis=-1, keepdims=True)
    d0 = jnp.sum(jnp.where(is0, pstart, 0.0), axis=-1, keepdims=True) + r0
    d1 = jnp.sum(jnp.where(is1, pstart, 0.0), axis=-1, keepdims=True) + r1
    dd_ref[...] = jnp.where(lane == 0, d0, jnp.where(lane == 1, d1, 0.0)).astype(I32)
    nb = be_ref.shape[0]
    blk = lax.broadcasted_iota(I32, (nb, LANES), 0).astype(F32)
    lane_b = lax.broadcasted_iota(I32, (nb, LANES), 1)
    ended = jnp.where((lane_b < N_EXPERTS) & (bend[0:1, :] <= blk), 1.0, 0.0)
    be = jnp.minimum(jnp.sum(ended, axis=-1, keepdims=True), float(N_EXPERTS - 1))
    be_ref[...] = jnp.broadcast_to(be, be_ref.shape).astype(I32)
    lane_c = lax.broadcasted_iota(I32, (SUBLANES, LANES), 1)
    used = jnp.sum(jnp.where(lane_c == N_EXPERTS - 1, bend, 0.0), axis=-1, keepdims=True)
    nu_ref[...] = jnp.broadcast_to(used, nu_ref.shape).astype(I32)
    end_ref[...] = (bend * BLK_E).astype(I32)


def _dest_call(rt, rk, cnt, n_blocks):
    n = rt.shape[0]
    nb_pad = -(-n_blocks // SUBLANES) * SUBLANES
    spec = pl.BlockSpec((TM_ROUTE, LANES), lambda i: (i, 0))
    const = lambda rows: pl.BlockSpec((rows, LANES), lambda i: (0, 0))
    return pl.pallas_call(
        _dest_kernel, grid=(n // TM_ROUTE,), in_specs=[spec, spec, const(SUBLANES)],
        out_specs=[spec, const(nb_pad), const(SUBLANES), const(SUBLANES)],
        out_shape=[jax.ShapeDtypeStruct((n, LANES), I32), jax.ShapeDtypeStruct((nb_pad, LANES), I32),
                   jax.ShapeDtypeStruct((SUBLANES, LANES), I32), jax.ShapeDtypeStruct((SUBLANES, LANES), I32)],
        compiler_params=_cparams(("arbitrary",)), name="dest",
    )(rt, rk, cnt)


def _slab_copy(src, src_row, dst, dst_row, sem):
    return pltpu.make_async_copy(src.at[pl.ds(pl.multiple_of(src_row * SUBLANES, SUBLANES), SUBLANES)],
                                 dst.at[pl.ds(pl.multiple_of(dst_row * SUBLANES, SUBLANES), SUBLANES)], sem)


def _dispatch_kernel(dest_ref, seg_end_ref, h2_ref, xs_hbm, zero_scr, sem, zsem):
    tm = TM_DISPATCH
    base = pl.program_id(0) * tm

    @pl.when(pl.program_id(0) == 0)
    def _():
        zero_scr[...] = jnp.zeros_like(zero_scr)
        clears = [pltpu.make_async_copy(
            zero_scr, xs_hbm.at[pl.ds(pl.multiple_of(jnp.maximum(seg_end_ref[e] - BLK_E, 0) * SUBLANES, SUBLANES),
                                      BLK_E * SUBLANES)], zsem) for e in range(N_EXPERTS)]
        for c in clears:
            c.start()
        for c in clears:
            c.wait()
        n_blocks = xs_hbm.shape[0] // (BLK_E * SUBLANES)
        first_free = lax.shift_right_logical(seg_end_ref[N_EXPERTS - 1], int(np.log2(BLK_E)))
        def tail_clear(b):
            row = pl.multiple_of((first_free + b) * (BLK_E * SUBLANES), BLK_E * SUBLANES)
            return pltpu.make_async_copy(zero_scr, xs_hbm.at[pl.ds(row, BLK_E * SUBLANES)], zsem)

        for b in range(N_EXPERTS):
            pl.when(first_free + b < n_blocks)(lambda b=b: tail_clear(b).start())
        for b in range(N_EXPERTS):
            pl.when(first_free + b < n_blocks)(lambda b=b: tail_clear(b).wait())

    def issue(r, _):
        for k in range(2):
            _slab_copy(h2_ref, r, xs_hbm, dest_ref[2 * (base + r) + k], sem).start(priority=k)
        return 0

    lax.fori_loop(0, tm, issue, 0, unroll=GATHER_UNROLL)
    for _ in range(2):
        pltpu.make_async_copy(h2_ref, xs_hbm.at[pl.ds(0, tm * SUBLANES)], sem).wait()


def _dispatch_call(dest, seg_end, h2s, n_slots):
    n = h2s.shape[0] // SUBLANES
    return pl.pallas_call(
        _dispatch_kernel,
        grid_spec=pltpu.PrefetchScalarGridSpec(
            num_scalar_prefetch=2, grid=(n // TM_DISPATCH,),
            in_specs=[pl.BlockSpec((TM_DISPATCH * SUBLANES, LANES), lambda i, d, e: (i, 0))],
            out_specs=pl.BlockSpec(memory_space=pl.ANY),
            scratch_shapes=[pltpu.VMEM((BLK_E * SUBLANES, LANES), F32),
                            pltpu.SemaphoreType.DMA(()), pltpu.SemaphoreType.DMA(())]),
        out_shape=jax.ShapeDtypeStruct((n_slots * SUBLANES, LANES), F32),
        compiler_params=_cparams(("arbitrary",)),
        name="dispatch",
    )(dest, seg_end, h2s)


def _experts_kernel(be_ref, nused_ref, xs_ref, wg_ref, wu_ref, wd_ref, yb_ref, x_scr, wg_scr, wu_scr, wd_scr):
    i = pl.program_id(0)

    @pl.when(i >= nused_ref[0])
    def _():
        yb_ref[...] = jnp.zeros_like(yb_ref)

    @pl.when(i < nused_ref[0])
    def _():
        prev = be_ref[jnp.maximum(i - 1, 0)]

        @pl.when((i == 0) | (be_ref[i] != prev))
        def _():
            wg_scr[...] = wg_ref[0].astype(BF16)
            wu_scr[...] = wu_ref[0].astype(BF16)
            wd_scr[...] = wd_ref[0].astype(BF16)

        for c in range(D_MODEL // LANES):
            x_scr[:, c * LANES:(c + 1) * LANES] = xs_ref[pl.ds(c, BLK_E, stride=SUBLANES), :].astype(BF16)
        xb = x_scr[...]
        g = _dot(xb, wg_scr[...])
        u = _dot(xb, wu_scr[...])
        hid = (g * _sigmoid(g) * u).astype(BF16)
        y = _dot(hid, wd_scr[...])
        for c in range(D_MODEL // LANES):
            yb_ref[pl.ds(c, BLK_E, stride=SUBLANES), :] = y[:, c * LANES:(c + 1) * LANES]


def _experts_call(block_e, n_used, xs, w_gate, w_up, w_down):
    n_blocks = block_e.shape[0]
    slab = pl.BlockSpec((BLK_E * SUBLANES, LANES), lambda i, be, nu: (i, 0))
    used_slab = pl.BlockSpec((BLK_E * SUBLANES, LANES), lambda i, be, nu: (jnp.minimum(i, nu[0] - 1), 0))
    return pl.pallas_call(
        _experts_kernel,
        grid_spec=pltpu.PrefetchScalarGridSpec(
            num_scalar_prefetch=2, grid=(n_blocks,),
            in_specs=[used_slab,
                      pl.BlockSpec((1, D_MODEL, D_EXPERT), lambda i, be, nu: (be[i], 0, 0)),
                      pl.BlockSpec((1, D_MODEL, D_EXPERT), lambda i, be, nu: (be[i], 0, 0)),
                      pl.BlockSpec((1, D_EXPERT, D_MODEL), lambda i, be, nu: (be[i], 0, 0))],
            out_specs=slab,
            scratch_shapes=[pltpu.VMEM((BLK_E, D_MODEL), BF16),
                            pltpu.VMEM((D_MODEL, D_EXPERT), BF16),
                            pltpu.VMEM((D_MODEL, D_EXPERT), BF16),
                            pltpu.VMEM((D_EXPERT, D_MODEL), BF16)]),
        out_shape=jax.ShapeDtypeStruct(xs.shape, F32),
        compiler_params=_cparams(("arbitrary",)),
        name="experts",
    )(block_e, n_used, xs, w_gate, w_up, w_down)


def _final_kernel(dest_ref, x1_ref, rt_ref, mod_ref, gfin_ref, yb_hbm, o_ref, g_scr, sem):
    tm = TM_FINAL
    step = pl.program_id(0)
    slot = step % 2
    slot_rows = 2 * tm * SUBLANES

    def gather(s, into):
        def issue(r, _):
            tok = s * tm + r
            for k in range(2):
                _slab_copy(yb_hbm, dest_ref[2 * tok + k], g_scr.at[into], k * tm + r,
                           sem.at[into]).start(priority=k)
            return 0
        lax.fori_loop(0, tm, issue, 0, unroll=GATHER_UNROLL)

    @pl.when(step == 0)
    def _():
        gather(0, 0)

    @pl.when(step + 1 < pl.num_programs(0))
    def _():
        gather(step + 1, 1 - slot)

    g_now = g_scr.at[slot]
    pltpu.make_async_copy(yb_hbm.at[pl.ds(0, slot_rows)], g_now, sem.at[slot]).wait()

    rt = rt_ref[...]
    lane = lax.broadcasted_iota(I32, (tm, LANES), 1)
    gw0 = jnp.sum(jnp.where(lane == 2, rt, 0.0), axis=-1, keepdims=True)
    gw1 = jnp.sum(jnp.where(lane == 3, rt, 0.0), axis=-1, keepdims=True)
    gt2 = mod_ref[0, 5:6, :]
    x1 = x1_ref[...]
    cols = []
    for c in range(D_MODEL // LANES):
        y0 = g_now[pl.ds(c, tm, stride=SUBLANES), :]
        y1 = g_now[pl.ds(tm * SUBLANES + c, tm, stride=SUBLANES), :]
        y = gw0 * y0 + gw1 * y1
        cols.append(x1[:, c * LANES:(c + 1) * LANES] + gt2[:, c * LANES:(c + 1) * LANES] * y)
    x2 = jnp.concatenate(cols, axis=1)
    var = jnp.mean(x2 * x2, axis=-1, keepdims=True)
    o_ref[...] = x2 * lax.rsqrt(var + RMS_EPS) * gfin_ref[...]


def _final_call(dest, x1, rt, mod3, g_final, yb, seq):
    n = x1.shape[0]
    tpb = seq // TM_FINAL
    return pl.pallas_call(
        _final_kernel,
        grid_spec=pltpu.PrefetchScalarGridSpec(
            num_scalar_prefetch=1, grid=(n // TM_FINAL,),
            in_specs=[pl.BlockSpec((TM_FINAL, D_MODEL), lambda i, d: (i, 0)),
                      pl.BlockSpec((TM_FINAL, LANES), lambda i, d: (i, 0)),
                      pl.BlockSpec((1, 6, D_MODEL), lambda i, d: (i // tpb, 0, 0)),
                      pl.BlockSpec((1, D_MODEL), lambda i, d: (0, 0)),
                      pl.BlockSpec(memory_space=pl.ANY)],
            out_specs=pl.BlockSpec((TM_FINAL, D_MODEL), lambda i, d: (i, 0)),
            scratch_shapes=[pltpu.VMEM((2, 2 * TM_FINAL * SUBLANES, LANES), F32),
                            pltpu.SemaphoreType.DMA((2,))]),
        out_shape=jax.ShapeDtypeStruct((n, D_MODEL), F32),
        compiler_params=_cparams(("arbitrary",)),
        name="final",
    )(dest, x1, rt, mod3, g_final, yb)


def _permute_w_in(w):
    o = np.cumsum([0, 512, 64, 64, 512, 64, 8, 512, 512, 512, 8, 1024, 1024])
    qa, ka, va, qi, ki, wi, qb, kb, vb, fb, ga, gb = [w[:, o[k]:o[k + 1]] for k in range(12)]
    aux = jnp.concatenate([wi, fb, jnp.zeros((w.shape[0], LANES - 2 * N_HEADS), w.dtype)], axis=1)
    w_rows = jnp.concatenate([kb, ga, gb, ka, ka, ki, ki, aux], axis=1).astype(BF16)
    w_t = jnp.concatenate([qa, qi, qb, vb, va], axis=1).T.astype(BF16)
    return w_rows, w_t


def _layer(x2, pos2, mod3, batch, seq, g_mix, w_in, b_forget, w_out_a, w_out_b, w_out, g_ffn,
           w_group, b_group, w_router, b_router, w_e_gate, w_e_up, w_e_down, g_final):
    n = x2.shape[0]
    inv_freq = ROPE_THETA ** (-jnp.arange(0, ROPE_DIM, 2, dtype=F32) / ROPE_DIM)
    jj = np.arange(LANES) % HEAD_DIM
    invf = jnp.where(jj < ROPE_DIM, inv_freq[jj % (ROPE_DIM // 2)], 0.0)[None, :].astype(F32)
    fbias = jnp.zeros((1, LANES), F32).at[0, N_HEADS:2 * N_HEADS].set(b_forget.astype(F32))

    invf8 = jnp.broadcast_to(inv_freq[:, None], (ROPE_DIM // 2, LANES)).astype(F32)
    w_rows, w_t = _permute_w_in(w_in)
    (qat, qit, qbt, kaug, vbt, sga, sgb, ka2, vat, ki2, aux_t) = _inproj_call(
        x2, pos2, pos2.reshape(batch, 1, seq), mod3, g_mix.reshape(1, D_MODEL), w_rows, w_t, invf, invf8, fbias,
        batch, seq)

    r3 = lambda a: a.reshape(batch, seq, a.shape[-1])
    obt = _fox_call(qbt, aux_t, r3(kaug), vbt)
    oat = _dsa_call(qat, qit, aux_t, r3(ki2), r3(ka2), vat)

    w_rt = jnp.concatenate([w_router, w_group, jnp.zeros((D_MODEL, LANES - N_EXPERTS - N_GROUPS), F32)], axis=1)
    wr_hi = w_rt.astype(BF16)
    wr_lo = (w_rt - wr_hi.astype(F32)).astype(BF16)
    br = jnp.concatenate([b_router, b_group, jnp.zeros((LANES - N_EXPERTS - N_GROUPS,), F32)])[None, :]
    x1, h2s, lg = _outproj_call(oat, obt, sga, sgb, x2, mod3,
                                g_ffn.reshape(1, D_MODEL), w_out_a.astype(BF16), w_out_b.astype(BF16),
                                w_out.astype(BF16), wr_hi, wr_lo, br, seq)
    rt, rk, cnt = _route_rank_call(lg)

    n_slots = 2 * n + N_EXPERTS * BLK_E
    n_blocks = n_slots // BLK_E
    dd, be, nu, seg_end = _dest_call(rt, rk, cnt, n_blocks)
    dest = dd[:, 0:2].reshape(-1)
    block_e = be[:n_blocks, 0]
    n_used = nu[0, 0:1]

    xs = _dispatch_call(dest, seg_end[0, :N_EXPERTS], h2s, n_slots)
    yb = _experts_call(block_e, n_used, xs, w_e_gate, w_e_up, w_e_down)
    return _final_call(dest, x1, rt, mod3, g_final.reshape(1, D_MODEL), yb, seq)


def kernel(x, c, positions, w_mod, b_mod, g_mix, w_in, b_forget, w_out_a, w_out_b, w_out, g_ffn, w_group,
           b_group, w_router, b_router, w_e_gate, w_e_up, w_e_down, g_final):
    batch, seq, d = x.shape
    depth = w_mod.shape[0]
    assert depth == 1 and d == D_MODEL, "kernel fuses the final norm into the single layer"
    n = batch * seq
    c8 = jnp.zeros((8, d), F32).at[:batch].set(c)
    mod = _mod_call(c8, w_mod[0], b_mod[0][None, :])
    mod3 = mod[:batch].reshape(batch, 6, d)
    out = _layer(x.reshape(n, d), positions.reshape(n, 1), mod3, batch, seq, g_mix[0], w_in[0], b_forget[0],
                 w_out_a[0], w_out_b[0], w_out[0], g_ffn[0], w_group[0], b_group[0], w_router[0], b_router[0],
                 w_e_gate[0], w_e_up[0], w_e_down[0], g_final)
    return out.reshape(batch, seq, d)
```

```python
import functools

import numpy as np
import jax
import jax.numpy as jnp
from jax import lax
from jax.experimental import pallas as pl
from jax.experimental.pallas import tpu as pltpu

F32 = jnp.float32
BF16 = jnp.bfloat16
I32 = jnp.int32

D_MODEL = 1024
HEAD_DIM = 64
N_HEADS = 8
CHUNK_SHIFT = 6
TOPK = 256
ROPE_DIM = 16
ROPE_THETA = 500000.0
N_GROUPS = 4
EXPERTS_PER_GROUP = 8
N_EXPERTS = 32
D_EXPERT = 512
RMS_EPS = 1e-6
NEG_INF = -1e30
ATTN_SCALE = HEAD_DIM ** -0.5
IDX_SCALE = HEAD_DIM ** -0.5
LOG2E = 1.4426950408889634

LANES = 128
SUBLANES = 8
VMEM_LIMIT = 56 * 1024 * 1024

TM_IN = 512
TQ_FOX = 512
TK_FOX = 512
TQ_DSA = 512
TK_DSA = 512
TM_OUT = 512
TM_ROUTE = 1024
TM_DISPATCH = 1024
BLK_E = 256
TM_FINAL = 512
GATHER_UNROLL = 8

RADIX_BLOCK = 32 * SUBLANES
BOUND_MARGIN = 1.02
BOUND_SAFE = 40.0
FOX_BIAS_SLACK = 1.0
FOX_K = 256
FOX_AUG_ROWS = 16

_SEG_KB, _SEG_GA, _SEG_GB, _SEG_KA, _SEG_KI, _SEG_AUX = 0, 512, 1536, 2560, 2688, 2816
_NC_IN = 2944
_ROW_QA, _ROW_QI, _ROW_QB, _ROW_VB, _ROW_VA = 0, 512, 1024, 1536, 2048
_NR_IN = 2048 + HEAD_DIM

_SENT_KEY = int(np.array(NEG_INF, np.float32).view(np.int32) ^ 0x7FFFFFFF)
_INT_MIN = -(2 ** 31)


def _cparams(sem):
    return pltpu.CompilerParams(dimension_semantics=sem, vmem_limit_bytes=VMEM_LIMIT)


def _dot(a, b):
    return jnp.dot(a, b, preferred_element_type=F32)


def _sigmoid(x):
    return 1.0 / (1.0 + jnp.exp(-x))


def _mod_kernel(c_ref, w_ref, b_ref, o_ref):
    c = c_ref[...]
    ca = c * _sigmoid(c)
    o_ref[...] = _dot(ca.astype(BF16), w_ref[...].astype(BF16)) + b_ref[...]


def _mod_call(c8, w_mod, b_mod):
    n_out = w_mod.shape[1]
    tn = 1024
    return pl.pallas_call(
        _mod_kernel,
        grid=(n_out // tn,),
        in_specs=[pl.BlockSpec((8, D_MODEL), lambda j: (0, 0)),
                  pl.BlockSpec((D_MODEL, tn), lambda j: (0, j)),
                  pl.BlockSpec((1, tn), lambda j: (0, j))],
        out_specs=pl.BlockSpec((8, tn), lambda j: (0, j)),
        out_shape=jax.ShapeDtypeStruct((8, n_out), F32),
        compiler_params=_cparams(("arbitrary",)),
        name="mod",
    )(c8, w_mod, b_mod)


def _inproj_kernel(x_ref, pos_ref, posr_ref, mod_ref, g_ref, w_ref, wt_ref, invf_ref, invf8_ref, fbias_ref,
                   sel_ref, ones_ref,
                   qat_ref, qit_ref, qbt_ref, kb_ref, vbt_ref, sga_ref, sgb_ref,
                   ka_ref, vat_ref, ki_ref, auxt_ref, h_scr, carry_scr):
    tm = TM_IN
    x = x_ref[...]
    var = jnp.mean(x * x, axis=-1, keepdims=True)
    tf = x * lax.rsqrt(var + RMS_EPS) * g_ref[...]
    sh = mod_ref[0, 0:1, :]
    sc = mod_ref[0, 1:2, :]
    h_scr[...] = (tf * (1.0 + sc) + sh).astype(BF16)
    hb = h_scr[...]

    lane = lax.broadcasted_iota(I32, (tm, LANES), 1)
    j = lane & (HEAD_DIM - 1)
    ang = pos_ref[...].astype(F32) * invf_ref[...]
    cs = jnp.cos(ang)
    sn = jnp.sin(ang)
    coef_next = jnp.where(j < ROPE_DIM // 2, -sn, 0.0)
    coef_prev = jnp.where((j >= ROPE_DIM // 2) & (j < ROPE_DIM), sn, 0.0)

    def rope(tc):
        return (tc * cs + pltpu.roll(tc, LANES - ROPE_DIM // 2, 1) * coef_next
                + pltpu.roll(tc, ROPE_DIM // 2, 1) * coef_prev)

    def seg(off, width):
        return _dot(hb, w_ref[:, off:off + width])

    def seg_t(row0, rows):
        return lax.dot_general(wt_ref[row0:row0 + rows, :], hb, (((1,), (1,)), ((), ())),
                               preferred_element_type=F32)

    ang8 = invf8_ref[:, 0:1] * posr_ref[0].astype(F32)
    cs8 = jnp.cos(ang8)
    sn8 = jnp.sin(ang8)
    half = ROPE_DIM // 2

    def rope_t(t, scale, out_ref):
        for h in range(N_HEADS):
            r0 = h * HEAD_DIM
            lo = t[r0:r0 + half, :]
            hi = t[r0 + half:r0 + ROPE_DIM, :]
            head = jnp.concatenate([lo * cs8 - hi * sn8, hi * cs8 + lo * sn8, t[r0 + ROPE_DIM:r0 + HEAD_DIM, :]],
                                   axis=0)
            out_ref[0, r0:r0 + HEAD_DIM, :] = (head * scale).astype(BF16)

    rope_t(seg_t(_ROW_QA, 512), ATTN_SCALE * LOG2E, qat_ref)
    rope_t(seg_t(_ROW_QI, 512), IDX_SCALE, qit_ref)
    qbt_ref[0] = (seg_t(_ROW_QB, 512) * (ATTN_SCALE * LOG2E)).astype(BF16)
    vbt_ref[0, 0] = seg_t(_ROW_VB, 512).astype(BF16)
    vat_ref[0, 0] = seg_t(_ROW_VA, HEAD_DIM).astype(BF16)
    sga_ref[...] = _sigmoid(seg(_SEG_GA, 1024)).astype(BF16)
    sgb_ref[...] = _sigmoid(seg(_SEG_GB, 1024)).astype(BF16)
    ka = rope(seg(_SEG_KA, LANES))
    ka_ref[...] = ka.astype(BF16)
    ki_ref[...] = rope(seg(_SEG_KI, LANES)).astype(BF16)

    z = seg(_SEG_AUX, LANES)
    zf = z + fbias_ref[...]
    logf = jnp.minimum(zf, 0.0) - jnp.log(1.0 + jnp.exp(-jnp.abs(zf)))
    is_f = (lane >= N_HEADS) & (lane < 2 * N_HEADS)
    logf = jnp.where(is_f, logf, 0.0)

    def split3(v):
        hi = v.astype(BF16)
        r1 = v - hi.astype(F32)
        mid = r1.astype(BF16)
        return hi, mid, (r1 - mid.astype(F32)).astype(BF16)

    rr = lax.broadcasted_iota(I32, (tm, tm), 0)
    cc = lax.broadcasted_iota(I32, (tm, tm), 1)
    tri = jnp.where(cc <= rr, 1.0, 0.0).astype(BF16)
    p_hi, p_mid, p_lo = split3(logf)

    @pl.when(pl.program_id(1) == 0)
    def _():
        carry_scr[...] = jnp.zeros_like(carry_scr)

    parts = _dot(tri, jnp.concatenate([p_hi, p_mid, p_lo], axis=1))
    cum = parts[:, 0:LANES] + parts[:, LANES:2 * LANES] + parts[:, 2 * LANES:3 * LANES] + carry_scr[...]
    carry_scr[...] = cum[tm - 1:tm, :]
    cum2 = cum * LOG2E
    aux = jnp.where(lane < N_HEADS, z * (N_HEADS ** -0.5), jnp.where(is_f, cum2, 0.0))

    kb = seg(_SEG_KB, 512)
    first = lane < HEAD_DIM

    def half_norms(blk):
        sq = blk * blk
        return (jnp.sum(jnp.where(first, sq, 0.0), axis=-1, keepdims=True),
                jnp.sum(jnp.where(first, 0.0, sq), axis=-1, keepdims=True))

    for p in range(N_HEADS // 2):
        n_even, n_odd = half_norms(kb[:, p * LANES:(p + 1) * LANES])
        aux = jnp.where(lane == 2 * N_HEADS + 2 * p, n_even, jnp.where(lane == 2 * N_HEADS + 2 * p + 1, n_odd, aux))
    aux = jnp.where(lane == 3 * N_HEADS, half_norms(ka)[0], aux)
    auxt_ref[0] = aux.T[0:4 * N_HEADS, :]

    c_hi, c_mid, c_lo = split3(cum2)
    kaug = _dot(jnp.concatenate([c_hi, c_mid, c_lo], axis=1), sel_ref[...]) + ones_ref[...]
    for p in range(N_HEADS // 2):
        kb_ref[:, p * FOX_K:p * FOX_K + LANES] = kb[:, p * LANES:(p + 1) * LANES].astype(BF16)
        kb_ref[:, p * FOX_K + LANES:(p + 1) * FOX_K] = kaug[:, p * LANES:(p + 1) * LANES].astype(BF16)


def _fox_routing_constants():
    sel = np.zeros((3, LANES, (N_HEADS // 2) * LANES), np.float32)
    ones = np.zeros((1, (N_HEADS // 2) * LANES), np.float32)
    for h in range(N_HEADS):
        for t in range(3):
            sel[t, N_HEADS + h, (h // 2) * LANES + 3 + 3 * (h % 2) + t] = -1.0
    for p in range(N_HEADS // 2):
        ones[0, p * LANES:p * LANES + 3] = 1.0
    return jnp.asarray(sel.reshape(3 * LANES, -1), BF16), jnp.asarray(ones, F32)


def _inproj_call(x2, pos2, pos_rows, mod3, g_mix, w_rows, w_t, invf, invf8, fbias, batch, seq):
    assert TM_IN == TK_FOX == TK_DSA, "value tiles are written per in-projection step"
    n = x2.shape[0]
    tpb = seq // TM_IN
    row = lambda b, s: (b * tpb + s, 0)
    rows = lambda w: (jax.ShapeDtypeStruct((n, w), BF16), pl.BlockSpec((TM_IN, w), row))
    chan = lambda c: (jax.ShapeDtypeStruct((batch, c, seq), BF16), pl.BlockSpec((1, c, TM_IN), lambda b, s: (b, 0, s)))
    tile = lambda c: (jax.ShapeDtypeStruct((batch, tpb, c, TM_IN), BF16),
                      pl.BlockSpec((1, 1, c, TM_IN), lambda b, s: (b, s, 0, 0)))
    aux_t = (jax.ShapeDtypeStruct((batch, 4 * N_HEADS, seq), F32),
             pl.BlockSpec((1, 4 * N_HEADS, TM_IN), lambda b, s: (b, 0, s)))
    outs = [chan(512), chan(512), chan(512), rows((N_HEADS // 2) * FOX_K), tile(512), rows(1024), rows(1024),
            rows(LANES), tile(HEAD_DIM), rows(LANES), aux_t]
    sel, ones = _fox_routing_constants()
    const = lambda a: pl.BlockSpec(a.shape, lambda b, s: (0,) * a.ndim)
    return pl.pallas_call(
        _inproj_kernel,
        grid=(batch, tpb),
        in_specs=[pl.BlockSpec((TM_IN, D_MODEL), row),
                  pl.BlockSpec((TM_IN, 1), row),
                  pl.BlockSpec((1, 1, TM_IN), lambda b, s: (b, 0, s)),
                  pl.BlockSpec((1, 6, D_MODEL), lambda b, s: (b, 0, 0)),
                  const(g_mix), const(w_rows), const(w_t), const(invf), const(invf8), const(fbias),
                  const(sel), const(ones)],
        out_specs=[o[1] for o in outs],
        out_shape=[o[0] for o in outs],
        scratch_shapes=[pltpu.VMEM((TM_IN, D_MODEL), BF16), pltpu.VMEM((1, LANES), F32)],
        compiler_params=_cparams(("arbitrary", "arbitrary")),
        name="inproj",
    )(x2, pos2, pos_rows, mod3, g_mix, w_rows, w_t, invf, invf8, fbias, sel, ones)


def _softmax_max(s, m_ref):
    m_ref[...] = jnp.maximum(m_ref[...], jnp.max(s, axis=0, keepdims=True))


def _softmax_accumulate(s, m_ref, l_ref, acc_ref, v_t):
    p = jnp.exp2(s - m_ref[...])
    l_ref[...] = l_ref[...] + jnp.sum(p, axis=0, keepdims=True)
    acc_ref[...] = acc_ref[...] + _dot(v_t, p.astype(BF16))


def _softmax_scratch(tq):
    return ([pltpu.VMEM((1, tq), F32) for _ in range(2 * N_HEADS)]
            + [pltpu.VMEM((HEAD_DIM, tq), F32) for _ in range(N_HEADS)])


def _softmax_split(refs):
    return refs[:N_HEADS], refs[N_HEADS:2 * N_HEADS], refs[2 * N_HEADS:3 * N_HEADS]


def _softmax_init(m_refs, l_refs, acc_refs):
    for h in range(N_HEADS):
        m_refs[h][...] = jnp.full(m_refs[h].shape, NEG_INF, F32)
        l_refs[h][...] = jnp.zeros(l_refs[h].shape, F32)
        acc_refs[h][...] = jnp.zeros(acc_refs[h].shape, F32)


def _softmax_finish(o_ref, l_refs, acc_refs):
    for h in range(N_HEADS):
        o_ref[0, h * HEAD_DIM:(h + 1) * HEAD_DIM, :] = (acc_refs[h][...] * (1.0 / l_refs[h][...])).astype(BF16)


def _logit_bounds(qt_ref, kn2, slack):
    k_max = jnp.sqrt(jnp.max(kn2, axis=-1, keepdims=True))
    bounds = []
    worst = jnp.float32(0.0)
    for h in range(N_HEADS):
        qh = qt_ref[0, h * HEAD_DIM:(h + 1) * HEAD_DIM, :].astype(F32)
        q_norm = jnp.sqrt(jnp.sum(qh * qh, axis=0, keepdims=True))
        b = q_norm * k_max[h:h + 1, :] * BOUND_MARGIN + slack
        bounds.append(b)
        worst = jnp.maximum(worst, jnp.max(b))
    return bounds, worst <= BOUND_SAFE


def _fox_kernel(qt_ref, cumt_ref, kn_ref, kaug_ref, vt_ref, o_ref, rhs_scr, *softmax_refs):
    m_scr, l_scr, acc_scr = _softmax_split(softmax_refs)
    tq, tk = TQ_FOX, TK_FOX
    i = pl.program_id(1)
    q0 = i * tq

    cq = cumt_ref[0]
    c_hi = cq.astype(BF16).astype(F32)
    c_r = cq - c_hi
    c_mid = c_r.astype(BF16).astype(F32)
    c_lo = c_r - c_mid
    row_q = lax.broadcasted_iota(I32, (LANES, tq), 0)
    row_a = lax.broadcasted_iota(I32, (FOX_AUG_ROWS, 2 * tq), 0)
    second = lax.broadcasted_iota(I32, (FOX_AUG_ROWS, 2 * tq), 1) >= tq
    for p in range(N_HEADS // 2):
        qp = qt_ref[0, p * LANES:(p + 1) * LANES, :]
        zq = jnp.zeros_like(qp)
        rhs_scr[p, 0:LANES, 0:tq] = jnp.where(row_q < HEAD_DIM, qp, zq)
        rhs_scr[p, 0:LANES, tq:2 * tq] = jnp.where(row_q >= HEAD_DIM, qp, zq)
        pair_row = lambda a: jnp.concatenate([a[2 * p:2 * p + 1, :], a[2 * p + 1:2 * p + 2, :]], axis=1)
        aug = jnp.where(row_a == 0, pair_row(c_hi),
              jnp.where(row_a == 1, pair_row(c_mid),
              jnp.where(row_a == 2, pair_row(c_lo),
              jnp.where((row_a >= 3) & (row_a < 6), jnp.where(second, 0.0, 1.0),
              jnp.where((row_a >= 6) & (row_a < 9), jnp.where(second, 1.0, 0.0), 0.0)))))
        rhs_scr[p, LANES:LANES + FOX_AUG_ROWS, :] = aug.astype(BF16)
        rhs_scr[p, LANES + FOX_AUG_ROWS:FOX_K, :] = jnp.zeros((FOX_K - LANES - FOX_AUG_ROWS, 2 * tq), BF16)

    _softmax_init(m_scr, l_scr, acc_scr)
    key_j = lax.broadcasted_iota(I32, (tk, tq), 0)
    qry_i = q0 + lax.broadcasted_iota(I32, (tk, tq), 1)

    def tile(kt, masked, second_pass):
        k0 = pl.multiple_of(kt * tk, tk)
        for p in range(N_HEADS // 2):
            st = _dot(kaug_ref[0, pl.ds(k0, tk), p * FOX_K:(p + 1) * FOX_K], rhs_scr[p])
            for hh in range(2):
                h = 2 * p + hh
                s = st[:, hh * tq:(hh + 1) * tq]
                if masked:
                    s = jnp.where(k0 + key_j <= qry_i, s, NEG_INF)
                if second_pass:
                    _softmax_accumulate(s, m_scr[h], l_scr[h], acc_scr[h],
                                        vt_ref[0, kt, h * HEAD_DIM:(h + 1) * HEAD_DIM, :])
                else:
                    _softmax_max(s, m_scr[h])

    n_full = i * (tq // tk)

    def sweep(second_pass):
        def full_tile(kt, carry):
            tile(kt, False, second_pass)
            return carry

        lax.fori_loop(0, n_full, full_tile, 0)
        for d in range(tq // tk):
            tile(n_full + d, True, second_pass)

    bounds, safe = _logit_bounds(qt_ref, kn_ref[0], FOX_BIAS_SLACK)

    @pl.when(safe)
    def _():
        for h in range(N_HEADS):
            m_scr[h][...] = bounds[h]
        sweep(True)

    @pl.when(jnp.logical_not(safe))
    def _():
        sweep(False)
        sweep(True)

    _softmax_finish(o_ref, l_scr, acc_scr)


def _fox_call(qbt, aux_t, kaug3, vbt4):
    batch, _, seq = qbt.shape
    return pl.pallas_call(
        _fox_kernel,
        grid=(batch, seq // TQ_FOX),
        in_specs=[pl.BlockSpec((1, 512, TQ_FOX), lambda b, i: (b, 0, i)),
                  pl.BlockSpec((1, N_HEADS, TQ_FOX), lambda b, i: (b, 1, i)),
                  pl.BlockSpec((1, N_HEADS, seq), lambda b, i: (b, 2, 0)),
                  pl.BlockSpec((1, seq, (N_HEADS // 2) * FOX_K), lambda b, i: (b, 0, 0)),
                  pl.BlockSpec((1, seq // TK_FOX, 512, TK_FOX), lambda b, i: (b, 0, 0, 0))],
        out_specs=pl.BlockSpec((1, 512, TQ_FOX), lambda b, i: (b, 0, i)),
        out_shape=jax.ShapeDtypeStruct((batch, 512, seq), BF16),
        scratch_shapes=[pltpu.VMEM((N_HEADS // 2, FOX_K, 2 * TQ_FOX), BF16)] + _softmax_scratch(TQ_FOX),
        compiler_params=_cparams(("arbitrary", "arbitrary")),
        name="fox",
    )(qbt, aux_t, aux_t, kaug3, vbt4)


def _bit_transpose32(words):
    a = list(words)
    j, mask = 16, 0x0000FFFF
    while j:
        mask_i = int(np.array(mask, np.uint32).view(np.int32))
        k = 0
        while k < 32:
            t = (a[k] ^ lax.shift_right_logical(a[k + j], j)) & mask_i
            a[k] = a[k] ^ t
            a[k + j] = a[k + j] ^ lax.shift_left(t, j)
            k = (k + j + 1) & ~j
        j >>= 1
        mask = (mask ^ (mask << j)) & 0xFFFFFFFF
    return a


def _dsa_kernel(qat_ref, qit_ref, wit_ref, kn_ref, ki_ref, ka_ref, vat_ref, o_ref,
                aqi_scr, aqa_scr, sc_scr, bias_scr, plane_scr, alive_scr, thr_scr, *softmax_refs, seq):
    m_scr, l_scr, acc_scr = _softmax_split(softmax_refs)
    tq, tk = TQ_DSA, TK_DSA
    blk_per_tile = tk // RADIX_BLOCK
    n_blocks = seq // RADIX_BLOCK
    i = pl.program_id(1)
    q0 = i * tq
    nk = lax.shift_right_logical(q0 + tq + tk - 1, int(np.log2(tk)))
    n_rest = (seq - nk * tk).astype(F32)

    row_q = lax.broadcasted_iota(I32, (LANES, tq), 0)
    for h in range(N_HEADS):
        keep = (row_q < HEAD_DIM) if h % 2 == 0 else (row_q >= HEAD_DIM)
        rows = slice((h // 2) * LANES, (h // 2 + 1) * LANES)
        qi_blk = qit_ref[0, rows, :]
        qa_blk = qat_ref[0, rows, :]
        aqi_scr[:, h * tq:(h + 1) * tq] = jnp.where(keep, qi_blk, jnp.zeros_like(qi_blk))
        aqa_scr[:, h * tq:(h + 1) * tq] = jnp.where(keep, qa_blk, jnp.zeros_like(qa_blk))

    w = wit_ref[0]
    key_j = lax.broadcasted_iota(I32, (tk, tq), 0)
    qry_chunk = lax.shift_right_logical(q0 + lax.broadcasted_iota(I32, (tk, tq), 1), CHUNK_SHIFT)

    def admissible(k0):
        return lax.shift_right_logical(k0 + key_j, CHUNK_SHIFT) <= qry_chunk

    def p1(kt, carry, masked):
        k0 = pl.multiple_of(kt * tk, tk)
        rel = _dot(ki_ref[0, pl.ds(k0, tk), :], aqi_scr[...])
        sc = w[0:1, :] * jnp.maximum(rel[:, 0:tq], 0.0)
        for h in range(1, N_HEADS):
            sc = sc + w[h:h + 1, :] * jnp.maximum(rel[:, h * tq:(h + 1) * tq], 0.0)
        if masked:
            sc = jnp.where(admissible(k0), sc, NEG_INF)
        sc = jnp.where(sc == 0.0, 0.0, sc)
        sc_scr[kt] = sc
        b = pltpu.bitcast(sc, I32)
        key = b ^ (lax.shift_right_arithmetic(b, 31) & 0x7FFFFFFF)
        for half in range(blk_per_tile):
            base = half * RADIX_BLOCK
            planes = _bit_transpose32([key[base + SUBLANES * m:base + SUBLANES * (m + 1), :] ^ _INT_MIN
                                       for m in range(32)])
            bl = kt * blk_per_tile + half
            for p in range(32):
                plane_scr[bl, p] = planes[p]
            alive_scr[bl] = jnp.full((SUBLANES, tq), -1, I32)
        return carry

    n_unmasked = lax.shift_right_logical(q0, int(np.log2(tk)))
    lax.fori_loop(0, n_unmasked, functools.partial(p1, masked=False), 0)
    lax.fori_loop(n_unmasked, nk, functools.partial(p1, masked=True), 0)

    def clear_block(bl, carry):
        plane_scr[bl] = jnp.zeros((32, SUBLANES, tq), I32)
        alive_scr[bl] = jnp.zeros((SUBLANES, tq), I32)
        return carry

    n_held = nk * blk_per_tile
    half_blocks = n_blocks // 2
    fits_half = n_held <= half_blocks
    lax.fori_loop(n_held, jnp.where(fits_half, half_blocks, n_blocks), clear_block, 0)

    sent_u = jnp.int32(_SENT_KEY ^ _INT_MIN)
    kf = float(TOPK)

    def radix_select(n_walk):
        def bit_step(p, carry):
            k_rem, thr_bits, rest_alive = carry
            shift = 31 - p
            sent_bit = lax.shift_right_logical(sent_u, shift) & 1
            ones = [alive_scr[bl] & plane_scr[bl, p] for bl in range(n_walk)]
            tot = lax.population_count(ones[0])
            for bl in range(1, n_walk):
                tot = tot + lax.population_count(ones[bl])
            cnt = jnp.sum(tot.astype(F32), axis=0, keepdims=True)
            cnt = cnt + jnp.where((rest_alive != 0) & (sent_bit != 0), n_rest, 0.0)
            take1 = cnt >= k_rem
            for bl in range(n_walk):
                alive_scr[bl] = jnp.where(take1, ones[bl], alive_scr[bl] & ~plane_scr[bl, p])
            k_rem = jnp.where(take1, k_rem, k_rem - cnt)
            thr_bits = thr_bits | jnp.where(take1, lax.shift_left(jnp.int32(1), shift), 0)
            rest_alive = jnp.where(take1 == (sent_bit != 0), rest_alive, 0)
            return k_rem, thr_bits, rest_alive

        _, thr_bits, _ = lax.fori_loop(
            0, 32, bit_step,
            (jnp.full((1, tq), kf, F32), jnp.zeros((1, tq), I32), jnp.ones((1, tq), I32)))
        thr_scr[...] = thr_bits

    pl.when(fits_half)(lambda: radix_select(half_blocks))
    pl.when(jnp.logical_not(fits_half))(lambda: radix_select(n_blocks))
    thr_key = thr_scr[...] ^ _INT_MIN
    thr0 = pltpu.bitcast(thr_key ^ (lax.shift_right_arithmetic(thr_key, 31) & 0x7FFFFFFF), F32)

    def rank_counts(thr):
        def body(kt, c):
            sc = sc_scr[kt]
            return (c[0] + jnp.sum(jnp.where(sc >= thr, 1.0, 0.0), axis=0, keepdims=True),
                    c[1] + jnp.sum(jnp.where(sc > thr, 1.0, 0.0), axis=0, keepdims=True))
        z = jnp.zeros((1, tq), F32)
        n_ge, n_gt = lax.fori_loop(0, nk, body, (z, z))
        return (n_ge + jnp.where(thr <= NEG_INF, n_rest, 0.0), n_gt + jnp.where(thr < NEG_INF, n_rest, 0.0))

    def misplaced(state):
        _, n_ge, n_gt = state
        return jnp.max(jnp.where((n_ge < kf) | (n_gt >= kf), 1.0, 0.0)) > 0.0

    def step_threshold(state):
        thr, n_ge, n_gt = state

        def body(kt, c):
            sc = sc_scr[kt]
            below = jnp.max(jnp.where(sc < thr, sc, -jnp.inf), axis=0, keepdims=True)
            above = jnp.min(jnp.where(sc > thr, sc, jnp.inf), axis=0, keepdims=True)
            return jnp.maximum(c[0], below), jnp.minimum(c[1], above)

        below, above = lax.fori_loop(0, nk, body, (jnp.full((1, tq), -jnp.inf, F32), jnp.full((1, tq), jnp.inf, F32)))
        has_rest = n_rest > 0.0
        below = jnp.where(has_rest & (thr > NEG_INF), jnp.maximum(below, NEG_INF), below)
        above = jnp.where(has_rest & (thr < NEG_INF), jnp.minimum(above, NEG_INF), above)
        thr = jnp.where(n_ge < kf, below, jnp.where(n_gt >= kf, above, thr))
        return (thr,) + rank_counts(thr)

    thr, n_ge, n_gt = lax.while_loop(misplaced, step_threshold, (thr0,) + rank_counts(thr0))
    need = kf - n_gt
    tie_free = jnp.min(jnp.where(n_ge == kf, 1.0, 0.0)) > 0.0

    tr = lax.broadcasted_iota(I32, (tk, tk), 0)
    tc = lax.broadcasted_iota(I32, (tk, tk), 1)
    tri = jnp.where(tc <= tr, 1.0, 0.0).astype(BF16)
    _softmax_init(m_scr, l_scr, acc_scr)

    def selection_bias(kt, tie):
        sc = sc_scr[kt]
        eq = sc == thr
        pref = _dot(tri, jnp.where(eq, 1.0, 0.0).astype(BF16)) + tie
        sel = (sc > thr) | (eq & (pref <= need))
        return jnp.where(sel & admissible(kt * tk), 0.0, NEG_INF), pref[tk - 1:tk, :]

    def logits(kt):
        return _dot(ka_ref[0, pl.ds(pl.multiple_of(kt * tk, tk), tk), :], aqa_scr[...])

    def accumulate(kt, bias):
        logit = logits(kt)
        v_t = vat_ref[0, kt]
        for h in range(N_HEADS):
            _softmax_accumulate(logit[:, h * tq:(h + 1) * tq] + bias, m_scr[h], l_scr[h], acc_scr[h], v_t)

    def p3_single(kt, tie):
        bias, tie = selection_bias(kt, tie)
        accumulate(kt, bias)
        return tie

    def p3_single_tie_free(kt, carry):
        accumulate(kt, jnp.where(sc_scr[kt] >= thr, 0.0, NEG_INF))
        return carry

    def p3_max(kt, tie):
        bias, tie = selection_bias(kt, tie)
        bias_scr[kt] = bias
        logit = logits(kt)
        for h in range(N_HEADS):
            _softmax_max(logit[:, h * tq:(h + 1) * tq] + bias, m_scr[h])
        return tie

    def p3_accumulate(kt, carry):
        accumulate(kt, bias_scr[kt])
        return carry

    kn2 = jnp.broadcast_to(kn_ref[0, 0:1, :], (N_HEADS, seq))
    bounds, safe = _logit_bounds(qat_ref, kn2, 0.0)
    no_tie = jnp.zeros((1, tq), F32)

    @pl.when(safe)
    def _():
        for h in range(N_HEADS):
            m_scr[h][...] = bounds[h]

    @pl.when(safe & tie_free)
    def _():
        lax.fori_loop(0, nk, p3_single_tie_free, 0)

    @pl.when(safe & jnp.logical_not(tie_free))
    def _():
        lax.fori_loop(0, nk, p3_single, no_tie)

    @pl.when(jnp.logical_not(safe))
    def _():
        lax.fori_loop(0, nk, p3_max, no_tie)
        lax.fori_loop(0, nk, p3_accumulate, 0)

    _softmax_finish(o_ref, l_scr, acc_scr)


def _dsa_call(qat, qit, aux_t, ki3, ka3, vat4):
    batch, _, seq = qat.shape
    qspec = pl.BlockSpec((1, 512, TQ_DSA), lambda b, i: (b, 0, i))
    kspec = pl.BlockSpec((1, seq, LANES), lambda b, i: (b, 0, 0))
    return pl.pallas_call(
        functools.partial(_dsa_kernel, seq=seq),
        grid=(batch, seq // TQ_DSA),
        in_specs=[qspec, qspec,
                  pl.BlockSpec((1, N_HEADS, TQ_DSA), lambda b, i: (b, 0, i)),
                  pl.BlockSpec((1, N_HEADS, seq), lambda b, i: (b, 3, 0)),
                  kspec, kspec,
                  pl.BlockSpec((1, seq // TK_DSA, HEAD_DIM, TK_DSA), lambda b, i: (b, 0, 0, 0))],
        out_specs=qspec,
        out_shape=jax.ShapeDtypeStruct((batch, 512, seq), BF16),
        scratch_shapes=[pltpu.VMEM((LANES, N_HEADS * TQ_DSA), BF16),
                        pltpu.VMEM((LANES, N_HEADS * TQ_DSA), BF16),
                        pltpu.VMEM((seq // TK_DSA, TK_DSA, TQ_DSA), F32),
                        pltpu.VMEM((seq // TK_DSA, TK_DSA, TQ_DSA), F32),
                        pltpu.VMEM((seq // RADIX_BLOCK, 32, SUBLANES, TQ_DSA), I32),
                        pltpu.VMEM((seq // RADIX_BLOCK, SUBLANES, TQ_DSA), I32),
                        pltpu.VMEM((1, TQ_DSA), I32)] + _softmax_scratch(TQ_DSA),
        compiler_params=_cparams(("arbitrary", "arbitrary")),
        name="dsa",
    )(qat, qit, aux_t, aux_t, ki3, ka3, vat4)


def _outproj_kernel(oa_ref, ob_ref, sga_ref, sgb_ref, x_ref, mod_ref, gffn_ref,
                    woa_ref, wob_ref, wo_ref, wr_hi_ref, wr_lo_ref, br_ref,
                    x1_ref, h2_ref, lg_ref):
    tm = TM_OUT
    tn_dot = lambda a, b: lax.dot_general(a, b, (((0,), (0,)), ((), ())), preferred_element_type=F32)
    ya = tn_dot(oa_ref[0], woa_ref[...])
    yb = tn_dot(ob_ref[0], wob_ref[...])
    merged = (sga_ref[...].astype(F32) * ya + sgb_ref[...].astype(F32) * yb).astype(BF16)
    mix = _dot(merged, wo_ref[...])
    gt1 = mod_ref[0, 2:3, :]
    sh2 = mod_ref[0, 3:4, :]
    sc2 = mod_ref[0, 4:5, :]
    x1 = x_ref[...] + gt1 * mix
    x1_ref[...] = x1
    var = jnp.mean(x1 * x1, axis=-1, keepdims=True)
    h2 = x1 * lax.rsqrt(var + RMS_EPS) * gffn_ref[...] * (1.0 + sc2) + sh2
    for c in range(D_MODEL // LANES):
        h2_ref[pl.ds(c, tm, stride=SUBLANES), :] = h2[:, c * LANES:(c + 1) * LANES]
    hi = h2.astype(BF16)
    lo = (h2 - hi.astype(F32)).astype(BF16)
    r = _dot(jnp.concatenate([hi, lo], axis=0), jnp.concatenate([wr_hi_ref[...], wr_lo_ref[...]], axis=1))
    lg_ref[...] = r[0:tm, 0:LANES] + r[0:tm, LANES:2 * LANES] + r[tm:2 * tm, 0:LANES] + br_ref[...]


def _outproj_call(oa2, ob2, sga, sgb, x2, mod3, g_ffn, woa, wob, wo, wr_hi, wr_lo, br, seq):
    n = x2.shape[0]
    tpb = seq // TM_OUT
    row = lambda w: pl.BlockSpec((TM_OUT, w), lambda i: (i, 0))
    full = lambda a: pl.BlockSpec(a.shape, lambda i: (0,) * a.ndim)
    return pl.pallas_call(
        _outproj_kernel,
        grid=(n // TM_OUT,),
        in_specs=[pl.BlockSpec((1, 512, TM_OUT), lambda i: (i // tpb, 0, i % tpb)),
                  pl.BlockSpec((1, 512, TM_OUT), lambda i: (i // tpb, 0, i % tpb)),
                  row(1024), row(1024), row(D_MODEL),
                  pl.BlockSpec((1, 6, D_MODEL), lambda i: (i // tpb, 0, 0)),
                  full(g_ffn), full(woa), full(wob), full(wo), full(wr_hi), full(wr_lo), full(br)],
        out_specs=[row(D_MODEL),
                   pl.BlockSpec((TM_OUT * SUBLANES, LANES), lambda i: (i, 0)),
                   row(LANES)],
        out_shape=[jax.ShapeDtypeStruct((n, D_MODEL), F32),
                   jax.ShapeDtypeStruct((n * SUBLANES, LANES), F32),
                   jax.ShapeDtypeStruct((n, LANES), F32)],
        compiler_params=_cparams(("arbitrary",)),
        name="outproj",
    )(oa2, ob2, sga, sgb, x2, mod3, g_ffn, woa, wob, wo, wr_hi, wr_lo, br)


def _route_rows(lg):
    tm = lg.shape[0]
    lane = lax.broadcasted_iota(I32, (tm, LANES), 1)
    big = jnp.int32(LANES)
    ninf = -jnp.inf
    gmask = (lane >= N_EXPERTS) & (lane < N_EXPERTS + N_GROUPS)
    g = jnp.where(gmask, lg, ninf)
    gmax = jnp.max(g, axis=-1, keepdims=True)
    grp = jnp.min(jnp.where(g == gmax, lane - N_EXPERTS, big), axis=-1, keepdims=True)
    p_grp = 1.0 / jnp.sum(jnp.where(gmask, jnp.exp(lg - gmax), 0.0), axis=-1, keepdims=True)
    lo = grp * EXPERTS_PER_GROUP
    emask = (lane >= lo) & (lane < lo + EXPERTS_PER_GROUP)
    ev = jnp.where(emask, lg, ninf)
    v0 = jnp.max(ev, axis=-1, keepdims=True)
    i0 = jnp.min(jnp.where(emask & (ev == v0), lane, big), axis=-1, keepdims=True)
    rest = emask & (lane != i0)
    ev1 = jnp.where(rest, lg, ninf)
    v1 = jnp.max(ev1, axis=-1, keepdims=True)
    i1 = jnp.min(jnp.where(rest & (ev1 == v1), lane, big), axis=-1, keepdims=True)
    e1 = jnp.exp(v1 - v0)
    w0 = p_grp / (1.0 + e1)
    w1 = p_grp * e1 / (1.0 + e1)
    rt = jnp.where(lane == 0, i0.astype(F32),
                   jnp.where(lane == 1, i1.astype(F32),
                             jnp.where(lane == 2, w0, jnp.where(lane == 3, w1, 0.0))))
    return rt, lane == i0, lane == i1


def _expert_onehots(rt, tm):
    lane = lax.broadcasted_iota(I32, (tm, LANES), 1)
    lane_f = lane.astype(F32)
    e0 = jnp.sum(jnp.where(lane == 0, rt, 0.0), axis=-1, keepdims=True)
    e1 = jnp.sum(jnp.where(lane == 1, rt, 0.0), axis=-1, keepdims=True)
    return lane, lane_f == e0, lane_f == e1


def _rank_rows(is0, is1, seen_before):
    tm = is0.shape[0]
    lane = lax.broadcasted_iota(I32, (tm, LANES), 1)
    hits = jnp.where(is0, 1.0, 0.0) + jnp.where(is1, 1.0, 0.0)
    rr = lax.broadcasted_iota(I32, (tm, tm), 0)
    cc = lax.broadcasted_iota(I32, (tm, tm), 1)
    before = jnp.where(cc < rr, 1.0, 0.0).astype(BF16)
    seen = _dot(before, hits.astype(BF16)) + seen_before
    r0 = jnp.sum(jnp.where(is0, seen, 0.0), axis=-1, keepdims=True)
    r1 = jnp.sum(jnp.where(is1, seen, 0.0), axis=-1, keepdims=True)
    rk = jnp.where(lane == 0, r0, jnp.where(lane == 1, r1, 0.0))
    return rk, seen_before + jnp.sum(hits, axis=0, keepdims=True)


def _route_rank_kernel(lg_ref, rt_ref, rk_ref, cnt_ref, seen_scr):
    @pl.when(pl.program_id(0) == 0)
    def _():
        seen_scr[...] = jnp.zeros_like(seen_scr)

    rt, is0, is1 = _route_rows(lg_ref[...])
    rt_ref[...] = rt
    rk, seen = _rank_rows(is0, is1, seen_scr[...])
    rk_ref[...] = rk
    seen_scr[...] = seen
    cnt_ref[...] = jnp.broadcast_to(seen, cnt_ref.shape)


def _route_rank_call(lg):
    n = lg.shape[0]
    spec = pl.BlockSpec((TM_ROUTE, LANES), lambda i: (i, 0))
    return pl.pallas_call(
        _route_rank_kernel, grid=(n // TM_ROUTE,), in_specs=[spec],
        out_specs=[spec, spec, pl.BlockSpec((SUBLANES, LANES), lambda i: (0, 0))],
        out_shape=[jax.ShapeDtypeStruct((n, LANES), F32), jax.ShapeDtypeStruct((n, LANES), F32),
                   jax.ShapeDtypeStruct((SUBLANES, LANES), F32)],
        scratch_shapes=[pltpu.VMEM((1, LANES), F32)],
        compiler_params=_cparams(("arbitrary",)), name="route_rank",
    )(lg)


def _dest_kernel(rt_ref, rk_ref, cnt_ref, dd_ref, be_ref, nu_ref, end_ref):
    tm = TM_ROUTE
    lane, is0, is1 = _expert_onehots(rt_ref[...], tm)
    blocks = jnp.floor((cnt_ref[...] + (BLK_E - 1)) * (1.0 / BLK_E))
    er = lax.broadcasted_iota(I32, (LANES, LANES), 0)
    ec = lax.broadcasted_iota(I32, (LANES, LANES), 1)
    upto = jnp.where(er <= ec, 1.0, 0.0).astype(BF16)
    bend = _dot(blocks.astype(BF16), upto)
    pstart = (bend[0:1, :] - blocks[0:1, :]) * BLK_E
    rk = rk_ref[...]
    r0 = jnp.sum(jnp.where(lane == 0, rk, 0.0), axis=-1, keepdims=True)
    r1 = jnp.sum(jnp.where(lane == 1, rk, 0.0), axis=-1, keepdims=True)
    d0 = jnp.sum(jnp.where(is0, pstart, 0.0), axis=-1, keepdims=True) + r0
    d1 = jnp.sum(jnp.where(is1, pstart, 0.0), axis=-1, keepdims=True) + r1
    dd_ref[...] = jnp.where(lane == 0, d0, jnp.where(lane == 1, d1, 0.0)).astype(I32)
    nb = be_ref.shape[0]
    blk = lax.broadcasted_iota(I32, (nb, LANES), 0).astype(F32)
    lane_b = lax.broadcasted_iota(I32, (nb, LANES), 1)
    ended = jnp.where((lane_b < N_EXPERTS) & (bend[0:1, :] <= blk), 1.0, 0.0)
    be = jnp.minimum(jnp.sum(ended, axis=-1, keepdims=True), float(N_EXPERTS - 1))
    be_ref[...] = jnp.broadcast_to(be, be_ref.shape).astype(I32)
    lane_c = lax.broadcasted_iota(I32, (SUBLANES, LANES), 1)
    used = jnp.sum(jnp.where(lane_c == N_EXPERTS - 1, bend, 0.0), axis=-1, keepdims=True)
    nu_ref[...] = jnp.broadcast_to(used, nu_ref.shape).astype(I32)
    end_ref[...] = (bend * BLK_E).astype(I32)


def _dest_call(rt, rk, cnt, n_blocks):
    n = rt.shape[0]
    nb_pad = -(-n_blocks // SUBLANES) * SUBLANES
    spec = pl.BlockSpec((TM_ROUTE, LANES), lambda i: (i, 0))
    const = lambda rows: pl.BlockSpec((rows, LANES), lambda i: (0, 0))
    return pl.pallas_call(
        _dest_kernel, grid=(n // TM_ROUTE,), in_specs=[spec, spec, const(SUBLANES)],
        out_specs=[spec, const(nb_pad), const(SUBLANES), const(SUBLANES)],
        out_shape=[jax.ShapeDtypeStruct((n, LANES), I32), jax.ShapeDtypeStruct((nb_pad, LANES), I32),
                   jax.ShapeDtypeStruct((SUBLANES, LANES), I32), jax.ShapeDtypeStruct((SUBLANES, LANES), I32)],
        compiler_params=_cparams(("arbitrary",)), name="dest",
    )(rt, rk, cnt)


def _slab_copy(src, src_row, dst, dst_row, sem):
    return pltpu.make_async_copy(src.at[pl.ds(pl.multiple_of(src_row * SUBLANES, SUBLANES), SUBLANES)],
                                 dst.at[pl.ds(pl.multiple_of(dst_row * SUBLANES, SUBLANES), SUBLANES)], sem)


def _dispatch_kernel(dest_ref, seg_end_ref, h2_ref, xs_hbm, zero_scr, sem, zsem):
    tm = TM_DISPATCH
    base = pl.program_id(0) * tm

    @pl.when(pl.program_id(0) == 0)
    def _():
        zero_scr[...] = jnp.zeros_like(zero_scr)
        clears = [pltpu.make_async_copy(
            zero_scr, xs_hbm.at[pl.ds(pl.multiple_of(jnp.maximum(seg_end_ref[e] - BLK_E, 0) * SUBLANES, SUBLANES),
                                      BLK_E * SUBLANES)], zsem) for e in range(N_EXPERTS)]
        for c in clears:
            c.start()
        for c in clears:
            c.wait()
        n_blocks = xs_hbm.shape[0] // (BLK_E * SUBLANES)
        first_free = lax.shift_right_logical(seg_end_ref[N_EXPERTS - 1], int(np.log2(BLK_E)))
        def tail_clear(b):
            row = pl.multiple_of((first_free + b) * (BLK_E * SUBLANES), BLK_E * SUBLANES)
            return pltpu.make_async_copy(zero_scr, xs_hbm.at[pl.ds(row, BLK_E * SUBLANES)], zsem)

        for b in range(N_EXPERTS):
            pl.when(first_free + b < n_blocks)(lambda b=b: tail_clear(b).start())
        for b in range(N_EXPERTS):
            pl.when(first_free + b < n_blocks)(lambda b=b: tail_clear(b).wait())

    def issue(r, _):
        for k in range(2):
            _slab_copy(h2_ref, r, xs_hbm, dest_ref[2 * (base + r) + k], sem).start(priority=k)
        return 0

    lax.fori_loop(0, tm, issue, 0, unroll=GATHER_UNROLL)
    for _ in range(2):
        pltpu.make_async_copy(h2_ref, xs_hbm.at[pl.ds(0, tm * SUBLANES)], sem).wait()


def _dispatch_call(dest, seg_end, h2s, n_slots):
    n = h2s.shape[0] // SUBLANES
    return pl.pallas_call(
        _dispatch_kernel,
        grid_spec=pltpu.PrefetchScalarGridSpec(
            num_scalar_prefetch=2, grid=(n // TM_DISPATCH,),
            in_specs=[pl.BlockSpec((TM_DISPATCH * SUBLANES, LANES), lambda i, d, e: (i, 0))],
            out_specs=pl.BlockSpec(memory_space=pl.ANY),
            scratch_shapes=[pltpu.VMEM((BLK_E * SUBLANES, LANES), F32),
                            pltpu.SemaphoreType.DMA(()), pltpu.SemaphoreType.DMA(())]),
        out_shape=jax.ShapeDtypeStruct((n_slots * SUBLANES, LANES), F32),
        compiler_params=_cparams(("arbitrary",)),
        name="dispatch",
    )(dest, seg_end, h2s)


def _experts_kernel(be_ref, nused_ref, xs_ref, wg_ref, wu_ref, wd_ref, yb_ref, x_scr, wg_scr, wu_scr, wd_scr):
    i = pl.program_id(0)

    @pl.when(i >= nused_ref[0])
    def _():
        yb_ref[...] = jnp.zeros_like(yb_ref)

    @pl.when(i < nused_ref[0])
    def _():
        prev = be_ref[jnp.maximum(i - 1, 0)]

        @pl.when((i == 0) | (be_ref[i] != prev))
        def _():
            wg_scr[...] = wg_ref[0].astype(BF16)
            wu_scr[...] = wu_ref[0].astype(BF16)
            wd_scr[...] = wd_ref[0].astype(BF16)

        for c in range(D_MODEL // LANES):
            x_scr[:, c * LANES:(c + 1) * LANES] = xs_ref[pl.ds(c, BLK_E, stride=SUBLANES), :].astype(BF16)
        xb = x_scr[...]
        g = _dot(xb, wg_scr[...])
        u = _dot(xb, wu_scr[...])
        hid = (g * _sigmoid(g) * u).astype(BF16)
        y = _dot(hid, wd_scr[...])
        for c in range(D_MODEL // LANES):
            yb_ref[pl.ds(c, BLK_E, stride=SUBLANES), :] = y[:, c * LANES:(c + 1) * LANES]


def _experts_call(block_e, n_used, xs, w_gate, w_up, w_down):
    n_blocks = block_e.shape[0]
    slab = pl.BlockSpec((BLK_E * SUBLANES, LANES), lambda i, be, nu: (i, 0))
    used_slab = pl.BlockSpec((BLK_E * SUBLANES, LANES), lambda i, be, nu: (jnp.minimum(i, nu[0] - 1), 0))
    return pl.pallas_call(
        _experts_kernel,
        grid_spec=pltpu.PrefetchScalarGridSpec(
            num_scalar_prefetch=2, grid=(n_blocks,),
            in_specs=[used_slab,
                      pl.BlockSpec((1, D_MODEL, D_EXPERT), lambda i, be, nu: (be[i], 0, 0)),
                      pl.BlockSpec((1, D_MODEL, D_EXPERT), lambda i, be, nu: (be[i], 0, 0)),
                      pl.BlockSpec((1, D_EXPERT, D_MODEL), lambda i, be, nu: (be[i], 0, 0))],
            out_specs=slab,
            scratch_shapes=[pltpu.VMEM((BLK_E, D_MODEL), BF16),
                            pltpu.VMEM((D_MODEL, D_EXPERT), BF16),
                            pltpu.VMEM((D_MODEL, D_EXPERT), BF16),
                            pltpu.VMEM((D_EXPERT, D_MODEL), BF16)]),
        out_shape=jax.ShapeDtypeStruct(xs.shape, F32),
        compiler_params=_cparams(("arbitrary",)),
        name="experts",
    )(block_e, n_used, xs, w_gate, w_up, w_down)


def _final_kernel(dest_ref, x1_ref, rt_ref, mod_ref, gfin_ref, yb_hbm, o_ref, g_scr, sem):
    tm = TM_FINAL
    step = pl.program_id(0)
    slot = step % 2
    slot_rows = 2 * tm * SUBLANES

    def gather(s, into):
        def issue(r, _):
            tok = s * tm + r
            for k in range(2):
                _slab_copy(yb_hbm, dest_ref[2 * tok + k], g_scr.at[into], k * tm + r,
                           sem.at[into]).start(priority=k)
            return 0
        lax.fori_loop(0, tm, issue, 0, unroll=GATHER_UNROLL)

    @pl.when(step == 0)
    def _():
        gather(0, 0)

    @pl.when(step + 1 < pl.num_programs(0))
    def _():
        gather(step + 1, 1 - slot)

    g_now = g_scr.at[slot]
    pltpu.make_async_copy(yb_hbm.at[pl.ds(0, slot_rows)], g_now, sem.at[slot]).wait()

    rt = rt_ref[...]
    lane = lax.broadcasted_iota(I32, (tm, LANES), 1)
    gw0 = jnp.sum(jnp.where(lane == 2, rt, 0.0), axis=-1, keepdims=True)
    gw1 = jnp.sum(jnp.where(lane == 3, rt, 0.0), axis=-1, keepdims=True)
    gt2 = mod_ref[0, 5:6, :]
    x1 = x1_ref[...]
    cols = []
    for c in range(D_MODEL // LANES):
        y0 = g_now[pl.ds(c, tm, stride=SUBLANES), :]
        y1 = g_now[pl.ds(tm * SUBLANES + c, tm, stride=SUBLANES), :]
        y = gw0 * y0 + gw1 * y1
        cols.append(x1[:, c * LANES:(c + 1) * LANES] + gt2[:, c * LANES:(c + 1) * LANES] * y)
    x2 = jnp.concatenate(cols, axis=1)
    var = jnp.mean(x2 * x2, axis=-1, keepdims=True)
    o_ref[...] = x2 * lax.rsqrt(var + RMS_EPS) * gfin_ref[...]


def _final_call(dest, x1, rt, mod3, g_final, yb, seq):
    n = x1.shape[0]
    tpb = seq // TM_FINAL
    return pl.pallas_call(
        _final_kernel,
        grid_spec=pltpu.PrefetchScalarGridSpec(
            num_scalar_prefetch=1, grid=(n // TM_FINAL,),
            in_specs=[pl.BlockSpec((TM_FINAL, D_MODEL), lambda i, d: (i, 0)),
                      pl.BlockSpec((TM_FINAL, LANES), lambda i, d: (i, 0)),
                      pl.BlockSpec((1, 6, D_MODEL), lambda i, d: (i // tpb, 0, 0)),
                      pl.BlockSpec((1, D_MODEL), lambda i, d: (0, 0)),
                      pl.BlockSpec(memory_space=pl.ANY)],
            out_specs=pl.BlockSpec((TM_FINAL, D_MODEL), lambda i, d: (i, 0)),
            scratch_shapes=[pltpu.VMEM((2, 2 * TM_FINAL * SUBLANES, LANES), F32),
                            pltpu.SemaphoreType.DMA((2,))]),
        out_shape=jax.ShapeDtypeStruct((n, D_MODEL), F32),
        compiler_params=_cparams(("arbitrary",)),
        name="final",
    )(dest, x1, rt, mod3, g_final, yb)


def _permute_w_in(w):
    o = np.cumsum([0, 512, 64, 64, 512, 64, 8, 512, 512, 512, 8, 1024, 1024])
    qa, ka, va, qi, ki, wi, qb, kb, vb, fb, ga, gb = [w[:, o[k]:o[k + 1]] for k in range(12)]
    aux = jnp.concatenate([wi, fb, jnp.zeros((w.shape[0], LANES - 2 * N_HEADS), w.dtype)], axis=1)
    w_rows = jnp.concatenate([kb, ga, gb, ka, ka, ki, ki, aux], axis=1).astype(BF16)
    w_t = jnp.concatenate([qa, qi, qb, vb, va], axis=1).T.astype(BF16)
    return w_rows, w_t


def _layer(x2, pos2, mod3, batch, seq, g_mix, w_in, b_forget, w_out_a, w_out_b, w_out, g_ffn,
           w_group, b_group, w_router, b_router, w_e_gate, w_e_up, w_e_down, g_final):
    n = x2.shape[0]
    inv_freq = ROPE_THETA ** (-jnp.arange(0, ROPE_DIM, 2, dtype=F32) / ROPE_DIM)
    jj = np.arange(LANES) % HEAD_DIM
    invf = jnp.where(jj < ROPE_DIM, inv_freq[jj % (ROPE_DIM // 2)], 0.0)[None, :].astype(F32)
    fbias = jnp.zeros((1, LANES), F32).at[0, N_HEADS:2 * N_HEADS].set(b_forget.astype(F32))

    invf8 = jnp.broadcast_to(inv_freq[:, None], (ROPE_DIM // 2, LANES)).astype(F32)
    w_rows, w_t = _permute_w_in(w_in)
    (qat, qit, qbt, kaug, vbt, sga, sgb, ka2, vat, ki2, aux_t) = _inproj_call(
        x2, pos2, pos2.reshape(batch, 1, seq), mod3, g_mix.reshape(1, D_MODEL), w_rows, w_t, invf, invf8, fbias,
        batch, seq)

    r3 = lambda a: a.reshape(batch, seq, a.shape[-1])
    obt = _fox_call(qbt, aux_t, r3(kaug), vbt)
    oat = _dsa_call(qat, qit, aux_t, r3(ki2), r3(ka2), vat)

    w_rt = jnp.concatenate([w_router, w_group, jnp.zeros((D_MODEL, LANES - N_EXPERTS - N_GROUPS), F32)], axis=1)
    wr_hi = w_rt.astype(BF16)
    wr_lo = (w_rt - wr_hi.astype(F32)).astype(BF16)
    br = jnp.concatenate([b_router, b_group, jnp.zeros((LANES - N_EXPERTS - N_GROUPS,), F32)])[None, :]
    x1, h2s, lg = _outproj_call(oat, obt, sga, sgb, x2, mod3,
                                g_ffn.reshape(1, D_MODEL), w_out_a.astype(BF16), w_out_b.astype(BF16),
                                w_out.astype(BF16), wr_hi, wr_lo, br, seq)
    rt, rk, cnt = _route_rank_call(lg)

    n_slots = 2 * n + N_EXPERTS * BLK_E
    n_blocks = n_slots // BLK_E
    dd, be, nu, seg_end = _dest_call(rt, rk, cnt, n_blocks)
    dest = dd[:, 0:2].reshape(-1)
    block_e = be[:n_blocks, 0]
    n_used = nu[0, 0:1]

    xs = _dispatch_call(dest, seg_end[0, :N_EXPERTS], h2s, n_slots)
    yb = _experts_call(block_e, n_used, xs, w_e_gate, w_e_up, w_e_down)
    return _final_call(dest, x1, rt, mod3, g_final.reshape(1, D_MODEL), yb, seq)


def kernel(x, c, positions, w_mod, b_mod, g_mix, w_in, b_forget, w_out_a, w_out_b, w_out, g_ffn, w_group,
           b_group, w_router, b_router, w_e_gate, w_e_up, w_e_down, g_final):
    batch, seq, d = x.shape
    depth = w_mod.shape[0]
    assert depth == 1 and d == D_MODEL, "kernel fuses the final norm into the single layer"
    n = batch * seq
    c8 = jnp.zeros((8, d), F32).at[:batch].set(c)
    mod = _mod_call(c8, w_mod[0], b_mod[0][None, :])
    mod3 = mod[:batch].reshape(batch, 6, d)
    out = _layer(x.reshape(n, d), positions.reshape(n, 1), mod3, batch, seq, g_mix[0], w_in[0], b_forget[0],
                 w_out_a[0], w_out_b[0], w_out[0], g_ffn[0], w_group[0], b_group[0], w_router[0], b_router[0],
                 w_e_gate[0], w_e_up[0], w_e_down[0], g_final)
    return out.reshape(batch, seq, d)
```

```python
import functools

import numpy as np
import jax
import jax.numpy as jnp
from jax import lax
from jax.experimental import pallas as pl
from jax.experimental.pallas import tpu as pltpu

F32 = jnp.float32
BF16 = jnp.bfloat16
I32 = jnp.int32

D_MODEL = 1024
HEAD_DIM = 64
N_HEADS = 8
CHUNK_SHIFT = 6
TOPK = 256
ROPE_DIM = 16
ROPE_THETA = 500000.0
N_GROUPS = 4
EXPERTS_PER_GROUP = 8
N_EXPERTS = 32
D_EXPERT = 512
RMS_EPS = 1e-6
NEG_INF = -1e30
ATTN_SCALE = HEAD_DIM ** -0.5
IDX_SCALE = HEAD_DIM ** -0.5
LOG2E = 1.4426950408889634

LANES = 128
SUBLANES = 8
VMEM_LIMIT = 56 * 1024 * 1024

TM_IN = 512
TQ_FOX = 512
TK_FOX = 512
TQ_DSA = 512
TK_DSA = 512
TM_OUT = 512
TM_ROUTE = 1024
TM_DISPATCH = 1024
BLK_E = 256
TM_FINAL = 512
GATHER_UNROLL = 8

RADIX_BLOCK = 32 * SUBLANES
BOUND_MARGIN = 1.02
BOUND_SAFE = 40.0
FOX_BIAS_SLACK = 1.0
FOX_K = 256
FOX_AUG_ROWS = 16

_SEG_KB, _SEG_GA, _SEG_GB, _SEG_KA, _SEG_KI, _SEG_AUX = 0, 512, 1536, 2560, 2688, 2816
_NC_IN = 2944
_ROW_QA, _ROW_QI, _ROW_QB, _ROW_VB, _ROW_VA = 0, 512, 1024, 1536, 2048
_NR_IN = 2048 + HEAD_DIM

_SENT_KEY = int(np.array(NEG_INF, np.float32).view(np.int32) ^ 0x7FFFFFFF)
_INT_MIN = -(2 ** 31)


def _cparams(sem):
    return pltpu.CompilerParams(dimension_semantics=sem, vmem_limit_bytes=VMEM_LIMIT)


def _dot(a, b):
    return jnp.dot(a, b, preferred_element_type=F32)


def _sigmoid(x):
    return 1.0 / (1.0 + jnp.exp(-x))


def _mod_kernel(c_ref, w_ref, b_ref, o_ref):
    c = c_ref[...]
    ca = c * _sigmoid(c)
    o_ref[...] = _dot(ca.astype(BF16), w_ref[...].astype(BF16)) + b_ref[...]


def _mod_call(c8, w_mod, b_mod):
    n_out = w_mod.shape[1]
    tn = 1024
    return pl.pallas_call(
        _mod_kernel,
        grid=(n_out // tn,),
        in_specs=[pl.BlockSpec((8, D_MODEL), lambda j: (0, 0)),
                  pl.BlockSpec((D_MODEL, tn), lambda j: (0, j)),
                  pl.BlockSpec((1, tn), lambda j: (0, j))],
        out_specs=pl.BlockSpec((8, tn), lambda j: (0, j)),
        out_shape=jax.ShapeDtypeStruct((8, n_out), F32),
        compiler_params=_cparams(("arbitrary",)),
        name="mod",
    )(c8, w_mod, b_mod)


def _inproj_kernel(x_ref, pos_ref, posr_ref, mod_ref, g_ref, w_ref, wt_ref, invf_ref, invf8_ref, fbias_ref,
                   sel_ref, ones_ref,
                   qat_ref, qit_ref, qbt_ref, kb_ref, vbt_ref, sga_ref, sgb_ref,
                   ka_ref, vat_ref, ki_ref, auxt_ref, h_scr, carry_scr):
    tm = TM_IN
    x = x_ref[...]
    var = jnp.mean(x * x, axis=-1, keepdims=True)
    tf = x * lax.rsqrt(var + RMS_EPS) * g_ref[...]
    sh = mod_ref[0, 0:1, :]
    sc = mod_ref[0, 1:2, :]
    h_scr[...] = (tf * (1.0 + sc) + sh).astype(BF16)
    hb = h_scr[...]

    lane = lax.broadcasted_iota(I32, (tm, LANES), 1)
    j = lane & (HEAD_DIM - 1)
    ang = pos_ref[...].astype(F32) * invf_ref[...]
    cs = jnp.cos(ang)
    sn = jnp.sin(ang)
    coef_next = jnp.where(j < ROPE_DIM // 2, -sn, 0.0)
    coef_prev = jnp.where((j >= ROPE_DIM // 2) & (j < ROPE_DIM), sn, 0.0)

    def rope(tc):
        return (tc * cs + pltpu.roll(tc, LANES - ROPE_DIM // 2, 1) * coef_next
                + pltpu.roll(tc, ROPE_DIM // 2, 1) * coef_prev)

    def seg(off, width):
        return _dot(hb, w_ref[:, off:off + width])

    def seg_t(row0, rows):
        return lax.dot_general(wt_ref[row0:row0 + rows, :], hb, (((1,), (1,)), ((), ())),
                               preferred_element_type=F32)

    ang8 = invf8_ref[:, 0:1] * posr_ref[0].astype(F32)
    cs8 = jnp.cos(ang8)
    sn8 = jnp.sin(ang8)
    half = ROPE_DIM // 2

    def rope_t(t, scale, out_ref):
        for h in range(N_HEADS):
            r0 = h * HEAD_DIM
            lo = t[r0:r0 + half, :]
            hi = t[r0 + half:r0 + ROPE_DIM, :]
            head = jnp.concatenate([lo * cs8 - hi * sn8, hi * cs8 + lo * sn8, t[r0 + ROPE_DIM:r0 + HEAD_DIM, :]],
                                   axis=0)
            out_ref[0, r0:r0 + HEAD_DIM, :] = (head * scale).astype(BF16)

    rope_t(seg_t(_ROW_QA, 512), ATTN_SCALE * LOG2E, qat_ref)
    rope_t(seg_t(_ROW_QI, 512), IDX_SCALE, qit_ref)
    qbt_ref[0] = (seg_t(_ROW_QB, 512) * (ATTN_SCALE * LOG2E)).astype(BF16)
    vbt_ref[0, 0] = seg_t(_ROW_VB, 512).astype(BF16)
    vat_ref[0, 0] = seg_t(_ROW_VA, HEAD_DIM).astype(BF16)
    sga_ref[...] = _sigmoid(seg(_SEG_GA, 1024)).astype(BF16)
    sgb_ref[...] = _sigmoid(seg(_SEG_GB, 1024)).astype(BF16)
    ka = rope(seg(_SEG_KA, LANES))
    ka_ref[...] = ka.astype(BF16)
    ki_ref[...] = rope(seg(_SEG_KI, LANES)).astype(BF16)

    z = seg(_SEG_AUX, LANES)
    zf = z + fbias_ref[...]
    logf = jnp.minimum(zf, 0.0) - jnp.log(1.0 + jnp.exp(-jnp.abs(zf)))
    is_f = (lane >= N_HEADS) & (lane < 2 * N_HEADS)
    logf = jnp.where(is_f, logf, 0.0)

    def split3(v):
        hi = v.astype(BF16)
        r1 = v - hi.astype(F32)
        mid = r1.astype(BF16)
        return hi, mid, (r1 - mid.astype(F32)).astype(BF16)

    rr = lax.broadcasted_iota(I32, (tm, tm), 0)
    cc = lax.broadcasted_iota(I32, (tm, tm), 1)
    tri = jnp.where(cc <= rr, 1.0, 0.0).astype(BF16)
    p_hi, p_mid, p_lo = split3(logf)

    @pl.when(pl.program_id(1) == 0)
    def _():
        carry_scr[...] = jnp.zeros_like(carry_scr)

    parts = _dot(tri, jnp.concatenate([p_hi, p_mid, p_lo], axis=1))
    cum = parts[:, 0:LANES] + parts[:, LANES:2 * LANES] + parts[:, 2 * LANES:3 * LANES] + carry_scr[...]
    carry_scr[...] = cum[tm - 1:tm, :]
    cum2 = cum * LOG2E
    aux = jnp.where(lane < N_HEADS, z * (N_HEADS ** -0.5), jnp.where(is_f, cum2, 0.0))

    kb = seg(_SEG_KB, 512)
    first = lane < HEAD_DIM

    def half_norms(blk):
        sq = blk * blk
        return (jnp.sum(jnp.where(first, sq, 0.0), axis=-1, keepdims=True),
                jnp.sum(jnp.where(first, 0.0, sq), axis=-1, keepdims=True))

    for p in range(N_HEADS // 2):
        n_even, n_odd = half_norms(kb[:, p * LANES:(p + 1) * LANES])
        aux = jnp.where(lane == 2 * N_HEADS + 2 * p, n_even, jnp.where(lane == 2 * N_HEADS + 2 * p + 1, n_odd, aux))
    aux = jnp.where(lane == 3 * N_HEADS, half_norms(ka)[0], aux)
    auxt_ref[0] = aux.T[0:4 * N_HEADS, :]

    c_hi, c_mid, c_lo = split3(cum2)
    kaug = _dot(jnp.concatenate([c_hi, c_mid, c_lo], axis=1), sel_ref[...]) + ones_ref[...]
    for p in range(N_HEADS // 2):
        kb_ref[:, p * FOX_K:p * FOX_K + LANES] = kb[:, p * LANES:(p + 1) * LANES].astype(BF16)
        kb_ref[:, p * FOX_K + LANES:(p + 1) * FOX_K] = kaug[:, p * LANES:(p + 1) * LANES].astype(BF16)


def _fox_routing_constants():
    sel = np.zeros((3, LANES, (N_HEADS // 2) * LANES), np.float32)
    ones = np.zeros((1, (N_HEADS // 2) * LANES), np.float32)
    for h in range(N_HEADS):
        for t in range(3):
            sel[t, N_HEADS + h, (h // 2) * LANES + 3 + 3 * (h % 2) + t] = -1.0
    for p in range(N_HEADS // 2):
        ones[0, p * LANES:p * LANES + 3] = 1.0
    return jnp.asarray(sel.reshape(3 * LANES, -1), BF16), jnp.asarray(ones, F32)


def _inproj_call(x2, pos2, pos_rows, mod3, g_mix, w_rows, w_t, invf, invf8, fbias, batch, seq):
    assert TM_IN == TK_FOX == TK_DSA, "value tiles are written per in-projection step"
    n = x2.shape[0]
    tpb = seq // TM_IN
    row = lambda b, s: (b * tpb + s, 0)
    rows = lambda w: (jax.ShapeDtypeStruct((n, w), BF16), pl.BlockSpec((TM_IN, w), row))
    chan = lambda c: (jax.ShapeDtypeStruct((batch, c, seq), BF16), pl.BlockSpec((1, c, TM_IN), lambda b, s: (b, 0, s)))
    tile = lambda c: (jax.ShapeDtypeStruct((batch, tpb, c, TM_IN), BF16),
                      pl.BlockSpec((1, 1, c, TM_IN), lambda b, s: (b, s, 0, 0)))
    aux_t = (jax.ShapeDtypeStruct((batch, 4 * N_HEADS, seq), F32),
             pl.BlockSpec((1, 4 * N_HEADS, TM_IN), lambda b, s: (b, 0, s)))
    outs = [chan(512), chan(512), chan(512), rows((N_HEADS // 2) * FOX_K), tile(512), rows(1024), rows(1024),
            rows(LANES), tile(HEAD_DIM), rows(LANES), aux_t]
    sel, ones = _fox_routing_constants()
    const = lambda a: pl.BlockSpec(a.shape, lambda b, s: (0,) * a.ndim)
    return pl.pallas_call(
        _inproj_kernel,
        grid=(batch, tpb),
        in_specs=[pl.BlockSpec((TM_IN, D_MODEL), row),
                  pl.BlockSpec((TM_IN, 1), row),
                  pl.BlockSpec((1, 1, TM_IN), lambda b, s: (b, 0, s)),
                  pl.BlockSpec((1, 6, D_MODEL), lambda b, s: (b, 0, 0)),
                  const(g_mix), const(w_rows), const(w_t), const(invf), const(invf8), const(fbias),
                  const(sel), const(ones)],
        out_specs=[o[1] for o in outs],
        out_shape=[o[0] for o in outs],
        scratch_shapes=[pltpu.VMEM((TM_IN, D_MODEL), BF16), pltpu.VMEM((1, LANES), F32)],
        compiler_params=_cparams(("arbitrary", "arbitrary")),
        name="inproj",
    )(x2, pos2, pos_rows, mod3, g_mix, w_rows, w_t, invf, invf8, fbias, sel, ones)


def _softmax_max(s, m_ref):
    m_ref[...] = jnp.maximum(m_ref[...], jnp.max(s, axis=0, keepdims=True))


def _softmax_accumulate(s, m_ref, l_ref, acc_ref, v_t):
    p = jnp.exp2(s - m_ref[...])
    l_ref[...] = l_ref[...] + jnp.sum(p, axis=0, keepdims=True)
    acc_ref[...] = acc_ref[...] + _dot(v_t, p.astype(BF16))


def _softmax_scratch(tq):
    return ([pltpu.VMEM((1, tq), F32) for _ in range(2 * N_HEADS)]
            + [pltpu.VMEM((HEAD_DIM, tq), F32) for _ in range(N_HEADS)])


def _softmax_split(refs):
    return refs[:N_HEADS], refs[N_HEADS:2 * N_HEADS], refs[2 * N_HEADS:3 * N_HEADS]


def _softmax_init(m_refs, l_refs, acc_refs):
    for h in range(N_HEADS):
        m_refs[h][...] = jnp.full(m_refs[h].shape, NEG_INF, F32)
        l_refs[h][...] = jnp.zeros(l_refs[h].shape, F32)
        acc_refs[h][...] = jnp.zeros(acc_refs[h].shape, F32)


def _softmax_finish(o_ref, l_refs, acc_refs):
    for h in range(N_HEADS):
        o_ref[0, h * HEAD_DIM:(h + 1) * HEAD_DIM, :] = (acc_refs[h][...] * (1.0 / l_refs[h][...])).astype(BF16)


def _logit_bounds(qt_ref, kn2, slack):
    k_max = jnp.sqrt(jnp.max(kn2, axis=-1, keepdims=True))
    bounds = []
    worst = jnp.float32(0.0)
    for h in range(N_HEADS):
        qh = qt_ref[0, h * HEAD_DIM:(h + 1) * HEAD_DIM, :].astype(F32)
        q_norm = jnp.sqrt(jnp.sum(qh * qh, axis=0, keepdims=True))
        b = q_norm * k_max[h:h + 1, :] * BOUND_MARGIN + slack
        bounds.append(b)
        worst = jnp.maximum(worst, jnp.max(b))
    return bounds, worst <= BOUND_SAFE


def _fox_kernel(qt_ref, cumt_ref, kn_ref, kaug_ref, vt_ref, o_ref, rhs_scr, *softmax_refs):
    m_scr, l_scr, acc_scr = _softmax_split(softmax_refs)
    tq, tk = TQ_FOX, TK_FOX
    i = pl.program_id(1)
    q0 = i * tq

    cq = cumt_ref[0]
    c_hi = cq.astype(BF16).astype(F32)
    c_r = cq - c_hi
    c_mid = c_r.astype(BF16).astype(F32)
    c_lo = c_r - c_mid
    row_q = lax.broadcasted_iota(I32, (LANES, tq), 0)
    row_a = lax.broadcasted_iota(I32, (FOX_AUG_ROWS, 2 * tq), 0)
    second = lax.broadcasted_iota(I32, (FOX_AUG_ROWS, 2 * tq), 1) >= tq
    for p in range(N_HEADS // 2):
        qp = qt_ref[0, p * LANES:(p + 1) * LANES, :]
        zq = jnp.zeros_like(qp)
        rhs_scr[p, 0:LANES, 0:tq] = jnp.where(row_q < HEAD_DIM, qp, zq)
        rhs_scr[p, 0:LANES, tq:2 * tq] = jnp.where(row_q >= HEAD_DIM, qp, zq)
        pair_row = lambda a: jnp.concatenate([a[2 * p:2 * p + 1, :], a[2 * p + 1:2 * p + 2, :]], axis=1)
        aug = jnp.where(row_a == 0, pair_row(c_hi),
              jnp.where(row_a == 1, pair_row(c_mid),
              jnp.where(row_a == 2, pair_row(c_lo),
              jnp.where((row_a >= 3) & (row_a < 6), jnp.where(second, 0.0, 1.0),
              jnp.where((row_a >= 6) & (row_a < 9), jnp.where(second, 1.0, 0.0), 0.0)))))
        rhs_scr[p, LANES:LANES + FOX_AUG_ROWS, :] = aug.astype(BF16)
        rhs_scr[p, LANES + FOX_AUG_ROWS:FOX_K, :] = jnp.zeros((FOX_K - LANES - FOX_AUG_ROWS, 2 * tq), BF16)

    _softmax_init(m_scr, l_scr, acc_scr)
    key_j = lax.broadcasted_iota(I32, (tk, tq), 0)
    qry_i = q0 + lax.broadcasted_iota(I32, (tk, tq), 1)

    def tile(kt, masked, second_pass):
        k0 = pl.multiple_of(kt * tk, tk)
        for p in range(N_HEADS // 2):
            st = _dot(kaug_ref[0, pl.ds(k0, tk), p * FOX_K:(p + 1) * FOX_K], rhs_scr[p])
            for hh in range(2):
                h = 2 * p + hh
                s = st[:, hh * tq:(hh + 1) * tq]
                if masked:
                    s = jnp.where(k0 + key_j <= qry_i, s, NEG_INF)
                if second_pass:
                    _softmax_accumulate(s, m_scr[h], l_scr[h], acc_scr[h],
                                        vt_ref[0, kt, h * HEAD_DIM:(h + 1) * HEAD_DIM, :])
                else:
                    _softmax_max(s, m_scr[h])

    n_full = i * (tq // tk)

    def sweep(second_pass):
        def full_tile(kt, carry):
            tile(kt, False, second_pass)
            return carry

        lax.fori_loop(0, n_full, full_tile, 0)
        for d in range(tq // tk):
            tile(n_full + d, True, second_pass)

    bounds, safe = _logit_bounds(qt_ref, kn_ref[0], FOX_BIAS_SLACK)

    @pl.when(safe)
    def _():
        for h in range(N_HEADS):
            m_scr[h][...] = bounds[h]
        sweep(True)

    @pl.when(jnp.logical_not(safe))
    def _():
        sweep(False)
        sweep(True)

    _softmax_finish(o_ref, l_scr, acc_scr)


def _fox_call(qbt, aux_t, kaug3, vbt4):
    batch, _, seq = qbt.shape
    return pl.pallas_call(
        _fox_kernel,
        grid=(batch, seq // TQ_FOX),
        in_specs=[pl.BlockSpec((1, 512, TQ_FOX), lambda b, i: (b, 0, i)),
                  pl.BlockSpec((1, N_HEADS, TQ_FOX), lambda b, i: (b, 1, i)),
                  pl.BlockSpec((1, N_HEADS, seq), lambda b, i: (b, 2, 0)),
                  pl.BlockSpec((1, seq, (N_HEADS // 2) * FOX_K), lambda b, i: (b, 0, 0)),
                  pl.BlockSpec((1, seq // TK_FOX, 512, TK_FOX), lambda b, i: (b, 0, 0, 0))],
        out_specs=pl.BlockSpec((1, 512, TQ_FOX), lambda b, i: (b, 0, i)),
        out_shape=jax.ShapeDtypeStruct((batch, 512, seq), BF16),
        scratch_shapes=[pltpu.VMEM((N_HEADS // 2, FOX_K, 2 * TQ_FOX), BF16)] + _softmax_scratch(TQ_FOX),
        compiler_params=_cparams(("arbitrary", "arbitrary")),
        name="fox",
    )(qbt, aux_t, aux_t, kaug3, vbt4)


def _bit_transpose32(words):
    a = list(words)
    j, mask = 16, 0x0000FFFF
    while j:
        mask_i = int(np.array(mask, np.uint32).view(np.int32))
        k = 0
        while k < 32:
            t = (a[k] ^ lax.shift_right_logical(a[k + j], j)) & mask_i
            a[k] = a[k] ^ t
            a[k + j] = a[k + j] ^ lax.shift_left(t, j)
            k = (k + j + 1) & ~j
        j >>= 1
        mask = (mask ^ (mask << j)) & 0xFFFFFFFF
    return a


def _dsa_kernel(qat_ref, qit_ref, wit_ref, kn_ref, ki_ref, ka_ref, vat_ref, o_ref,
                aqi_scr, aqa_scr, sc_scr, bias_scr, plane_scr, alive_scr, thr_scr, *softmax_refs, seq):
    m_scr, l_scr, acc_scr = _softmax_split(softmax_refs)
    tq, tk = TQ_DSA, TK_DSA
    blk_per_tile = tk // RADIX_BLOCK
    n_blocks = seq // RADIX_BLOCK
    i = pl.program_id(1)
    q0 = i * tq
    nk = lax.shift_right_logical(q0 + tq + tk - 1, int(np.log2(tk)))
    n_rest = (seq - nk * tk).astype(F32)

    row_q = lax.broadcasted_iota(I32, (LANES, tq), 0)
    for h in range(N_HEADS):
        keep = (row_q < HEAD_DIM) if h % 2 == 0 else (row_q >= HEAD_DIM)
        rows = slice((h // 2) * LANES, (h // 2 + 1) * LANES)
        qi_blk = qit_ref[0, rows, :]
        qa_blk = qat_ref[0, rows, :]
        aqi_scr[:, h * tq:(h + 1) * tq] = jnp.where(keep, qi_blk, jnp.zeros_like(qi_blk))
        aqa_scr[:, h * tq:(h + 1) * tq] = jnp.where(keep, qa_blk, jnp.zeros_like(qa_blk))

    w = wit_ref[0]
    key_j = lax.broadcasted_iota(I32, (tk, tq), 0)
    qry_chunk = lax.shift_right_logical(q0 + lax.broadcasted_iota(I32, (tk, tq), 1), CHUNK_SHIFT)

    def admissible(k0):
        return lax.shift_right_logical(k0 + key_j, CHUNK_SHIFT) <= qry_chunk

    def p1(kt, carry, masked):
        k0 = pl.multiple_of(kt * tk, tk)
        rel = _dot(ki_ref[0, pl.ds(k0, tk), :], aqi_scr[...])
        sc = w[0:1, :] * jnp.maximum(rel[:, 0:tq], 0.0)
        for h in range(1, N_HEADS):
            sc = sc + w[h:h + 1, :] * jnp.maximum(rel[:, h * tq:(h + 1) * tq], 0.0)
        if masked:
            sc = jnp.where(admissible(k0), sc, NEG_INF)
        sc = jnp.where(sc == 0.0, 0.0, sc)
        sc_scr[kt] = sc
        b = pltpu.bitcast(sc, I32)
        key = b ^ (lax.shift_right_arithmetic(b, 31) & 0x7FFFFFFF)
        for half in range(blk_per_tile):
            base = half * RADIX_BLOCK
            planes = _bit_transpose32([key[base + SUBLANES * m:base + SUBLANES * (m + 1), :] ^ _INT_MIN
                                       for m in range(32)])
            bl = kt * blk_per_tile + half
            for p in range(32):
                plane_scr[bl, p] = planes[p]
            alive_scr[bl] = jnp.full((SUBLANES, tq), -1, I32)
        return carry

    n_unmasked = lax.shift_right_logical(q0, int(np.log2(tk)))
    lax.fori_loop(0, n_unmasked, functools.partial(p1, masked=False), 0)
    lax.fori_loop(n_unmasked, nk, functools.partial(p1, masked=True), 0)

    def clear_block(bl, carry):
        plane_scr[bl] = jnp.zeros((32, SUBLANES, tq), I32)
        alive_scr[bl] = jnp.zeros((SUBLANES, tq), I32)
        return carry

    n_held = nk * blk_per_tile
    quarter = n_blocks // 4
    n_walk = lax.div(n_held + quarter - 1, quarter) * quarter
    lax.fori_loop(n_held, n_walk, clear_block, 0)

    sent_u = jnp.int32(_SENT_KEY ^ _INT_MIN)
    kf = float(TOPK)

    def radix_select(n_walk):
        def bit_step(p, carry):
            k_rem, thr_bits, rest_alive = carry
            shift = 31 - p
            sent_bit = lax.shift_right_logical(sent_u, shift) & 1
            ones = [alive_scr[bl] & plane_scr[bl, p] for bl in range(n_walk)]
            tot = lax.population_count(ones[0])
            for bl in range(1, n_walk):
                tot = tot + lax.population_count(ones[bl])
            cnt = jnp.sum(tot.astype(F32), axis=0, keepdims=True)
            cnt = cnt + jnp.where((rest_alive != 0) & (sent_bit != 0), n_rest, 0.0)
            take1 = cnt >= k_rem
            for bl in range(n_walk):
                alive_scr[bl] = jnp.where(take1, ones[bl], alive_scr[bl] & ~plane_scr[bl, p])
            k_rem = jnp.where(take1, k_rem, k_rem - cnt)
            thr_bits = thr_bits | jnp.where(take1, lax.shift_left(jnp.int32(1), shift), 0)
            rest_alive = jnp.where(take1 == (sent_bit != 0), rest_alive, 0)
            return k_rem, thr_bits, rest_alive

        _, thr_bits, _ = lax.fori_loop(
            0, 32, bit_step,
            (jnp.full((1, tq), kf, F32), jnp.zeros((1, tq), I32), jnp.ones((1, tq), I32)))
        thr_scr[...] = thr_bits

    for walk in range(quarter, n_blocks + 1, quarter):
        pl.when(n_walk == walk)(functools.partial(radix_select, walk))
    thr_key = thr_scr[...] ^ _INT_MIN
    thr0 = pltpu.bitcast(thr_key ^ (lax.shift_right_arithmetic(thr_key, 31) & 0x7FFFFFFF), F32)

    def rank_counts(thr):
        def body(kt, c):
            sc = sc_scr[kt]
            return (c[0] + jnp.sum(jnp.where(sc >= thr, 1.0, 0.0), axis=0, keepdims=True),
                    c[1] + jnp.sum(jnp.where(sc > thr, 1.0, 0.0), axis=0, keepdims=True))
        z = jnp.zeros((1, tq), F32)
        n_ge, n_gt = lax.fori_loop(0, nk, body, (z, z))
        return (n_ge + jnp.where(thr <= NEG_INF, n_rest, 0.0), n_gt + jnp.where(thr < NEG_INF, n_rest, 0.0))

    def misplaced(state):
        _, n_ge, n_gt = state
        return jnp.max(jnp.where((n_ge < kf) | (n_gt >= kf), 1.0, 0.0)) > 0.0

    def step_threshold(state):
        thr, n_ge, n_gt = state

        def body(kt, c):
            sc = sc_scr[kt]
            below = jnp.max(jnp.where(sc < thr, sc, -jnp.inf), axis=0, keepdims=True)
            above = jnp.min(jnp.where(sc > thr, sc, jnp.inf), axis=0, keepdims=True)
            return jnp.maximum(c[0], below), jnp.minimum(c[1], above)

        below, above = lax.fori_loop(0, nk, body, (jnp.full((1, tq), -jnp.inf, F32), jnp.full((1, tq), jnp.inf, F32)))
        has_rest = n_rest > 0.0
        below = jnp.where(has_rest & (thr > NEG_INF), jnp.maximum(below, NEG_INF), below)
        above = jnp.where(has_rest & (thr < NEG_INF), jnp.minimum(above, NEG_INF), above)
        thr = jnp.where(n_ge < kf, below, jnp.where(n_gt >= kf, above, thr))
        return (thr,) + rank_counts(thr)

    thr, n_ge, n_gt = lax.while_loop(misplaced, step_threshold, (thr0,) + rank_counts(thr0))
    need = kf - n_gt
    tie_free = jnp.min(jnp.where(n_ge == kf, 1.0, 0.0)) > 0.0

    tr = lax.broadcasted_iota(I32, (tk, tk), 0)
    tc = lax.broadcasted_iota(I32, (tk, tk), 1)
    tri = jnp.where(tc <= tr, 1.0, 0.0).astype(BF16)
    _softmax_init(m_scr, l_scr, acc_scr)

    def selection_bias(kt, tie):
        sc = sc_scr[kt]
        eq = sc == thr
        pref = _dot(tri, jnp.where(eq, 1.0, 0.0).astype(BF16)) + tie
        sel = (sc > thr) | (eq & (pref <= need))
        return jnp.where(sel & admissible(kt * tk), 0.0, NEG_INF), pref[tk - 1:tk, :]

    def logits(kt):
        return _dot(ka_ref[0, pl.ds(pl.multiple_of(kt * tk, tk), tk), :], aqa_scr[...])

    def accumulate(kt, bias):
        logit = logits(kt)
        v_t = vat_ref[0, kt]
        for h in range(N_HEADS):
            _softmax_accumulate(logit[:, h * tq:(h + 1) * tq] + bias, m_scr[h], l_scr[h], acc_scr[h], v_t)

    def p3_single(kt, tie):
        bias, tie = selection_bias(kt, tie)
        accumulate(kt, bias)
        return tie

    def p3_single_tie_free(kt, carry):
        accumulate(kt, jnp.where(sc_scr[kt] >= thr, 0.0, NEG_INF))
        return carry

    def p3_max(kt, tie):
        bias, tie = selection_bias(kt, tie)
        bias_scr[kt] = bias
        logit = logits(kt)
        for h in range(N_HEADS):
            _softmax_max(logit[:, h * tq:(h + 1) * tq] + bias, m_scr[h])
        return tie

    def p3_accumulate(kt, carry):
        accumulate(kt, bias_scr[kt])
        return carry

    kn2 = jnp.broadcast_to(kn_ref[0, 0:1, :], (N_HEADS, seq))
    bounds, safe = _logit_bounds(qat_ref, kn2, 0.0)
    no_tie = jnp.zeros((1, tq), F32)

    @pl.when(safe)
    def _():
        for h in range(N_HEADS):
            m_scr[h][...] = bounds[h]

    @pl.when(safe & tie_free)
    def _():
        lax.fori_loop(0, nk, p3_single_tie_free, 0)

    @pl.when(safe & jnp.logical_not(tie_free))
    def _():
        lax.fori_loop(0, nk, p3_single, no_tie)

    @pl.when(jnp.logical_not(safe))
    def _():
        lax.fori_loop(0, nk, p3_max, no_tie)
        lax.fori_loop(0, nk, p3_accumulate, 0)

    _softmax_finish(o_ref, l_scr, acc_scr)


def _dsa_call(qat, qit, aux_t, ki3, ka3, vat4):
    batch, _, seq = qat.shape
    qspec = pl.BlockSpec((1, 512, TQ_DSA), lambda b, i: (b, 0, i))
    kspec = pl.BlockSpec((1, seq, LANES), lambda b, i: (b, 0, 0))
    return pl.pallas_call(
        functools.partial(_dsa_kernel, seq=seq),
        grid=(batch, seq // TQ_DSA),
        in_specs=[qspec, qspec,
                  pl.BlockSpec((1, N_HEADS, TQ_DSA), lambda b, i: (b, 0, i)),
                  pl.BlockSpec((1, N_HEADS, seq), lambda b, i: (b, 3, 0)),
                  kspec, kspec,
                  pl.BlockSpec((1, seq // TK_DSA, HEAD_DIM, TK_DSA), lambda b, i: (b, 0, 0, 0))],
        out_specs=qspec,
        out_shape=jax.ShapeDtypeStruct((batch, 512, seq), BF16),
        scratch_shapes=[pltpu.VMEM((LANES, N_HEADS * TQ_DSA), BF16),
                        pltpu.VMEM((LANES, N_HEADS * TQ_DSA), BF16),
                        pltpu.VMEM((seq // TK_DSA, TK_DSA, TQ_DSA), F32),
                        pltpu.VMEM((seq // TK_DSA, TK_DSA, TQ_DSA), F32),
                        pltpu.VMEM((seq // RADIX_BLOCK, 32, SUBLANES, TQ_DSA), I32),
                        pltpu.VMEM((seq // RADIX_BLOCK, SUBLANES, TQ_DSA), I32),
                        pltpu.VMEM((1, TQ_DSA), I32)] + _softmax_scratch(TQ_DSA),
        compiler_params=_cparams(("arbitrary", "arbitrary")),
        name="dsa",
    )(qat, qit, aux_t, aux_t, ki3, ka3, vat4)


def _outproj_kernel(oa_ref, ob_ref, sga_ref, sgb_ref, x_ref, mod_ref, gffn_ref,
                    woa_ref, wob_ref, wo_ref, wr_hi_ref, wr_lo_ref, br_ref,
                    x1_ref, h2_ref, lg_ref):
    tm = TM_OUT
    tn_dot = lambda a, b: lax.dot_general(a, b, (((0,), (0,)), ((), ())), preferred_element_type=F32)
    ya = tn_dot(oa_ref[0], woa_ref[...])
    yb = tn_dot(ob_ref[0], wob_ref[...])
    merged = (sga_ref[...].astype(F32) * ya + sgb_ref[...].astype(F32) * yb).astype(BF16)
    mix = _dot(merged, wo_ref[...])
    gt1 = mod_ref[0, 2:3, :]
    sh2 = mod_ref[0, 3:4, :]
    sc2 = mod_ref[0, 4:5, :]
    x1 = x_ref[...] + gt1 * mix
    x1_ref[...] = x1
    var = jnp.mean(x1 * x1, axis=-1, keepdims=True)
    h2 = x1 * lax.rsqrt(var + RMS_EPS) * gffn_ref[...] * (1.0 + sc2) + sh2
    for c in range(D_MODEL // LANES):
        h2_ref[pl.ds(c, tm, stride=SUBLANES), :] = h2[:, c * LANES:(c + 1) * LANES]
    hi = h2.astype(BF16)
    lo = (h2 - hi.astype(F32)).astype(BF16)
    r = _dot(jnp.concatenate([hi, lo], axis=0), jnp.concatenate([wr_hi_ref[...], wr_lo_ref[...]], axis=1))
    lg_ref[...] = r[0:tm, 0:LANES] + r[0:tm, LANES:2 * LANES] + r[tm:2 * tm, 0:LANES] + br_ref[...]


def _outproj_call(oa2, ob2, sga, sgb, x2, mod3, g_ffn, woa, wob, wo, wr_hi, wr_lo, br, seq):
    n = x2.shape[0]
    tpb = seq // TM_OUT
    row = lambda w: pl.BlockSpec((TM_OUT, w), lambda i: (i, 0))
    full = lambda a: pl.BlockSpec(a.shape, lambda i: (0,) * a.ndim)
    return pl.pallas_call(
        _outproj_kernel,
        grid=(n // TM_OUT,),
        in_specs=[pl.BlockSpec((1, 512, TM_OUT), lambda i: (i // tpb, 0, i % tpb)),
                  pl.BlockSpec((1, 512, TM_OUT), lambda i: (i // tpb, 0, i % tpb)),
                  row(1024), row(1024), row(D_MODEL),
                  pl.BlockSpec((1, 6, D_MODEL), lambda i: (i // tpb, 0, 0)),
                  full(g_ffn), full(woa), full(wob), full(wo), full(wr_hi), full(wr_lo), full(br)],
        out_specs=[row(D_MODEL),
                   pl.BlockSpec((TM_OUT * SUBLANES, LANES), lambda i: (i, 0)),
                   row(LANES)],
        out_shape=[jax.ShapeDtypeStruct((n, D_MODEL), F32),
                   jax.ShapeDtypeStruct((n * SUBLANES, LANES), F32),
                   jax.ShapeDtypeStruct((n, LANES), F32)],
        compiler_params=_cparams(("arbitrary",)),
        name="outproj",
    )(oa2, ob2, sga, sgb, x2, mod3, g_ffn, woa, wob, wo, wr_hi, wr_lo, br)


def _route_rows(lg):
    tm = lg.shape[0]
    lane = lax.broadcasted_iota(I32, (tm, LANES), 1)
    big = jnp.int32(LANES)
    ninf = -jnp.inf
    gmask = (lane >= N_EXPERTS) & (lane < N_EXPERTS + N_GROUPS)
    g = jnp.where(gmask, lg, ninf)
    gmax = jnp.max(g, axis=-1, keepdims=True)
    grp = jnp.min(jnp.where(g == gmax, lane - N_EXPERTS, big), axis=-1, keepdims=True)
    p_grp = 1.0 / jnp.sum(jnp.where(gmask, jnp.exp(lg - gmax), 0.0), axis=-1, keepdims=True)
    lo = grp * EXPERTS_PER_GROUP
    emask = (lane >= lo) & (lane < lo + EXPERTS_PER_GROUP)
    ev = jnp.where(emask, lg, ninf)
    v0 = jnp.max(ev, axis=-1, keepdims=True)
    i0 = jnp.min(jnp.where(emask & (ev == v0), lane, big), axis=-1, keepdims=True)
    rest = emask & (lane != i0)
    ev1 = jnp.where(rest, lg, ninf)
    v1 = jnp.max(ev1, axis=-1, keepdims=True)
    i1 = jnp.min(jnp.where(rest & (ev1 == v1), lane, big), axis=-1, keepdims=True)
    e1 = jnp.exp(v1 - v0)
    w0 = p_grp / (1.0 + e1)
    w1 = p_grp * e1 / (1.0 + e1)
    rt = jnp.where(lane == 0, i0.astype(F32),
                   jnp.where(lane == 1, i1.astype(F32),
                             jnp.where(lane == 2, w0, jnp.where(lane == 3, w1, 0.0))))
    return rt, lane == i0, lane == i1


def _expert_onehots(rt, tm):
    lane = lax.broadcasted_iota(I32, (tm, LANES), 1)
    lane_f = lane.astype(F32)
    e0 = jnp.sum(jnp.where(lane == 0, rt, 0.0), axis=-1, keepdims=True)
    e1 = jnp.sum(jnp.where(lane == 1, rt, 0.0), axis=-1, keepdims=True)
    return lane, lane_f == e0, lane_f == e1


def _rank_rows(is0, is1, seen_before):
    tm = is0.shape[0]
    lane = lax.broadcasted_iota(I32, (tm, LANES), 1)
    hits = jnp.where(is0, 1.0, 0.0) + jnp.where(is1, 1.0, 0.0)
    rr = lax.broadcasted_iota(I32, (tm, tm), 0)
    cc = lax.broadcasted_iota(I32, (tm, tm), 1)
    before = jnp.where(cc < rr, 1.0, 0.0).astype(BF16)
    seen = _dot(before, hits.astype(BF16)) + seen_before
    r0 = jnp.sum(jnp.where(is0, seen, 0.0), axis=-1, keepdims=True)
    r1 = jnp.sum(jnp.where(is1, seen, 0.0), axis=-1, keepdims=True)
    rk = jnp.where(lane == 0, r0, jnp.where(lane == 1, r1, 0.0))
    return rk, seen_before + jnp.sum(hits, axis=0, keepdims=True)


def _route_rank_kernel(lg_ref, rt_ref, rk_ref, cnt_ref, seen_scr):
    @pl.when(pl.program_id(0) == 0)
    def _():
        seen_scr[...] = jnp.zeros_like(seen_scr)

    rt, is0, is1 = _route_rows(lg_ref[...])
    rt_ref[...] = rt
    rk, seen = _rank_rows(is0, is1, seen_scr[...])
    rk_ref[...] = rk
    seen_scr[...] = seen
    cnt_ref[...] = jnp.broadcast_to(seen, cnt_ref.shape)


def _route_rank_call(lg):
    n = lg.shape[0]
    spec = pl.BlockSpec((TM_ROUTE, LANES), lambda i: (i, 0))
    return pl.pallas_call(
        _route_rank_kernel, grid=(n // TM_ROUTE,), in_specs=[spec],
        out_specs=[spec, spec, pl.BlockSpec((SUBLANES, LANES), lambda i: (0, 0))],
        out_shape=[jax.ShapeDtypeStruct((n, LANES), F32), jax.ShapeDtypeStruct((n, LANES), F32),
                   jax.ShapeDtypeStruct((SUBLANES, LANES), F32)],
        scratch_shapes=[pltpu.VMEM((1, LANES), F32)],
        compiler_params=_cparams(("arbitrary",)), name="route_rank",
    )(lg)


def _dest_kernel(rt_ref, rk_ref, cnt_ref, dd_ref, be_ref, nu_ref, end_ref):
    tm = TM_ROUTE
    lane, is0, is1 = _expert_onehots(rt_ref[...], tm)
    blocks = jnp.floor((cnt_ref[...] + (BLK_E - 1)) * (1.0 / BLK_E))
    er = lax.broadcasted_iota(I32, (LANES, LANES), 0)
    ec = lax.broadcasted_iota(I32, (LANES, LANES), 1)
    upto = jnp.where(er <= ec, 1.0, 0.0).astype(BF16)
    bend = _dot(blocks.astype(BF16), upto)
    pstart = (bend[0:1, :] - blocks[0:1, :]) * BLK_E
    rk = rk_ref[...]
    r0 = jnp.sum(jnp.where(lane == 0, rk, 0.0), axis=-1, keepdims=True)
    r1 = jnp.sum(jnp.where(lane == 1, rk, 0.0), axis=-1, keepdims=True)
    d0 = jnp.sum(jnp.where(is0, pstart, 0.0), axis=-1, keepdims=True) + r0
    d1 = jnp.sum(jnp.where(is1, pstart, 0.0), axis=-1, keepdims=True) + r1
    dd_ref[...] = jnp.where(lane == 0, d0, jnp.where(lane == 1, d1, 0.0)).astype(I32)
    nb = be_ref.shape[0]
    blk = lax.broadcasted_iota(I32, (nb, LANES), 0).astype(F32)
    lane_b = lax.broadcasted_iota(I32, (nb, LANES), 1)
    ended = jnp.where((lane_b < N_EXPERTS) & (bend[0:1, :] <= blk), 1.0, 0.0)
    be = jnp.minimum(jnp.sum(ended, axis=-1, keepdims=True), float(N_EXPERTS - 1))
    be_ref[...] = jnp.broadcast_to(be, be_ref.shape).astype(I32)
    lane_c = lax.broadcasted_iota(I32, (SUBLANES, LANES), 1)
    used = jnp.sum(jnp.where(lane_c == N_EXPERTS - 1, bend, 0.0), axis=-1, keepdims=True)
    nu_ref[...] = jnp.broadcast_to(used, nu_ref.shape).astype(I32)
    end_ref[...] = (bend * BLK_E).astype(I32)


def _dest_call(rt, rk, cnt, n_blocks):
    n = rt.shape[0]
    nb_pad = -(-n_blocks // SUBLANES) * SUBLANES
    spec = pl.BlockSpec((TM_ROUTE, LANES), lambda i: (i, 0))
    const = lambda rows: pl.BlockSpec((rows, LANES), lambda i: (0, 0))
    return pl.pallas_call(
        _dest_kernel, grid=(n // TM_ROUTE,), in_specs=[spec, spec, const(SUBLANES)],
        out_specs=[spec, const(nb_pad), const(SUBLANES), const(SUBLANES)],
        out_shape=[jax.ShapeDtypeStruct((n, LANES), I32), jax.ShapeDtypeStruct((nb_pad, LANES), I32),
                   jax.ShapeDtypeStruct((SUBLANES, LANES), I32), jax.ShapeDtypeStruct((SUBLANES, LANES), I32)],
        compiler_params=_cparams(("arbitrary",)), name="dest",
    )(rt, rk, cnt)


def _slab_copy(src, src_row, dst, dst_row, sem):
    return pltpu.make_async_copy(src.at[pl.ds(pl.multiple_of(src_row * SUBLANES, SUBLANES), SUBLANES)],
                                 dst.at[pl.ds(pl.multiple_of(dst_row * SUBLANES, SUBLANES), SUBLANES)], sem)


def _dispatch_kernel(dest_ref, seg_end_ref, h2_ref, xs_hbm, zero_scr, sem, zsem):
    tm = TM_DISPATCH
    base = pl.program_id(0) * tm

    @pl.when(pl.program_id(0) == 0)
    def _():
        zero_scr[...] = jnp.zeros_like(zero_scr)
        clears = [pltpu.make_async_copy(
            zero_scr, xs_hbm.at[pl.ds(pl.multiple_of(jnp.maximum(seg_end_ref[e] - BLK_E, 0) * SUBLANES, SUBLANES),
                                      BLK_E * SUBLANES)], zsem) for e in range(N_EXPERTS)]
        for c in clears:
            c.start()
        for c in clears:
            c.wait()
        n_blocks = xs_hbm.shape[0] // (BLK_E * SUBLANES)
        first_free = lax.shift_right_logical(seg_end_ref[N_EXPERTS - 1], int(np.log2(BLK_E)))
        def tail_clear(b):
            row = pl.multiple_of((first_free + b) * (BLK_E * SUBLANES), BLK_E * SUBLANES)
            return pltpu.make_async_copy(zero_scr, xs_hbm.at[pl.ds(row, BLK_E * SUBLANES)], zsem)

        for b in range(N_EXPERTS):
            pl.when(first_free + b < n_blocks)(lambda b=b: tail_clear(b).start())
        for b in range(N_EXPERTS):
            pl.when(first_free + b < n_blocks)(lambda b=b: tail_clear(b).wait())

    def issue(r, _):
        for k in range(2):
            _slab_copy(h2_ref, r, xs_hbm, dest_ref[2 * (base + r) + k], sem).start(priority=k)
        return 0

    lax.fori_loop(0, tm, issue, 0, unroll=GATHER_UNROLL)
    for _ in range(2):
        pltpu.make_async_copy(h2_ref, xs_hbm.at[pl.ds(0, tm * SUBLANES)], sem).wait()


def _dispatch_call(dest, seg_end, h2s, n_slots):
    n = h2s.shape[0] // SUBLANES
    return pl.pallas_call(
        _dispatch_kernel,
        grid_spec=pltpu.PrefetchScalarGridSpec(
            num_scalar_prefetch=2, grid=(n // TM_DISPATCH,),
            in_specs=[pl.BlockSpec((TM_DISPATCH * SUBLANES, LANES), lambda i, d, e: (i, 0))],
            out_specs=pl.BlockSpec(memory_space=pl.ANY),
            scratch_shapes=[pltpu.VMEM((BLK_E * SUBLANES, LANES), F32),
                            pltpu.SemaphoreType.DMA(()), pltpu.SemaphoreType.DMA(())]),
        out_shape=jax.ShapeDtypeStruct((n_slots * SUBLANES, LANES), F32),
        compiler_params=_cparams(("arbitrary",)),
        name="dispatch",
    )(dest, seg_end, h2s)


def _experts_kernel(be_ref, nused_ref, xs_ref, wg_ref, wu_ref, wd_ref, yb_ref, x_scr, wg_scr, wu_scr, wd_scr):
    i = pl.program_id(0)

    @pl.when(i >= nused_ref[0])
    def _():
        yb_ref[...] = jnp.zeros_like(yb_ref)

    @pl.when(i < nused_ref[0])
    def _():
        prev = be_ref[jnp.maximum(i - 1, 0)]

        @pl.when((i == 0) | (be_ref[i] != prev))
        def _():
            wg_scr[...] = wg_ref[0].astype(BF16)
            wu_scr[...] = wu_ref[0].astype(BF16)
            wd_scr[...] = wd_ref[0].astype(BF16)

        for c in range(D_MODEL // LANES):
            x_scr[:, c * LANES:(c + 1) * LANES] = xs_ref[pl.ds(c, BLK_E, stride=SUBLANES), :].astype(BF16)
        xb = x_scr[...]
        g = _dot(xb, wg_scr[...])
        u = _dot(xb, wu_scr[...])
        hid = (g * _sigmoid(g) * u).astype(BF16)
        y = _dot(hid, wd_scr[...])
        for c in range(D_MODEL // LANES):
            yb_ref[pl.ds(c, BLK_E, stride=SUBLANES), :] = y[:, c * LANES:(c + 1) * LANES]


def _experts_call(block_e, n_used, xs, w_gate, w_up, w_down):
    n_blocks = block_e.shape[0]
    slab = pl.BlockSpec((BLK_E * SUBLANES, LANES), lambda i, be, nu: (i, 0))
    used_slab = pl.BlockSpec((BLK_E * SUBLANES, LANES), lambda i, be, nu: (jnp.minimum(i, nu[0] - 1), 0))
    return pl.pallas_call(
        _experts_kernel,
        grid_spec=pltpu.PrefetchScalarGridSpec(
            num_scalar_prefetch=2, grid=(n_blocks,),
            in_specs=[used_slab,
                      pl.BlockSpec((1, D_MODEL, D_EXPERT), lambda i, be, nu: (be[i], 0, 0)),
                      pl.BlockSpec((1, D_MODEL, D_EXPERT), lambda i, be, nu: (be[i], 0, 0)),
                      pl.BlockSpec((1, D_EXPERT, D_MODEL), lambda i, be, nu: (be[i], 0, 0))],
            out_specs=slab,
            scratch_shapes=[pltpu.VMEM((BLK_E, D_MODEL), BF16),
                            pltpu.VMEM((D_MODEL, D_EXPERT), BF16),
                            pltpu.VMEM((D_MODEL, D_EXPERT), BF16),
                            pltpu.VMEM((D_EXPERT, D_MODEL), BF16)]),
        out_shape=jax.ShapeDtypeStruct(xs.shape, F32),
        compiler_params=_cparams(("arbitrary",)),
        name="experts",
    )(block_e, n_used, xs, w_gate, w_up, w_down)


def _final_kernel(dest_ref, x1_ref, rt_ref, mod_ref, gfin_ref, yb_hbm, o_ref, g_scr, sem):
    tm = TM_FINAL
    step = pl.program_id(0)
    slot = step % 2
    slot_rows = 2 * tm * SUBLANES

    def gather(s, into):
        def issue(r, _):
            tok = s * tm + r
            for k in range(2):
                _slab_copy(yb_hbm, dest_ref[2 * tok + k], g_scr.at[into], k * tm + r,
                           sem.at[into]).start(priority=k)
            return 0
        lax.fori_loop(0, tm, issue, 0, unroll=GATHER_UNROLL)

    @pl.when(step == 0)
    def _():
        gather(0, 0)

    @pl.when(step + 1 < pl.num_programs(0))
    def _():
        gather(step + 1, 1 - slot)

    g_now = g_scr.at[slot]
    pltpu.make_async_copy(yb_hbm.at[pl.ds(0, slot_rows)], g_now, sem.at[slot]).wait()

    rt = rt_ref[...]
    lane = lax.broadcasted_iota(I32, (tm, LANES), 1)
    gw0 = jnp.sum(jnp.where(lane == 2, rt, 0.0), axis=-1, keepdims=True)
    gw1 = jnp.sum(jnp.where(lane == 3, rt, 0.0), axis=-1, keepdims=True)
    gt2 = mod_ref[0, 5:6, :]
    x1 = x1_ref[...]
    cols = []
    for c in range(D_MODEL // LANES):
        y0 = g_now[pl.ds(c, tm, stride=SUBLANES), :]
        y1 = g_now[pl.ds(tm * SUBLANES + c, tm, stride=SUBLANES), :]
        y = gw0 * y0 + gw1 * y1
        cols.append(x1[:, c * LANES:(c + 1) * LANES] + gt2[:, c * LANES:(c + 1) * LANES] * y)
    x2 = jnp.concatenate(cols, axis=1)
    var = jnp.mean(x2 * x2, axis=-1, keepdims=True)
    o_ref[...] = x2 * lax.rsqrt(var + RMS_EPS) * gfin_ref[...]


def _final_call(dest, x1, rt, mod3, g_final, yb, seq):
    n = x1.shape[0]
    tpb = seq // TM_FINAL
    return pl.pallas_call(
        _final_kernel,
        grid_spec=pltpu.PrefetchScalarGridSpec(
            num_scalar_prefetch=1, grid=(n // TM_FINAL,),
            in_specs=[pl.BlockSpec((TM_FINAL, D_MODEL), lambda i, d: (i, 0)),
                      pl.BlockSpec((TM_FINAL, LANES), lambda i, d: (i, 0)),
                      pl.BlockSpec((1, 6, D_MODEL), lambda i, d: (i // tpb, 0, 0)),
                      pl.BlockSpec((1, D_MODEL), lambda i, d: (0, 0)),
                      pl.BlockSpec(memory_space=pl.ANY)],
            out_specs=pl.BlockSpec((TM_FINAL, D_MODEL), lambda i, d: (i, 0)),
            scratch_shapes=[pltpu.VMEM((2, 2 * TM_FINAL * SUBLANES, LANES), F32),
                            pltpu.SemaphoreType.DMA((2,))]),
        out_shape=jax.ShapeDtypeStruct((n, D_MODEL), F32),
        compiler_params=_cparams(("arbitrary",)),
        name="final",
    )(dest, x1, rt, mod3, g_final, yb)


def _permute_w_in(w):
    o = np.cumsum([0, 512, 64, 64, 512, 64, 8, 512, 512, 512, 8, 1024, 1024])
    qa, ka, va, qi, ki, wi, qb, kb, vb, fb, ga, gb = [w[:, o[k]:o[k + 1]] for k in range(12)]
    aux = jnp.concatenate([wi, fb, jnp.zeros((w.shape[0], LANES - 2 * N_HEADS), w.dtype)], axis=1)
    w_rows = jnp.concatenate([kb, ga, gb, ka, ka, ki, ki, aux], axis=1).astype(BF16)
    w_t = jnp.concatenate([qa, qi, qb, vb, va], axis=1).T.astype(BF16)
    return w_rows, w_t


def _layer(x2, pos2, mod3, batch, seq, g_mix, w_in, b_forget, w_out_a, w_out_b, w_out, g_ffn,
           w_group, b_group, w_router, b_router, w_e_gate, w_e_up, w_e_down, g_final):
    n = x2.shape[0]
    inv_freq = ROPE_THETA ** (-jnp.arange(0, ROPE_DIM, 2, dtype=F32) / ROPE_DIM)
    jj = np.arange(LANES) % HEAD_DIM
    invf = jnp.where(jj < ROPE_DIM, inv_freq[jj % (ROPE_DIM // 2)], 0.0)[None, :].astype(F32)
    fbias = jnp.zeros((1, LANES), F32).at[0, N_HEADS:2 * N_HEADS].set(b_forget.astype(F32))

    invf8 = jnp.broadcast_to(inv_freq[:, None], (ROPE_DIM // 2, LANES)).astype(F32)
    w_rows, w_t = _permute_w_in(w_in)
    (qat, qit, qbt, kaug, vbt, sga, sgb, ka2, vat, ki2, aux_t) = _inproj_call(
        x2, pos2, pos2.reshape(batch, 1, seq), mod3, g_mix.reshape(1, D_MODEL), w_rows, w_t, invf, invf8, fbias,
        batch, seq)

    r3 = lambda a: a.reshape(batch, seq, a.shape[-1])
    obt = _fox_call(qbt, aux_t, r3(kaug), vbt)
    oat = _dsa_call(qat, qit, aux_t, r3(ki2), r3(ka2), vat)

    w_rt = jnp.concatenate([w_router, w_group, jnp.zeros((D_MODEL, LANES - N_EXPERTS - N_GROUPS), F32)], axis=1)
    wr_hi = w_rt.astype(BF16)
    wr_lo = (w_rt - wr_hi.astype(F32)).astype(BF16)
    br = jnp.concatenate([b_router, b_group, jnp.zeros((LANES - N_EXPERTS - N_GROUPS,), F32)])[None, :]
    x1, h2s, lg = _outproj_call(oat, obt, sga, sgb, x2, mod3,
                                g_ffn.reshape(1, D_MODEL), w_out_a.astype(BF16), w_out_b.astype(BF16),
                                w_out.astype(BF16), wr_hi, wr_lo, br, seq)
    rt, rk, cnt = _route_rank_call(lg)

    n_slots = 2 * n + N_EXPERTS * BLK_E
    n_blocks = n_slots // BLK_E
    dd, be, nu, seg_end = _dest_call(rt, rk, cnt, n_blocks)
    dest = dd[:, 0:2].reshape(-1)
    block_e = be[:n_blocks, 0]
    n_used = nu[0, 0:1]

    xs = _dispatch_call(dest, seg_end[0, :N_EXPERTS], h2s, n_slots)
    yb = _experts_call(block_e, n_used, xs, w_e_gate, w_e_up, w_e_down)
    return _final_call(dest, x1, rt, mod3, g_final.reshape(1, D_MODEL), yb, seq)


def kernel(x, c, positions, w_mod, b_mod, g_mix, w_in, b_forget, w_out_a, w_out_b, w_out, g_ffn, w_group,
           b_group, w_router, b_router, w_e_gate, w_e_up, w_e_down, g_final):
    batch, seq, d = x.shape
    depth = w_mod.shape[0]
    assert depth == 1 and d == D_MODEL, "kernel fuses the final norm into the single layer"
    n = batch * seq
    c8 = jnp.zeros((8, d), F32).at[:batch].set(c)
    mod = _mod_call(c8, w_mod[0], b_mod[0][None, :])
    mod3 = mod[:batch].reshape(batch, 6, d)
    out = _layer(x.reshape(n, d), positions.reshape(n, 1), mod3, batch, seq, g_mix[0], w_in[0], b_forget[0],
                 w_out_a[0], w_out_b[0], w_out[0], g_ffn[0], w_group[0], b_group[0], w_router[0], b_router[0],
                 w_e_gate[0], w_e_up[0], w_e_down[0], g_final)
    return out.reshape(batch, seq, d)
```

```python
import functools

import numpy as np
import jax
import jax.numpy as jnp
from jax import lax
from jax.experimental import pallas as pl
from jax.experimental.pallas import tpu as pltpu

F32 = jnp.float32
BF16 = jnp.bfloat16
I32 = jnp.int32

D_MODEL = 1024
HEAD_DIM = 64
N_HEADS = 8
CHUNK_SHIFT = 6
TOPK = 256
ROPE_DIM = 16
ROPE_THETA = 500000.0
N_GROUPS = 4
EXPERTS_PER_GROUP = 8
N_EXPERTS = 32
D_EXPERT = 512
RMS_EPS = 1e-6
NEG_INF = -1e30
ATTN_SCALE = HEAD_DIM ** -0.5
IDX_SCALE = HEAD_DIM ** -0.5
LOG2E = 1.4426950408889634

LANES = 128
SUBLANES = 8
VMEM_LIMIT = 56 * 1024 * 1024

TM_IN = 512
TQ_FOX = 512
TK_FOX = 512
TQ_DSA = 512
TK_DSA = 512
TM_OUT = 512
TM_ROUTE = 1024
TM_DISPATCH = 1024
BLK_E = 256
TM_FINAL = 512
GATHER_UNROLL = 8

RADIX_BLOCK = 32 * SUBLANES
BOUND_MARGIN = 1.02
BOUND_SAFE = 40.0
FOX_BIAS_SLACK = 1.0
FOX_K = 256
FOX_AUG_ROWS = 16

_SEG_KB, _SEG_GA, _SEG_GB, _SEG_KA, _SEG_KI, _SEG_AUX = 0, 512, 1536, 2560, 2688, 2816
_NC_IN = 2944
_ROW_QA, _ROW_QI, _ROW_QB, _ROW_VB, _ROW_VA = 0, 512, 1024, 1536, 2048
_NR_IN = 2048 + HEAD_DIM

_SENT_KEY = int(np.array(NEG_INF, np.float32).view(np.int32) ^ 0x7FFFFFFF)
_INT_MIN = -(2 ** 31)


def _cparams(sem):
    return pltpu.CompilerParams(dimension_semantics=sem, vmem_limit_bytes=VMEM_LIMIT)


def _dot(a, b):
    return jnp.dot(a, b, preferred_element_type=F32)


def _sigmoid(x):
    return 1.0 / (1.0 + jnp.exp(-x))


def _mod_kernel(c_ref, w_ref, b_ref, o_ref):
    c = c_ref[...]
    ca = c * _sigmoid(c)
    o_ref[...] = _dot(ca.astype(BF16), w_ref[...].astype(BF16)) + b_ref[...]


def _mod_call(c8, w_mod, b_mod):
    n_out = w_mod.shape[1]
    tn = 1024
    return pl.pallas_call(
        _mod_kernel,
        grid=(n_out // tn,),
        in_specs=[pl.BlockSpec((8, D_MODEL), lambda j: (0, 0)),
                  pl.BlockSpec((D_MODEL, tn), lambda j: (0, j)),
                  pl.BlockSpec((1, tn), lambda j: (0, j))],
        out_specs=pl.BlockSpec((8, tn), lambda j: (0, j)),
        out_shape=jax.ShapeDtypeStruct((8, n_out), F32),
        compiler_params=_cparams(("arbitrary",)),
        name="mod",
    )(c8, w_mod, b_mod)


def _inproj_kernel(x_ref, pos_ref, posr_ref, mod_ref, g_ref, w_ref, wt_ref, invf_ref, invf8_ref, fbias_ref,
                   sel_ref, ones_ref,
                   qat_ref, qit_ref, qbt_ref, kb_ref, vbt_ref, sga_ref, sgb_ref,
                   ka_ref, vat_ref, ki_ref, auxt_ref, h_scr, carry_scr):
    tm = TM_IN
    x = x_ref[...]
    var = jnp.mean(x * x, axis=-1, keepdims=True)
    tf = x * lax.rsqrt(var + RMS_EPS) * g_ref[...]
    sh = mod_ref[0, 0:1, :]
    sc = mod_ref[0, 1:2, :]
    h_scr[...] = (tf * (1.0 + sc) + sh).astype(BF16)
    hb = h_scr[...]

    lane = lax.broadcasted_iota(I32, (tm, LANES), 1)
    j = lane & (HEAD_DIM - 1)
    ang = pos_ref[...].astype(F32) * invf_ref[...]
    cs = jnp.cos(ang)
    sn = jnp.sin(ang)
    coef_next = jnp.where(j < ROPE_DIM // 2, -sn, 0.0)
    coef_prev = jnp.where((j >= ROPE_DIM // 2) & (j < ROPE_DIM), sn, 0.0)

    def rope(tc):
        return (tc * cs + pltpu.roll(tc, LANES - ROPE_DIM // 2, 1) * coef_next
                + pltpu.roll(tc, ROPE_DIM // 2, 1) * coef_prev)

    def seg(off, width):
        return _dot(hb, w_ref[:, off:off + width])

    def seg_t(row0, rows):
        return lax.dot_general(wt_ref[row0:row0 + rows, :], hb, (((1,), (1,)), ((), ())),
                               preferred_element_type=F32)

    ang8 = invf8_ref[:, 0:1] * posr_ref[0].astype(F32)
    cs8 = jnp.cos(ang8)
    sn8 = jnp.sin(ang8)
    half = ROPE_DIM // 2

    def rope_t(t, scale, out_ref):
        for h in range(N_HEADS):
            r0 = h * HEAD_DIM
            lo = t[r0:r0 + half, :]
            hi = t[r0 + half:r0 + ROPE_DIM, :]
            head = jnp.concatenate([lo * cs8 - hi * sn8, hi * cs8 + lo * sn8, t[r0 + ROPE_DIM:r0 + HEAD_DIM, :]],
                                   axis=0)
            out_ref[0, r0:r0 + HEAD_DIM, :] = (head * scale).astype(BF16)

    rope_t(seg_t(_ROW_QA, 512), ATTN_SCALE * LOG2E, qat_ref)
    rope_t(seg_t(_ROW_QI, 512), IDX_SCALE, qit_ref)
    qbt_ref[0] = (seg_t(_ROW_QB, 512) * (ATTN_SCALE * LOG2E)).astype(BF16)
    vbt_ref[0, 0] = seg_t(_ROW_VB, 512).astype(BF16)
    vat_ref[0, 0] = seg_t(_ROW_VA, HEAD_DIM).astype(BF16)
    sga_ref[...] = _sigmoid(seg(_SEG_GA, 1024)).astype(BF16)
    sgb_ref[...] = _sigmoid(seg(_SEG_GB, 1024)).astype(BF16)
    ka = rope(seg(_SEG_KA, LANES))
    ka_ref[...] = ka.astype(BF16)
    ki_ref[...] = rope(seg(_SEG_KI, LANES)).astype(BF16)

    z = seg(_SEG_AUX, LANES)
    zf = z + fbias_ref[...]
    logf = jnp.minimum(zf, 0.0) - jnp.log(1.0 + jnp.exp(-jnp.abs(zf)))
    is_f = (lane >= N_HEADS) & (lane < 2 * N_HEADS)
    logf = jnp.where(is_f, logf, 0.0)

    def split3(v):
        hi = v.astype(BF16)
        r1 = v - hi.astype(F32)
        mid = r1.astype(BF16)
        return hi, mid, (r1 - mid.astype(F32)).astype(BF16)

    rr = lax.broadcasted_iota(I32, (tm, tm), 0)
    cc = lax.broadcasted_iota(I32, (tm, tm), 1)
    tri = jnp.where(cc <= rr, 1.0, 0.0).astype(BF16)
    p_hi, p_mid, p_lo = split3(logf)

    @pl.when(pl.program_id(1) == 0)
    def _():
        carry_scr[...] = jnp.zeros_like(carry_scr)

    parts = _dot(tri, jnp.concatenate([p_hi, p_mid, p_lo], axis=1))
    cum = parts[:, 0:LANES] + parts[:, LANES:2 * LANES] + parts[:, 2 * LANES:3 * LANES] + carry_scr[...]
    carry_scr[...] = cum[tm - 1:tm, :]
    cum2 = cum * LOG2E
    aux = jnp.where(lane < N_HEADS, z * (N_HEADS ** -0.5), jnp.where(is_f, cum2, 0.0))

    kb = seg(_SEG_KB, 512)
    first = lane < HEAD_DIM

    def half_norms(blk):
        sq = blk * blk
        return (jnp.sum(jnp.where(first, sq, 0.0), axis=-1, keepdims=True),
                jnp.sum(jnp.where(first, 0.0, sq), axis=-1, keepdims=True))

    for p in range(N_HEADS // 2):
        n_even, n_odd = half_norms(kb[:, p * LANES:(p + 1) * LANES])
        aux = jnp.where(lane == 2 * N_HEADS + 2 * p, n_even, jnp.where(lane == 2 * N_HEADS + 2 * p + 1, n_odd, aux))
    aux = jnp.where(lane == 3 * N_HEADS, half_norms(ka)[0], aux)
    auxt_ref[0] = aux.T[0:4 * N_HEADS, :]

    c_hi, c_mid, c_lo = split3(cum2)
    kaug = _dot(jnp.concatenate([c_hi, c_mid, c_lo], axis=1), sel_ref[...]) + ones_ref[...]
    for p in range(N_HEADS // 2):
        kb_ref[:, p * FOX_K:p * FOX_K + LANES] = kb[:, p * LANES:(p + 1) * LANES].astype(BF16)
        kb_ref[:, p * FOX_K + LANES:(p + 1) * FOX_K] = kaug[:, p * LANES:(p + 1) * LANES].astype(BF16)


def _fox_routing_constants():
    sel = np.zeros((3, LANES, (N_HEADS // 2) * LANES), np.float32)
    ones = np.zeros((1, (N_HEADS // 2) * LANES), np.float32)
    for h in range(N_HEADS):
        for t in range(3):
            sel[t, N_HEADS + h, (h // 2) * LANES + 3 + 3 * (h % 2) + t] = -1.0
    for p in range(N_HEADS // 2):
        ones[0, p * LANES:p * LANES + 3] = 1.0
    return jnp.asarray(sel.reshape(3 * LANES, -1), BF16), jnp.asarray(ones, F32)


def _inproj_call(x2, pos2, pos_rows, mod3, g_mix, w_rows, w_t, invf, invf8, fbias, batch, seq):
    assert TM_IN == TK_FOX == TK_DSA, "value tiles are written per in-projection step"
    n = x2.shape[0]
    tpb = seq // TM_IN
    row = lambda b, s: (b * tpb + s, 0)
    rows = lambda w: (jax.ShapeDtypeStruct((n, w), BF16), pl.BlockSpec((TM_IN, w), row))
    chan = lambda c: (jax.ShapeDtypeStruct((batch, c, seq), BF16), pl.BlockSpec((1, c, TM_IN), lambda b, s: (b, 0, s)))
    tile = lambda c: (jax.ShapeDtypeStruct((batch, tpb, c, TM_IN), BF16),
                      pl.BlockSpec((1, 1, c, TM_IN), lambda b, s: (b, s, 0, 0)))
    aux_t = (jax.ShapeDtypeStruct((batch, 4 * N_HEADS, seq), F32),
             pl.BlockSpec((1, 4 * N_HEADS, TM_IN), lambda b, s: (b, 0, s)))
    outs = [chan(512), chan(512), chan(512), rows((N_HEADS // 2) * FOX_K), tile(512), rows(1024), rows(1024),
            rows(LANES), tile(HEAD_DIM), rows(LANES), aux_t]
    sel, ones = _fox_routing_constants()
    const = lambda a: pl.BlockSpec(a.shape, lambda b, s: (0,) * a.ndim)
    return pl.pallas_call(
        _inproj_kernel,
        grid=(batch, tpb),
        in_specs=[pl.BlockSpec((TM_IN, D_MODEL), row),
                  pl.BlockSpec((TM_IN, 1), row),
                  pl.BlockSpec((1, 1, TM_IN), lambda b, s: (b, 0, s)),
                  pl.BlockSpec((1, 6, D_MODEL), lambda b, s: (b, 0, 0)),
                  const(g_mix), const(w_rows), const(w_t), const(invf), const(invf8), const(fbias),
                  const(sel), const(ones)],
        out_specs=[o[1] for o in outs],
        out_shape=[o[0] for o in outs],
        scratch_shapes=[pltpu.VMEM((TM_IN, D_MODEL), BF16), pltpu.VMEM((1, LANES), F32)],
        compiler_params=_cparams(("arbitrary", "arbitrary")),
        name="inproj",
    )(x2, pos2, pos_rows, mod3, g_mix, w_rows, w_t, invf, invf8, fbias, sel, ones)


def _softmax_max(s, m_ref):
    m_ref[...] = jnp.maximum(m_ref[...], jnp.max(s, axis=0, keepdims=True))


def _softmax_accumulate(s, m_ref, l_ref, acc_ref, v_t):
    p = jnp.exp2(s - m_ref[...])
    l_ref[...] = l_ref[...] + jnp.sum(p, axis=0, keepdims=True)
    acc_ref[...] = acc_ref[...] + _dot(v_t, p.astype(BF16))


def _softmax_scratch(tq):
    return ([pltpu.VMEM((1, tq), F32) for _ in range(2 * N_HEADS)]
            + [pltpu.VMEM((HEAD_DIM, tq), F32) for _ in range(N_HEADS)])


def _softmax_split(refs):
    return refs[:N_HEADS], refs[N_HEADS:2 * N_HEADS], refs[2 * N_HEADS:3 * N_HEADS]


def _softmax_init(m_refs, l_refs, acc_refs):
    for h in range(N_HEADS):
        m_refs[h][...] = jnp.full(m_refs[h].shape, NEG_INF, F32)
        l_refs[h][...] = jnp.zeros(l_refs[h].shape, F32)
        acc_refs[h][...] = jnp.zeros(acc_refs[h].shape, F32)


def _softmax_finish(o_ref, l_refs, acc_refs):
    for h in range(N_HEADS):
        o_ref[0, h * HEAD_DIM:(h + 1) * HEAD_DIM, :] = (acc_refs[h][...] * (1.0 / l_refs[h][...])).astype(BF16)


def _logit_bounds(qt_ref, kn2, slack):
    k_max = jnp.sqrt(jnp.max(kn2, axis=-1, keepdims=True))
    bounds = []
    worst = jnp.float32(0.0)
    for h in range(N_HEADS):
        qh = qt_ref[0, h * HEAD_DIM:(h + 1) * HEAD_DIM, :].astype(F32)
        q_norm = jnp.sqrt(jnp.sum(qh * qh, axis=0, keepdims=True))
        b = q_norm * k_max[h:h + 1, :] * BOUND_MARGIN + slack
        bounds.append(b)
        worst = jnp.maximum(worst, jnp.max(b))
    return bounds, worst <= BOUND_SAFE


def _fox_kernel(qt_ref, cumt_ref, kn_ref, kaug_ref, vt_ref, o_ref, rhs_scr, *softmax_refs):
    m_scr, l_scr, acc_scr = _softmax_split(softmax_refs)
    tq, tk = TQ_FOX, TK_FOX
    i = pl.program_id(1)
    q0 = i * tq

    cq = cumt_ref[0]
    c_hi = cq.astype(BF16).astype(F32)
    c_r = cq - c_hi
    c_mid = c_r.astype(BF16).astype(F32)
    c_lo = c_r - c_mid
    row_q = lax.broadcasted_iota(I32, (LANES, tq), 0)
    row_a = lax.broadcasted_iota(I32, (FOX_AUG_ROWS, 2 * tq), 0)
    second = lax.broadcasted_iota(I32, (FOX_AUG_ROWS, 2 * tq), 1) >= tq
    for p in range(N_HEADS // 2):
        qp = qt_ref[0, p * LANES:(p + 1) * LANES, :]
        zq = jnp.zeros_like(qp)
        rhs_scr[p, 0:LANES, 0:tq] = jnp.where(row_q < HEAD_DIM, qp, zq)
        rhs_scr[p, 0:LANES, tq:2 * tq] = jnp.where(row_q >= HEAD_DIM, qp, zq)
        pair_row = lambda a: jnp.concatenate([a[2 * p:2 * p + 1, :], a[2 * p + 1:2 * p + 2, :]], axis=1)
        aug = jnp.where(row_a == 0, pair_row(c_hi),
              jnp.where(row_a == 1, pair_row(c_mid),
              jnp.where(row_a == 2, pair_row(c_lo),
              jnp.where((row_a >= 3) & (row_a < 6), jnp.where(second, 0.0, 1.0),
              jnp.where((row_a >= 6) & (row_a < 9), jnp.where(second, 1.0, 0.0), 0.0)))))
        rhs_scr[p, LANES:LANES + FOX_AUG_ROWS, :] = aug.astype(BF16)
        rhs_scr[p, LANES + FOX_AUG_ROWS:FOX_K, :] = jnp.zeros((FOX_K - LANES - FOX_AUG_ROWS, 2 * tq), BF16)

    _softmax_init(m_scr, l_scr, acc_scr)
    key_j = lax.broadcasted_iota(I32, (tk, tq), 0)
    qry_i = q0 + lax.broadcasted_iota(I32, (tk, tq), 1)

    def tile(kt, masked, second_pass):
        k0 = pl.multiple_of(kt * tk, tk)
        for p in range(N_HEADS // 2):
            st = _dot(kaug_ref[0, pl.ds(k0, tk), p * FOX_K:(p + 1) * FOX_K], rhs_scr[p])
            for hh in range(2):
                h = 2 * p + hh
                s = st[:, hh * tq:(hh + 1) * tq]
                if masked:
                    s = jnp.where(k0 + key_j <= qry_i, s, NEG_INF)
                if second_pass:
                    _softmax_accumulate(s, m_scr[h], l_scr[h], acc_scr[h],
                                        vt_ref[0, kt, h * HEAD_DIM:(h + 1) * HEAD_DIM, :])
                else:
                    _softmax_max(s, m_scr[h])

    n_full = i * (tq // tk)

    def sweep(second_pass):
        def full_tile(kt, carry):
            tile(kt, False, second_pass)
            return carry

        lax.fori_loop(0, n_full, full_tile, 0)
        for d in range(tq // tk):
            tile(n_full + d, True, second_pass)

    bounds, safe = _logit_bounds(qt_ref, kn_ref[0], FOX_BIAS_SLACK)

    @pl.when(safe)
    def _():
        for h in range(N_HEADS):
            m_scr[h][...] = bounds[h]
        sweep(True)

    @pl.when(jnp.logical_not(safe))
    def _():
        sweep(False)
        sweep(True)

    _softmax_finish(o_ref, l_scr, acc_scr)


def _fox_call(qbt, aux_t, kaug3, vbt4):
    batch, _, seq = qbt.shape
    return pl.pallas_call(
        _fox_kernel,
        grid=(batch, seq // TQ_FOX),
        in_specs=[pl.BlockSpec((1, 512, TQ_FOX), lambda b, i: (b, 0, i)),
                  pl.BlockSpec((1, N_HEADS, TQ_FOX), lambda b, i: (b, 1, i)),
                  pl.BlockSpec((1, N_HEADS, seq), lambda b, i: (b, 2, 0)),
                  pl.BlockSpec((1, seq, (N_HEADS // 2) * FOX_K), lambda b, i: (b, 0, 0)),
                  pl.BlockSpec((1, seq // TK_FOX, 512, TK_FOX), lambda b, i: (b, 0, 0, 0))],
        out_specs=pl.BlockSpec((1, 512, TQ_FOX), lambda b, i: (b, 0, i)),
        out_shape=jax.ShapeDtypeStruct((batch, 512, seq), BF16),
        scratch_shapes=[pltpu.VMEM((N_HEADS // 2, FOX_K, 2 * TQ_FOX), BF16)] + _softmax_scratch(TQ_FOX),
        compiler_params=_cparams(("arbitrary", "arbitrary")),
        name="fox",
    )(qbt, aux_t, aux_t, kaug3, vbt4)


def _bit_transpose32(words):
    a = list(words)
    j, mask = 16, 0x0000FFFF
    while j:
        mask_i = int(np.array(mask, np.uint32).view(np.int32))
        k = 0
        while k < 32:
            t = (a[k] ^ lax.shift_right_logical(a[k + j], j)) & mask_i
            a[k] = a[k] ^ t
            a[k + j] = a[k + j] ^ lax.shift_left(t, j)
            k = (k + j + 1) & ~j
        j >>= 1
        mask = (mask ^ (mask << j)) & 0xFFFFFFFF
    return a


def _dsa_kernel(qat_ref, qit_ref, wit_ref, kn_ref, ki_ref, ka_ref, vat_ref, o_ref,
                aqi_scr, aqa_scr, sc_scr, bias_scr, plane_scr, alive_scr, thr_scr, *softmax_refs, seq):
    m_scr, l_scr, acc_scr = _softmax_split(softmax_refs)
    tq, tk = TQ_DSA, TK_DSA
    blk_per_tile = tk // RADIX_BLOCK
    n_blocks = seq // RADIX_BLOCK
    i = pl.program_id(1)
    q0 = i * tq
    nk = lax.shift_right_logical(q0 + tq + tk - 1, int(np.log2(tk)))
    n_rest = (seq - nk * tk).astype(F32)

    row_q = lax.broadcasted_iota(I32, (LANES, tq), 0)
    for h in range(N_HEADS):
        keep = (row_q < HEAD_DIM) if h % 2 == 0 else (row_q >= HEAD_DIM)
        rows = slice((h // 2) * LANES, (h // 2 + 1) * LANES)
        qi_blk = qit_ref[0, rows, :]
        qa_blk = qat_ref[0, rows, :]
        aqi_scr[:, h * tq:(h + 1) * tq] = jnp.where(keep, qi_blk, jnp.zeros_like(qi_blk))
        aqa_scr[:, h * tq:(h + 1) * tq] = jnp.where(keep, qa_blk, jnp.zeros_like(qa_blk))

    w = wit_ref[0]
    key_j = lax.broadcasted_iota(I32, (tk, tq), 0)
    qry_chunk = lax.shift_right_logical(q0 + lax.broadcasted_iota(I32, (tk, tq), 1), CHUNK_SHIFT)

    def admissible(k0):
        return lax.shift_right_logical(k0 + key_j, CHUNK_SHIFT) <= qry_chunk

    def p1(kt, carry, masked):
        k0 = pl.multiple_of(kt * tk, tk)
        rel = _dot(ki_ref[0, pl.ds(k0, tk), :], aqi_scr[...])
        sc = w[0:1, :] * jnp.maximum(rel[:, 0:tq], 0.0)
        for h in range(1, N_HEADS):
            sc = sc + w[h:h + 1, :] * jnp.maximum(rel[:, h * tq:(h + 1) * tq], 0.0)
        if masked:
            sc = jnp.where(admissible(k0), sc, NEG_INF)
        sc = jnp.where(sc == 0.0, 0.0, sc)
        sc_scr[kt] = sc
        b = pltpu.bitcast(sc, I32)
        key = b ^ (lax.shift_right_arithmetic(b, 31) & 0x7FFFFFFF)
        for half in range(blk_per_tile):
            base = half * RADIX_BLOCK
            planes = _bit_transpose32([key[base + SUBLANES * m:base + SUBLANES * (m + 1), :] ^ _INT_MIN
                                       for m in range(32)])
            bl = kt * blk_per_tile + half
            for p in range(32):
                plane_scr[bl, p] = planes[p]
            alive_scr[bl] = jnp.full((SUBLANES, tq), -1, I32)
        return carry

    n_unmasked = lax.shift_right_logical(q0, int(np.log2(tk)))
    lax.fori_loop(0, n_unmasked, functools.partial(p1, masked=False), 0)
    lax.fori_loop(n_unmasked, nk, functools.partial(p1, masked=True), 0)

    def clear_block(bl, carry):
        plane_scr[bl] = jnp.zeros((32, SUBLANES, tq), I32)
        alive_scr[bl] = jnp.zeros((SUBLANES, tq), I32)
        return carry

    n_held = nk * blk_per_tile
    quarter = n_blocks // 4
    n_walk = lax.div(n_held + quarter - 1, quarter) * quarter
    lax.fori_loop(n_held, n_walk, clear_block, 0)

    sent_u = jnp.int32(_SENT_KEY ^ _INT_MIN)
    kf = float(TOPK)

    def radix_select(n_walk):
        def bit_step(p, carry):
            k_rem, thr_bits, rest_alive = carry
            shift = 31 - p
            sent_bit = lax.shift_right_logical(sent_u, shift) & 1
            ones = [alive_scr[bl] & plane_scr[bl, p] for bl in range(n_walk)]
            tot = lax.population_count(ones[0])
            for bl in range(1, n_walk):
                tot = tot + lax.population_count(ones[bl])
            cnt = jnp.sum(tot.astype(F32), axis=0, keepdims=True)
            cnt = cnt + jnp.where((rest_alive != 0) & (sent_bit != 0), n_rest, 0.0)
            take1 = cnt >= k_rem
            for bl in range(n_walk):
                alive_scr[bl] = jnp.where(take1, ones[bl], alive_scr[bl] & ~plane_scr[bl, p])
            k_rem = jnp.where(take1, k_rem, k_rem - cnt)
            thr_bits = thr_bits | jnp.where(take1, lax.shift_left(jnp.int32(1), shift), 0)
            rest_alive = jnp.where(take1 == (sent_bit != 0), rest_alive, 0)
            return k_rem, thr_bits, rest_alive

        _, thr_bits, _ = lax.fori_loop(
            0, 32, bit_step,
            (jnp.full((1, tq), kf, F32), jnp.zeros((1, tq), I32), jnp.ones((1, tq), I32)))
        thr_scr[...] = thr_bits

    for walk in range(quarter, n_blocks + 1, quarter):
        pl.when(n_walk == walk)(functools.partial(radix_select, walk))
    thr_key = thr_scr[...] ^ _INT_MIN
    thr0 = pltpu.bitcast(thr_key ^ (lax.shift_right_arithmetic(thr_key, 31) & 0x7FFFFFFF), F32)

    def rank_counts(thr):
        def body(kt, c):
            sc = sc_scr[kt]
            return (c[0] + jnp.sum(jnp.where(sc >= thr, 1.0, 0.0), axis=0, keepdims=True),
                    c[1] + jnp.sum(jnp.where(sc > thr, 1.0, 0.0), axis=0, keepdims=True))
        z = jnp.zeros((1, tq), F32)
        n_ge, n_gt = lax.fori_loop(0, nk, body, (z, z))
        return (n_ge + jnp.where(thr <= NEG_INF, n_rest, 0.0), n_gt + jnp.where(thr < NEG_INF, n_rest, 0.0))

    def misplaced(state):
        _, n_ge, n_gt = state
        return jnp.max(jnp.where((n_ge < kf) | (n_gt >= kf), 1.0, 0.0)) > 0.0

    def step_threshold(state):
        thr, n_ge, n_gt = state

        def body(kt, c):
            sc = sc_scr[kt]
            below = jnp.max(jnp.where(sc < thr, sc, -jnp.inf), axis=0, keepdims=True)
            above = jnp.min(jnp.where(sc > thr, sc, jnp.inf), axis=0, keepdims=True)
            return jnp.maximum(c[0], below), jnp.minimum(c[1], above)

        below, above = lax.fori_loop(0, nk, body, (jnp.full((1, tq), -jnp.inf, F32), jnp.full((1, tq), jnp.inf, F32)))
        has_rest = n_rest > 0.0
        below = jnp.where(has_rest & (thr > NEG_INF), jnp.maximum(below, NEG_INF), below)
        above = jnp.where(has_rest & (thr < NEG_INF), jnp.minimum(above, NEG_INF), above)
        thr = jnp.where(n_ge < kf, below, jnp.where(n_gt >= kf, above, thr))
        return (thr,) + rank_counts(thr)

    thr, n_ge, n_gt = lax.while_loop(misplaced, step_threshold, (thr0,) + rank_counts(thr0))
    need = kf - n_gt
    tie_free = jnp.min(jnp.where(n_ge == kf, 1.0, 0.0)) > 0.0

    tr = lax.broadcasted_iota(I32, (tk, tk), 0)
    tc = lax.broadcasted_iota(I32, (tk, tk), 1)
    tri = jnp.where(tc <= tr, 1.0, 0.0).astype(BF16)
    _softmax_init(m_scr, l_scr, acc_scr)

    def selection_bias(kt, tie):
        sc = sc_scr[kt]
        eq = sc == thr
        pref = _dot(tri, jnp.where(eq, 1.0, 0.0).astype(BF16)) + tie
        sel = (sc > thr) | (eq & (pref <= need))
        return jnp.where(sel & admissible(kt * tk), 0.0, NEG_INF), pref[tk - 1:tk, :]

    def logits(kt):
        return _dot(ka_ref[0, pl.ds(pl.multiple_of(kt * tk, tk), tk), :], aqa_scr[...])

    def accumulate(kt, bias):
        logit = logits(kt)
        v_t = vat_ref[0, kt]
        for h in range(N_HEADS):
            _softmax_accumulate(logit[:, h * tq:(h + 1) * tq] + bias, m_scr[h], l_scr[h], acc_scr[h], v_t)

    def p3_single(kt, tie):
        bias, tie = selection_bias(kt, tie)
        accumulate(kt, bias)
        return tie

    def p3_single_tie_free(kt, carry):
        accumulate(kt, jnp.where(sc_scr[kt] >= thr, 0.0, NEG_INF))
        return carry

    def p3_max(kt, tie):
        bias, tie = selection_bias(kt, tie)
        bias_scr[kt] = bias
        logit = logits(kt)
        for h in range(N_HEADS):
            _softmax_max(logit[:, h * tq:(h + 1) * tq] + bias, m_scr[h])
        return tie

    def p3_accumulate(kt, carry):
        accumulate(kt, bias_scr[kt])
        return carry

    kn2 = jnp.broadcast_to(kn_ref[0, 0:1, :], (N_HEADS, seq))
    bounds, safe = _logit_bounds(qat_ref, kn2, 0.0)
    no_tie = jnp.zeros((1, tq), F32)

    @pl.when(safe)
    def _():
        for h in range(N_HEADS):
            m_scr[h][...] = bounds[h]

    @pl.when(safe & tie_free)
    def _():
        lax.fori_loop(0, nk, p3_single_tie_free, 0)

    @pl.when(safe & jnp.logical_not(tie_free))
    def _():
        lax.fori_loop(0, nk, p3_single, no_tie)

    @pl.when(jnp.logical_not(safe))
    def _():
        lax.fori_loop(0, nk, p3_max, no_tie)
        lax.fori_loop(0, nk, p3_accumulate, 0)

    _softmax_finish(o_ref, l_scr, acc_scr)


def _dsa_call(qat, qit, aux_t, ki3, ka3, vat4):
    batch, _, seq = qat.shape
    qspec = pl.BlockSpec((1, 512, TQ_DSA), lambda b, i: (b, 0, i))
    kspec = pl.BlockSpec((1, seq, LANES), lambda b, i: (b, 0, 0))
    return pl.pallas_call(
        functools.partial(_dsa_kernel, seq=seq),
        grid=(batch, seq // TQ_DSA),
        in_specs=[qspec, qspec,
                  pl.BlockSpec((1, N_HEADS, TQ_DSA), lambda b, i: (b, 0, i)),
                  pl.BlockSpec((1, N_HEADS, seq), lambda b, i: (b, 3, 0)),
                  kspec, kspec,
                  pl.BlockSpec((1, seq // TK_DSA, HEAD_DIM, TK_DSA), lambda b, i: (b, 0, 0, 0))],
        out_specs=qspec,
        out_shape=jax.ShapeDtypeStruct((batch, 512, seq), BF16),
        scratch_shapes=[pltpu.VMEM((LANES, N_HEADS * TQ_DSA), BF16),
                        pltpu.VMEM((LANES, N_HEADS * TQ_DSA), BF16),
                        pltpu.VMEM((seq // TK_DSA, TK_DSA, TQ_DSA), F32),
                        pltpu.VMEM((seq // TK_DSA, TK_DSA, TQ_DSA), F32),
                        pltpu.VMEM((seq // RADIX_BLOCK, 32, SUBLANES, TQ_DSA), I32),
                        pltpu.VMEM((seq // RADIX_BLOCK, SUBLANES, TQ_DSA), I32),
                        pltpu.VMEM((1, TQ_DSA), I32)] + _softmax_scratch(TQ_DSA),
        compiler_params=_cparams(("arbitrary", "arbitrary")),
        name="dsa",
    )(qat, qit, aux_t, aux_t, ki3, ka3, vat4)


def _outproj_kernel(oa_ref, ob_ref, sga_ref, sgb_ref, x_ref, mod_ref, gffn_ref,
                    woa_ref, wob_ref, wo_ref, wr_hi_ref, wr_lo_ref, br_ref,
                    x1_ref, h2_ref, lg_ref):
    tm = TM_OUT
    tn_dot = lambda a, b: lax.dot_general(a, b, (((0,), (0,)), ((), ())), preferred_element_type=F32)
    ya = tn_dot(oa_ref[0], woa_ref[...])
    yb = tn_dot(ob_ref[0], wob_ref[...])
    merged = (sga_ref[...].astype(F32) * ya + sgb_ref[...].astype(F32) * yb).astype(BF16)
    mix = _dot(merged, wo_ref[...])
    gt1 = mod_ref[0, 2:3, :]
    sh2 = mod_ref[0, 3:4, :]
    sc2 = mod_ref[0, 4:5, :]
    x1 = x_ref[...] + gt1 * mix
    x1_ref[...] = x1
    var = jnp.mean(x1 * x1, axis=-1, keepdims=True)
    h2 = x1 * lax.rsqrt(var + RMS_EPS) * gffn_ref[...] * (1.0 + sc2) + sh2
    for c in range(D_MODEL // LANES):
        h2_ref[pl.ds(c, tm, stride=SUBLANES), :] = h2[:, c * LANES:(c + 1) * LANES]
    hi = h2.astype(BF16)
    lo = (h2 - hi.astype(F32)).astype(BF16)
    r = _dot(jnp.concatenate([hi, lo], axis=0), jnp.concatenate([wr_hi_ref[...], wr_lo_ref[...]], axis=1))
    lg_ref[...] = r[0:tm, 0:LANES] + r[0:tm, LANES:2 * LANES] + r[tm:2 * tm, 0:LANES] + br_ref[...]


def _outproj_call(oa2, ob2, sga, sgb, x2, mod3, g_ffn, woa, wob, wo, wr_hi, wr_lo, br, seq):
    n = x2.shape[0]
    tpb = seq // TM_OUT
    row = lambda w: pl.BlockSpec((TM_OUT, w), lambda i: (i, 0))
    full = lambda a: pl.BlockSpec(a.shape, lambda i: (0,) * a.ndim)
    return pl.pallas_call(
        _outproj_kernel,
        grid=(n // TM_OUT,),
        in_specs=[pl.BlockSpec((1, 512, TM_OUT), lambda i: (i // tpb, 0, i % tpb)),
                  pl.BlockSpec((1, 512, TM_OUT), lambda i: (i // tpb, 0, i % tpb)),
                  row(1024), row(1024), row(D_MODEL),
                  pl.BlockSpec((1, 6, D_MODEL), lambda i: (i // tpb, 0, 0)),
                  full(g_ffn), full(woa), full(wob), full(wo), full(wr_hi), full(wr_lo), full(br)],
        out_specs=[row(D_MODEL),
                   pl.BlockSpec((TM_OUT * SUBLANES, LANES), lambda i: (i, 0)),
                   row(LANES)],
        out_shape=[jax.ShapeDtypeStruct((n, D_MODEL), F32),
                   jax.ShapeDtypeStruct((n * SUBLANES, LANES), F32),
                   jax.ShapeDtypeStruct((n, LANES), F32)],
        compiler_params=_cparams(("arbitrary",)),
        name="outproj",
    )(oa2, ob2, sga, sgb, x2, mod3, g_ffn, woa, wob, wo, wr_hi, wr_lo, br)


def _route_rows(lg):
    tm = lg.shape[0]
    lane = lax.broadcasted_iota(I32, (tm, LANES), 1)
    big = jnp.int32(LANES)
    ninf = -jnp.inf
    gmask = (lane >= N_EXPERTS) & (lane < N_EXPERTS + N_GROUPS)
    g = jnp.where(gmask, lg, ninf)
    gmax = jnp.max(g, axis=-1, keepdims=True)
    grp = jnp.min(jnp.where(g == gmax, lane - N_EXPERTS, big), axis=-1, keepdims=True)
    p_grp = 1.0 / jnp.sum(jnp.where(gmask, jnp.exp(lg - gmax), 0.0), axis=-1, keepdims=True)
    lo = grp * EXPERTS_PER_GROUP
    emask = (lane >= lo) & (lane < lo + EXPERTS_PER_GROUP)
    ev = jnp.where(emask, lg, ninf)
    v0 = jnp.max(ev, axis=-1, keepdims=True)
    i0 = jnp.min(jnp.where(emask & (ev == v0), lane, big), axis=-1, keepdims=True)
    rest = emask & (lane != i0)
    ev1 = jnp.where(rest, lg, ninf)
    v1 = jnp.max(ev1, axis=-1, keepdims=True)
    i1 = jnp.min(jnp.where(rest & (ev1 == v1), lane, big), axis=-1, keepdims=True)
    e1 = jnp.exp(v1 - v0)
    w0 = p_grp / (1.0 + e1)
    w1 = p_grp * e1 / (1.0 + e1)
    rt = jnp.where(lane == 0, i0.astype(F32),
                   jnp.where(lane == 1, i1.astype(F32),
                             jnp.where(lane == 2, w0, jnp.where(lane == 3, w1, 0.0))))
    return rt, lane == i0, lane == i1


def _expert_onehots(rt, tm):
    lane = lax.broadcasted_iota(I32, (tm, LANES), 1)
    lane_f = lane.astype(F32)
    e0 = jnp.sum(jnp.where(lane == 0, rt, 0.0), axis=-1, keepdims=True)
    e1 = jnp.sum(jnp.where(lane == 1, rt, 0.0), axis=-1, keepdims=True)
    return lane, lane_f == e0, lane_f == e1


def _rank_rows(is0, is1, seen_before):
    tm = is0.shape[0]
    lane = lax.broadcasted_iota(I32, (tm, LANES), 1)
    hits = jnp.where(is0, 1.0, 0.0) + jnp.where(is1, 1.0, 0.0)
    rr = lax.broadcasted_iota(I32, (tm, tm), 0)
    cc = lax.broadcasted_iota(I32, (tm, tm), 1)
    before = jnp.where(cc < rr, 1.0, 0.0).astype(BF16)
    seen = _dot(before, hits.astype(BF16)) + seen_before
    r0 = jnp.sum(jnp.where(is0, seen, 0.0), axis=-1, keepdims=True)
    r1 = jnp.sum(jnp.where(is1, seen, 0.0), axis=-1, keepdims=True)
    rk = jnp.where(lane == 0, r0, jnp.where(lane == 1, r1, 0.0))
    return rk, seen_before + jnp.sum(hits, axis=0, keepdims=True)


def _route_rank_kernel(lg_ref, rt_ref, rk_ref, cnt_ref, seen_scr):
    @pl.when(pl.program_id(0) == 0)
    def _():
        seen_scr[...] = jnp.zeros_like(seen_scr)

    rt, is0, is1 = _route_rows(lg_ref[...])
    rt_ref[...] = rt
    rk, seen = _rank_rows(is0, is1, seen_scr[...])
    rk_ref[...] = rk
    seen_scr[...] = seen
    cnt_ref[...] = jnp.broadcast_to(seen, cnt_ref.shape)


def _route_rank_call(lg):
    n = lg.shape[0]
    spec = pl.BlockSpec((TM_ROUTE, LANES), lambda i: (i, 0))
    return pl.pallas_call(
        _route_rank_kernel, grid=(n // TM_ROUTE,), in_specs=[spec],
        out_specs=[spec, spec, pl.BlockSpec((SUBLANES, LANES), lambda i: (0, 0))],
        out_shape=[jax.ShapeDtypeStruct((n, LANES), F32), jax.ShapeDtypeStruct((n, LANES), F32),
                   jax.ShapeDtypeStruct((SUBLANES, LANES), F32)],
        scratch_shapes=[pltpu.VMEM((1, LANES), F32)],
        compiler_params=_cparams(("arbitrary",)), name="route_rank",
    )(lg)


def _dest_kernel(rt_ref, rk_ref, cnt_ref, dd_ref, be_ref, nu_ref, end_ref):
    tm = TM_ROUTE
    lane, is0, is1 = _expert_onehots(rt_ref[...], tm)
    blocks = jnp.floor((cnt_ref[...] + (BLK_E - 1)) * (1.0 / BLK_E))
    er = lax.broadcasted_iota(I32, (LANES, LANES), 0)
    ec = lax.broadcasted_iota(I32, (LANES, LANES), 1)
    upto = jnp.where(er <= ec, 1.0, 0.0).astype(BF16)
    bend = _dot(blocks.astype(BF16), upto)
    pstart = (bend[0:1, :] - blocks[0:1, :]) * BLK_E
    rk = rk_ref[...]
    r0 = jnp.sum(jnp.where(lane == 0, rk, 0.0), axis=-1, keepdims=True)
    r1 = jnp.sum(jnp.where(lane == 1, rk, 0.0), axis=-1, keepdims=True)
    d0 = jnp.sum(jnp.where(is0, pstart, 0.0), axis=-1, keepdims=True) + r0
    d1 = jnp.sum(jnp.where(is1, pstart, 0.0), axis=-1, keepdims=True) + r1
    dd_ref[...] = jnp.where(lane == 0, d0, jnp.where(lane == 1, d1, 0.0)).astype(I32)
    nb = be_ref.shape[0]
    blk = lax.broadcasted_iota(I32, (nb, LANES), 0).astype(F32)
    lane_b = lax.broadcasted_iota(I32, (nb, LANES), 1)
    ended = jnp.where((lane_b < N_EXPERTS) & (bend[0:1, :] <= blk), 1.0, 0.0)
    be = jnp.minimum(jnp.sum(ended, axis=-1, keepdims=True), float(N_EXPERTS - 1))
    be_ref[...] = jnp.broadcast_to(be, be_ref.shape).astype(I32)
    lane_c = lax.broadcasted_iota(I32, (SUBLANES, LANES), 1)
    used = jnp.sum(jnp.where(lane_c == N_EXPERTS - 1, bend, 0.0), axis=-1, keepdims=True)
    nu_ref[...] = jnp.broadcast_to(used, nu_ref.shape).astype(I32)
    end_ref[...] = (bend * BLK_E).astype(I32)


def _dest_call(rt, rk, cnt, n_blocks):
    n = rt.shape[0]
    nb_pad = -(-n_blocks // SUBLANES) * SUBLANES
    spec = pl.BlockSpec((TM_ROUTE, LANES), lambda i: (i, 0))
    const = lambda rows: pl.BlockSpec((rows, LANES), lambda i: (0, 0))
    return pl.pallas_call(
        _dest_kernel, grid=(n // TM_ROUTE,), in_specs=[spec, spec, const(SUBLANES)],
        out_specs=[spec, const(nb_pad), const(SUBLANES), const(SUBLANES)],
        out_shape=[jax.ShapeDtypeStruct((n, LANES), I32), jax.ShapeDtypeStruct((nb_pad, LANES), I32),
                   jax.ShapeDtypeStruct((SUBLANES, LANES), I32), jax.ShapeDtypeStruct((SUBLANES, LANES), I32)],
        compiler_params=_cparams(("arbitrary",)), name="dest",
    )(rt, rk, cnt)


def _slab_copy(src, src_row, dst, dst_row, sem):
    return pltpu.make_async_copy(src.at[pl.ds(pl.multiple_of(src_row * SUBLANES, SUBLANES), SUBLANES)],
                                 dst.at[pl.ds(pl.multiple_of(dst_row * SUBLANES, SUBLANES), SUBLANES)], sem)


def _dispatch_kernel(dest_ref, seg_end_ref, h2_ref, xs_hbm, zero_scr, sem, zsem):
    tm = TM_DISPATCH
    base = pl.program_id(0) * tm

    @pl.when(pl.program_id(0) == 0)
    def _():
        zero_scr[...] = jnp.zeros_like(zero_scr)
        clears = [pltpu.make_async_copy(
            zero_scr, xs_hbm.at[pl.ds(pl.multiple_of(jnp.maximum(seg_end_ref[e] - BLK_E, 0) * SUBLANES, SUBLANES),
                                      BLK_E * SUBLANES)], zsem) for e in range(N_EXPERTS)]
        for c in clears:
            c.start()
        for c in clears:
            c.wait()
        n_blocks = xs_hbm.shape[0] // (BLK_E * SUBLANES)
        first_free = lax.shift_right_logical(seg_end_ref[N_EXPERTS - 1], int(np.log2(BLK_E)))
        def tail_clear(b):
            row = pl.multiple_of((first_free + b) * (BLK_E * SUBLANES), BLK_E * SUBLANES)
            return pltpu.make_async_copy(zero_scr, xs_hbm.at[pl.ds(row, BLK_E * SUBLANES)], zsem)

        for b in range(N_EXPERTS):
            pl.when(first_free + b < n_blocks)(lambda b=b: tail_clear(b).start())
        for b in range(N_EXPERTS):
            pl.when(first_free + b < n_blocks)(lambda b=b: tail_clear(b).wait())

    def issue(r, _):
        for k in range(2):
            _slab_copy(h2_ref, r, xs_hbm, dest_ref[2 * (base + r) + k], sem).start(priority=k)
        return 0

    lax.fori_loop(0, tm, issue, 0, unroll=GATHER_UNROLL)
    for _ in range(2):
        pltpu.make_async_copy(h2_ref, xs_hbm.at[pl.ds(0, tm * SUBLANES)], sem).wait()


def _dispatch_call(dest, seg_end, h2s, n_slots):
    n = h2s.shape[0] // SUBLANES
    return pl.pallas_call(
        _dispatch_kernel,
        grid_spec=pltpu.PrefetchScalarGridSpec(
            num_scalar_prefetch=2, grid=(n // TM_DISPATCH,),
            in_specs=[pl.BlockSpec((TM_DISPATCH * SUBLANES, LANES), lambda i, d, e: (i, 0))],
            out_specs=pl.BlockSpec(memory_space=pl.ANY),
            scratch_shapes=[pltpu.VMEM((BLK_E * SUBLANES, LANES), F32),
                            pltpu.SemaphoreType.DMA(()), pltpu.SemaphoreType.DMA(())]),
        out_shape=jax.ShapeDtypeStruct((n_slots * SUBLANES, LANES), F32),
        compiler_params=_cparams(("arbitrary",)),
        name="dispatch",
    )(dest, seg_end, h2s)


def _experts_kernel(be_ref, nused_ref, xs_ref, wg_hbm, wu_hbm, wd_hbm, yb_ref,
                    x_scr, wg_f32, wu_f32, wd_f32, wsem, run_smem, wg_scr, wu_scr, wd_scr):
    i = pl.program_id(0)
    n_used = nused_ref[0]
    n_blocks = be_ref.shape[0]

    def weight_copies(e, slot):
        return [pltpu.make_async_copy(wg_hbm.at[e], wg_f32.at[slot], wsem.at[slot, 0]),
                pltpu.make_async_copy(wu_hbm.at[e], wu_f32.at[slot], wsem.at[slot, 1]),
                pltpu.make_async_copy(wd_hbm.at[e], wd_f32.at[slot], wsem.at[slot, 2])]

    @pl.when((i == 0) & (n_used > 0))
    def _():
        run_smem[0] = 0
        for c in weight_copies(be_ref[0], 0):
            c.start()

    @pl.when(i >= n_used)
    def _():
        yb_ref[...] = jnp.zeros_like(yb_ref)

    @pl.when(i < n_used)
    def _():
        e = be_ref[i]
        prev = be_ref[jnp.maximum(i - 1, 0)]

        @pl.when((i == 0) | (e != prev))
        def _():
            slot = run_smem[0] % 2
            for c in weight_copies(e, slot):
                c.wait()
            wg_scr[...] = wg_f32[slot].astype(BF16)
            wu_scr[...] = wu_f32[slot].astype(BF16)
            wd_scr[...] = wd_f32[slot].astype(BF16)
            nxt = lax.while_loop(
                lambda j: (j < n_used) & (be_ref[jnp.minimum(j, n_blocks - 1)] == e), lambda j: j + 1, i + 1)

            @pl.when(nxt < n_used)
            def _():
                for c in weight_copies(be_ref[jnp.minimum(nxt, n_blocks - 1)], 1 - slot):
                    c.start()

            run_smem[0] = run_smem[0] + 1

        for c in range(D_MODEL // LANES):
            x_scr[:, c * LANES:(c + 1) * LANES] = xs_ref[pl.ds(c, BLK_E, stride=SUBLANES), :].astype(BF16)
        xb = x_scr[...]
        g = _dot(xb, wg_scr[...])
        u = _dot(xb, wu_scr[...])
        hid = (g * _sigmoid(g) * u).astype(BF16)
        y = _dot(hid, wd_scr[...])
        for c in range(D_MODEL // LANES):
            yb_ref[pl.ds(c, BLK_E, stride=SUBLANES), :] = y[:, c * LANES:(c + 1) * LANES]


def _experts_call(block_e, n_used, xs, w_gate, w_up, w_down):
    n_blocks = block_e.shape[0]
    slab = pl.BlockSpec((BLK_E * SUBLANES, LANES), lambda i, be, nu: (i, 0))
    used_slab = pl.BlockSpec((BLK_E * SUBLANES, LANES), lambda i, be, nu: (jnp.minimum(i, nu[0] - 1), 0))
    return pl.pallas_call(
        _experts_kernel,
        grid_spec=pltpu.PrefetchScalarGridSpec(
            num_scalar_prefetch=2, grid=(n_blocks,),
            in_specs=[used_slab] + [pl.BlockSpec(memory_space=pl.ANY)] * 3,
            out_specs=slab,
            scratch_shapes=[pltpu.VMEM((BLK_E, D_MODEL), BF16),
                            pltpu.VMEM((2, D_MODEL, D_EXPERT), F32),
                            pltpu.VMEM((2, D_MODEL, D_EXPERT), F32),
                            pltpu.VMEM((2, D_EXPERT, D_MODEL), F32),
                            pltpu.SemaphoreType.DMA((2, 3)),
                            pltpu.SMEM((1,), I32),
                            pltpu.VMEM((D_MODEL, D_EXPERT), BF16),
                            pltpu.VMEM((D_MODEL, D_EXPERT), BF16),
                            pltpu.VMEM((D_EXPERT, D_MODEL), BF16)]),
        out_shape=jax.ShapeDtypeStruct(xs.shape, F32),
        compiler_params=_cparams(("arbitrary",)),
        name="experts",
    )(block_e, n_used, xs, w_gate, w_up, w_down)


def _final_kernel(dest_ref, x1_ref, rt_ref, mod_ref, gfin_ref, yb_hbm, o_ref, g_scr, sem):
    tm = TM_FINAL
    step = pl.program_id(0)
    slot = step % 2
    slot_rows = 2 * tm * SUBLANES

    def gather(s, into):
        def issue(r, _):
            tok = s * tm + r
            for k in range(2):
                _slab_copy(yb_hbm, dest_ref[2 * tok + k], g_scr.at[into], k * tm + r,
                           sem.at[into]).start(priority=k)
            return 0
        lax.fori_loop(0, tm, issue, 0, unroll=GATHER_UNROLL)

    @pl.when(step == 0)
    def _():
        gather(0, 0)

    @pl.when(step + 1 < pl.num_programs(0))
    def _():
        gather(step + 1, 1 - slot)

    g_now = g_scr.at[slot]
    pltpu.make_async_copy(yb_hbm.at[pl.ds(0, slot_rows)], g_now, sem.at[slot]).wait()

    rt = rt_ref[...]
    lane = lax.broadcasted_iota(I32, (tm, LANES), 1)
    gw0 = jnp.sum(jnp.where(lane == 2, rt, 0.0), axis=-1, keepdims=True)
    gw1 = jnp.sum(jnp.where(lane == 3, rt, 0.0), axis=-1, keepdims=True)
    gt2 = mod_ref[0, 5:6, :]
    x1 = x1_ref[...]
    cols = []
    for c in range(D_MODEL // LANES):
        y0 = g_now[pl.ds(c, tm, stride=SUBLANES), :]
        y1 = g_now[pl.ds(tm * SUBLANES + c, tm, stride=SUBLANES), :]
        y = gw0 * y0 + gw1 * y1
        cols.append(x1[:, c * LANES:(c + 1) * LANES] + gt2[:, c * LANES:(c + 1) * LANES] * y)
    x2 = jnp.concatenate(cols, axis=1)
    var = jnp.mean(x2 * x2, axis=-1, keepdims=True)
    o_ref[...] = x2 * lax.rsqrt(var + RMS_EPS) * gfin_ref[...]


def _final_call(dest, x1, rt, mod3, g_final, yb, seq):
    n = x1.shape[0]
    tpb = seq // TM_FINAL
    return pl.pallas_call(
        _final_kernel,
        grid_spec=pltpu.PrefetchScalarGridSpec(
            num_scalar_prefetch=1, grid=(n // TM_FINAL,),
            in_specs=[pl.BlockSpec((TM_FINAL, D_MODEL), lambda i, d: (i, 0)),
                      pl.BlockSpec((TM_FINAL, LANES), lambda i, d: (i, 0)),
                      pl.BlockSpec((1, 6, D_MODEL), lambda i, d: (i // tpb, 0, 0)),
                      pl.BlockSpec((1, D_MODEL), lambda i, d: (0, 0)),
                      pl.BlockSpec(memory_space=pl.ANY)],
            out_specs=pl.BlockSpec((TM_FINAL, D_MODEL), lambda i, d: (i, 0)),
            scratch_shapes=[pltpu.VMEM((2, 2 * TM_FINAL * SUBLANES, LANES), F32),
                            pltpu.SemaphoreType.DMA((2,))]),
        out_shape=jax.ShapeDtypeStruct((n, D_MODEL), F32),
        compiler_params=_cparams(("arbitrary",)),
        name="final",
    )(dest, x1, rt, mod3, g_final, yb)


def _permute_w_in(w):
    o = np.cumsum([0, 512, 64, 64, 512, 64, 8, 512, 512, 512, 8, 1024, 1024])
    qa, ka, va, qi, ki, wi, qb, kb, vb, fb, ga, gb = [w[:, o[k]:o[k + 1]] for k in range(12)]
    aux = jnp.concatenate([wi, fb, jnp.zeros((w.shape[0], LANES - 2 * N_HEADS), w.dtype)], axis=1)
    w_rows = jnp.concatenate([kb, ga, gb, ka, ka, ki, ki, aux], axis=1).astype(BF16)
    w_t = jnp.concatenate([qa, qi, qb, vb, va], axis=1).T.astype(BF16)
    return w_rows, w_t


def _layer(x2, pos2, mod3, batch, seq, g_mix, w_in, b_forget, w_out_a, w_out_b, w_out, g_ffn,
           w_group, b_group, w_router, b_router, w_e_gate, w_e_up, w_e_down, g_final):
    n = x2.shape[0]
    inv_freq = ROPE_THETA ** (-jnp.arange(0, ROPE_DIM, 2, dtype=F32) / ROPE_DIM)
    jj = np.arange(LANES) % HEAD_DIM
    invf = jnp.where(jj < ROPE_DIM, inv_freq[jj % (ROPE_DIM // 2)], 0.0)[None, :].astype(F32)
    fbias = jnp.zeros((1, LANES), F32).at[0, N_HEADS:2 * N_HEADS].set(b_forget.astype(F32))

    invf8 = jnp.broadcast_to(inv_freq[:, None], (ROPE_DIM // 2, LANES)).astype(F32)
    w_rows, w_t = _permute_w_in(w_in)
    (qat, qit, qbt, kaug, vbt, sga, sgb, ka2, vat, ki2, aux_t) = _inproj_call(
        x2, pos2, pos2.reshape(batch, 1, seq), mod3, g_mix.reshape(1, D_MODEL), w_rows, w_t, invf, invf8, fbias,
        batch, seq)

    r3 = lambda a: a.reshape(batch, seq, a.shape[-1])
    obt = _fox_call(qbt, aux_t, r3(kaug), vbt)
    oat = _dsa_call(qat, qit, aux_t, r3(ki2), r3(ka2), vat)

    w_rt = jnp.concatenate([w_router, w_group, jnp.zeros((D_MODEL, LANES - N_EXPERTS - N_GROUPS), F32)], axis=1)
    wr_hi = w_rt.astype(BF16)
    wr_lo = (w_rt - wr_hi.astype(F32)).astype(BF16)
    br = jnp.concatenate([b_router, b_group, jnp.zeros((LANES - N_EXPERTS - N_GROUPS,), F32)])[None, :]
    x1, h2s, lg = _outproj_call(oat, obt, sga, sgb, x2, mod3,
                                g_ffn.reshape(1, D_MODEL), w_out_a.astype(BF16), w_out_b.astype(BF16),
                                w_out.astype(BF16), wr_hi, wr_lo, br, seq)
    rt, rk, cnt = _route_rank_call(lg)

    n_slots = 2 * n + N_EXPERTS * BLK_E
    n_blocks = n_slots // BLK_E
    dd, be, nu, seg_end = _dest_call(rt, rk, cnt, n_blocks)
    dest = dd[:, 0:2].reshape(-1)
    block_e = be[:n_blocks, 0]
    n_used = nu[0, 0:1]

    xs = _dispatch_call(dest, seg_end[0, :N_EXPERTS], h2s, n_slots)
    yb = _experts_call(block_e, n_used, xs, w_e_gate, w_e_up, w_e_down)
    return _final_call(dest, x1, rt, mod3, g_final.reshape(1, D_MODEL), yb, seq)


def kernel(x, c, positions, w_mod, b_mod, g_mix, w_in, b_forget, w_out_a, w_out_b, w_out, g_ffn, w_group,
           b_group, w_router, b_router, w_e_gate, w_e_up, w_e_down, g_final):
    batch, seq, d = x.shape
    depth = w_mod.shape[0]
    assert depth == 1 and d == D_MODEL, "kernel fuses the final norm into the single layer"
    n = batch * seq
    c8 = jnp.zeros((8, d), F32).at[:batch].set(c)
    mod = _mod_call(c8, w_mod[0], b_mod[0][None, :])
    mod3 = mod[:batch].reshape(batch, 6, d)
    out = _layer(x.reshape(n, d), positions.reshape(n, 1), mod3, batch, seq, g_mix[0], w_in[0], b_forget[0],
                 w_out_a[0], w_out_b[0], w_out[0], g_ffn[0], w_group[0], b_group[0], w_router[0], b_router[0],
                 w_e_gate[0], w_e_up[0], w_e_down[0], g_final)
    return out.reshape(batch, seq, d)
```

```python
import functools

import numpy as np
import jax
import jax.numpy as jnp
from jax import lax
from jax.experimental import pallas as pl
from jax.experimental.pallas import tpu as pltpu

F32 = jnp.float32
BF16 = jnp.bfloat16
I32 = jnp.int32

D_MODEL = 1024
HEAD_DIM = 64
N_HEADS = 8
CHUNK_SHIFT = 6
TOPK = 256
ROPE_DIM = 16
ROPE_THETA = 500000.0
N_GROUPS = 4
EXPERTS_PER_GROUP = 8
N_EXPERTS = 32
D_EXPERT = 512
RMS_EPS = 1e-6
NEG_INF = -1e30
ATTN_SCALE = HEAD_DIM ** -0.5
IDX_SCALE = HEAD_DIM ** -0.5
LOG2E = 1.4426950408889634

LANES = 128
SUBLANES = 8
VMEM_LIMIT = 56 * 1024 * 1024

TM_IN = 512
TQ_FOX = 512
TK_FOX = 512
TQ_DSA = 512
TK_DSA = 512
TM_OUT = 512
TM_ROUTE = 1024
TM_DISPATCH = 1024
BLK_E = 256
TM_FINAL = 512
GATHER_UNROLL = 8

RADIX_BLOCK = 32 * SUBLANES
BOUND_MARGIN = 1.02
BOUND_SAFE = 40.0
FOX_BIAS_SLACK = 1.0
FOX_K = 256
FOX_AUG_ROWS = 16

_SEG_KB, _SEG_GA, _SEG_GB, _SEG_KA, _SEG_KI, _SEG_AUX = 0, 512, 1536, 2560, 2688, 2816
_NC_IN = 2944
_ROW_QA, _ROW_QI, _ROW_QB, _ROW_VB, _ROW_VA = 0, 512, 1024, 1536, 2048
_NR_IN = 2048 + HEAD_DIM

_SENT_KEY = int(np.array(NEG_INF, np.float32).view(np.int32) ^ 0x7FFFFFFF)
_INT_MIN = -(2 ** 31)


def _cparams(sem, fuse_inputs=None):
    return pltpu.CompilerParams(dimension_semantics=sem, vmem_limit_bytes=VMEM_LIMIT, allow_input_fusion=fuse_inputs)


def _dot(a, b):
    return jnp.dot(a, b, preferred_element_type=F32)


def _sigmoid(x):
    return 1.0 / (1.0 + jnp.exp(-x))


def _mod_kernel(c_ref, w_ref, b_ref, o_ref):
    c = c_ref[...]
    ca = c * _sigmoid(c)
    o_ref[...] = _dot(ca.astype(BF16), w_ref[...].astype(BF16)) + b_ref[...]


def _mod_call(c8, w_mod, b_mod):
    n_out = w_mod.shape[1]
    tn = 1024
    return pl.pallas_call(
        _mod_kernel,
        grid=(n_out // tn,),
        in_specs=[pl.BlockSpec((8, D_MODEL), lambda j: (0, 0)),
                  pl.BlockSpec((D_MODEL, tn), lambda j: (0, j)),
                  pl.BlockSpec((1, tn), lambda j: (0, j))],
        out_specs=pl.BlockSpec((8, tn), lambda j: (0, j)),
        out_shape=jax.ShapeDtypeStruct((8, n_out), F32),
        compiler_params=_cparams(("arbitrary",)),
        name="mod",
    )(c8, w_mod, b_mod)


def _inproj_kernel(x_ref, pos_ref, posr_ref, mod_ref, g_ref, w_ref, wt_ref, invf_ref, invf8_ref, fbias_ref,
                   sel_ref, ones_ref,
                   qat_ref, qit_ref, qbt_ref, kb_ref, vbt_ref, sga_ref, sgb_ref,
                   ka_ref, vat_ref, ki_ref, auxt_ref, h_scr, carry_scr):
    tm = TM_IN
    x = x_ref[...]
    var = jnp.mean(x * x, axis=-1, keepdims=True)
    tf = x * lax.rsqrt(var + RMS_EPS) * g_ref[...]
    sh = mod_ref[0, 0:1, :]
    sc = mod_ref[0, 1:2, :]
    h_scr[...] = (tf * (1.0 + sc) + sh).astype(BF16)
    hb = h_scr[...]

    lane = lax.broadcasted_iota(I32, (tm, LANES), 1)
    j = lane & (HEAD_DIM - 1)
    ang = pos_ref[...].astype(F32) * invf_ref[...]
    cs = jnp.cos(ang)
    sn = jnp.sin(ang)
    coef_next = jnp.where(j < ROPE_DIM // 2, -sn, 0.0)
    coef_prev = jnp.where((j >= ROPE_DIM // 2) & (j < ROPE_DIM), sn, 0.0)

    def rope(tc):
        return (tc * cs + pltpu.roll(tc, LANES - ROPE_DIM // 2, 1) * coef_next
                + pltpu.roll(tc, ROPE_DIM // 2, 1) * coef_prev)

    def seg(off, width):
        return _dot(hb, w_ref[:, off:off + width])

    def seg_t(row0, rows):
        return lax.dot_general(wt_ref[row0:row0 + rows, :], hb, (((1,), (1,)), ((), ())),
                               preferred_element_type=F32)

    ang8 = invf8_ref[:, 0:1] * posr_ref[0].astype(F32)
    cs8 = jnp.cos(ang8)
    sn8 = jnp.sin(ang8)
    half = ROPE_DIM // 2

    def rope_t(t, scale, out_ref):
        for h in range(N_HEADS):
            r0 = h * HEAD_DIM
            lo = t[r0:r0 + half, :]
            hi = t[r0 + half:r0 + ROPE_DIM, :]
            head = jnp.concatenate([lo * cs8 - hi * sn8, hi * cs8 + lo * sn8, t[r0 + ROPE_DIM:r0 + HEAD_DIM, :]],
                                   axis=0)
            out_ref[0, r0:r0 + HEAD_DIM, :] = (head * scale).astype(BF16)

    rope_t(seg_t(_ROW_QA, 512), ATTN_SCALE * LOG2E, qat_ref)
    rope_t(seg_t(_ROW_QI, 512), IDX_SCALE, qit_ref)
    qbt_ref[0] = (seg_t(_ROW_QB, 512) * (ATTN_SCALE * LOG2E)).astype(BF16)
    vbt_ref[0, 0] = seg_t(_ROW_VB, 512).astype(BF16)
    vat_ref[0, 0] = seg_t(_ROW_VA, HEAD_DIM).astype(BF16)
    sga_ref[...] = _sigmoid(seg(_SEG_GA, 1024)).astype(BF16)
    sgb_ref[...] = _sigmoid(seg(_SEG_GB, 1024)).astype(BF16)
    ka = rope(seg(_SEG_KA, LANES))
    ka_ref[...] = ka.astype(BF16)
    ki_ref[...] = rope(seg(_SEG_KI, LANES)).astype(BF16)

    z = seg(_SEG_AUX, LANES)
    zf = z + fbias_ref[...]
    logf = jnp.minimum(zf, 0.0) - jnp.log(1.0 + jnp.exp(-jnp.abs(zf)))
    is_f = (lane >= N_HEADS) & (lane < 2 * N_HEADS)
    logf = jnp.where(is_f, logf, 0.0)

    def split3(v):
        hi = v.astype(BF16)
        r1 = v - hi.astype(F32)
        mid = r1.astype(BF16)
        return hi, mid, (r1 - mid.astype(F32)).astype(BF16)

    rr = lax.broadcasted_iota(I32, (tm, tm), 0)
    cc = lax.broadcasted_iota(I32, (tm, tm), 1)
    tri = jnp.where(cc <= rr, 1.0, 0.0).astype(BF16)
    p_hi, p_mid, p_lo = split3(logf)

    @pl.when(pl.program_id(1) == 0)
    def _():
        carry_scr[...] = jnp.zeros_like(carry_scr)

    parts = _dot(tri, jnp.concatenate([p_hi, p_mid, p_lo], axis=1))
    cum = parts[:, 0:LANES] + parts[:, LANES:2 * LANES] + parts[:, 2 * LANES:3 * LANES] + carry_scr[...]
    carry_scr[...] = cum[tm - 1:tm, :]
    cum2 = cum * LOG2E
    aux = jnp.where(lane < N_HEADS, z * (N_HEADS ** -0.5), jnp.where(is_f, cum2, 0.0))

    kb = seg(_SEG_KB, 512)
    first = lane < HEAD_DIM

    def half_norms(blk):
        sq = blk * blk
        return (jnp.sum(jnp.where(first, sq, 0.0), axis=-1, keepdims=True),
                jnp.sum(jnp.where(first, 0.0, sq), axis=-1, keepdims=True))

    for p in range(N_HEADS // 2):
        n_even, n_odd = half_norms(kb[:, p * LANES:(p + 1) * LANES])
        aux = jnp.where(lane == 2 * N_HEADS + 2 * p, n_even, jnp.where(lane == 2 * N_HEADS + 2 * p + 1, n_odd, aux))
    aux = jnp.where(lane == 3 * N_HEADS, half_norms(ka)[0], aux)
    auxt_ref[0] = aux.T[0:4 * N_HEADS, :]

    c_hi, c_mid, c_lo = split3(cum2)
    kaug = _dot(jnp.concatenate([c_hi, c_mid, c_lo], axis=1), sel_ref[...]) + ones_ref[...]
    for p in range(N_HEADS // 2):
        kb_ref[:, p * FOX_K:p * FOX_K + LANES] = kb[:, p * LANES:(p + 1) * LANES].astype(BF16)
        kb_ref[:, p * FOX_K + LANES:(p + 1) * FOX_K] = kaug[:, p * LANES:(p + 1) * LANES].astype(BF16)


def _fox_routing_constants():
    sel = np.zeros((3, LANES, (N_HEADS // 2) * LANES), np.float32)
    ones = np.zeros((1, (N_HEADS // 2) * LANES), np.float32)
    for h in range(N_HEADS):
        for t in range(3):
            sel[t, N_HEADS + h, (h // 2) * LANES + 3 + 3 * (h % 2) + t] = -1.0
    for p in range(N_HEADS // 2):
        ones[0, p * LANES:p * LANES + 3] = 1.0
    return jnp.asarray(sel.reshape(3 * LANES, -1), BF16), jnp.asarray(ones, F32)


def _inproj_call(x2, pos2, pos_rows, mod3, g_mix, w_rows, w_t, invf, invf8, fbias, batch, seq):
    assert TM_IN == TK_FOX == TK_DSA, "value tiles are written per in-projection step"
    n = x2.shape[0]
    tpb = seq // TM_IN
    row = lambda b, s: (b * tpb + s, 0)
    rows = lambda w: (jax.ShapeDtypeStruct((n, w), BF16), pl.BlockSpec((TM_IN, w), row))
    chan = lambda c: (jax.ShapeDtypeStruct((batch, c, seq), BF16), pl.BlockSpec((1, c, TM_IN), lambda b, s: (b, 0, s)))
    tile = lambda c: (jax.ShapeDtypeStruct((batch, tpb, c, TM_IN), BF16),
                      pl.BlockSpec((1, 1, c, TM_IN), lambda b, s: (b, s, 0, 0)))
    aux_t = (jax.ShapeDtypeStruct((batch, 4 * N_HEADS, seq), F32),
             pl.BlockSpec((1, 4 * N_HEADS, TM_IN), lambda b, s: (b, 0, s)))
    outs = [chan(512), chan(512), chan(512), rows((N_HEADS // 2) * FOX_K), tile(512), rows(1024), rows(1024),
            rows(LANES), tile(HEAD_DIM), rows(LANES), aux_t]
    sel, ones = _fox_routing_constants()
    const = lambda a: pl.BlockSpec(a.shape, lambda b, s: (0,) * a.ndim)
    return pl.pallas_call(
        _inproj_kernel,
        grid=(batch, tpb),
        in_specs=[pl.BlockSpec((TM_IN, D_MODEL), row),
                  pl.BlockSpec((TM_IN, 1), row),
                  pl.BlockSpec((1, 1, TM_IN), lambda b, s: (b, 0, s)),
                  pl.BlockSpec((1, 6, D_MODEL), lambda b, s: (b, 0, 0)),
                  const(g_mix), const(w_rows), const(w_t), const(invf), const(invf8), const(fbias),
                  const(sel), const(ones)],
        out_specs=[o[1] for o in outs],
        out_shape=[o[0] for o in outs],
        scratch_shapes=[pltpu.VMEM((TM_IN, D_MODEL), BF16), pltpu.VMEM((1, LANES), F32)],
        compiler_params=_cparams(("arbitrary", "arbitrary")),
        name="inproj",
    )(x2, pos2, pos_rows, mod3, g_mix, w_rows, w_t, invf, invf8, fbias, sel, ones)


def _softmax_max(s, m_ref):
    m_ref[...] = jnp.maximum(m_ref[...], jnp.max(s, axis=0, keepdims=True))


def _softmax_accumulate(s, m_ref, l_ref, acc_ref, v_t):
    p = jnp.exp2(s - m_ref[...])
    l_ref[...] = l_ref[...] + jnp.sum(p, axis=0, keepdims=True)
    acc_ref[...] = acc_ref[...] + _dot(v_t, p.astype(BF16))


def _softmax_scratch(tq):
    return ([pltpu.VMEM((1, tq), F32) for _ in range(2 * N_HEADS)]
            + [pltpu.VMEM((HEAD_DIM, tq), F32) for _ in range(N_HEADS)])


def _softmax_split(refs):
    return refs[:N_HEADS], refs[N_HEADS:2 * N_HEADS], refs[2 * N_HEADS:3 * N_HEADS]


def _softmax_init(m_refs, l_refs, acc_refs):
    for h in range(N_HEADS):
        m_refs[h][...] = jnp.full(m_refs[h].shape, NEG_INF, F32)
        l_refs[h][...] = jnp.zeros(l_refs[h].shape, F32)
        acc_refs[h][...] = jnp.zeros(acc_refs[h].shape, F32)


def _softmax_finish(o_ref, l_refs, acc_refs):
    for h in range(N_HEADS):
        o_ref[0, h * HEAD_DIM:(h + 1) * HEAD_DIM, :] = (acc_refs[h][...] * (1.0 / l_refs[h][...])).astype(BF16)


def _logit_bounds(qt_ref, kn2, slack):
    k_max = jnp.sqrt(jnp.max(kn2, axis=-1, keepdims=True))
    bounds = []
    worst = jnp.float32(0.0)
    for h in range(N_HEADS):
        qh = qt_ref[0, h * HEAD_DIM:(h + 1) * HEAD_DIM, :].astype(F32)
        q_norm = jnp.sqrt(jnp.sum(qh * qh, axis=0, keepdims=True))
        b = q_norm * k_max[h:h + 1, :] * BOUND_MARGIN + slack
        bounds.append(b)
        worst = jnp.maximum(worst, jnp.max(b))
    return bounds, worst <= BOUND_SAFE


def _fox_kernel(qt_ref, cumt_ref, kn_ref, kaug_ref, vt_ref, o_ref, rhs_scr, *softmax_refs):
    m_scr, l_scr, acc_scr = _softmax_split(softmax_refs)
    tq, tk = TQ_FOX, TK_FOX
    i = pl.program_id(1)
    q0 = i * tq

    cq = cumt_ref[0]
    c_hi = cq.astype(BF16).astype(F32)
    c_r = cq - c_hi
    c_mid = c_r.astype(BF16).astype(F32)
    c_lo = c_r - c_mid
    row_q = lax.broadcasted_iota(I32, (LANES, tq), 0)
    row_a = lax.broadcasted_iota(I32, (FOX_AUG_ROWS, 2 * tq), 0)
    second = lax.broadcasted_iota(I32, (FOX_AUG_ROWS, 2 * tq), 1) >= tq
    for p in range(N_HEADS // 2):
        qp = qt_ref[0, p * LANES:(p + 1) * LANES, :]
        zq = jnp.zeros_like(qp)
        rhs_scr[p, 0:LANES, 0:tq] = jnp.where(row_q < HEAD_DIM, qp, zq)
        rhs_scr[p, 0:LANES, tq:2 * tq] = jnp.where(row_q >= HEAD_DIM, qp, zq)
        pair_row = lambda a: jnp.concatenate([a[2 * p:2 * p + 1, :], a[2 * p + 1:2 * p + 2, :]], axis=1)
        aug = jnp.where(row_a == 0, pair_row(c_hi),
              jnp.where(row_a == 1, pair_row(c_mid),
              jnp.where(row_a == 2, pair_row(c_lo),
              jnp.where((row_a >= 3) & (row_a < 6), jnp.where(second, 0.0, 1.0),
              jnp.where((row_a >= 6) & (row_a < 9), jnp.where(second, 1.0, 0.0), 0.0)))))
        rhs_scr[p, LANES:LANES + FOX_AUG_ROWS, :] = aug.astype(BF16)
        rhs_scr[p, LANES + FOX_AUG_ROWS:FOX_K, :] = jnp.zeros((FOX_K - LANES - FOX_AUG_ROWS, 2 * tq), BF16)

    _softmax_init(m_scr, l_scr, acc_scr)
    key_j = lax.broadcasted_iota(I32, (tk, tq), 0)
    qry_i = q0 + lax.broadcasted_iota(I32, (tk, tq), 1)

    def tile(kt, masked, second_pass):
        k0 = pl.multiple_of(kt * tk, tk)
        for p in range(N_HEADS // 2):
            st = _dot(kaug_ref[0, pl.ds(k0, tk), p * FOX_K:(p + 1) * FOX_K], rhs_scr[p])
            for hh in range(2):
                h = 2 * p + hh
                s = st[:, hh * tq:(hh + 1) * tq]
                if masked:
                    s = jnp.where(k0 + key_j <= qry_i, s, NEG_INF)
                if second_pass:
                    _softmax_accumulate(s, m_scr[h], l_scr[h], acc_scr[h],
                                        vt_ref[0, kt, h * HEAD_DIM:(h + 1) * HEAD_DIM, :])
                else:
                    _softmax_max(s, m_scr[h])

    n_full = i * (tq // tk)

    def sweep(second_pass):
        def full_tile(kt, carry):
            tile(kt, False, second_pass)
            return carry

        lax.fori_loop(0, n_full, full_tile, 0)
        for d in range(tq // tk):
            tile(n_full + d, True, second_pass)

    bounds, safe = _logit_bounds(qt_ref, kn_ref[0], FOX_BIAS_SLACK)

    @pl.when(safe)
    def _():
        for h in range(N_HEADS):
            m_scr[h][...] = bounds[h]
        sweep(True)

    @pl.when(jnp.logical_not(safe))
    def _():
        sweep(False)
        sweep(True)

    _softmax_finish(o_ref, l_scr, acc_scr)


def _fox_call(qbt, aux_t, kaug3, vbt4):
    batch, _, seq = qbt.shape
    return pl.pallas_call(
        _fox_kernel,
        grid=(batch, seq // TQ_FOX),
        in_specs=[pl.BlockSpec((1, 512, TQ_FOX), lambda b, i: (b, 0, i)),
                  pl.BlockSpec((1, N_HEADS, TQ_FOX), lambda b, i: (b, 1, i)),
                  pl.BlockSpec((1, N_HEADS, seq), lambda b, i: (b, 2, 0)),
                  pl.BlockSpec((1, seq, (N_HEADS // 2) * FOX_K), lambda b, i: (b, 0, 0)),
                  pl.BlockSpec((1, seq // TK_FOX, 512, TK_FOX), lambda b, i: (b, 0, 0, 0))],
        out_specs=pl.BlockSpec((1, 512, TQ_FOX), lambda b, i: (b, 0, i)),
        out_shape=jax.ShapeDtypeStruct((batch, 512, seq), BF16),
        scratch_shapes=[pltpu.VMEM((N_HEADS // 2, FOX_K, 2 * TQ_FOX), BF16)] + _softmax_scratch(TQ_FOX),
        compiler_params=_cparams(("arbitrary", "arbitrary")),
        name="fox",
    )(qbt, aux_t, aux_t, kaug3, vbt4)


def _bit_transpose32(words):
    a = list(words)
    j, mask = 16, 0x0000FFFF
    while j:
        mask_i = int(np.array(mask, np.uint32).view(np.int32))
        k = 0
        while k < 32:
            t = (a[k] ^ lax.shift_right_logical(a[k + j], j)) & mask_i
            a[k] = a[k] ^ t
            a[k + j] = a[k + j] ^ lax.shift_left(t, j)
            k = (k + j + 1) & ~j
        j >>= 1
        mask = (mask ^ (mask << j)) & 0xFFFFFFFF
    return a


def _dsa_kernel(qat_ref, qit_ref, wit_ref, kn_ref, ki_ref, ka_ref, vat_ref, o_ref,
                aqi_scr, aqa_scr, sc_scr, bias_scr, plane_scr, alive_scr, thr_scr, *softmax_refs, seq):
    m_scr, l_scr, acc_scr = _softmax_split(softmax_refs)
    tq, tk = TQ_DSA, TK_DSA
    blk_per_tile = tk // RADIX_BLOCK
    n_blocks = seq // RADIX_BLOCK
    i = pl.program_id(1)
    q0 = i * tq
    nk = lax.shift_right_logical(q0 + tq + tk - 1, int(np.log2(tk)))
    n_rest = (seq - nk * tk).astype(F32)

    row_q = lax.broadcasted_iota(I32, (LANES, tq), 0)
    for h in range(N_HEADS):
        keep = (row_q < HEAD_DIM) if h % 2 == 0 else (row_q >= HEAD_DIM)
        rows = slice((h // 2) * LANES, (h // 2 + 1) * LANES)
        qi_blk = qit_ref[0, rows, :]
        qa_blk = qat_ref[0, rows, :]
        aqi_scr[:, h * tq:(h + 1) * tq] = jnp.where(keep, qi_blk, jnp.zeros_like(qi_blk))
        aqa_scr[:, h * tq:(h + 1) * tq] = jnp.where(keep, qa_blk, jnp.zeros_like(qa_blk))

    w = wit_ref[0]
    key_j = lax.broadcasted_iota(I32, (tk, tq), 0)
    qry_chunk = lax.shift_right_logical(q0 + lax.broadcasted_iota(I32, (tk, tq), 1), CHUNK_SHIFT)

    def admissible(k0):
        return lax.shift_right_logical(k0 + key_j, CHUNK_SHIFT) <= qry_chunk

    def p1(kt, carry, masked):
        k0 = pl.multiple_of(kt * tk, tk)
        rel = _dot(ki_ref[0, pl.ds(k0, tk), :], aqi_scr[...])
        sc = w[0:1, :] * jnp.maximum(rel[:, 0:tq], 0.0)
        for h in range(1, N_HEADS):
            sc = sc + w[h:h + 1, :] * jnp.maximum(rel[:, h * tq:(h + 1) * tq], 0.0)
        if masked:
            sc = jnp.where(admissible(k0), sc, NEG_INF)
        sc = jnp.where(sc == 0.0, 0.0, sc)
        sc_scr[kt] = sc
        b = pltpu.bitcast(sc, I32)
        key = b ^ (lax.shift_right_arithmetic(b, 31) & 0x7FFFFFFF)
        for half in range(blk_per_tile):
            base = half * RADIX_BLOCK
            planes = _bit_transpose32([key[base + SUBLANES * m:base + SUBLANES * (m + 1), :] ^ _INT_MIN
                                       for m in range(32)])
            bl = kt * blk_per_tile + half
            for p in range(32):
                plane_scr[bl, p] = planes[p]
            alive_scr[bl] = jnp.full((SUBLANES, tq), -1, I32)
        return carry

    n_unmasked = lax.shift_right_logical(q0, int(np.log2(tk)))
    lax.fori_loop(0, n_unmasked, functools.partial(p1, masked=False), 0)
    lax.fori_loop(n_unmasked, nk, functools.partial(p1, masked=True), 0)

    def clear_block(bl, carry):
        plane_scr[bl] = jnp.zeros((32, SUBLANES, tq), I32)
        alive_scr[bl] = jnp.zeros((SUBLANES, tq), I32)
        return carry

    n_held = nk * blk_per_tile
    quarter = n_blocks // 4
    n_walk = lax.div(n_held + quarter - 1, quarter) * quarter
    lax.fori_loop(n_held, n_walk, clear_block, 0)

    sent_u = jnp.int32(_SENT_KEY ^ _INT_MIN)
    kf = float(TOPK)

    def radix_select(n_walk):
        def bit_step(p, carry):
            k_rem, thr_bits, rest_alive = carry
            shift = 31 - p
            sent_bit = lax.shift_right_logical(sent_u, shift) & 1
            ones = [alive_scr[bl] & plane_scr[bl, p] for bl in range(n_walk)]
            tot = lax.population_count(ones[0])
            for bl in range(1, n_walk):
                tot = tot + lax.population_count(ones[bl])
            cnt = jnp.sum(tot.astype(F32), axis=0, keepdims=True)
            cnt = cnt + jnp.where((rest_alive != 0) & (sent_bit != 0), n_rest, 0.0)
            take1 = cnt >= k_rem
            for bl in range(n_walk):
                alive_scr[bl] = jnp.where(take1, ones[bl], alive_scr[bl] & ~plane_scr[bl, p])
            k_rem = jnp.where(take1, k_rem, k_rem - cnt)
            thr_bits = thr_bits | jnp.where(take1, lax.shift_left(jnp.int32(1), shift), 0)
            rest_alive = jnp.where(take1 == (sent_bit != 0), rest_alive, 0)
            return k_rem, thr_bits, rest_alive

        _, thr_bits, _ = lax.fori_loop(
            0, 32, bit_step,
            (jnp.full((1, tq), kf, F32), jnp.zeros((1, tq), I32), jnp.ones((1, tq), I32)))
        thr_scr[...] = thr_bits

    for walk in range(quarter, n_blocks + 1, quarter):
        pl.when(n_walk == walk)(functools.partial(radix_select, walk))
    thr_key = thr_scr[...] ^ _INT_MIN
    thr0 = pltpu.bitcast(thr_key ^ (lax.shift_right_arithmetic(thr_key, 31) & 0x7FFFFFFF), F32)

    def rank_counts(thr):
        def body(kt, c):
            sc = sc_scr[kt]
            return (c[0] + jnp.sum(jnp.where(sc >= thr, 1.0, 0.0), axis=0, keepdims=True),
                    c[1] + jnp.sum(jnp.where(sc > thr, 1.0, 0.0), axis=0, keepdims=True))
        z = jnp.zeros((1, tq), F32)
        n_ge, n_gt = lax.fori_loop(0, nk, body, (z, z))
        return (n_ge + jnp.where(thr <= NEG_INF, n_rest, 0.0), n_gt + jnp.where(thr < NEG_INF, n_rest, 0.0))

    def misplaced(state):
        _, n_ge, n_gt = state
        return jnp.max(jnp.where((n_ge < kf) | (n_gt >= kf), 1.0, 0.0)) > 0.0

    def step_threshold(state):
        thr, n_ge, n_gt = state

        def body(kt, c):
            sc = sc_scr[kt]
            below = jnp.max(jnp.where(sc < thr, sc, -jnp.inf), axis=0, keepdims=True)
            above = jnp.min(jnp.where(sc > thr, sc, jnp.inf), axis=0, keepdims=True)
            return jnp.maximum(c[0], below), jnp.minimum(c[1], above)

        below, above = lax.fori_loop(0, nk, body, (jnp.full((1, tq), -jnp.inf, F32), jnp.full((1, tq), jnp.inf, F32)))
        has_rest = n_rest > 0.0
        below = jnp.where(has_rest & (thr > NEG_INF), jnp.maximum(below, NEG_INF), below)
        above = jnp.where(has_rest & (thr < NEG_INF), jnp.minimum(above, NEG_INF), above)
        thr = jnp.where(n_ge < kf, below, jnp.where(n_gt >= kf, above, thr))
        return (thr,) + rank_counts(thr)

    thr, n_ge, n_gt = lax.while_loop(misplaced, step_threshold, (thr0,) + rank_counts(thr0))
    need = kf - n_gt
    tie_free = jnp.min(jnp.where(n_ge == kf, 1.0, 0.0)) > 0.0

    tr = lax.broadcasted_iota(I32, (tk, tk), 0)
    tc = lax.broadcasted_iota(I32, (tk, tk), 1)
    tri = jnp.where(tc <= tr, 1.0, 0.0).astype(BF16)
    _softmax_init(m_scr, l_scr, acc_scr)

    def selection_bias(kt, tie):
        sc = sc_scr[kt]
        eq = sc == thr
        pref = _dot(tri, jnp.where(eq, 1.0, 0.0).astype(BF16)) + tie
        sel = (sc > thr) | (eq & (pref <= need))
        return jnp.where(sel & admissible(kt * tk), 0.0, NEG_INF), pref[tk - 1:tk, :]

    def logits(kt):
        return _dot(ka_ref[0, pl.ds(pl.multiple_of(kt * tk, tk), tk), :], aqa_scr[...])

    def accumulate(kt, bias):
        logit = logits(kt)
        v_t = vat_ref[0, kt]
        for h in range(N_HEADS):
            _softmax_accumulate(logit[:, h * tq:(h + 1) * tq] + bias, m_scr[h], l_scr[h], acc_scr[h], v_t)

    def p3_single(kt, tie):
        bias, tie = selection_bias(kt, tie)
        accumulate(kt, bias)
        return tie

    def p3_single_tie_free(kt, carry):
        accumulate(kt, jnp.where(sc_scr[kt] >= thr, 0.0, NEG_INF))
        return carry

    def p3_max(kt, tie):
        bias, tie = selection_bias(kt, tie)
        bias_scr[kt] = bias
        logit = logits(kt)
        for h in range(N_HEADS):
            _softmax_max(logit[:, h * tq:(h + 1) * tq] + bias, m_scr[h])
        return tie

    def p3_accumulate(kt, carry):
        accumulate(kt, bias_scr[kt])
        return carry

    kn2 = jnp.broadcast_to(kn_ref[0, 0:1, :], (N_HEADS, seq))
    bounds, safe = _logit_bounds(qat_ref, kn2, 0.0)
    no_tie = jnp.zeros((1, tq), F32)

    @pl.when(safe)
    def _():
        for h in range(N_HEADS):
            m_scr[h][...] = bounds[h]

    @pl.when(safe & tie_free)
    def _():
        lax.fori_loop(0, nk, p3_single_tie_free, 0)

    @pl.when(safe & jnp.logical_not(tie_free))
    def _():
        lax.fori_loop(0, nk, p3_single, no_tie)

    @pl.when(jnp.logical_not(safe))
    def _():
        lax.fori_loop(0, nk, p3_max, no_tie)
        lax.fori_loop(0, nk, p3_accumulate, 0)

    _softmax_finish(o_ref, l_scr, acc_scr)


def _dsa_call(qat, qit, aux_t, ki3, ka3, vat4):
    batch, _, seq = qat.shape
    qspec = pl.BlockSpec((1, 512, TQ_DSA), lambda b, i: (b, 0, i))
    kspec = pl.BlockSpec((1, seq, LANES), lambda b, i: (b, 0, 0))
    return pl.pallas_call(
        functools.partial(_dsa_kernel, seq=seq),
        grid=(batch, seq // TQ_DSA),
        in_specs=[qspec, qspec,
                  pl.BlockSpec((1, N_HEADS, TQ_DSA), lambda b, i: (b, 0, i)),
                  pl.BlockSpec((1, N_HEADS, seq), lambda b, i: (b, 3, 0)),
                  kspec, kspec,
                  pl.BlockSpec((1, seq // TK_DSA, HEAD_DIM, TK_DSA), lambda b, i: (b, 0, 0, 0))],
        out_specs=qspec,
        out_shape=jax.ShapeDtypeStruct((batch, 512, seq), BF16),
        scratch_shapes=[pltpu.VMEM((LANES, N_HEADS * TQ_DSA), BF16),
                        pltpu.VMEM((LANES, N_HEADS * TQ_DSA), BF16),
                        pltpu.VMEM((seq // TK_DSA, TK_DSA, TQ_DSA), F32),
                        pltpu.VMEM((seq // TK_DSA, TK_DSA, TQ_DSA), F32),
                        pltpu.VMEM((seq // RADIX_BLOCK, 32, SUBLANES, TQ_DSA), I32),
                        pltpu.VMEM((seq // RADIX_BLOCK, SUBLANES, TQ_DSA), I32),
                        pltpu.VMEM((1, TQ_DSA), I32)] + _softmax_scratch(TQ_DSA),
        compiler_params=_cparams(("arbitrary", "arbitrary")),
        name="dsa",
    )(qat, qit, aux_t, aux_t, ki3, ka3, vat4)


def _outproj_kernel(oa_ref, ob_ref, sga_ref, sgb_ref, x_ref, mod_ref, gffn_ref,
                    woa_ref, wob_ref, wo_ref, wr_hi_ref, wr_lo_ref, br_ref,
                    x1_ref, h2_ref, lg_ref):
    tm = TM_OUT
    tn_dot = lambda a, b: lax.dot_general(a, b, (((0,), (0,)), ((), ())), preferred_element_type=F32)
    ya = tn_dot(oa_ref[0], woa_ref[...])
    yb = tn_dot(ob_ref[0], wob_ref[...])
    merged = (sga_ref[...].astype(F32) * ya + sgb_ref[...].astype(F32) * yb).astype(BF16)
    mix = _dot(merged, wo_ref[...])
    gt1 = mod_ref[0, 2:3, :]
    sh2 = mod_ref[0, 3:4, :]
    sc2 = mod_ref[0, 4:5, :]
    x1 = x_ref[...] + gt1 * mix
    x1_ref[...] = x1
    var = jnp.mean(x1 * x1, axis=-1, keepdims=True)
    h2 = x1 * lax.rsqrt(var + RMS_EPS) * gffn_ref[...] * (1.0 + sc2) + sh2
    for c in range(D_MODEL // LANES):
        h2_ref[pl.ds(c, tm, stride=SUBLANES), :] = h2[:, c * LANES:(c + 1) * LANES]
    hi = h2.astype(BF16)
    lo = (h2 - hi.astype(F32)).astype(BF16)
    r = _dot(jnp.concatenate([hi, lo], axis=0), jnp.concatenate([wr_hi_ref[...], wr_lo_ref[...]], axis=1))
    lg_ref[...] = r[0:tm, 0:LANES] + r[0:tm, LANES:2 * LANES] + r[tm:2 * tm, 0:LANES] + br_ref[...]


def _outproj_call(oa2, ob2, sga, sgb, x2, mod3, g_ffn, woa, wob, wo, wr_hi, wr_lo, br, seq):
    n = x2.shape[0]
    tpb = seq // TM_OUT
    row = lambda w: pl.BlockSpec((TM_OUT, w), lambda i: (i, 0))
    full = lambda a: pl.BlockSpec(a.shape, lambda i: (0,) * a.ndim)
    return pl.pallas_call(
        _outproj_kernel,
        grid=(n // TM_OUT,),
        in_specs=[pl.BlockSpec((1, 512, TM_OUT), lambda i: (i // tpb, 0, i % tpb)),
                  pl.BlockSpec((1, 512, TM_OUT), lambda i: (i // tpb, 0, i % tpb)),
                  row(1024), row(1024), row(D_MODEL),
                  pl.BlockSpec((1, 6, D_MODEL), lambda i: (i // tpb, 0, 0)),
                  full(g_ffn), full(woa), full(wob), full(wo), full(wr_hi), full(wr_lo), full(br)],
        out_specs=[row(D_MODEL),
                   pl.BlockSpec((TM_OUT * SUBLANES, LANES), lambda i: (i, 0)),
                   row(LANES)],
        out_shape=[jax.ShapeDtypeStruct((n, D_MODEL), F32),
                   jax.ShapeDtypeStruct((n * SUBLANES, LANES), F32),
                   jax.ShapeDtypeStruct((n, LANES), F32)],
        compiler_params=_cparams(("arbitrary",), [False] * 7 + [True] * 5 + [False]),
        name="outproj",
    )(oa2, ob2, sga, sgb, x2, mod3, g_ffn, woa, wob, wo, wr_hi, wr_lo, br)


def _route_rows(lg):
    tm = lg.shape[0]
    lane = lax.broadcasted_iota(I32, (tm, LANES), 1)
    big = jnp.int32(LANES)
    ninf = -jnp.inf
    gmask = (lane >= N_EXPERTS) & (lane < N_EXPERTS + N_GROUPS)
    g = jnp.where(gmask, lg, ninf)
    gmax = jnp.max(g, axis=-1, keepdims=True)
    grp = jnp.min(jnp.where(g == gmax, lane - N_EXPERTS, big), axis=-1, keepdims=True)
    p_grp = 1.0 / jnp.sum(jnp.where(gmask, jnp.exp(lg - gmax), 0.0), axis=-1, keepdims=True)
    lo = grp * EXPERTS_PER_GROUP
    emask = (lane >= lo) & (lane < lo + EXPERTS_PER_GROUP)
    ev = jnp.where(emask, lg, ninf)
    v0 = jnp.max(ev, axis=-1, keepdims=True)
    i0 = jnp.min(jnp.where(emask & (ev == v0), lane, big), axis=-1, keepdims=True)
    rest = emask & (lane != i0)
    ev1 = jnp.where(rest, lg, ninf)
    v1 = jnp.max(ev1, axis=-1, keepdims=True)
    i1 = jnp.min(jnp.where(rest & (ev1 == v1), lane, big), axis=-1, keepdims=True)
    e1 = jnp.exp(v1 - v0)
    w0 = p_grp / (1.0 + e1)
    w1 = p_grp * e1 / (1.0 + e1)
    rt = jnp.where(lane == 0, i0.astype(F32),
                   jnp.where(lane == 1, i1.astype(F32),
                             jnp.where(lane == 2, w0, jnp.where(lane == 3, w1, 0.0))))
    return rt, lane == i0, lane == i1


def _expert_onehots(rt, tm):
    lane = lax.broadcasted_iota(I32, (tm, LANES), 1)
    lane_f = lane.astype(F32)
    e0 = jnp.sum(jnp.where(lane == 0, rt, 0.0), axis=-1, keepdims=True)
    e1 = jnp.sum(jnp.where(lane == 1, rt, 0.0), axis=-1, keepdims=True)
    return lane, lane_f == e0, lane_f == e1


def _rank_rows(is0, is1, seen_before):
    tm = is0.shape[0]
    lane = lax.broadcasted_iota(I32, (tm, LANES), 1)
    hits = jnp.where(is0, 1.0, 0.0) + jnp.where(is1, 1.0, 0.0)
    rr = lax.broadcasted_iota(I32, (tm, tm), 0)
    cc = lax.broadcasted_iota(I32, (tm, tm), 1)
    before = jnp.where(cc < rr, 1.0, 0.0).astype(BF16)
    seen = _dot(before, hits.astype(BF16)) + seen_before
    r0 = jnp.sum(jnp.where(is0, seen, 0.0), axis=-1, keepdims=True)
    r1 = jnp.sum(jnp.where(is1, seen, 0.0), axis=-1, keepdims=True)
    rk = jnp.where(lane == 0, r0, jnp.where(lane == 1, r1, 0.0))
    return rk, seen_before + jnp.sum(hits, axis=0, keepdims=True)


def _route_rank_kernel(lg_ref, rt_ref, rk_ref, cnt_ref, seen_scr):
    @pl.when(pl.program_id(0) == 0)
    def _():
        seen_scr[...] = jnp.zeros_like(seen_scr)

    rt, is0, is1 = _route_rows(lg_ref[...])
    rt_ref[...] = rt
    rk, seen = _rank_rows(is0, is1, seen_scr[...])
    rk_ref[...] = rk
    seen_scr[...] = seen
    cnt_ref[...] = jnp.broadcast_to(seen, cnt_ref.shape)


def _route_rank_call(lg):
    n = lg.shape[0]
    spec = pl.BlockSpec((TM_ROUTE, LANES), lambda i: (i, 0))
    return pl.pallas_call(
        _route_rank_kernel, grid=(n // TM_ROUTE,), in_specs=[spec],
        out_specs=[spec, spec, pl.BlockSpec((SUBLANES, LANES), lambda i: (0, 0))],
        out_shape=[jax.ShapeDtypeStruct((n, LANES), F32), jax.ShapeDtypeStruct((n, LANES), F32),
                   jax.ShapeDtypeStruct((SUBLANES, LANES), F32)],
        scratch_shapes=[pltpu.VMEM((1, LANES), F32)],
        compiler_params=_cparams(("arbitrary",)), name="route_rank",
    )(lg)


def _dest_kernel(rt_ref, rk_ref, cnt_ref, dd_ref, be_ref, nu_ref, end_ref):
    tm = TM_ROUTE
    lane, is0, is1 = _expert_onehots(rt_ref[...], tm)
    blocks = jnp.floor((cnt_ref[...] + (BLK_E - 1)) * (1.0 / BLK_E))
    er = lax.broadcasted_iota(I32, (LANES, LANES), 0)
    ec = lax.broadcasted_iota(I32, (LANES, LANES), 1)
    upto = jnp.where(er <= ec, 1.0, 0.0).astype(BF16)
    bend = _dot(blocks.astype(BF16), upto)
    pstart = (bend[0:1, :] - blocks[0:1, :]) * BLK_E
    rk = rk_ref[...]
    r0 = jnp.sum(jnp.where(lane == 0, rk, 0.0), axis=-1, keepdims=True)
    r1 = jnp.sum(jnp.where(lane == 1, rk, 0.0), axis=-1, keepdims=True)
    d0 = jnp.sum(jnp.where(is0, pstart, 0.0), axis=-1, keepdims=True) + r0
    d1 = jnp.sum(jnp.where(is1, pstart, 0.0), axis=-1, keepdims=True) + r1
    dd_ref[...] = jnp.where(lane == 0, d0, jnp.where(lane == 1, d1, 0.0)).astype(I32)
    nb = be_ref.shape[0]
    blk = lax.broadcasted_iota(I32, (nb, LANES), 0).astype(F32)
    lane_b = lax.broadcasted_iota(I32, (nb, LANES), 1)
    ended = jnp.where((lane_b < N_EXPERTS) & (bend[0:1, :] <= blk), 1.0, 0.0)
    be = jnp.minimum(jnp.sum(ended, axis=-1, keepdims=True), float(N_EXPERTS - 1))
    be_ref[...] = jnp.broadcast_to(be, be_ref.shape).astype(I32)
    lane_c = lax.broadcasted_iota(I32, (SUBLANES, LANES), 1)
    used = jnp.sum(jnp.where(lane_c == N_EXPERTS - 1, bend, 0.0), axis=-1, keepdims=True)
    nu_ref[...] = jnp.broadcast_to(used, nu_ref.shape).astype(I32)
    end_ref[...] = (bend * BLK_E).astype(I32)


def _dest_call(rt, rk, cnt, n_blocks):
    n = rt.shape[0]
    nb_pad = -(-n_blocks // SUBLANES) * SUBLANES
    spec = pl.BlockSpec((TM_ROUTE, LANES), lambda i: (i, 0))
    const = lambda rows: pl.BlockSpec((rows, LANES), lambda i: (0, 0))
    return pl.pallas_call(
        _dest_kernel, grid=(n // TM_ROUTE,), in_specs=[spec, spec, const(SUBLANES)],
        out_specs=[spec, const(nb_pad), const(SUBLANES), const(SUBLANES)],
        out_shape=[jax.ShapeDtypeStruct((n, LANES), I32), jax.ShapeDtypeStruct((nb_pad, LANES), I32),
                   jax.ShapeDtypeStruct((SUBLANES, LANES), I32), jax.ShapeDtypeStruct((SUBLANES, LANES), I32)],
        compiler_params=_cparams(("arbitrary",)), name="dest",
    )(rt, rk, cnt)


def _slab_copy(src, src_row, dst, dst_row, sem):
    return pltpu.make_async_copy(src.at[pl.ds(pl.multiple_of(src_row * SUBLANES, SUBLANES), SUBLANES)],
                                 dst.at[pl.ds(pl.multiple_of(dst_row * SUBLANES, SUBLANES), SUBLANES)], sem)


def _dispatch_kernel(dest_ref, seg_end_ref, h2_ref, xs_hbm, zero_scr, sem, zsem):
    tm = TM_DISPATCH
    base = pl.program_id(0) * tm

    @pl.when(pl.program_id(0) == 0)
    def _():
        zero_scr[...] = jnp.zeros_like(zero_scr)
        clears = [pltpu.make_async_copy(
            zero_scr, xs_hbm.at[pl.ds(pl.multiple_of(jnp.maximum(seg_end_ref[e] - BLK_E, 0) * SUBLANES, SUBLANES),
                                      BLK_E * SUBLANES)], zsem) for e in range(N_EXPERTS)]
        for c in clears:
            c.start()
        for c in clears:
            c.wait()
        n_blocks = xs_hbm.shape[0] // (BLK_E * SUBLANES)
        first_free = lax.shift_right_logical(seg_end_ref[N_EXPERTS - 1], int(np.log2(BLK_E)))
        def tail_clear(b):
            row = pl.multiple_of((first_free + b) * (BLK_E * SUBLANES), BLK_E * SUBLANES)
            return pltpu.make_async_copy(zero_scr, xs_hbm.at[pl.ds(row, BLK_E * SUBLANES)], zsem)

        for b in range(N_EXPERTS):
            pl.when(first_free + b < n_blocks)(lambda b=b: tail_clear(b).start())
        for b in range(N_EXPERTS):
            pl.when(first_free + b < n_blocks)(lambda b=b: tail_clear(b).wait())

    def issue(r, _):
        for k in range(2):
            _slab_copy(h2_ref, r, xs_hbm, dest_ref[2 * (base + r) + k], sem).start(priority=k)
        return 0

    lax.fori_loop(0, tm, issue, 0, unroll=GATHER_UNROLL)
    for _ in range(2):
        pltpu.make_async_copy(h2_ref, xs_hbm.at[pl.ds(0, tm * SUBLANES)], sem).wait()


def _dispatch_call(dest, seg_end, h2s, n_slots):
    n = h2s.shape[0] // SUBLANES
    return pl.pallas_call(
        _dispatch_kernel,
        grid_spec=pltpu.PrefetchScalarGridSpec(
            num_scalar_prefetch=2, grid=(n // TM_DISPATCH,),
            in_specs=[pl.BlockSpec((TM_DISPATCH * SUBLANES, LANES), lambda i, d, e: (i, 0))],
            out_specs=pl.BlockSpec(memory_space=pl.ANY),
            scratch_shapes=[pltpu.VMEM((BLK_E * SUBLANES, LANES), F32),
                            pltpu.SemaphoreType.DMA(()), pltpu.SemaphoreType.DMA(())]),
        out_shape=jax.ShapeDtypeStruct((n_slots * SUBLANES, LANES), F32),
        compiler_params=_cparams(("arbitrary",)),
        name="dispatch",
    )(dest, seg_end, h2s)


def _experts_kernel(be_ref, nused_ref, xs_ref, wg_hbm, wu_hbm, wd_hbm, yb_ref,
                    x_scr, wg_f32, wu_f32, wd_f32, wsem, run_smem, wg_scr, wu_scr, wd_scr):
    i = pl.program_id(0)
    n_used = nused_ref[0]
    n_blocks = be_ref.shape[0]

    def weight_copies(e, slot):
        return [pltpu.make_async_copy(wg_hbm.at[e], wg_f32.at[slot], wsem.at[slot, 0]),
                pltpu.make_async_copy(wu_hbm.at[e], wu_f32.at[slot], wsem.at[slot, 1]),
                pltpu.make_async_copy(wd_hbm.at[e], wd_f32.at[slot], wsem.at[slot, 2])]

    @pl.when((i == 0) & (n_used > 0))
    def _():
        run_smem[0] = 0
        for c in weight_copies(be_ref[0], 0):
            c.start()

    @pl.when(i >= n_used)
    def _():
        yb_ref[...] = jnp.zeros_like(yb_ref)

    @pl.when(i < n_used)
    def _():
        e = be_ref[i]
        prev = be_ref[jnp.maximum(i - 1, 0)]

        @pl.when((i == 0) | (e != prev))
        def _():
            slot = run_smem[0] % 2
            for c in weight_copies(e, slot):
                c.wait()
            wg_scr[...] = wg_f32[slot].astype(BF16)
            wu_scr[...] = wu_f32[slot].astype(BF16)
            wd_scr[...] = wd_f32[slot].astype(BF16)
            nxt = lax.while_loop(
                lambda j: (j < n_used) & (be_ref[jnp.minimum(j, n_blocks - 1)] == e), lambda j: j + 1, i + 1)

            @pl.when(nxt < n_used)
            def _():
                for c in weight_copies(be_ref[jnp.minimum(nxt, n_blocks - 1)], 1 - slot):
                    c.start()

            run_smem[0] = run_smem[0] + 1

        for c in range(D_MODEL // LANES):
            x_scr[:, c * LANES:(c + 1) * LANES] = xs_ref[pl.ds(c, BLK_E, stride=SUBLANES), :].astype(BF16)
        xb = x_scr[...]
        g = _dot(xb, wg_scr[...])
        u = _dot(xb, wu_scr[...])
        hid = (g * _sigmoid(g) * u).astype(BF16)
        y = _dot(hid, wd_scr[...])
        for c in range(D_MODEL // LANES):
            yb_ref[pl.ds(c, BLK_E, stride=SUBLANES), :] = y[:, c * LANES:(c + 1) * LANES]


def _experts_call(block_e, n_used, xs, w_gate, w_up, w_down):
    n_blocks = block_e.shape[0]
    slab = pl.BlockSpec((BLK_E * SUBLANES, LANES), lambda i, be, nu: (i, 0))
    used_slab = pl.BlockSpec((BLK_E * SUBLANES, LANES), lambda i, be, nu: (jnp.minimum(i, nu[0] - 1), 0))
    return pl.pallas_call(
        _experts_kernel,
        grid_spec=pltpu.PrefetchScalarGridSpec(
            num_scalar_prefetch=2, grid=(n_blocks,),
            in_specs=[used_slab] + [pl.BlockSpec(memory_space=pl.ANY)] * 3,
            out_specs=slab,
            scratch_shapes=[pltpu.VMEM((BLK_E, D_MODEL), BF16),
                            pltpu.VMEM((2, D_MODEL, D_EXPERT), F32),
                            pltpu.VMEM((2, D_MODEL, D_EXPERT), F32),
                            pltpu.VMEM((2, D_EXPERT, D_MODEL), F32),
                            pltpu.SemaphoreType.DMA((2, 3)),
                            pltpu.SMEM((1,), I32),
                            pltpu.VMEM((D_MODEL, D_EXPERT), BF16),
                            pltpu.VMEM((D_MODEL, D_EXPERT), BF16),
                            pltpu.VMEM((D_EXPERT, D_MODEL), BF16)]),
        out_shape=jax.ShapeDtypeStruct(xs.shape, F32),
        compiler_params=_cparams(("arbitrary",)),
        name="experts",
    )(block_e, n_used, xs, w_gate, w_up, w_down)


def _final_kernel(dest_ref, x1_ref, rt_ref, mod_ref, gfin_ref, yb_hbm, o_ref, g_scr, sem):
    tm = TM_FINAL
    step = pl.program_id(0)
    slot = step % 2
    slot_rows = 2 * tm * SUBLANES

    def gather(s, into):
        def issue(r, _):
            tok = s * tm + r
            for k in range(2):
                _slab_copy(yb_hbm, dest_ref[2 * tok + k], g_scr.at[into], k * tm + r,
                           sem.at[into]).start(priority=k)
            return 0
        lax.fori_loop(0, tm, issue, 0, unroll=GATHER_UNROLL)

    @pl.when(step == 0)
    def _():
        gather(0, 0)

    @pl.when(step + 1 < pl.num_programs(0))
    def _():
        gather(step + 1, 1 - slot)

    g_now = g_scr.at[slot]
    pltpu.make_async_copy(yb_hbm.at[pl.ds(0, slot_rows)], g_now, sem.at[slot]).wait()

    rt = rt_ref[...]
    lane = lax.broadcasted_iota(I32, (tm, LANES), 1)
    gw0 = jnp.sum(jnp.where(lane == 2, rt, 0.0), axis=-1, keepdims=True)
    gw1 = jnp.sum(jnp.where(lane == 3, rt, 0.0), axis=-1, keepdims=True)
    gt2 = mod_ref[0, 5:6, :]
    x1 = x1_ref[...]
    cols = []
    for c in range(D_MODEL // LANES):
        y0 = g_now[pl.ds(c, tm, stride=SUBLANES), :]
        y1 = g_now[pl.ds(tm * SUBLANES + c, tm, stride=SUBLANES), :]
        y = gw0 * y0 + gw1 * y1
        cols.append(x1[:, c * LANES:(c + 1) * LANES] + gt2[:, c * LANES:(c + 1) * LANES] * y)
    x2 = jnp.concatenate(cols, axis=1)
    var = jnp.mean(x2 * x2, axis=-1, keepdims=True)
    o_ref[...] = x2 * lax.rsqrt(var + RMS_EPS) * gfin_ref[...]


def _final_call(dest, x1, rt, mod3, g_final, yb, seq):
    n = x1.shape[0]
    tpb = seq // TM_FINAL
    return pl.pallas_call(
        _final_kernel,
        grid_spec=pltpu.PrefetchScalarGridSpec(
            num_scalar_prefetch=1, grid=(n // TM_FINAL,),
            in_specs=[pl.BlockSpec((TM_FINAL, D_MODEL), lambda i, d: (i, 0)),
                      pl.BlockSpec((TM_FINAL, LANES), lambda i, d: (i, 0)),
                      pl.BlockSpec((1, 6, D_MODEL), lambda i, d: (i // tpb, 0, 0)),
                      pl.BlockSpec((1, D_MODEL), lambda i, d: (0, 0)),
                      pl.BlockSpec(memory_space=pl.ANY)],
            out_specs=pl.BlockSpec((TM_FINAL, D_MODEL), lambda i, d: (i, 0)),
            scratch_shapes=[pltpu.VMEM((2, 2 * TM_FINAL * SUBLANES, LANES), F32),
                            pltpu.SemaphoreType.DMA((2,))]),
        out_shape=jax.ShapeDtypeStruct((n, D_MODEL), F32),
        compiler_params=_cparams(("arbitrary",)),
        name="final",
    )(dest, x1, rt, mod3, g_final, yb)


def _permute_w_in(w):
    o = np.cumsum([0, 512, 64, 64, 512, 64, 8, 512, 512, 512, 8, 1024, 1024])
    qa, ka, va, qi, ki, wi, qb, kb, vb, fb, ga, gb = [w[:, o[k]:o[k + 1]] for k in range(12)]
    aux = jnp.concatenate([wi, fb, jnp.zeros((w.shape[0], LANES - 2 * N_HEADS), w.dtype)], axis=1)
    w_rows = jnp.concatenate([kb, ga, gb, ka, ka, ki, ki, aux], axis=1).astype(BF16)
    w_t = jnp.concatenate([qa, qi, qb, vb, va], axis=1).T.astype(BF16)
    return w_rows, w_t


def _layer(x2, pos2, mod3, batch, seq, g_mix, w_in, b_forget, w_out_a, w_out_b, w_out, g_ffn,
           w_group, b_group, w_router, b_router, w_e_gate, w_e_up, w_e_down, g_final):
    n = x2.shape[0]
    inv_freq = ROPE_THETA ** (-jnp.arange(0, ROPE_DIM, 2, dtype=F32) / ROPE_DIM)
    jj = np.arange(LANES) % HEAD_DIM
    invf = jnp.where(jj < ROPE_DIM, inv_freq[jj % (ROPE_DIM // 2)], 0.0)[None, :].astype(F32)
    fbias = jnp.zeros((1, LANES), F32).at[0, N_HEADS:2 * N_HEADS].set(b_forget.astype(F32))

    invf8 = jnp.broadcast_to(inv_freq[:, None], (ROPE_DIM // 2, LANES)).astype(F32)
    w_rows, w_t = _permute_w_in(w_in)
    (qat, qit, qbt, kaug, vbt, sga, sgb, ka2, vat, ki2, aux_t) = _inproj_call(
        x2, pos2, pos2.reshape(batch, 1, seq), mod3, g_mix.reshape(1, D_MODEL), w_rows, w_t, invf, invf8, fbias,
        batch, seq)

    r3 = lambda a: a.reshape(batch, seq, a.shape[-1])
    obt = _fox_call(qbt, aux_t, r3(kaug), vbt)
    oat = _dsa_call(qat, qit, aux_t, r3(ki2), r3(ka2), vat)

    w_rt = jnp.concatenate([w_router, w_group, jnp.zeros((D_MODEL, LANES - N_EXPERTS - N_GROUPS), F32)], axis=1)
    wr_hi = w_rt.astype(BF16)
    wr_lo = (w_rt - wr_hi.astype(F32)).astype(BF16)
    br = jnp.concatenate([b_router, b_group, jnp.zeros((LANES - N_EXPERTS - N_GROUPS,), F32)])[None, :]
    x1, h2s, lg = _outproj_call(oat, obt, sga, sgb, x2, mod3,
                                g_ffn.reshape(1, D_MODEL), w_out_a.astype(BF16), w_out_b.astype(BF16),
                                w_out.astype(BF16), wr_hi, wr_lo, br, seq)
    rt, rk, cnt = _route_rank_call(lg)

    n_slots = 2 * n + N_EXPERTS * BLK_E
    n_blocks = n_slots // BLK_E
    dd, be, nu, seg_end = _dest_call(rt, rk, cnt, n_blocks)
    dest = dd[:, 0:2].reshape(-1)
    block_e = be[:n_blocks, 0]
    n_used = nu[0, 0:1]

    xs = _dispatch_call(dest, seg_end[0, :N_EXPERTS], h2s, n_slots)
    yb = _experts_call(block_e, n_used, xs, w_e_gate, w_e_up, w_e_down)
    return _final_call(dest, x1, rt, mod3, g_final.reshape(1, D_MODEL), yb, seq)


def kernel(x, c, positions, w_mod, b_mod, g_mix, w_in, b_forget, w_out_a, w_out_b, w_out, g_ffn, w_group,
           b_group, w_router, b_router, w_e_gate, w_e_up, w_e_down, g_final):
    batch, seq, d = x.shape
    depth = w_mod.shape[0]
    assert depth == 1 and d == D_MODEL, "kernel fuses the final norm into the single layer"
    n = batch * seq
    c8 = jnp.zeros((8, d), F32).at[:batch].set(c)
    mod = _mod_call(c8, w_mod[0], b_mod[0][None, :])
    mod3 = mod[:batch].reshape(batch, 6, d)
    out = _layer(x.reshape(n, d), positions.reshape(n, 1), mod3, batch, seq, g_mix[0], w_in[0], b_forget[0],
                 w_out_a[0], w_out_b[0], w_out[0], g_ffn[0], w_group[0], b_group[0], w_router[0], b_router[0],
                 w_e_gate[0], w_e_up[0], w_e_down[0], g_final)
    return out.reshape(batch, seq, d)
```
